```python
import math
import jax, jax.numpy as jnp
from jax import lax
import numpy as np

D_MODEL = 1024
BATCH = 8
SEQ = 2048
DEPTH = 2

CTX_LEN = 256
GRID_W = 64
CHUNK = 64
EPS = 1e-6
MLSTM_HEAD_DIM = 128
MLSTM_WIDTH = D_MODEL // 2
MLSTM_HEADS = MLSTM_WIDTH // MLSTM_HEAD_DIM
S5_WIDTH = D_MODEL - MLSTM_WIDTH
S5_GROUP = 16
S5_GROUPS = S5_WIDTH // S5_GROUP
S5_STATE = 64
GDN_HEAD_DIM = 128
GDN_WIDTH = D_MODEL
GDN_HEADS = GDN_WIDTH // GDN_HEAD_DIM
FFN_HIDDEN = ((8 * D_MODEL + 3 * 256 - 1) // (3 * 256)) * 256
EVEN_IN = 4 * MLSTM_WIDTH + 4 * MLSTM_HEADS + S5_WIDTH
ODD_IN = 4 * GDN_WIDTH + 4 * GDN_HEADS

kernel_name = "hybrid_mlstm_s5_gdn_prefix_dit"


def rmsnorm(x, w):
    xf = x.astype(jnp.float32)
    y = xf * lax.rsqrt(jnp.mean(xf * xf, axis=-1, keepdims=True) + EPS)
    return (y * w.astype(jnp.float32)).astype(x.dtype)


def modulate(x, shift, scale):
    return x * (1.0 + scale) + shift


def swiglu(h, w1, w3, w2):
    return (jax.nn.silu(h @ w1) * (h @ w3)) @ w2


def _split_cols(p, sizes):
    idx = [int(v) for v in np.cumsum(sizes)[:-1]]
    return jnp.split(p, idx, axis=-1)


def _heads(t, n_heads):
    b, s, w = t.shape
    return t.reshape(b, s, n_heads, w // n_heads).transpose(0, 2, 1, 3)


def _merge(t):
    b, h, s, d = t.shape
    return t.transpose(0, 2, 1, 3).reshape(b, s, h * d)


def _gate(t):
    return jnp.swapaxes(t.astype(jnp.float32), 1, 2)


def _l2norm(t):
    return t * lax.rsqrt(jnp.sum(t * t, axis=-1, keepdims=True) + EPS)


def _to_dir(c_part, l_part, reverse):
    if reverse:
        c_part, l_part = jnp.flip(c_part, 2), jnp.flip(l_part, 2)
    return jnp.concatenate([c_part, l_part], axis=2)


def _from_dir(y, n_ctx, reverse):
    yc, yl = y[:, :, :n_ctx], y[:, :, n_ctx:]
    if reverse:
        yc, yl = jnp.flip(yc, 2), jnp.flip(yl, 2)
    return yc, yl


def mlstm_chunkwise(q, k, v, ig, lf):
    b, h, t, d = q.shape
    nc, L = t // CHUNK, CHUNK
    q, k, v = (a.reshape(b, h, nc, L, d) for a in (q, k, v))
    ig, lf = ig.reshape(b, h, nc, L), lf.reshape(b, h, nc, L)
    F = jnp.cumsum(lf, axis=-1)
    F_last = F[..., -1]
    w = F_last[..., None] - F + ig
    m_loc = jnp.max(w, axis=-1)
    e = jnp.exp(w - m_loc[..., None])
    C_loc = jnp.einsum('bhcl,bhcld,bhcle->bhcde', e, k, v)
    n_loc = jnp.einsum('bhcl,bhcld->bhcd', e, k)

    def step(carry, xs):
        C, n, m = carry
        fl, ml, Cl, nl = xs
        m_new = jnp.maximum(fl + m, ml)
        a_prev, a_loc = jnp.exp(fl + m - m_new), jnp.exp(ml - m_new)
        C_new = a_prev[..., None, None] * C + a_loc[..., None, None] * Cl
        n_new = a_prev[..., None] * n + a_loc[..., None] * nl
        return (C_new, n_new, m_new), (C, n, m)

    init = (jnp.zeros((b, h, d, d), q.dtype), jnp.zeros((b, h, d), q.dtype), jnp.zeros((b, h), q.dtype))
    xs = tuple(jnp.moveaxis(a, 2, 0) for a in (F_last, m_loc, C_loc, n_loc))
    _, (C_in, n_in, m_in) = lax.scan(step, init, xs)
    C_in, n_in, m_in = (jnp.moveaxis(a, 0, 2) for a in (C_in, n_in, m_in))

    causal = jnp.tril(jnp.ones((L, L), bool))
    Dm = jnp.where(causal, F[..., :, None] - F[..., None, :] + ig[..., None, :], -jnp.inf)
    inter = F + m_in[..., None]
    m_t = jnp.maximum(jnp.max(Dm, axis=-1), inter)
    P = jnp.exp(Dm - m_t[..., None]) * jnp.einsum('bhcld,bhcsd->bhcls', q, k)
    a_inter = jnp.exp(inter - m_t)
    num = jnp.einsum('bhcls,bhcse->bhcle', P, v) + a_inter[..., None] * jnp.einsum('bhcld,bhcde->bhcle', q, C_in)
    den = jnp.sum(P, axis=-1) + a_inter * jnp.einsum('bhcld,bhcd->bhcl', q, n_in)
    out = num / jnp.maximum(jnp.abs(den), jnp.exp(-m_t))[..., None]
    return out.reshape(b, h, t, d)


def s5_discretize(lam_re, lam_im, log_dt, b_re, b_im):
    f32 = jnp.float32
    lam_re, lam_im, b_re, b_im = (a.astype(f32) for a in (lam_re, lam_im, b_re, b_im))
    dt = jnp.exp(log_dt.astype(f32))[:, None]
    mag, ang = jnp.exp(lam_re * dt), lam_im * dt
    ab_re, ab_im = mag * jnp.cos(ang), mag * jnp.sin(ang)
    nr, ni = ab_re - 1.0, ab_im
    den = lam_re * lam_re + lam_im * lam_im
    co_re = (nr * lam_re + ni * lam_im) / den
    co_im = (ni * lam_re - nr * lam_im) / den
    bb_re = co_re[..., None] * b_re - co_im[..., None] * b_im
    bb_im = co_re[..., None] * b_im + co_im[..., None] * b_re
    return ab_re, ab_im, bb_re, bb_im


def complex_linear_scan(a_re, a_im, b_re, b_im):
    def combine(e1, e2):
        a1r, a1i, b1r, b1i = e1
        a2r, a2i, b2r, b2i = e2
        return (a1r * a2r - a1i * a2i, a1r * a2i + a1i * a2r,
                a2r * b1r - a2i * b1i + b2r, a2r * b1i + a2i * b1r + b2i)
    _, _, s_re, s_im = lax.associative_scan(combine, (a_re, a_im, b_re, b_im), axis=2)
    return s_re, s_im


def gated_delta_chunkwise(q, k, v, g, beta):
    b, h, t, dk = q.shape
    dv = v.shape[-1]
    nc, L = t // CHUNK, CHUNK
    q, k = q.reshape(b, h, nc, L, dk), k.reshape(b, h, nc, L, dk)
    v = v.reshape(b, h, nc, L, dv)
    g, beta = g.reshape(b, h, nc, L), beta.reshape(b, h, nc, L)
    G = jnp.cumsum(g, axis=-1)
    lower = jnp.tril(jnp.ones((L, L), bool))
    strict = jnp.tril(jnp.ones((L, L), bool), -1)
    gamma = jnp.exp(jnp.where(lower, G[..., :, None] - G[..., None, :], -jnp.inf))
    kb = k * beta[..., None]
    A = jnp.where(strict, jnp.einsum('bhcid,bhcjd->bhcij', kb, k) * gamma, 0.0)
    rhs = jnp.concatenate([v * beta[..., None], kb * jnp.exp(G)[..., None]], axis=-1)
    sol = lax.linalg.triangular_solve(A + jnp.eye(L, dtype=A.dtype), rhs,
                                      left_side=True, lower=True, unit_diagonal=True)
    U, W = sol[..., :dv], sol[..., dv:]
    attn = jnp.einsum('bhcid,bhcjd->bhcij', q, k) * gamma
    qg = q * jnp.exp(G)[..., None]
    kd = k * jnp.exp(G[..., -1:] - G)[..., None]
    gl = jnp.exp(G[..., -1])

    def step(S, xs):
        U_c, W_c, qg_c, kd_c, gl_c = xs
        v_new = U_c - jnp.einsum('bhld,bhde->bhle', W_c, S)
        o_inter = jnp.einsum('bhld,bhde->bhle', qg_c, S)
        S = gl_c[..., None, None] * S + jnp.einsum('bhld,bhle->bhde', kd_c, v_new)
        return S, (v_new, o_inter)

    xs = tuple(jnp.moveaxis(a, 2, 0) for a in (U, W, qg, kd, gl))
    _, (v_new, o_inter) = lax.scan(step, jnp.zeros((b, h, dk, dv), q.dtype), xs)
    v_new, o_inter = jnp.moveaxis(v_new, 0, 2), jnp.moveaxis(o_inter, 0, 2)
    o = o_inter + jnp.einsum('bhcij,bhcje->bhcie', attn, v_new)
    return o.reshape(b, h, t, dv)


def _depthwise_conv3x3(x, w):
    return lax.conv_general_dilated(x, w, window_strides=(1, 1), padding='SAME',
                                    dimension_numbers=('NHWC', 'HWIO', 'NHWC'),
                                    feature_group_count=x.shape[-1])


def even_mixer(hc, hl, w_in, i_bias, f_bias, head_norm_w, lam_re, lam_im, log_dt,
               b_re, b_im, c_re, c_im, s5_d, w_glu, w_out, with_ctx):
    f32 = jnp.float32
    n_ctx = hc.shape[1]
    sizes = [MLSTM_WIDTH] * 4 + [MLSTM_HEADS] * 4 + [S5_WIDTH]
    pc, pl = _split_cols(hc @ w_in, sizes), _split_cols(hl @ w_in, sizes)
    kscale = 1.0 / math.sqrt(MLSTM_HEAD_DIM)

    qc, kc, vc = _heads(pc[0].astype(f32), MLSTM_HEADS), _heads(pc[1].astype(f32), MLSTM_HEADS) * kscale, _heads(pc[2].astype(f32), MLSTM_HEADS)
    ql, kl, vl = _heads(pl[0].astype(f32), MLSTM_HEADS), _heads(pl[1].astype(f32), MLSTM_HEADS) * kscale, _heads(pl[2].astype(f32), MLSTM_HEADS)
    gc, gl = [_gate(t) for t in pc[4:8]], [_gate(t) for t in pl[4:8]]
    hA_c, hA_l = 0.0, 0.0
    for r, rev in enumerate((False, True)):
        ig = _to_dir(gc[r], gl[r], rev) + i_bias[r].astype(f32)[None, :, None]
        lf = jax.nn.log_sigmoid(_to_dir(gc[2 + r], gl[2 + r], rev) + f_bias[r].astype(f32)[None, :, None])
        hseq = mlstm_chunkwise(_to_dir(qc, ql, rev), _to_dir(kc, kl, rev), _to_dir(vc, vl, rev), ig, lf)
        h_c, h_l = _from_dir(hseq, n_ctx, rev)
        hA_c, hA_l = hA_c + h_c, hA_l + h_l

    b = hl.shape[0]
    uc = pc[8].astype(f32).reshape(b, n_ctx, S5_GROUPS, S5_GROUP)
    ul = pl[8].astype(f32).reshape(b, -1, S5_GROUPS, S5_GROUP)
    yB_c, yB_l = 0.0, 0.0
    for r, rev in enumerate((False, True)):
        ab_re, ab_im, bb_re, bb_im = s5_discretize(lam_re[r], lam_im[r], log_dt[r], b_re[r], b_im[r])
        seq_re = _to_dir(jnp.einsum('btgp,gnp->bgtn', uc, bb_re), jnp.einsum('btgp,gnp->bgtn', ul, bb_re), rev)
        seq_im = _to_dir(jnp.einsum('btgp,gnp->bgtn', uc, bb_im), jnp.einsum('btgp,gnp->bgtn', ul, bb_im), rev)
        a_re = jnp.broadcast_to(ab_re[None, :, None, :], seq_re.shape)
        a_im = jnp.broadcast_to(ab_im[None, :, None, :], seq_im.shape)
        s_re, s_im = complex_linear_scan(a_re, a_im, seq_re, seq_im)
        sc_re, sl_re = _from_dir(s_re, n_ctx, rev)
        sc_im, sl_im = _from_dir(s_im, n_ctx, rev)
        cr, ci = c_re[r].astype(f32), c_im[r].astype(f32)
        yB_c = yB_c + jnp.einsum('bgtn,gpn->btgp', sc_re, cr) - jnp.einsum('bgtn,gpn->btgp', sc_im, ci)
        yB_l = yB_l + jnp.einsum('bgtn,gpn->btgp', sl_re, cr) - jnp.einsum('bgtn,gpn->btgp', sl_im, ci)

    mh_w = head_norm_w.reshape(MLSTM_HEADS, 1, MLSTM_HEAD_DIM)
    d_skip = s5_d.astype(f32).reshape(S5_GROUPS, S5_GROUP)
    wg = w_glu.astype(f32)

    def finish(hA, o_pre, yB, u):
        a_out = _merge(rmsnorm(hA, mh_w)) * jax.nn.sigmoid(o_pre.astype(f32))
        yb = jax.nn.gelu(yB + d_skip * u)
        yb = yb.reshape(yb.shape[0], yb.shape[1], S5_WIDTH)
        ga, gg = jnp.split(yb @ wg, 2, axis=-1)
        b_out = ga * jax.nn.sigmoid(gg)
        return jnp.concatenate([a_out, b_out], axis=-1).astype(hl.dtype) @ w_out

    y_lat = finish(hA_l, pl[3], yB_l, ul)
    y_ctx = finish(hA_c, pc[3], yB_c, uc) if with_ctx else None
    return y_ctx, y_lat


def odd_mixer(hc, hl, rows, w_in, conv_w, a_log, dt_bias, head_norm_w, w_out, with_ctx):
    f32 = jnp.float32
    n_ctx = hc.shape[1]
    b = hl.shape[0]
    sizes = [3 * GDN_WIDTH, GDN_WIDTH] + [GDN_HEADS] * 4
    pc, pl = _split_cols(hc @ w_in, sizes), _split_cols(hl @ w_in, sizes)
    qkv_c = jax.nn.silu(_depthwise_conv3x3(pc[0].reshape(b, 1, n_ctx, 3 * GDN_WIDTH), conv_w)).reshape(b, n_ctx, 3 * GDN_WIDTH)
    qkv_l = jax.nn.silu(_depthwise_conv3x3(pl[0].reshape(b, rows, GRID_W, 3 * GDN_WIDTH), conv_w)).reshape(b, -1, 3 * GDN_WIDTH)
    qscale = 1.0 / math.sqrt(GDN_HEAD_DIM)

    def qkv_heads(t):
        q, k, v = jnp.split(t.astype(f32), 3, axis=-1)
        return (_l2norm(_heads(q, GDN_HEADS)) * qscale, _l2norm(_heads(k, GDN_HEADS)), _heads(v, GDN_HEADS))

    qc, kc, vc = qkv_heads(qkv_c)
    ql, kl, vl = qkv_heads(qkv_l)
    gc, gl = [_gate(t) for t in pc[2:6]], [_gate(t) for t in pl[2:6]]
    o_c, o_l = 0.0, 0.0
    for r, rev in enumerate((False, True)):
        a_pre = _to_dir(gc[r], gl[r], rev)
        g = -jnp.exp(a_log[r].astype(f32))[None, :, None] * jax.nn.softplus(a_pre + dt_bias[r].astype(f32)[None, :, None])
        beta = jax.nn.sigmoid(_to_dir(gc[2 + r], gl[2 + r], rev))
        o = gated_delta_chunkwise(_to_dir(qc, ql, rev), _to_dir(kc, kl, rev), _to_dir(vc, vl, rev), g, beta)
        oc, ol = _from_dir(o, n_ctx, rev)
        o_c, o_l = o_c + oc, o_l + ol

    hw = head_norm_w.reshape(GDN_HEADS, 1, GDN_HEAD_DIM)

    def finish(o, z):
        y = _merge(rmsnorm(o, hw)) * jax.nn.silu(z.astype(f32))
        return y.astype(hl.dtype) @ w_out

    y_lat = finish(o_l, pl[1])
    y_ctx = finish(o_c, pc[1]) if with_ctx else None
    return y_ctx, y_lat


def setup_inputs(seed: int = 0) -> dict:
    key = jax.random.key(seed)
    keys = iter(jax.random.split(key, 48))
    f32 = jnp.float32
    n_even, n_odd = (DEPTH + 1) // 2, DEPTH // 2
    d = D_MODEL

    def normal(shape, scale=1.0):
        return jax.random.normal(next(keys), shape, f32) * scale

    def uniform(shape, lo, hi):
        return jax.random.uniform(next(keys), shape, f32, lo, hi)

    def gain(shape):
        return 1.0 + normal(shape, 0.05)

    s5_shape = (n_even, 2, S5_GROUPS, S5_STATE)
    log_dt_s5 = uniform((n_even, 2, S5_GROUPS), math.log(1e-3), math.log(1e-1))
    dt_gdn = jnp.exp(uniform((n_odd, 2, GDN_HEADS), math.log(1e-3), math.log(1e-1)))
    return {
        "x": normal((BATCH, SEQ, d)),
        "c": normal((BATCH, d)),
        "ctx": normal((BATCH, CTX_LEN, d)),
        "c_ctx": normal((d,)),
        "ada_w": normal((DEPTH, d, 6 * d), 0.3 * d ** -0.5),
        "ada_b": normal((DEPTH, 6 * d), 0.01),
        "norm1_w": gain((DEPTH, d)),
        "norm2_w": gain((DEPTH, d)),
        "ffn_w1": normal((DEPTH, d, FFN_HIDDEN), d ** -0.5),
        "ffn_w3": normal((DEPTH, d, FFN_HIDDEN), d ** -0.5),
        "ffn_w2": normal((DEPTH, FFN_HIDDEN, d), FFN_HIDDEN ** -0.5),
        "final_norm_w": gain((d,)),
        "ev_w_in": normal((n_even, d, EVEN_IN), d ** -0.5),
        "ev_i_bias": normal((n_even, 2, MLSTM_HEADS), 0.1),
        "ev_f_bias": uniform((n_even, 2, MLSTM_HEADS), 3.0, 6.0),
        "ev_head_norm_w": gain((n_even, MLSTM_WIDTH)),
        "ev_lam_re": -0.5 + normal(s5_shape, 0.01),
        "ev_lam_im": jnp.pi * jnp.arange(S5_STATE, dtype=f32) + normal(s5_shape, 0.01),
        "ev_log_dt": log_dt_s5,
        "ev_b_re": normal((n_even, 2, S5_GROUPS, S5_STATE, S5_GROUP), (2 * S5_GROUP) ** -0.5),
        "ev_b_im": normal((n_even, 2, S5_GROUPS, S5_STATE, S5_GROUP), (2 * S5_GROUP) ** -0.5),
        "ev_c_re": normal((n_even, 2, S5_GROUPS, S5_GROUP, S5_STATE), S5_STATE ** -0.5),
        "ev_c_im": normal((n_even, 2, S5_GROUPS, S5_GROUP, S5_STATE), S5_STATE ** -0.5),
        "ev_d": normal((n_even, S5_WIDTH)),
        "ev_w_glu": normal((n_even, S5_WIDTH, 2 * S5_WIDTH), S5_WIDTH ** -0.5),
        "ev_w_out": normal((n_even, MLSTM_WIDTH + S5_WIDTH, d), d ** -0.5),
        "od_w_in": normal((n_odd, d, ODD_IN), d ** -0.5),
        "od_conv_w": normal((n_odd, 3, 3, 1, 3 * GDN_WIDTH), 1.0 / 3.0),
        "od_a_log": jnp.log(uniform((n_odd, 2, GDN_HEADS), 1.0, 16.0)),
        "od_dt_bias": dt_gdn + jnp.log(-jnp.expm1(-dt_gdn)),
        "od_head_norm_w": gain((n_odd, GDN_WIDTH)),
        "od_w_out": normal((n_odd, GDN_WIDTH, d), d ** -0.5),
    }


def reference(x, c, ctx, c_ctx, ada_w, ada_b, norm1_w, norm2_w, ffn_w1, ffn_w3, ffn_w2, final_norm_w,
              ev_w_in, ev_i_bias, ev_f_bias, ev_head_norm_w, ev_lam_re, ev_lam_im, ev_log_dt,
              ev_b_re, ev_b_im, ev_c_re, ev_c_im, ev_d, ev_w_glu, ev_w_out,
              od_w_in, od_conv_w, od_a_log, od_dt_bias, od_head_norm_w, od_w_out):
    rows = x.shape[1] // GRID_W
    s_lat = jax.nn.silu(c)
    s_ctx = jax.nn.silu(c_ctx)
    for i in range(DEPTH):
        with_ctx = i < DEPTH - 1
        j = i // 2
        m_l = jnp.split((s_lat @ ada_w[i] + ada_b[i])[:, None, :], 6, axis=-1)
        m_c = jnp.split((s_ctx @ ada_w[i] + ada_b[i])[None, None, :], 6, axis=-1)
        hl = modulate(rmsnorm(x, norm1_w[i]), m_l[0], m_l[1])
        hc = modulate(rmsnorm(ctx, norm1_w[i]), m_c[0], m_c[1])
        if i % 2 == 0:
            y_ctx, y_lat = even_mixer(hc, hl, ev_w_in[j], ev_i_bias[j], ev_f_bias[j], ev_head_norm_w[j],
                                      ev_lam_re[j], ev_lam_im[j], ev_log_dt[j], ev_b_re[j], ev_b_im[j],
                                      ev_c_re[j], ev_c_im[j], ev_d[j], ev_w_glu[j], ev_w_out[j], with_ctx)
        else:
            y_ctx, y_lat = odd_mixer(hc, hl, rows, od_w_in[j], od_conv_w[j], od_a_log[j], od_dt_bias[j],
                                     od_head_norm_w[j], od_w_out[j], with_ctx)
        x = x + m_l[2] * y_lat
        x = x + m_l[5] * swiglu(modulate(rmsnorm(x, norm2_w[i]), m_l[3], m_l[4]), ffn_w1[i], ffn_w3[i], ffn_w2[i])
        if with_ctx:
            ctx = ctx + m_c[2] * y_ctx
            ctx = ctx + m_c[5] * swiglu(modulate(rmsnorm(ctx, norm2_w[i]), m_c[3], m_c[4]), ffn_w1[i], ffn_w3[i], ffn_w2[i])
    return rmsnorm(x, final_norm_w)
```

```python
import functools
import math

import jax
import jax.numpy as jnp
from jax import lax
from jax.experimental import pallas as pl
from jax.experimental.pallas import tpu as pltpu

F32 = jnp.float32
BF16 = jnp.bfloat16

D_MODEL = 1024
CHUNK = 64
GRID_W = 64
EPS = 1e-6
LANES = 128
HEAD_DIM = 128
MLSTM_WIDTH = D_MODEL // 2
MLSTM_HEADS = MLSTM_WIDTH // HEAD_DIM
S5_WIDTH = D_MODEL - MLSTM_WIDTH
S5_GROUP = 16
S5_GROUPS = S5_WIDTH // S5_GROUP
S5_STATE = 64
S5_HALVES = 2
S5_HALF_W = S5_WIDTH // S5_HALVES
S5_HALF_N = (S5_GROUPS // S5_HALVES) * S5_STATE
GDN_HEADS = D_MODEL // HEAD_DIM
GDN_QKV = 3 * D_MODEL
FFN_HIDDEN = ((8 * D_MODEL + 3 * 256 - 1) // (3 * 256)) * 256
EVEN_MAIN = 4 * MLSTM_WIDTH + LANES
ODD_MAIN = 4 * D_MODEL + LANES
VMEM_LIMIT = 56 * 1024 * 1024

_NT = (((1,), (1,)), ((), ()))
_TN = (((0,), (0,)), ((), ()))


def _cparams(sem):
    return pltpu.CompilerParams(dimension_semantics=sem, vmem_limit_bytes=VMEM_LIMIT)


def _dot(a, b):
    return jnp.dot(a.astype(BF16), b.astype(BF16), preferred_element_type=F32)


def _dot_nt(a, b):
    return lax.dot_general(a.astype(BF16), b.astype(BF16), _NT, preferred_element_type=F32)


def _dot_tn(a, b):
    return lax.dot_general(a.astype(BF16), b.astype(BF16), _TN, preferred_element_type=F32)


def _rms(x, w):
    return x * lax.rsqrt(jnp.mean(x * x, axis=-1, keepdims=True) + EPS) * w


def _ada_kernel(s_ref, w_ref, b_ref, o_ref):
    s = s_ref[...]
    s = s * jax.nn.sigmoid(s)
    o_ref[0] = _dot(s, w_ref[0]) + b_ref[0]


def _ada(cvec, ada_w, ada_b):
    depth, d, n = ada_w.shape
    tn = 1536
    rows = cvec.shape[0]
    return pl.pallas_call(
        _ada_kernel,
        grid=(depth, n // tn),
        in_specs=[pl.BlockSpec((rows, d), lambda l, j: (0, 0)),
                  pl.BlockSpec((1, d, tn), lambda l, j: (l, 0, j)),
                  pl.BlockSpec((1, 1, tn), lambda l, j: (l, 0, j))],
        out_specs=pl.BlockSpec((1, rows, tn), lambda l, j: (l, 0, j)),
        out_shape=jax.ShapeDtypeStruct((depth, rows, n), F32),
        compiler_params=_cparams(("parallel", "parallel")),
        name="ada_mod",
    )(cvec, ada_w, ada_b.reshape(depth, 1, n))


def _mod_spec(layer, chunk, ctx_tiles, ctx_row):
    def imap(b, i, *_):
        return (layer, jnp.where(i < ctx_tiles, ctx_row, b), 0, chunk)
    return pl.BlockSpec((1, 1, 1, D_MODEL), imap)


def _inproj_kernel(x_ref, nw_ref, sh_ref, sc_ref, w_ref, *rest, splits, n_chunk):
    out_refs, h_scr = rest[:-1], rest[-1]
    x = x_ref[0]
    h = _rms(x, nw_ref[...]) * (1.0 + sc_ref[0, 0]) + sh_ref[0, 0]
    h_scr[...] = h.astype(BF16)
    for o_ref, (c0, c1) in zip(out_refs, splits):
        for n0 in range(c0, c1, n_chunk):
            n1 = min(n0 + n_chunk, c1)
            val = jnp.dot(h_scr[...], w_ref[:, n0:n1], preferred_element_type=F32)
            if len(o_ref.shape) == 3:
                o_ref[0, :, n0 - c0:n1 - c0] = val.astype(o_ref.dtype)
            else:
                o_ref[:, n0 - c0:n1 - c0] = val.astype(o_ref.dtype)


def _inproj(xc, norm_w, mods, layer, w_bf16, splits, token_major, tm, n_ctx):
    bsz, t, d = xc.shape
    n = w_bf16.shape[1]
    ctx_tiles = n_ctx // tm
    out_specs, out_shapes = [], []
    for (c0, c1), tmaj in zip(splits, token_major):
        wd = c1 - c0
        if tmaj:
            out_specs.append(pl.BlockSpec((tm, wd), lambda b, i: (i, b)))
            out_shapes.append(jax.ShapeDtypeStruct((t, bsz * wd), F32))
        else:
            out_specs.append(pl.BlockSpec((1, tm, wd), lambda b, i: (b, i, 0)))
            out_shapes.append(jax.ShapeDtypeStruct((bsz, t, wd), F32))
    return pl.pallas_call(
        functools.partial(_inproj_kernel, splits=tuple(splits), n_chunk=512),
        grid=(bsz, t // tm),
        in_specs=[pl.BlockSpec((1, tm, d), lambda b, i: (b, i, 0)),
                  pl.BlockSpec((1, d), lambda b, i: (0, 0)),
                  _mod_spec(layer, 0, ctx_tiles, bsz),
                  _mod_spec(layer, 1, ctx_tiles, bsz),
                  pl.BlockSpec((d, n), lambda b, i: (0, 0))],
        out_specs=out_specs,
        out_shape=out_shapes,
        scratch_shapes=[pltpu.VMEM((tm, d), BF16)],
        compiler_params=_cparams(("parallel", "parallel")),
        name=f"inproj_l{layer}",
    )(xc, norm_w.reshape(1, d), mods, mods, w_bf16)


def _ffn_kernel(x_ref, nw_ref, sh_ref, sc_ref, g_ref, w1_ref, w3_ref, w2_ref, fw_ref, o_ref,
                h_scr, acc_scr, *, final):
    k = pl.program_id(2)

    @pl.when(k == 0)
    def _():
        h = _rms(x_ref[0], nw_ref[...]) * (1.0 + sc_ref[0, 0]) + sh_ref[0, 0]
        h_scr[...] = h.astype(BF16)
        acc_scr[...] = jnp.zeros_like(acc_scr)

    h = h_scr[...]
    a = jnp.dot(h, w1_ref[...], preferred_element_type=F32)
    g = jnp.dot(h, w3_ref[...], preferred_element_type=F32)
    t = (a * jax.nn.sigmoid(a)) * g
    acc_scr[...] += jnp.dot(t.astype(BF16), w2_ref[...], preferred_element_type=F32)

    @pl.when(k == pl.num_programs(2) - 1)
    def _():
        y = x_ref[0] + g_ref[0, 0] * acc_scr[...]
        if final:
            y = _rms(y, fw_ref[...])
        o_ref[0] = y


def _ffn(x, norm_w, mods, layer, w1, w3, w2, final_w, tm, th, n_ctx, row0, final):
    bsz, tx, d = x.shape
    hid = w1.shape[1]
    t0 = row0 // tm
    ctx_tiles = (n_ctx - row0) // tm if n_ctx > row0 else 0
    mspec = lambda c: _mod_spec(layer, c, ctx_tiles, bsz)
    return pl.pallas_call(
        functools.partial(_ffn_kernel, final=final),
        grid=(bsz, (tx - row0) // tm, hid // th),
        in_specs=[pl.BlockSpec((1, tm, d), lambda b, i, k: (b, i + t0, 0)),
                  pl.BlockSpec((1, d), lambda b, i, k: (0, 0)),
                  mspec(3), mspec(4), mspec(5),
                  pl.BlockSpec((d, th), lambda b, i, k: (0, k)),
                  pl.BlockSpec((d, th), lambda b, i, k: (0, k)),
                  pl.BlockSpec((th, d), lambda b, i, k: (k, 0)),
                  pl.BlockSpec((1, d), lambda b, i, k: (0, 0))],
        out_specs=pl.BlockSpec((1, tm, d), lambda b, i, k: (b, i, 0)),
        out_shape=jax.ShapeDtypeStruct((bsz, tx - row0, d), F32),
        scratch_shapes=[pltpu.VMEM((tm, d), BF16), pltpu.VMEM((tm, d), F32)],
        compiler_params=_cparams(("parallel", "parallel", "arbitrary")),
        name=f"ffn_l{layer}",
    )(x, norm_w.reshape(1, d), mods, mods, mods, w1, w3, w2, final_w.reshape(1, d))


def _chunk_masks():
    row = lax.broadcasted_iota(jnp.int32, (CHUNK, CHUNK), 0)
    col = lax.broadcasted_iota(jnp.int32, (CHUNK, CHUNK), 1)
    return row, col


def _to_col(row_vec, eye):
    return jnp.sum(jnp.where(eye, jnp.broadcast_to(row_vec, eye.shape), 0.0), axis=1, keepdims=True)


def _to_row(col_vec, eye):
    return jnp.sum(jnp.where(eye, jnp.broadcast_to(col_vec, eye.shape), 0.0), axis=0, keepdims=True)


def _scan_chunk_index(i, rev, n_chunks, ctx_chunks):
    if not rev:
        return i
    return jnp.where(i < ctx_chunks, ctx_chunks - 1 - i, n_chunks + ctx_chunks - 1 - i)


def _mlstm_kernel(bias_ref, q_ref, k_ref, v_ref, g_ref, o_ref, *, n_chunks, ctx_chunks):
    hd = pl.program_id(1)
    row, col = _chunk_masks()
    eye = row == col
    kscale = 1.0 / math.sqrt(HEAD_DIM)
    o_ref[...] = jnp.zeros_like(o_ref)

    def one_dir(r, i, state):
        c_st, n_st, m_st = state
        rev = r == 1
        c = _scan_chunk_index(i, rev, n_chunks, ctx_chunks)
        rows = pl.ds(pl.multiple_of(c * CHUNK, CHUNK), CHUNK)
        incl = (col >= row) if rev else (col <= row)
        q = q_ref[0, rows, :]
        k = k_ref[0, rows, :] * kscale
        v = v_ref[0, rows, :]
        ig_row = g_ref[0, r * MLSTM_HEADS + hd, pl.ds(c, 1), :] + bias_ref[r * MLSTM_HEADS + hd]
        f_pre = (g_ref[0, (2 + r) * MLSTM_HEADS + hd, pl.ds(c, 1), :]
                 + bias_ref[(2 + r) * MLSTM_HEADS + hd])
        lf_row = jax.nn.log_sigmoid(f_pre)
        f_col = jnp.sum(jnp.where(incl, jnp.broadcast_to(lf_row, incl.shape), 0.0),
                        axis=1, keepdims=True)
        f_row = _to_row(f_col, eye)
        ig_col = _to_col(ig_row, eye)
        f_last = jnp.sum(lf_row, axis=1, keepdims=True)

        dm = jnp.where(incl, f_col - f_row + ig_row, -jnp.inf)
        inter = f_col + m_st
        m_t = jnp.maximum(jnp.max(dm, axis=1, keepdims=True), inter)
        p = jnp.exp(dm - m_t) * _dot_nt(q, k)
        a_inter = jnp.exp(inter - m_t)
        num = _dot(p, v) + a_inter * _dot(q, c_st)
        den = jnp.sum(p, axis=1, keepdims=True) + a_inter * jnp.sum(q * n_st, axis=1, keepdims=True)
        out = num / jnp.maximum(jnp.abs(den), jnp.exp(-m_t))
        o_ref[0, rows, :] += out

        w_col = f_last - f_col + ig_col
        m_loc = jnp.max(w_col, axis=0, keepdims=True)
        ek = jnp.exp(w_col - m_loc) * k
        c_loc = _dot_tn(ek, v)
        n_loc = jnp.sum(ek, axis=0, keepdims=True)
        m_new = jnp.maximum(f_last + m_st, m_loc)
        a_prev = jnp.exp(f_last + m_st - m_new)
        a_loc = jnp.exp(m_loc - m_new)
        return (a_prev * c_st + a_loc * c_loc, a_prev * n_st + a_loc * n_loc, m_new)

    def step(i, carry):
        return (one_dir(0, i, carry[0]), one_dir(1, i, carry[1]))

    init = (jnp.zeros((HEAD_DIM, HEAD_DIM), F32), jnp.zeros((1, HEAD_DIM), F32), jnp.zeros((1, 1), F32))
    lax.fori_loop(0, n_chunks, step, (init, init))


def _mlstm(main, gates_row, bias, n_ctx):
    bsz, t, _ = main.shape
    n_chunks = t // CHUNK
    qkv_spec = lambda off: pl.BlockSpec((1, t, HEAD_DIM), lambda b, h: (b, 0, off + h))
    return pl.pallas_call(
        functools.partial(_mlstm_kernel, n_chunks=n_chunks, ctx_chunks=n_ctx // CHUNK),
        grid=(bsz, MLSTM_HEADS),
        in_specs=[pl.BlockSpec(memory_space=pltpu.SMEM),
                  qkv_spec(0), qkv_spec(MLSTM_HEADS), qkv_spec(2 * MLSTM_HEADS),
                  pl.BlockSpec((1, 4 * MLSTM_HEADS, n_chunks, CHUNK), lambda b, h: (b, 0, 0, 0))],
        out_specs=pl.BlockSpec((1, t, HEAD_DIM), lambda b, h: (b, 0, h)),
        out_shape=jax.ShapeDtypeStruct((bsz, t, MLSTM_WIDTH), F32),
        compiler_params=_cparams(("parallel", "parallel")),
        name="mlstm",
    )(bias, main, main, main, gates_row)


def _s5_prep_kernel(lr_ref, li_ref, ldt_ref, br_ref, bi_ref, cr_ref, ci_ref, bd_ref, cdt_ref, a_ref):
    lr, li = lr_ref[0, 0], li_ref[0, 0]
    dt = jnp.exp(ldt_ref[0, 0])
    mag, ang = jnp.exp(lr * dt), li * dt
    ab_re, ab_im = mag * jnp.cos(ang), mag * jnp.sin(ang)
    nr, ni = ab_re - 1.0, ab_im
    den = lr * lr + li * li
    co_re = (nr * lr + ni * li) / den
    co_im = (ni * lr - nr * li) / den
    b_re, b_im = br_ref[0, 0], bi_ref[0, 0]
    bb_re = co_re * b_re - co_im * b_im
    bb_im = co_re * b_im + co_im * b_re
    c_re, c_im = cr_ref[0, 0], ci_ref[0, 0]
    lane_group = lax.broadcasted_iota(jnp.int32, (S5_GROUP, S5_HALF_N), 1) // S5_STATE
    n = S5_HALF_N
    for g in range(S5_GROUPS // S5_HALVES):
        sel = lane_group == g
        rows = slice(g * S5_GROUP, (g + 1) * S5_GROUP)
        bd_ref[0, 0, rows, :n] = jnp.where(sel, bb_re, 0.0).astype(bd_ref.dtype)
        bd_ref[0, 0, rows, n:] = jnp.where(sel, bb_im, 0.0).astype(bd_ref.dtype)
        cdt_ref[0, 0, rows, :n] = jnp.where(sel, c_re, 0.0).astype(cdt_ref.dtype)
        cdt_ref[0, 0, rows, n:] = jnp.where(sel, -c_im, 0.0).astype(cdt_ref.dtype)
    a_ref[0, 0, :, :n] = jnp.broadcast_to(ab_re, (8, n))
    a_ref[0, 0, :, n:] = jnp.broadcast_to(ab_im, (8, n))


def _s5_prep(lam_re, lam_im, log_dt, b_re, b_im, c_re, c_im):
    gh = S5_GROUPS // S5_HALVES
    vec = lambda a: a.reshape(2, S5_HALVES, 1, S5_HALF_N)
    ldt = vec(jnp.broadcast_to(log_dt[:, :, None], (2, S5_GROUPS, S5_STATE)))
    bt = lambda a: a.reshape(2, S5_HALVES, gh, S5_STATE, S5_GROUP).transpose(0, 1, 4, 2, 3).reshape(
        2, S5_HALVES, S5_GROUP, S5_HALF_N)
    ct = lambda a: a.reshape(2, S5_HALVES, gh, S5_GROUP, S5_STATE).transpose(0, 1, 3, 2, 4).reshape(
        2, S5_HALVES, S5_GROUP, S5_HALF_N)
    vspec = pl.BlockSpec((1, 1, 1, S5_HALF_N), lambda r, h: (r, h, 0, 0))
    mspec = pl.BlockSpec((1, 1, S5_GROUP, S5_HALF_N), lambda r, h: (r, h, 0, 0))
    ospec = pl.BlockSpec((1, 1, S5_HALF_W, 2 * S5_HALF_N), lambda r, h: (r, h, 0, 0))
    return pl.pallas_call(
        _s5_prep_kernel,
        grid=(2, S5_HALVES),
        in_specs=[vspec, vspec, vspec, mspec, mspec, mspec, mspec],
        out_specs=[ospec, ospec, pl.BlockSpec((1, 1, 8, 2 * S5_HALF_N), lambda r, h: (r, h, 0, 0))],
        out_shape=[jax.ShapeDtypeStruct((2, S5_HALVES, S5_HALF_W, 2 * S5_HALF_N), BF16),
                   jax.ShapeDtypeStruct((2, S5_HALVES, S5_HALF_W, 2 * S5_HALF_N), BF16),
                   jax.ShapeDtypeStruct((2, S5_HALVES, 8, 2 * S5_HALF_N), F32)],
        compiler_params=_cparams(("parallel", "parallel")),
        name="s5_prep",
    )(vec(lam_re), vec(lam_im), ldt, bt(b_re), bt(b_im), ct(c_re), ct(c_im))


def _s5_kernel(u_ref, bd_ref, cdt_ref, a_ref, y_ref, bu_scr, s_scr, st_scr, *, tb, bsz):
    r = pl.program_id(0)
    n = S5_HALF_N

    @pl.when(pl.program_id(2) == 0)
    def _():
        st_scr[...] = jnp.zeros_like(st_scr)

    bu_scr[...] = _dot(u_ref[...], bd_ref[0, 0])
    a_re, a_im = a_ref[0, 0, :, :n], a_ref[0, 0, :, n:]

    def step(j, st):
        s_re, s_im = st
        t = jnp.where(r == 0, j, tb - 1 - j)
        rows = pl.ds(pl.multiple_of(t * bsz, bsz), bsz)
        n_re = a_re * s_re - a_im * s_im + bu_scr[rows, :n]
        n_im = a_re * s_im + a_im * s_re + bu_scr[rows, n:]
        s_scr[rows, :n] = n_re
        s_scr[rows, n:] = n_im
        return n_re, n_im

    s_re, s_im = lax.fori_loop(0, tb, step, (st_scr[:, :n], st_scr[:, n:]))
    st_scr[:, :n] = s_re
    st_scr[:, n:] = s_im
    y_ref[0] = _dot_nt(s_scr[...], cdt_ref[0, 0])


def _s5(u_tb, bd, cdt, avec, bsz, n_ctx, tb):
    rows_total = u_tb.shape[0]
    t = rows_total // bsz
    nb, ncb = t // tb, n_ctx // tb

    def blk(r, i):
        rev = jnp.where(i < ncb, ncb - 1 - i, nb + ncb - 1 - i)
        return jnp.where(r == 0, i, rev)

    wspec = pl.BlockSpec((1, 1, S5_HALF_W, 2 * S5_HALF_N), lambda r, h, i: (r, h, 0, 0))
    return pl.pallas_call(
        functools.partial(_s5_kernel, tb=tb, bsz=bsz),
        grid=(2, S5_HALVES, nb),
        in_specs=[pl.BlockSpec((tb * bsz, S5_HALF_W), lambda r, h, i: (blk(r, i), h)),
                  wspec, wspec,
                  pl.BlockSpec((1, 1, 8, 2 * S5_HALF_N), lambda r, h, i: (r, h, 0, 0))],
        out_specs=pl.BlockSpec((1, tb * bsz, S5_HALF_W), lambda r, h, i: (r, blk(r, i), h)),
        out_shape=jax.ShapeDtypeStruct((2, rows_total, S5_WIDTH), F32),
        scratch_shapes=[pltpu.VMEM((tb * bsz, 2 * S5_HALF_N), F32),
                        pltpu.VMEM((tb * bsz, 2 * S5_HALF_N), F32),
                        pltpu.VMEM((bsz, 2 * S5_HALF_N), F32)],
        compiler_params=_cparams(("parallel", "parallel", "arbitrary")),
        name="s5_scan",
    )(u_tb, bd, cdt, avec)


def _even_finish_kernel(x_ref, ha_ref, op_ref, yb_ref, u_ref, mhw_ref, ds_ref, wg_ref, wo_ref,
                        g_ref, o_ref):
    ha = ha_ref[0]
    parts = []
    for h in range(MLSTM_HEADS):
        sl = slice(h * HEAD_DIM, (h + 1) * HEAD_DIM)
        parts.append(_rms(ha[:, sl], mhw_ref[:, sl]))
    a_out = jnp.concatenate(parts, axis=-1) * jax.nn.sigmoid(op_ref[0])
    yb = jax.nn.gelu(yb_ref[0] + yb_ref[1] + ds_ref[...] * u_ref[...])
    glu = _dot(yb, wg_ref[...])
    b_out = glu[:, :S5_WIDTH] * jax.nn.sigmoid(glu[:, S5_WIDTH:])
    y = _dot(a_out, wo_ref[:MLSTM_WIDTH, :]) + _dot(b_out, wo_ref[MLSTM_WIDTH:, :])
    o_ref[0] = x_ref[0] + g_ref[0, 0] * y


def _even_finish(xc, ha, main, yb, u, mods, layer, mh_w, d_skip, wg, wo, tm, n_ctx):
    bsz, t, d = xc.shape
    ctx_tiles = n_ctx // tm
    yb3 = yb.reshape(2, t, bsz * S5_WIDTH)
    full = lambda shape: pl.BlockSpec(shape, lambda b, i: tuple(0 for _ in shape))
    return pl.pallas_call(
        _even_finish_kernel,
        grid=(bsz, t // tm),
        in_specs=[pl.BlockSpec((1, tm, d), lambda b, i: (b, i, 0)),
                  pl.BlockSpec((1, tm, MLSTM_WIDTH), lambda b, i: (b, i, 0)),
                  pl.BlockSpec((1, tm, MLSTM_WIDTH), lambda b, i: (b, i, 3)),
                  pl.BlockSpec((2, tm, S5_WIDTH), lambda b, i: (0, i, b)),
                  pl.BlockSpec((tm, S5_WIDTH), lambda b, i: (i, b)),
                  full((1, MLSTM_WIDTH)), full((1, S5_WIDTH)),
                  full((S5_WIDTH, 2 * S5_WIDTH)), full((d, d)),
                  _mod_spec(layer, 2, ctx_tiles, bsz)],
        out_specs=pl.BlockSpec((1, tm, d), lambda b, i: (b, i, 0)),
        out_shape=jax.ShapeDtypeStruct((bsz, t, d), F32),
        compiler_params=_cparams(("parallel", "parallel")),
        name="even_finish",
    )(xc, ha, main, yb3, u, mh_w.reshape(1, -1), d_skip.reshape(1, -1), wg, wo, mods)


def _conv_kernel(x_ref, w_ref, o_ref, *, n_ctx, seq):
    j = pl.program_id(1)
    w = w_ref[...]
    wrow = lambda k: w[k:k + 1, :]
    n_rows = seq // GRID_W
    tpos = lax.broadcasted_iota(jnp.int32, (GRID_W, LANES), 0)
    head_blocks = D_MODEL // LANES
    is_qk = j < 2 * head_blocks
    post_scale = jnp.where(j < head_blocks, 1.0 / math.sqrt(HEAD_DIM), 1.0)

    def finish(y):
        y = y * jax.nn.sigmoid(y)
        nrm = y * lax.rsqrt(jnp.sum(y * y, axis=-1, keepdims=True) + EPS) * post_scale
        return jnp.where(is_qk, nrm, y)

    def shifted(r0, r1, r2, first_mask, last_mask):
        prev = jnp.where(first_mask, 0.0, pltpu.roll(r0, 1, axis=0))
        nxt = jnp.where(last_mask, 0.0, pltpu.roll(r2, GRID_W - 1, axis=0))
        return r1 + prev + nxt

    first = tpos == 0
    last = tpos == GRID_W - 1

    def lat_row(r, _):
        base = n_ctx + r * GRID_W
        cur = x_ref[0, pl.ds(pl.multiple_of(base, GRID_W), GRID_W), :]
        up_i = jnp.maximum(r - 1, 0)
        dn_i = jnp.minimum(r + 1, n_rows - 1)
        up = x_ref[0, pl.ds(pl.multiple_of(n_ctx + up_i * GRID_W, GRID_W), GRID_W), :]
        dn = x_ref[0, pl.ds(pl.multiple_of(n_ctx + dn_i * GRID_W, GRID_W), GRID_W), :]
        up = jnp.where(r > 0, up, 0.0)
        dn = jnp.where(r < n_rows - 1, dn, 0.0)
        rs = [wrow(dc) * up + wrow(3 + dc) * cur + wrow(6 + dc) * dn for dc in range(3)]
        o_ref[0, pl.ds(pl.multiple_of(base, GRID_W), GRID_W), :] = finish(
            shifted(rs[0], rs[1], rs[2], first, last)).astype(o_ref.dtype)
        return 0

    lax.fori_loop(0, n_rows, lat_row, 0)

    n_seg = n_ctx // GRID_W

    def ctx_seg(s, _):
        base = pl.multiple_of(s * GRID_W, GRID_W)
        cur = x_ref[0, pl.ds(base, GRID_W), :]
        pb = jnp.maximum(s * GRID_W - 8, 0)
        prev_blk = x_ref[0, pl.ds(pl.multiple_of(pb, 8), 8), :]
        nb = jnp.minimum(s * GRID_W + GRID_W, n_ctx - 8)
        next_blk = x_ref[0, pl.ds(pl.multiple_of(nb, 8), 8), :]
        prev_tok = jnp.where(s > 0, prev_blk[7:8, :], 0.0)
        next_tok = jnp.where(s < n_seg - 1, next_blk[0:1, :], 0.0)
        left = jnp.where(first, prev_tok, pltpu.roll(cur, 1, axis=0))
        right = jnp.where(last, next_tok, pltpu.roll(cur, GRID_W - 1, axis=0))
        y = wrow(3) * left + wrow(4) * cur + wrow(5) * right
        o_ref[0, pl.ds(base, GRID_W), :] = finish(y).astype(o_ref.dtype)
        return 0

    lax.fori_loop(0, n_seg, ctx_seg, 0)


def _gdn_conv(p1, conv_w, n_ctx):
    bsz, t, _ = p1.shape
    return pl.pallas_call(
        functools.partial(_conv_kernel, n_ctx=n_ctx, seq=t - n_ctx),
        grid=(bsz, GDN_QKV // LANES),
        in_specs=[pl.BlockSpec((1, t, LANES), lambda b, j: (b, 0, j)),
                  pl.BlockSpec((9, LANES), lambda b, j: (0, j))],
        out_specs=pl.BlockSpec((1, t, LANES), lambda b, j: (b, 0, j)),
        out_shape=jax.ShapeDtypeStruct((bsz, t, GDN_QKV), F32),
        compiler_params=_cparams(("parallel", "parallel")),
        name="gdn_conv",
    )(p1, conv_w.reshape(9, GDN_QKV))


def _unit_tri_inverse(a):
    inv = -a
    pw = a
    k = 1
    eye = None
    while True:
        k *= 2
        pw = jnp.dot(pw, pw, preferred_element_type=F32, precision=lax.Precision.HIGHEST)
        if eye is None:
            r, c = _chunk_masks()
            eye = (r == c).astype(F32)
            inv = eye + inv
        inv = inv + jnp.dot(inv, pw, preferred_element_type=F32, precision=lax.Precision.HIGHEST)
        if 2 * k >= CHUNK:
            return inv


def _gdn_kernel(par_ref, q_ref, k_ref, v_ref, g_ref, o_ref, *, n_chunks, ctx_chunks):
    hd = pl.program_id(1)
    row, col = _chunk_masks()
    eye = row == col
    o_ref[...] = jnp.zeros_like(o_ref)

    def one_dir(r, i, s_st):
        rev = r == 1
        c = _scan_chunk_index(i, rev, n_chunks, ctx_chunks)
        rows = pl.ds(pl.multiple_of(c * CHUNK, CHUNK), CHUNK)
        incl = (col >= row) if rev else (col <= row)
        strict = (col > row) if rev else (col < row)
        q = q_ref[0, rows, :]
        k = k_ref[0, rows, :]
        v = v_ref[0, rows, :]
        a_pre = g_ref[0, r * GDN_HEADS + hd, pl.ds(c, 1), :]
        b_pre = g_ref[0, (2 + r) * GDN_HEADS + hd, pl.ds(c, 1), :]
        a_log = jnp.full((1, CHUNK), par_ref[r * GDN_HEADS + hd], F32)
        dt_b = par_ref[(2 + r) * GDN_HEADS + hd]
        g_row = -jnp.exp(a_log) * jax.nn.softplus(a_pre + dt_b)
        beta_col = _to_col(jax.nn.sigmoid(b_pre), eye)
        gc_col = jnp.sum(jnp.where(incl, jnp.broadcast_to(g_row, incl.shape), 0.0),
                         axis=1, keepdims=True)
        gc_row = _to_row(gc_col, eye)
        g_last = jnp.sum(g_row, axis=1, keepdims=True)
        gamma = jnp.exp(jnp.where(incl, gc_col - gc_row, -jnp.inf))
        kb = k * beta_col
        a_mat = jnp.where(strict, _dot_nt(kb, k) * gamma, 0.0)
        t_inv = _unit_tri_inverse(a_mat)
        eg = jnp.exp(gc_col)
        rhs = jnp.concatenate([v * beta_col, kb * eg], axis=-1)
        sol = _dot(t_inv, rhs)
        u_c, w_c = sol[:, :HEAD_DIM], sol[:, HEAD_DIM:]
        attn = _dot_nt(q, k) * gamma
        v_new = u_c - _dot(w_c, s_st)
        o = _dot(q * eg, s_st) + _dot(attn, v_new)
        o_ref[0, rows, :] += o
        kd = k * jnp.exp(g_last - gc_col)
        return jnp.exp(g_last) * s_st + _dot_tn(kd, v_new)

    def step(i, carry):
        return (one_dir(0, i, carry[0]), one_dir(1, i, carry[1]))

    init = jnp.zeros((HEAD_DIM, HEAD_DIM), F32)
    lax.fori_loop(0, n_chunks, step, (init, init))


def _gdn(qkv, gates_row, par, n_ctx):
    bsz, t, _ = qkv.shape
    n_chunks = t // CHUNK
    spec = lambda off: pl.BlockSpec((1, t, HEAD_DIM), lambda b, h: (b, 0, off + h))
    return pl.pallas_call(
        functools.partial(_gdn_kernel, n_chunks=n_chunks, ctx_chunks=n_ctx // CHUNK),
        grid=(bsz, GDN_HEADS),
        in_specs=[pl.BlockSpec(memory_space=pltpu.SMEM),
                  spec(0), spec(GDN_HEADS), spec(2 * GDN_HEADS),
                  pl.BlockSpec((1, 4 * GDN_HEADS, n_chunks, CHUNK), lambda b, h: (b, 0, 0, 0))],
        out_specs=pl.BlockSpec((1, t, HEAD_DIM), lambda b, h: (b, 0, h)),
        out_shape=jax.ShapeDtypeStruct((bsz, t, D_MODEL), F32),
        compiler_params=_cparams(("parallel", "parallel")),
        name="gdn",
    )(par, qkv, qkv, qkv, gates_row)


def _odd_finish_kernel(x_ref, o_in_ref, z_ref, hw_ref, wo_ref, g_ref, o_ref):
    o = o_in_ref[0]
    parts = []
    for h in range(GDN_HEADS):
        sl = slice(h * HEAD_DIM, (h + 1) * HEAD_DIM)
        parts.append(_rms(o[:, sl], hw_ref[:, sl]))
    z = z_ref[0]
    y = jnp.concatenate(parts, axis=-1) * (z * jax.nn.sigmoid(z))
    o_ref[0] = x_ref[0] + g_ref[0, 0] * _dot(y, wo_ref[...])


def _odd_finish(xc, o, p1, mods, layer, hw, wo, tm, n_ctx):
    bsz, t, d = xc.shape
    t0 = n_ctx // tm
    return pl.pallas_call(
        _odd_finish_kernel,
        grid=(bsz, (t - n_ctx) // tm),
        in_specs=[pl.BlockSpec((1, tm, d), lambda b, i: (b, i + t0, 0)),
                  pl.BlockSpec((1, tm, d), lambda b, i: (b, i + t0, 0)),
                  pl.BlockSpec((1, tm, d), lambda b, i: (b, i + t0, 3)),
                  pl.BlockSpec((1, d), lambda b, i: (0, 0)),
                  pl.BlockSpec((d, d), lambda b, i: (0, 0)),
                  _mod_spec(layer, 2, 0, bsz)],
        out_specs=pl.BlockSpec((1, tm, d), lambda b, i: (b, i, 0)),
        out_shape=jax.ShapeDtypeStruct((bsz, t - n_ctx, d), F32),
        compiler_params=_cparams(("parallel", "parallel")),
        name="odd_finish",
    )(xc, o, p1, hw.reshape(1, d), wo, mods)


def _gates_row(gates):
    bsz, t, g = gates.shape
    return gates.reshape(bsz, t // CHUNK, CHUNK, g).transpose(0, 3, 1, 2)


def _pad_cols(w, n):
    return jnp.pad(w, ((0, 0), (0, n - w.shape[1])))


def kernel(x, c, ctx, c_ctx, ada_w, ada_b, norm1_w, norm2_w, ffn_w1, ffn_w3, ffn_w2, final_norm_w,
           ev_w_in, ev_i_bias, ev_f_bias, ev_head_norm_w, ev_lam_re, ev_lam_im, ev_log_dt,
           ev_b_re, ev_b_im, ev_c_re, ev_c_im, ev_d, ev_w_glu, ev_w_out,
           od_w_in, od_conv_w, od_a_log, od_dt_bias, od_head_norm_w, od_w_out):
    bsz, seq, d = x.shape
    n_ctx = ctx.shape[1]
    assert d == D_MODEL and seq % GRID_W == 0 and n_ctx % CHUNK == 0 and bsz % 8 == 0
    assert ada_w.shape[0] == 2
    tm = math.gcd(256, n_ctx)
    tm_ffn = math.gcd(512, seq, n_ctx)
    th = FFN_HIDDEN // 2
    tb = CHUNK

    xc = jnp.concatenate([ctx, x], axis=1)
    mod_rows = 16
    cvec = jnp.zeros((mod_rows, d), F32).at[:bsz].set(c).at[bsz].set(c_ctx)
    mods = _ada(cvec, ada_w, ada_b).reshape(2, mod_rows, 1, 6 * d)
    w1, w3, w2 = ffn_w1.astype(BF16), ffn_w3.astype(BF16), ffn_w2.astype(BF16)

    qkvo = 4 * MLSTM_WIDTH
    n_g = 4 * MLSTM_HEADS
    w_in = ev_w_in[0]
    w_even = jnp.concatenate([_pad_cols(w_in[:, :qkvo + n_g], EVEN_MAIN), w_in[:, qkvo + n_g:]],
                             axis=1).astype(BF16)
    main, u = _inproj(xc, norm1_w[0], mods, 0, w_even,
                      [(0, EVEN_MAIN), (EVEN_MAIN, EVEN_MAIN + S5_WIDTH)], [False, True], tm, n_ctx)
    g_row = _gates_row(main[:, :, qkvo:qkvo + n_g])
    bias = jnp.concatenate([ev_i_bias[0].reshape(-1), ev_f_bias[0].reshape(-1)]).astype(F32)
    ha = _mlstm(main, g_row, bias, n_ctx)
    bd, cdt, avec = _s5_prep(ev_lam_re[0], ev_lam_im[0], ev_log_dt[0], ev_b_re[0], ev_b_im[0],
                             ev_c_re[0], ev_c_im[0])
    yb = _s5(u.reshape(-1, S5_WIDTH), bd, cdt, avec, bsz, n_ctx, tb)
    xc = _even_finish(xc, ha, main, yb, u, mods, 0, ev_head_norm_w[0], ev_d[0],
                      ev_w_glu[0].astype(BF16), ev_w_out[0].astype(BF16), tm, n_ctx)
    xc = _ffn(xc, norm2_w[0], mods, 0, w1[0], w3[0], w2[0], final_norm_w, tm_ffn, th, n_ctx, 0, False)

    n_g = 4 * GDN_HEADS
    w_odd = _pad_cols(od_w_in[0], ODD_MAIN).astype(BF16)
    (p1,) = _inproj(xc, norm1_w[1], mods, 1, w_odd, [(0, ODD_MAIN)], [False], tm, n_ctx)
    qkv = _gdn_conv(p1, od_conv_w[0], n_ctx)
    g_row = _gates_row(p1[:, :, 4 * d:4 * d + n_g])
    par = jnp.concatenate([od_a_log[0].reshape(-1), od_dt_bias[0].reshape(-1)]).astype(F32)
    o = _gdn(qkv, g_row, par, n_ctx)
    xl = _odd_finish(xc, o, p1, mods, 1, od_head_norm_w[0], od_w_out[0].astype(BF16), tm, n_ctx)
    return _ffn(xl, norm2_w[1], mods, 1, w1[1], w3[1], w2[1], final_norm_w, tm_ffn, th, 0, 0, True)
```

```python
import functools
import math

import jax
import jax.numpy as jnp
from jax import lax
from jax.experimental import pallas as pl
from jax.experimental.pallas import tpu as pltpu

F32 = jnp.float32
BF16 = jnp.bfloat16

D_MODEL = 1024
CHUNK = 64
GRID_W = 64
EPS = 1e-6
LANES = 128
HEAD_DIM = 128
MLSTM_WIDTH = D_MODEL // 2
MLSTM_HEADS = MLSTM_WIDTH // HEAD_DIM
S5_WIDTH = D_MODEL - MLSTM_WIDTH
S5_GROUP = 16
S5_GROUPS = S5_WIDTH // S5_GROUP
S5_STATE = 64
S5_HALVES = 2
S5_HALF_W = S5_WIDTH // S5_HALVES
S5_HALF_N = (S5_GROUPS // S5_HALVES) * S5_STATE
GDN_HEADS = D_MODEL // HEAD_DIM
GDN_QKV = 3 * D_MODEL
FFN_HIDDEN = ((8 * D_MODEL + 3 * 256 - 1) // (3 * 256)) * 256
EVEN_MAIN = 4 * MLSTM_WIDTH + LANES
ODD_MAIN = 4 * D_MODEL + LANES
VMEM_LIMIT = 56 * 1024 * 1024

_NT = (((1,), (1,)), ((), ()))
_TN = (((0,), (0,)), ((), ()))


def _cparams(sem):
    return pltpu.CompilerParams(dimension_semantics=sem, vmem_limit_bytes=VMEM_LIMIT)


def _dot(a, b):
    return jnp.dot(a.astype(BF16), b.astype(BF16), preferred_element_type=F32)


def _dot_nt(a, b):
    return lax.dot_general(a.astype(BF16), b.astype(BF16), _NT, preferred_element_type=F32)


def _dot_tn(a, b):
    return lax.dot_general(a.astype(BF16), b.astype(BF16), _TN, preferred_element_type=F32)


def _rms(x, w):
    return x * lax.rsqrt(jnp.mean(x * x, axis=-1, keepdims=True) + EPS) * w


def _ada_kernel(s_ref, w_ref, b_ref, o_ref):
    s = s_ref[...]
    s = s * jax.nn.sigmoid(s)
    o_ref[0] = _dot(s, w_ref[0]) + b_ref[0]


def _ada(cvec, ada_w, ada_b):
    depth, d, n = ada_w.shape
    tn = 1536
    rows = cvec.shape[0]
    return pl.pallas_call(
        _ada_kernel,
        grid=(depth, n // tn),
        in_specs=[pl.BlockSpec((rows, d), lambda l, j: (0, 0)),
                  pl.BlockSpec((1, d, tn), lambda l, j: (l, 0, j)),
                  pl.BlockSpec((1, 1, tn), lambda l, j: (l, 0, j))],
        out_specs=pl.BlockSpec((1, rows, tn), lambda l, j: (l, 0, j)),
        out_shape=jax.ShapeDtypeStruct((depth, rows, n), F32),
        compiler_params=_cparams(("parallel", "parallel")),
        name="ada_mod",
    )(cvec, ada_w, ada_b.reshape(depth, 1, n))


def _mod_spec(layer, chunk, ctx_tiles, ctx_row):
    def imap(b, i, *_):
        return (layer, jnp.where(i < ctx_tiles, ctx_row, b), 0, chunk)
    return pl.BlockSpec((1, 1, 1, D_MODEL), imap)


def _inproj_kernel(x_ref, nw_ref, sh_ref, sc_ref, w_ref, *rest, splits, n_chunk):
    out_refs, h_scr = rest[:-1], rest[-1]
    x = x_ref[0]
    h = _rms(x, nw_ref[...]) * (1.0 + sc_ref[0, 0]) + sh_ref[0, 0]
    h_scr[...] = h.astype(BF16)
    for o_ref, (c0, c1) in zip(out_refs, splits):
        for n0 in range(c0, c1, n_chunk):
            n1 = min(n0 + n_chunk, c1)
            val = jnp.dot(h_scr[...], w_ref[:, n0:n1], preferred_element_type=F32)
            if len(o_ref.shape) == 3:
                o_ref[0, :, n0 - c0:n1 - c0] = val.astype(o_ref.dtype)
            else:
                o_ref[:, n0 - c0:n1 - c0] = val.astype(o_ref.dtype)


def _inproj(xc, norm_w, mods, layer, w_bf16, splits, token_major, tm, n_ctx):
    bsz, t, d = xc.shape
    n = w_bf16.shape[1]
    ctx_tiles = n_ctx // tm
    out_specs, out_shapes = [], []
    for (c0, c1), tmaj in zip(splits, token_major):
        wd = c1 - c0
        if tmaj:
            out_specs.append(pl.BlockSpec((tm, wd), lambda b, i: (i, b)))
            out_shapes.append(jax.ShapeDtypeStruct((t, bsz * wd), F32))
        else:
            out_specs.append(pl.BlockSpec((1, tm, wd), lambda b, i: (b, i, 0)))
            out_shapes.append(jax.ShapeDtypeStruct((bsz, t, wd), F32))
    return pl.pallas_call(
        functools.partial(_inproj_kernel, splits=tuple(splits), n_chunk=512),
        grid=(bsz, t // tm),
        in_specs=[pl.BlockSpec((1, tm, d), lambda b, i: (b, i, 0)),
                  pl.BlockSpec((1, d), lambda b, i: (0, 0)),
                  _mod_spec(layer, 0, ctx_tiles, bsz),
                  _mod_spec(layer, 1, ctx_tiles, bsz),
                  pl.BlockSpec((d, n), lambda b, i: (0, 0))],
        out_specs=out_specs,
        out_shape=out_shapes,
        scratch_shapes=[pltpu.VMEM((tm, d), BF16)],
        compiler_params=_cparams(("parallel", "parallel")),
        name=f"inproj_l{layer}",
    )(xc, norm_w.reshape(1, d), mods, mods, w_bf16)


def _ffn_kernel(x_ref, nw_ref, sh_ref, sc_ref, g_ref, w1_ref, w3_ref, w2_ref, fw_ref, o_ref,
                h_scr, acc_scr, *, final):
    k = pl.program_id(2)

    @pl.when(k == 0)
    def _():
        h = _rms(x_ref[0], nw_ref[...]) * (1.0 + sc_ref[0, 0]) + sh_ref[0, 0]
        h_scr[...] = h.astype(BF16)
        acc_scr[...] = jnp.zeros_like(acc_scr)

    h = h_scr[...]
    a = jnp.dot(h, w1_ref[...], preferred_element_type=F32)
    g = jnp.dot(h, w3_ref[...], preferred_element_type=F32)
    t = (a * jax.nn.sigmoid(a)) * g
    acc_scr[...] += jnp.dot(t.astype(BF16), w2_ref[...], preferred_element_type=F32)

    @pl.when(k == pl.num_programs(2) - 1)
    def _():
        y = x_ref[0] + g_ref[0, 0] * acc_scr[...]
        if final:
            y = _rms(y, fw_ref[...])
        o_ref[0] = y


def _ffn(x, norm_w, mods, layer, w1, w3, w2, final_w, tm, th, n_ctx, row0, final):
    bsz, tx, d = x.shape
    hid = w1.shape[1]
    t0 = row0 // tm
    ctx_tiles = (n_ctx - row0) // tm if n_ctx > row0 else 0
    mspec = lambda c: _mod_spec(layer, c, ctx_tiles, bsz)
    return pl.pallas_call(
        functools.partial(_ffn_kernel, final=final),
        grid=(bsz, (tx - row0) // tm, hid // th),
        in_specs=[pl.BlockSpec((1, tm, d), lambda b, i, k: (b, i + t0, 0)),
                  pl.BlockSpec((1, d), lambda b, i, k: (0, 0)),
                  mspec(3), mspec(4), mspec(5),
                  pl.BlockSpec((d, th), lambda b, i, k: (0, k)),
                  pl.BlockSpec((d, th), lambda b, i, k: (0, k)),
                  pl.BlockSpec((th, d), lambda b, i, k: (k, 0)),
                  pl.BlockSpec((1, d), lambda b, i, k: (0, 0))],
        out_specs=pl.BlockSpec((1, tm, d), lambda b, i, k: (b, i, 0)),
        out_shape=jax.ShapeDtypeStruct((bsz, tx - row0, d), F32),
        scratch_shapes=[pltpu.VMEM((tm, d), BF16), pltpu.VMEM((tm, d), F32)],
        compiler_params=_cparams(("parallel", "parallel", "arbitrary")),
        name=f"ffn_l{layer}",
    )(x, norm_w.reshape(1, d), mods, mods, mods, w1, w3, w2, final_w.reshape(1, d))


def _chunk_masks():
    row = lax.broadcasted_iota(jnp.int32, (CHUNK, CHUNK), 0)
    col = lax.broadcasted_iota(jnp.int32, (CHUNK, CHUNK), 1)
    return row, col


def _to_col(row_vec, eye):
    return jnp.sum(jnp.where(eye, jnp.broadcast_to(row_vec, eye.shape), 0.0), axis=1, keepdims=True)


def _to_row(col_vec, eye):
    return jnp.sum(jnp.where(eye, jnp.broadcast_to(col_vec, eye.shape), 0.0), axis=0, keepdims=True)


def _scan_chunk_index(i, rev, n_chunks, ctx_chunks):
    if not rev:
        return i
    return jnp.where(i < ctx_chunks, ctx_chunks - 1 - i, n_chunks + ctx_chunks - 1 - i)


def _mlstm_kernel(bias_ref, q_ref, k_ref, v_ref, g_ref, o_ref, num_scr, cl_scr, mi_scr, fc_scr, sc_scr,
                  *, n_chunks, ctx_chunks, group):
    hd = pl.program_id(1)
    row, col = _chunk_masks()
    eye = row == col
    kscale = 1.0 / math.sqrt(HEAD_DIM)
    ones_blk = jnp.ones((CHUNK, HEAD_DIM), BF16)

    def gate_terms(r, ig_raw, f_raw):
        incl = (col >= row) if r == 1 else (col <= row)
        ig_row = ig_raw + bias_ref[r * MLSTM_HEADS + hd]
        lf_row = jax.nn.log_sigmoid(f_raw + bias_ref[(2 + r) * MLSTM_HEADS + hd])
        f_col = jnp.sum(jnp.where(incl, jnp.broadcast_to(lf_row, incl.shape), 0.0),
                        axis=1, keepdims=True)
        f_row = _to_row(f_col, eye)
        f_last = jnp.sum(lf_row, axis=1, keepdims=True)
        dm = jnp.where(incl, f_col - f_row + ig_row, -jnp.inf)
        m_intra = jnp.max(dm, axis=1, keepdims=True)
        w_col = f_last - f_col + _to_col(ig_row, eye)
        m_loc = jnp.max(w_col, axis=0, keepdims=True)
        return f_col, f_last, jnp.exp(dm - m_intra), m_intra, jnp.exp(w_col - m_loc), m_loc

    def local_group(gi, _):
        chunks = [gi * group + g for g in range(group)]
        loaded = []
        for c in chunks:
            rows = pl.ds(pl.multiple_of(c * CHUNK, CHUNK), CHUNK)
            gates = [(g_ref[0, r * MLSTM_HEADS + hd, pl.ds(c, 1), :],
                      g_ref[0, (2 + r) * MLSTM_HEADS + hd, pl.ds(c, 1), :]) for r in (0, 1)]
            v1 = jnp.concatenate([v_ref[0, rows, :].astype(BF16), ones_blk], axis=-1)
            loaded.append((q_ref[0, rows, :], k_ref[0, rows, :] * kscale, v1, gates))
        chains = [(g, r) for g in range(group) for r in (0, 1)]
        terms = [gate_terms(r, *loaded[g][3][r]) for g, r in chains]
        qks = [_dot_nt(q, k) for q, k, _, _ in loaded]
        c_locs = [_dot_tn(tm[4] * loaded[g][1], loaded[g][2]) for (g, r), tm in zip(chains, terms)]
        nums = [_dot(tm[2] * qks[g], loaded[g][2]) for (g, r), tm in zip(chains, terms)]
        for (g, r), tm, c_loc, num in zip(chains, terms, c_locs, nums):
            f_col, f_last, _, m_intra, _, m_loc = tm
            c = chunks[g]
            num_scr[r, c] = num
            cl_scr[r, c] = c_loc
            mi_scr[r, c] = jnp.broadcast_to(m_intra, (CHUNK, HEAD_DIM))
            fc_scr[r, c] = jnp.broadcast_to(f_col, (CHUNK, HEAD_DIM))
            sc_scr[r, c, :8, :] = jnp.broadcast_to(f_last, (8, HEAD_DIM))
            sc_scr[r, c, 8:, :] = jnp.broadcast_to(m_loc, (8, HEAD_DIM))
        return 0

    lax.fori_loop(0, n_chunks // group, local_group, 0)
    o_ref[...] = jnp.zeros_like(o_ref)

    def step(i, carry):
        cs = [_scan_chunk_index(i, r == 1, n_chunks, ctx_chunks) for r in (0, 1)]
        rows = [pl.ds(pl.multiple_of(c * CHUNK, CHUNK), CHUNK) for c in cs]
        qcs = [_dot(q_ref[0, rw, :], st[0]) for rw, st in zip(rows, carry)]
        new = []
        for r in (0, 1):
            s_st, m_st = carry[r]
            c = cs[r]
            mi, na = mi_scr[r, c], num_scr[r, c]
            inter = fc_scr[r, c] + m_st[0:1, :]
            m_t = jnp.maximum(mi, inter)
            a_loc_t, a_inter = jnp.exp(mi - m_t), jnp.exp(inter - m_t)
            num = a_loc_t * na[:, :HEAD_DIM] + a_inter * qcs[r][:, :HEAD_DIM]
            den = a_loc_t * na[:, HEAD_DIM:] + a_inter * qcs[r][:, HEAD_DIM:]
            o_ref[0, rows[r], :] += num / jnp.maximum(jnp.abs(den), jnp.exp(-m_t))
            f_last, m_loc = sc_scr[r, c, :8, :], sc_scr[r, c, 8:, :]
            m_new = jnp.maximum(f_last + m_st, m_loc)
            a_prev = jnp.exp(f_last + m_st - m_new)[0:1, :]
            a_loc = jnp.exp(m_loc - m_new)[0:1, :]
            a_prev = jnp.concatenate([a_prev, a_prev], axis=1)
            a_loc = jnp.concatenate([a_loc, a_loc], axis=1)
            new.append((a_prev * s_st + a_loc * cl_scr[r, c], m_new))
        return tuple(new)

    init = (jnp.zeros((HEAD_DIM, 2 * HEAD_DIM), F32), jnp.zeros((8, HEAD_DIM), F32))
    lax.fori_loop(0, n_chunks, step, (init, init), unroll=2)


def _mlstm(main, gates_row, bias, n_ctx):
    bsz, t, _ = main.shape
    n_chunks = t // CHUNK
    qkv_spec = lambda off: pl.BlockSpec((1, t, HEAD_DIM), lambda b, h: (b, 0, off + h))
    return pl.pallas_call(
        functools.partial(_mlstm_kernel, n_chunks=n_chunks, ctx_chunks=n_ctx // CHUNK,
                          group=math.gcd(6, n_chunks)),
        grid=(bsz, MLSTM_HEADS),
        in_specs=[pl.BlockSpec(memory_space=pltpu.SMEM),
                  qkv_spec(0), qkv_spec(MLSTM_HEADS), qkv_spec(2 * MLSTM_HEADS),
                  pl.BlockSpec((1, 4 * MLSTM_HEADS, n_chunks, CHUNK), lambda b, h: (b, 0, 0, 0))],
        out_specs=pl.BlockSpec((1, t, HEAD_DIM), lambda b, h: (b, 0, h)),
        out_shape=jax.ShapeDtypeStruct((bsz, t, MLSTM_WIDTH), F32),
        scratch_shapes=[pltpu.VMEM((2, n_chunks, CHUNK, 2 * HEAD_DIM), F32),
                        pltpu.VMEM((2, n_chunks, HEAD_DIM, 2 * HEAD_DIM), F32),
                        pltpu.VMEM((2, n_chunks, CHUNK, HEAD_DIM), F32),
                        pltpu.VMEM((2, n_chunks, CHUNK, HEAD_DIM), F32),
                        pltpu.VMEM((2, n_chunks, 16, HEAD_DIM), F32)],
        compiler_params=_cparams(("parallel", "parallel")),
        name="mlstm",
    )(bias, main, main, main, gates_row)


def _s5_prep_kernel(lr_ref, li_ref, ldt_ref, br_ref, bi_ref, cr_ref, ci_ref, bd_ref, cdt_ref, a_ref):
    lr, li = lr_ref[0, 0], li_ref[0, 0]
    dt = jnp.exp(ldt_ref[0, 0])
    mag, ang = jnp.exp(lr * dt), li * dt
    ab_re, ab_im = mag * jnp.cos(ang), mag * jnp.sin(ang)
    nr, ni = ab_re - 1.0, ab_im
    den = lr * lr + li * li
    co_re = (nr * lr + ni * li) / den
    co_im = (ni * lr - nr * li) / den
    b_re, b_im = br_ref[0, 0], bi_ref[0, 0]
    bb_re = co_re * b_re - co_im * b_im
    bb_im = co_re * b_im + co_im * b_re
    c_re, c_im = cr_ref[0, 0], ci_ref[0, 0]
    lane_group = lax.broadcasted_iota(jnp.int32, (S5_GROUP, S5_HALF_N), 1) // S5_STATE
    n = S5_HALF_N
    for g in range(S5_GROUPS // S5_HALVES):
        sel = lane_group == g
        rows = slice(g * S5_GROUP, (g + 1) * S5_GROUP)
        bd_ref[0, 0, rows, :n] = jnp.where(sel, bb_re, 0.0).astype(bd_ref.dtype)
        bd_ref[0, 0, rows, n:] = jnp.where(sel, bb_im, 0.0).astype(bd_ref.dtype)
        cdt_ref[0, 0, rows, :n] = jnp.where(sel, c_re, 0.0).astype(cdt_ref.dtype)
        cdt_ref[0, 0, rows, n:] = jnp.where(sel, -c_im, 0.0).astype(cdt_ref.dtype)
    a_ref[0, 0, :, :n] = jnp.broadcast_to(ab_re, (8, n))
    a_ref[0, 0, :, n:] = jnp.broadcast_to(ab_im, (8, n))


def _s5_prep(lam_re, lam_im, log_dt, b_re, b_im, c_re, c_im):
    gh = S5_GROUPS // S5_HALVES
    vec = lambda a: a.reshape(2, S5_HALVES, 1, S5_HALF_N)
    ldt = vec(jnp.broadcast_to(log_dt[:, :, None], (2, S5_GROUPS, S5_STATE)))
    bt = lambda a: a.reshape(2, S5_HALVES, gh, S5_STATE, S5_GROUP).transpose(0, 1, 4, 2, 3).reshape(
        2, S5_HALVES, S5_GROUP, S5_HALF_N)
    ct = lambda a: a.reshape(2, S5_HALVES, gh, S5_GROUP, S5_STATE).transpose(0, 1, 3, 2, 4).reshape(
        2, S5_HALVES, S5_GROUP, S5_HALF_N)
    vspec = pl.BlockSpec((1, 1, 1, S5_HALF_N), lambda r, h: (r, h, 0, 0))
    mspec = pl.BlockSpec((1, 1, S5_GROUP, S5_HALF_N), lambda r, h: (r, h, 0, 0))
    ospec = pl.BlockSpec((1, 1, S5_HALF_W, 2 * S5_HALF_N), lambda r, h: (r, h, 0, 0))
    return pl.pallas_call(
        _s5_prep_kernel,
        grid=(2, S5_HALVES),
        in_specs=[vspec, vspec, vspec, mspec, mspec, mspec, mspec],
        out_specs=[ospec, ospec, pl.BlockSpec((1, 1, 8, 2 * S5_HALF_N), lambda r, h: (r, h, 0, 0))],
        out_shape=[jax.ShapeDtypeStruct((2, S5_HALVES, S5_HALF_W, 2 * S5_HALF_N), BF16),
                   jax.ShapeDtypeStruct((2, S5_HALVES, S5_HALF_W, 2 * S5_HALF_N), BF16),
                   jax.ShapeDtypeStruct((2, S5_HALVES, 8, 2 * S5_HALF_N), F32)],
        compiler_params=_cparams(("parallel", "parallel")),
        name="s5_prep",
    )(vec(lam_re), vec(lam_im), ldt, bt(b_re), bt(b_im), ct(c_re), ct(c_im))


def _s5_kernel(u_ref, bd_ref, cdt_ref, a_ref, y_ref, bu_scr, s_scr, st_scr, *, tb, bsz):
    r = pl.program_id(0)
    n = S5_HALF_N

    @pl.when(pl.program_id(2) == 0)
    def _():
        st_scr[...] = jnp.zeros_like(st_scr)

    bu_scr[...] = _dot(u_ref[...], bd_ref[0, 0])
    a_re, a_im = a_ref[0, 0, :, :n], a_ref[0, 0, :, n:]

    def step(j, st):
        s_re, s_im = st
        t = jnp.where(r == 0, j, tb - 1 - j)
        rows = pl.ds(pl.multiple_of(t * bsz, bsz), bsz)
        n_re = a_re * s_re - a_im * s_im + bu_scr[rows, :n]
        n_im = a_re * s_im + a_im * s_re + bu_scr[rows, n:]
        s_scr[rows, :n] = n_re
        s_scr[rows, n:] = n_im
        return n_re, n_im

    s_re, s_im = lax.fori_loop(0, tb, step, (st_scr[:, :n], st_scr[:, n:]))
    st_scr[:, :n] = s_re
    st_scr[:, n:] = s_im
    y_ref[0] = _dot_nt(s_scr[...], cdt_ref[0, 0])


def _s5(u_tb, bd, cdt, avec, bsz, n_ctx, tb):
    rows_total = u_tb.shape[0]
    t = rows_total // bsz
    nb, ncb = t // tb, n_ctx // tb

    def blk(r, i):
        rev = jnp.where(i < ncb, ncb - 1 - i, nb + ncb - 1 - i)
        return jnp.where(r == 0, i, rev)

    wspec = pl.BlockSpec((1, 1, S5_HALF_W, 2 * S5_HALF_N), lambda r, h, i: (r, h, 0, 0))
    return pl.pallas_call(
        functools.partial(_s5_kernel, tb=tb, bsz=bsz),
        grid=(2, S5_HALVES, nb),
        in_specs=[pl.BlockSpec((tb * bsz, S5_HALF_W), lambda r, h, i: (blk(r, i), h)),
                  wspec, wspec,
                  pl.BlockSpec((1, 1, 8, 2 * S5_HALF_N), lambda r, h, i: (r, h, 0, 0))],
        out_specs=pl.BlockSpec((1, tb * bsz, S5_HALF_W), lambda r, h, i: (r, blk(r, i), h)),
        out_shape=jax.ShapeDtypeStruct((2, rows_total, S5_WIDTH), F32),
        scratch_shapes=[pltpu.VMEM((tb * bsz, 2 * S5_HALF_N), F32),
                        pltpu.VMEM((tb * bsz, 2 * S5_HALF_N), F32),
                        pltpu.VMEM((bsz, 2 * S5_HALF_N), F32)],
        compiler_params=_cparams(("parallel", "parallel", "arbitrary")),
        name="s5_scan",
    )(u_tb, bd, cdt, avec)


def _even_finish_kernel(x_ref, ha_ref, op_ref, yb_ref, u_ref, mhw_ref, ds_ref, wg_ref, wo_ref,
                        g_ref, o_ref):
    ha = ha_ref[0]
    parts = []
    for h in range(MLSTM_HEADS):
        sl = slice(h * HEAD_DIM, (h + 1) * HEAD_DIM)
        parts.append(_rms(ha[:, sl], mhw_ref[:, sl]))
    a_out = jnp.concatenate(parts, axis=-1) * jax.nn.sigmoid(op_ref[0])
    yb = jax.nn.gelu(yb_ref[0] + yb_ref[1] + ds_ref[...] * u_ref[...])
    glu = _dot(yb, wg_ref[...])
    b_out = glu[:, :S5_WIDTH] * jax.nn.sigmoid(glu[:, S5_WIDTH:])
    y = _dot(a_out, wo_ref[:MLSTM_WIDTH, :]) + _dot(b_out, wo_ref[MLSTM_WIDTH:, :])
    o_ref[0] = x_ref[0] + g_ref[0, 0] * y


def _even_finish(xc, ha, main, yb, u, mods, layer, mh_w, d_skip, wg, wo, tm, n_ctx):
    bsz, t, d = xc.shape
    ctx_tiles = n_ctx // tm
    yb3 = yb.reshape(2, t, bsz * S5_WIDTH)
    full = lambda shape: pl.BlockSpec(shape, lambda b, i: tuple(0 for _ in shape))
    return pl.pallas_call(
        _even_finish_kernel,
        grid=(bsz, t // tm),
        in_specs=[pl.BlockSpec((1, tm, d), lambda b, i: (b, i, 0)),
                  pl.BlockSpec((1, tm, MLSTM_WIDTH), lambda b, i: (b, i, 0)),
                  pl.BlockSpec((1, tm, MLSTM_WIDTH), lambda b, i: (b, i, 3)),
                  pl.BlockSpec((2, tm, S5_WIDTH), lambda b, i: (0, i, b)),
                  pl.BlockSpec((tm, S5_WIDTH), lambda b, i: (i, b)),
                  full((1, MLSTM_WIDTH)), full((1, S5_WIDTH)),
                  full((S5_WIDTH, 2 * S5_WIDTH)), full((d, d)),
                  _mod_spec(layer, 2, ctx_tiles, bsz)],
        out_specs=pl.BlockSpec((1, tm, d), lambda b, i: (b, i, 0)),
        out_shape=jax.ShapeDtypeStruct((bsz, t, d), F32),
        compiler_params=_cparams(("parallel", "parallel")),
        name="even_finish",
    )(xc, ha, main, yb3, u, mh_w.reshape(1, -1), d_skip.reshape(1, -1), wg, wo, mods)


def _conv_kernel(x_ref, w_ref, o_ref, *, n_ctx, seq):
    j = pl.program_id(1)
    w = w_ref[...]
    wrow = lambda k: w[k:k + 1, :]
    n_rows = seq // GRID_W
    tpos = lax.broadcasted_iota(jnp.int32, (GRID_W, LANES), 0)
    head_blocks = D_MODEL // LANES
    is_qk = j < 2 * head_blocks
    post_scale = jnp.where(j < head_blocks, 1.0 / math.sqrt(HEAD_DIM), 1.0)

    def finish(y):
        y = y * jax.nn.sigmoid(y)
        nrm = y * lax.rsqrt(jnp.sum(y * y, axis=-1, keepdims=True) + EPS) * post_scale
        return jnp.where(is_qk, nrm, y)

    def shifted(r0, r1, r2, first_mask, last_mask):
        prev = jnp.where(first_mask, 0.0, pltpu.roll(r0, 1, axis=0))
        nxt = jnp.where(last_mask, 0.0, pltpu.roll(r2, GRID_W - 1, axis=0))
        return r1 + prev + nxt

    first = tpos == 0
    last = tpos == GRID_W - 1

    def lat_row(r, _):
        base = n_ctx + r * GRID_W
        cur = x_ref[0, pl.ds(pl.multiple_of(base, GRID_W), GRID_W), :]
        up_i = jnp.maximum(r - 1, 0)
        dn_i = jnp.minimum(r + 1, n_rows - 1)
        up = x_ref[0, pl.ds(pl.multiple_of(n_ctx + up_i * GRID_W, GRID_W), GRID_W), :]
        dn = x_ref[0, pl.ds(pl.multiple_of(n_ctx + dn_i * GRID_W, GRID_W), GRID_W), :]
        up = jnp.where(r > 0, up, 0.0)
        dn = jnp.where(r < n_rows - 1, dn, 0.0)
        rs = [wrow(dc) * up + wrow(3 + dc) * cur + wrow(6 + dc) * dn for dc in range(3)]
        o_ref[0, pl.ds(pl.multiple_of(base, GRID_W), GRID_W), :] = finish(
            shifted(rs[0], rs[1], rs[2], first, last)).astype(o_ref.dtype)
        return 0

    lax.fori_loop(0, n_rows, lat_row, 0)

    n_seg = n_ctx // GRID_W

    def ctx_seg(s, _):
        base = pl.multiple_of(s * GRID_W, GRID_W)
        cur = x_ref[0, pl.ds(base, GRID_W), :]
        pb = jnp.maximum(s * GRID_W - 8, 0)
        prev_blk = x_ref[0, pl.ds(pl.multiple_of(pb, 8), 8), :]
        nb = jnp.minimum(s * GRID_W + GRID_W, n_ctx - 8)
        next_blk = x_ref[0, pl.ds(pl.multiple_of(nb, 8), 8), :]
        prev_tok = jnp.where(s > 0, prev_blk[7:8, :], 0.0)
        next_tok = jnp.where(s < n_seg - 1, next_blk[0:1, :], 0.0)
        left = jnp.where(first, prev_tok, pltpu.roll(cur, 1, axis=0))
        right = jnp.where(last, next_tok, pltpu.roll(cur, GRID_W - 1, axis=0))
        y = wrow(3) * left + wrow(4) * cur + wrow(5) * right
        o_ref[0, pl.ds(base, GRID_W), :] = finish(y).astype(o_ref.dtype)
        return 0

    lax.fori_loop(0, n_seg, ctx_seg, 0)


def _gdn_conv(p1, conv_w, n_ctx):
    bsz, t, _ = p1.shape
    return pl.pallas_call(
        functools.partial(_conv_kernel, n_ctx=n_ctx, seq=t - n_ctx),
        grid=(bsz, GDN_QKV // LANES),
        in_specs=[pl.BlockSpec((1, t, LANES), lambda b, j: (b, 0, j)),
                  pl.BlockSpec((9, LANES), lambda b, j: (0, j))],
        out_specs=pl.BlockSpec((1, t, LANES), lambda b, j: (b, 0, j)),
        out_shape=jax.ShapeDtypeStruct((bsz, t, GDN_QKV), F32),
        compiler_params=_cparams(("parallel", "parallel")),
        name="gdn_conv",
    )(p1, conv_w.reshape(9, GDN_QKV))


def _unit_tri_inverses(mats, eye_f):
    ps = [eye_f - a for a in mats]
    pws = [_dot(a, a) for a in mats]
    k = 2
    while 2 * k < CHUNK:
        res = [_dot(jnp.concatenate([p, pw], axis=0), pw) for p, pw in zip(ps, pws)]
        ps = [p + r[:CHUNK] for p, r in zip(ps, res)]
        pws = [r[CHUNK:] for r in res]
        k *= 2
    return [p + _dot(p, pw) for p, pw in zip(ps, pws)]


def _gdn_kernel(par_ref, q_ref, k_ref, v_ref, g_ref, o_ref, qk_scr, ol_scr, nl_scr, gl_scr, *,
                n_chunks, ctx_chunks, group):
    hd = pl.program_id(1)
    row, col = _chunk_masks()
    eye = row == col
    eye_f = eye.astype(F32)

    def load(c):
        rows = pl.ds(pl.multiple_of(c * CHUNK, CHUNK), CHUNK)
        gates = [(g_ref[0, r * GDN_HEADS + hd, pl.ds(c, 1), :],
                  g_ref[0, (2 + r) * GDN_HEADS + hd, pl.ds(c, 1), :]) for r in (0, 1)]
        return q_ref[0, rows, :], k_ref[0, rows, :], v_ref[0, rows, :], gates

    def gate_terms(r, a_pre, b_pre):
        incl = (col >= row) if r == 1 else (col <= row)
        a_log = jnp.full((1, CHUNK), par_ref[r * GDN_HEADS + hd], F32)
        dt_b = par_ref[(2 + r) * GDN_HEADS + hd]
        g_row = -jnp.exp(a_log) * jax.nn.softplus(a_pre + dt_b)
        beta_col = _to_col(jax.nn.sigmoid(b_pre), eye)
        gc_col = jnp.sum(jnp.where(incl, jnp.broadcast_to(g_row, incl.shape), 0.0),
                         axis=1, keepdims=True)
        gc_row = _to_row(gc_col, eye)
        g_last = jnp.sum(g_row, axis=1, keepdims=True)
        gamma = jnp.exp(jnp.where(incl, gc_col - gc_row, -jnp.inf))
        return beta_col, gc_col, g_last, gamma

    def local_group(gi, _):
        chunks = [gi * group + g for g in range(group)]
        loaded = [load(c) for c in chunks]
        kqs = [_dot_nt(jnp.concatenate([k, q], axis=0), k) for q, k, _, _ in loaded]
        chains = [(g, r) for g in range(group) for r in (0, 1)]
        terms = [gate_terms(r, *loaded[g][3][r]) for g, r in chains]
        a_mats = []
        for (g, r), (beta_col, _, _, gamma) in zip(chains, terms):
            strict = (col > row) if r == 1 else (col < row)
            a_mats.append(jnp.where(strict, beta_col * kqs[g][:CHUNK] * gamma, 0.0))
        t_invs = _unit_tri_inverses(a_mats, eye_f)
        sols = []
        for (g, r), (beta_col, gc_col, _, _), t_inv in zip(chains, terms, t_invs):
            _, k, v, _ = loaded[g]
            rhs = jnp.concatenate([v * beta_col, k * (beta_col * jnp.exp(gc_col))], axis=-1)
            sols.append(_dot(t_inv, rhs))
        auws = [_dot(kqs[g][CHUNK:] * tm[3], sol) for (g, r), tm, sol in zip(chains, terms, sols)]
        kuws = [_dot_tn(loaded[g][1] * jnp.exp(tm[2] - tm[1]), sol)
                for (g, r), tm, sol in zip(chains, terms, sols)]
        for (g, r), (_, gc_col, g_last, _), auw, kuw in zip(chains, terms, auws, kuws):
            c = chunks[g]
            q = loaded[g][0]
            qk_scr[r, c, :CHUNK, :] = (q * jnp.exp(gc_col) - auw[:, HEAD_DIM:]).astype(qk_scr.dtype)
            qk_scr[r, c, CHUNK:, :] = kuw[:, HEAD_DIM:].astype(qk_scr.dtype)
            ol_scr[r, c] = auw[:, :HEAD_DIM]
            nl_scr[r, c] = kuw[:, :HEAD_DIM]
            gl_scr[r, c] = jnp.broadcast_to(jnp.exp(g_last), (8, HEAD_DIM))
        return 0

    lax.fori_loop(0, n_chunks // group, local_group, 0)
    o_ref[...] = jnp.zeros_like(o_ref)

    def step(i, carry):
        new = []
        for r in (0, 1):
            c = _scan_chunk_index(i, r == 1, n_chunks, ctx_chunks)
            rows = pl.ds(pl.multiple_of(c * CHUNK, CHUNK), CHUNK)
            s = carry[r]
            res = jnp.dot(qk_scr[r, c], s.astype(BF16), preferred_element_type=F32)
            o_ref[0, rows, :] += ol_scr[r, c] + res[:CHUNK]
            new.append(gl_scr[r, c][0:1, :] * s - res[CHUNK:] + nl_scr[r, c])
        return tuple(new)

    init = jnp.zeros((HEAD_DIM, HEAD_DIM), F32)
    lax.fori_loop(0, n_chunks, step, (init, init), unroll=2)


def _gdn(qkv, gates_row, par, n_ctx):
    bsz, t, _ = qkv.shape
    n_chunks = t // CHUNK
    group = math.gcd(9, n_chunks)
    spec = lambda off: pl.BlockSpec((1, t, HEAD_DIM), lambda b, h: (b, 0, off + h))
    return pl.pallas_call(
        functools.partial(_gdn_kernel, n_chunks=n_chunks, ctx_chunks=n_ctx // CHUNK, group=group),
        grid=(bsz, GDN_HEADS),
        in_specs=[pl.BlockSpec(memory_space=pltpu.SMEM),
                  spec(0), spec(GDN_HEADS), spec(2 * GDN_HEADS),
                  pl.BlockSpec((1, 4 * GDN_HEADS, n_chunks, CHUNK), lambda b, h: (b, 0, 0, 0))],
        out_specs=pl.BlockSpec((1, t, HEAD_DIM), lambda b, h: (b, 0, h)),
        out_shape=jax.ShapeDtypeStruct((bsz, t, D_MODEL), F32),
        scratch_shapes=[pltpu.VMEM((2, n_chunks, CHUNK + HEAD_DIM, HEAD_DIM), BF16),
                        pltpu.VMEM((2, n_chunks, CHUNK, HEAD_DIM), F32),
                        pltpu.VMEM((2, n_chunks, HEAD_DIM, HEAD_DIM), F32),
                        pltpu.VMEM((2, n_chunks, 8, HEAD_DIM), F32)],
        compiler_params=_cparams(("parallel", "parallel")),
        name="gdn",
    )(par, qkv, qkv, qkv, gates_row)


def _odd_finish_kernel(x_ref, o_in_ref, z_ref, hw_ref, wo_ref, g_ref, o_ref):
    o = o_in_ref[0]
    parts = []
    for h in range(GDN_HEADS):
        sl = slice(h * HEAD_DIM, (h + 1) * HEAD_DIM)
        parts.append(_rms(o[:, sl], hw_ref[:, sl]))
    z = z_ref[0]
    y = jnp.concatenate(parts, axis=-1) * (z * jax.nn.sigmoid(z))
    o_ref[0] = x_ref[0] + g_ref[0, 0] * _dot(y, wo_ref[...])


def _odd_finish(xc, o, p1, mods, layer, hw, wo, tm, n_ctx):
    bsz, t, d = xc.shape
    t0 = n_ctx // tm
    return pl.pallas_call(
        _odd_finish_kernel,
        grid=(bsz, (t - n_ctx) // tm),
        in_specs=[pl.BlockSpec((1, tm, d), lambda b, i: (b, i + t0, 0)),
                  pl.BlockSpec((1, tm, d), lambda b, i: (b, i + t0, 0)),
                  pl.BlockSpec((1, tm, d), lambda b, i: (b, i + t0, 3)),
                  pl.BlockSpec((1, d), lambda b, i: (0, 0)),
                  pl.BlockSpec((d, d), lambda b, i: (0, 0)),
                  _mod_spec(layer, 2, 0, bsz)],
        out_specs=pl.BlockSpec((1, tm, d), lambda b, i: (b, i, 0)),
        out_shape=jax.ShapeDtypeStruct((bsz, t - n_ctx, d), F32),
        compiler_params=_cparams(("parallel", "parallel")),
        name="odd_finish",
    )(xc, o, p1, hw.reshape(1, d), wo, mods)


def _gates_row(gates):
    bsz, t, g = gates.shape
    return gates.reshape(bsz, t // CHUNK, CHUNK, g).transpose(0, 3, 1, 2)


def _pad_cols(w, n):
    return jnp.pad(w, ((0, 0), (0, n - w.shape[1])))


def kernel(x, c, ctx, c_ctx, ada_w, ada_b, norm1_w, norm2_w, ffn_w1, ffn_w3, ffn_w2, final_norm_w,
           ev_w_in, ev_i_bias, ev_f_bias, ev_head_norm_w, ev_lam_re, ev_lam_im, ev_log_dt,
           ev_b_re, ev_b_im, ev_c_re, ev_c_im, ev_d, ev_w_glu, ev_w_out,
           od_w_in, od_conv_w, od_a_log, od_dt_bias, od_head_norm_w, od_w_out):
    bsz, seq, d = x.shape
    n_ctx = ctx.shape[1]
    assert d == D_MODEL and seq % GRID_W == 0 and n_ctx % CHUNK == 0 and bsz % 8 == 0
    assert ada_w.shape[0] == 2
    tm = math.gcd(256, n_ctx)
    tm_ffn = math.gcd(512, seq, n_ctx)
    th = FFN_HIDDEN // 2
    tb = CHUNK

    xc = jnp.concatenate([ctx, x], axis=1)
    mod_rows = 16
    cvec = jnp.zeros((mod_rows, d), F32).at[:bsz].set(c).at[bsz].set(c_ctx)
    mods = _ada(cvec, ada_w, ada_b).reshape(2, mod_rows, 1, 6 * d)
    w1, w3, w2 = ffn_w1.astype(BF16), ffn_w3.astype(BF16), ffn_w2.astype(BF16)

    qkvo = 4 * MLSTM_WIDTH
    n_g = 4 * MLSTM_HEADS
    w_in = ev_w_in[0]
    w_even = jnp.concatenate([_pad_cols(w_in[:, :qkvo + n_g], EVEN_MAIN), w_in[:, qkvo + n_g:]],
                             axis=1).astype(BF16)
    main, u = _inproj(xc, norm1_w[0], mods, 0, w_even,
                      [(0, EVEN_MAIN), (EVEN_MAIN, EVEN_MAIN + S5_WIDTH)], [False, True], tm, n_ctx)
    g_row = _gates_row(main[:, :, qkvo:qkvo + n_g])
    bias = jnp.concatenate([ev_i_bias[0].reshape(-1), ev_f_bias[0].reshape(-1)]).astype(F32)
    ha = _mlstm(main, g_row, bias, n_ctx)
    bd, cdt, avec = _s5_prep(ev_lam_re[0], ev_lam_im[0], ev_log_dt[0], ev_b_re[0], ev_b_im[0],
                             ev_c_re[0], ev_c_im[0])
    yb = _s5(u.reshape(-1, S5_WIDTH), bd, cdt, avec, bsz, n_ctx, tb)
    xc = _even_finish(xc, ha, main, yb, u, mods, 0, ev_head_norm_w[0], ev_d[0],
                      ev_w_glu[0].astype(BF16), ev_w_out[0].astype(BF16), tm, n_ctx)
    xc = _ffn(xc, norm2_w[0], mods, 0, w1[0], w3[0], w2[0], final_norm_w, tm_ffn, th, n_ctx, 0, False)

    n_g = 4 * GDN_HEADS
    w_odd = _pad_cols(od_w_in[0], ODD_MAIN).astype(BF16)
    (p1,) = _inproj(xc, norm1_w[1], mods, 1, w_odd, [(0, ODD_MAIN)], [False], tm, n_ctx)
    qkv = _gdn_conv(p1, od_conv_w[0], n_ctx)
    g_row = _gates_row(p1[:, :, 4 * d:4 * d + n_g])
    par = jnp.concatenate([od_a_log[0].reshape(-1), od_dt_bias[0].reshape(-1)]).astype(F32)
    o = _gdn(qkv, g_row, par, n_ctx)
    xl = _odd_finish(xc, o, p1, mods, 1, od_head_norm_w[0], od_w_out[0].astype(BF16), tm, n_ctx)
    return _ffn(xl, norm2_w[1], mods, 1, w1[1], w3[1], w2[1], final_norm_w, tm_ffn, th, 0, 0, True)
```

```python
import functools
import math

import jax
import jax.numpy as jnp
from jax import lax
from jax.experimental import pallas as pl
from jax.experimental.pallas import tpu as pltpu

F32 = jnp.float32
BF16 = jnp.bfloat16

D_MODEL = 1024
CHUNK = 64
GRID_W = 64
EPS = 1e-6
LANES = 128
HEAD_DIM = 128
MLSTM_WIDTH = D_MODEL // 2
MLSTM_HEADS = MLSTM_WIDTH // HEAD_DIM
S5_WIDTH = D_MODEL - MLSTM_WIDTH
S5_GROUP = 16
S5_GROUPS = S5_WIDTH // S5_GROUP
S5_STATE = 64
S5_HALVES = 2
S5_HALF_W = S5_WIDTH // S5_HALVES
S5_HALF_N = (S5_GROUPS // S5_HALVES) * S5_STATE
GDN_HEADS = D_MODEL // HEAD_DIM
GDN_QKV = 3 * D_MODEL
FFN_HIDDEN = ((8 * D_MODEL + 3 * 256 - 1) // (3 * 256)) * 256
VMEM_LIMIT = 56 * 1024 * 1024

_NT = (((1,), (1,)), ((), ()))
_TN = (((0,), (0,)), ((), ()))


def _cparams(sem):
    return pltpu.CompilerParams(dimension_semantics=sem, vmem_limit_bytes=VMEM_LIMIT)


def _dot(a, b):
    return jnp.dot(a.astype(BF16), b.astype(BF16), preferred_element_type=F32)


def _dot_nt(a, b):
    return lax.dot_general(a.astype(BF16), b.astype(BF16), _NT, preferred_element_type=F32)


def _dot_tn(a, b):
    return lax.dot_general(a.astype(BF16), b.astype(BF16), _TN, preferred_element_type=F32)


def _rms(x, w):
    return x * lax.rsqrt(jnp.mean(x * x, axis=-1, keepdims=True) + EPS) * w


def _ada_kernel(s_ref, w_ref, b_ref, o_ref):
    s = s_ref[...]
    s = s * jax.nn.sigmoid(s)
    o_ref[0] = _dot(s, w_ref[0]) + b_ref[0]


def _ada(cvec, ada_w, ada_b):
    depth, d, n = ada_w.shape
    tn = 1536
    rows = cvec.shape[0]
    return pl.pallas_call(
        _ada_kernel,
        grid=(depth, n // tn),
        in_specs=[pl.BlockSpec((rows, d), lambda l, j: (0, 0)),
                  pl.BlockSpec((1, d, tn), lambda l, j: (l, 0, j)),
                  pl.BlockSpec((1, 1, tn), lambda l, j: (l, 0, j))],
        out_specs=pl.BlockSpec((1, rows, tn), lambda l, j: (l, 0, j)),
        out_shape=jax.ShapeDtypeStruct((depth, rows, n), F32),
        compiler_params=_cparams(("parallel", "parallel")),
        name="ada_mod",
    )(cvec, ada_w, ada_b.reshape(depth, 1, n))


def _ctx_mod_spec(layer, chunk, ctx_row):
    return pl.BlockSpec((1, 1, 1, D_MODEL), lambda b, i: (layer, ctx_row, 0, chunk))


def _lat_mod_spec(layer, chunk):
    return pl.BlockSpec((1, 1, 1, D_MODEL), lambda b, i: (layer, b, 0, chunk))


def _row_groups(n_top, tm, ctx_refs, lat_refs):
    lat = [r[0, 0] for r in lat_refs]
    if n_top == 0:
        return [(slice(0, tm), lat)]
    first = pl.program_id(1) == 0
    top = [jnp.where(first, c[0, 0], l) for c, l in zip(ctx_refs, lat)]
    return [(slice(0, n_top), top), (slice(n_top, tm), lat)]


def _inproj_kernel(x_ref, nw_ref, shc_ref, scc_ref, shl_ref, scl_ref, w_ref, *rest, outs, n_top,
                   n_chunk):
    out_refs, h_scr, g_scr = rest[:-2], rest[-2], rest[-1]
    tm = h_scr.shape[0]
    for rows, (sh, sc) in _row_groups(n_top, tm, (shc_ref, scc_ref), (shl_ref, scl_ref)):
        h_scr[rows, :] = (_rms(x_ref[0, rows, :], nw_ref[...]) * (1.0 + sc) + sh).astype(BF16)
    for o_ref, (c0, c1, kind) in zip(out_refs, outs):
        for n0 in range(c0, c1, n_chunk):
            n1 = min(n0 + n_chunk, c1)
            val = jnp.dot(h_scr[...], w_ref[:, n0:n1], preferred_element_type=F32)
            if kind == "row":
                o_ref[0, :, n0 - c0:n1 - c0] = val.astype(o_ref.dtype)
            elif kind == "tok":
                o_ref[:, n0 - c0:n1 - c0] = val.astype(o_ref.dtype)
            else:
                g_scr[...] = val
                o_ref[0] = g_scr[...].T[:o_ref.shape[1], :].astype(o_ref.dtype)


def _inproj(xc, norm_w, mods, layer, w_bf16, outs, tm, n_ctx):
    bsz, t, d = xc.shape
    n = w_bf16.shape[1]
    assert t % tm == 0 and (n_ctx == 0 or n_ctx <= tm)
    out_specs, out_shapes = [], []
    for c0, c1, kind, dtype, wd in outs:
        if kind == "row":
            out_specs.append(pl.BlockSpec((1, tm, wd), lambda b, i: (b, i, 0)))
            out_shapes.append(jax.ShapeDtypeStruct((bsz, t, wd), dtype))
        elif kind == "tok":
            out_specs.append(pl.BlockSpec((tm, wd), lambda b, i: (i, b)))
            out_shapes.append(jax.ShapeDtypeStruct((t, bsz * wd), dtype))
        else:
            assert c1 - c0 == LANES and tm % LANES == 0
            out_specs.append(pl.BlockSpec((1, wd, tm), lambda b, i: (b, 0, i)))
            out_shapes.append(jax.ShapeDtypeStruct((bsz, wd, t), dtype))
    ctx_spec = lambda c: _ctx_mod_spec(layer, c, bsz)
    lat_spec = lambda c: _lat_mod_spec(layer, c)
    const = lambda shape: pl.BlockSpec(shape, lambda b, i: (0, 0), pipeline_mode=pl.Buffered(1))
    return pl.pallas_call(
        functools.partial(_inproj_kernel, outs=tuple(o[:3] for o in outs), n_top=n_ctx, n_chunk=512),
        grid=(bsz, t // tm),
        in_specs=[pl.BlockSpec((1, tm, d), lambda b, i: (b, i, 0)),
                  const((1, d)), ctx_spec(0), ctx_spec(1), lat_spec(0), lat_spec(1), const((d, n))],
        out_specs=out_specs,
        out_shape=out_shapes,
        scratch_shapes=[pltpu.VMEM((tm, d), BF16), pltpu.VMEM((tm, LANES), F32)],
        compiler_params=_cparams(("parallel", "parallel")),
        name=f"inproj_l{layer}",
    )(xc, norm_w.reshape(1, d), mods, mods, mods, mods, w_bf16)


def _ffn_kernel(x_ref, nw_ref, shc_ref, scc_ref, gc_ref, shl_ref, scl_ref, gl_ref,
                w1_ref, w3_ref, w2_ref, fw_ref, o_ref, h_scr, *, final, n_top, th):
    tm, hid = h_scr.shape[0], w1_ref.shape[1]
    groups = _row_groups(n_top, tm, (shc_ref, scc_ref, gc_ref), (shl_ref, scl_ref, gl_ref))
    for rows, (sh, sc, _) in groups:
        h_scr[rows, :] = (_rms(x_ref[0, rows, :], nw_ref[...]) * (1.0 + sc) + sh).astype(BF16)
    acc = None
    for c0 in range(0, hid, th):
        h = h_scr[...]
        a = jnp.dot(h, w1_ref[:, c0:c0 + th], preferred_element_type=F32)
        g = jnp.dot(h, w3_ref[:, c0:c0 + th], preferred_element_type=F32)
        t = ((a * jax.nn.sigmoid(a)) * g).astype(BF16)
        part = jnp.dot(t, w2_ref[c0:c0 + th, :], preferred_element_type=F32)
        acc = part if acc is None else acc + part
    for rows, (_, _, gate) in groups:
        y = x_ref[0, rows, :] + gate * acc[rows, :]
        if final:
            y = _rms(y, fw_ref[...])
        o_ref[0, rows, :] = y


def _ffn(x, norm_w, mods, layer, w1, w3, w2, final_w, tm, th, n_ctx, final):
    bsz, tx, d = x.shape
    hid = w1.shape[1]
    assert n_ctx <= tm and tx % tm == 0 and hid % th == 0
    ctx_spec = lambda c: _ctx_mod_spec(layer, c, bsz)
    lat_spec = lambda c: _lat_mod_spec(layer, c)
    const = lambda shape: pl.BlockSpec(shape, lambda b, i: (0, 0), pipeline_mode=pl.Buffered(1))
    return pl.pallas_call(
        functools.partial(_ffn_kernel, final=final, n_top=n_ctx, th=th),
        grid=(bsz, tx // tm),
        in_specs=[pl.BlockSpec((1, tm, d), lambda b, i: (b, i, 0)),
                  const((1, d)),
                  ctx_spec(3), ctx_spec(4), ctx_spec(5), lat_spec(3), lat_spec(4), lat_spec(5),
                  const((d, hid)), const((d, hid)), const((hid, d)), const((1, d))],
        out_specs=pl.BlockSpec((1, tm, d), lambda b, i: (b, i, 0)),
        out_shape=jax.ShapeDtypeStruct((bsz, tx, d), F32),
        scratch_shapes=[pltpu.VMEM((tm, d), BF16)],
        compiler_params=_cparams(("parallel", "parallel")),
        name=f"ffn_l{layer}",
    )(x, norm_w.reshape(1, d), mods, mods, mods, mods, mods, mods, w1, w3, w2, final_w.reshape(1, d))


def _chunk_masks():
    row = lax.broadcasted_iota(jnp.int32, (CHUNK, CHUNK), 0)
    col = lax.broadcasted_iota(jnp.int32, (CHUNK, CHUNK), 1)
    return row, col


def _to_col(row_vec, eye):
    return jnp.sum(jnp.where(eye, jnp.broadcast_to(row_vec, eye.shape), 0.0), axis=1, keepdims=True)


def _to_row(col_vec, eye):
    return jnp.sum(jnp.where(eye, jnp.broadcast_to(col_vec, eye.shape), 0.0), axis=0, keepdims=True)


def _scan_chunk_index(i, rev, n_chunks, ctx_chunks):
    if not rev:
        return i
    return jnp.where(i < ctx_chunks, ctx_chunks - 1 - i, n_chunks + ctx_chunks - 1 - i)


def _mlstm_kernel(bias_ref, q_ref, k_ref, v_ref, g_ref, o_ref, num_scr, cl_scr, mi_scr, fc_scr, sc_scr,
                  *, n_chunks, ctx_chunks, group):
    hd = pl.program_id(1)
    row, col = _chunk_masks()
    eye = row == col
    kscale = 1.0 / math.sqrt(HEAD_DIM)
    ones_blk = jnp.ones((CHUNK, HEAD_DIM), BF16)

    def gate_terms(r, ig_raw, f_raw):
        incl = (col >= row) if r == 1 else (col <= row)
        ig_row = ig_raw + bias_ref[r * MLSTM_HEADS + hd]
        lf_row = jax.nn.log_sigmoid(f_raw + bias_ref[(2 + r) * MLSTM_HEADS + hd])
        f_col = jnp.sum(jnp.where(incl, jnp.broadcast_to(lf_row, incl.shape), 0.0),
                        axis=1, keepdims=True)
        f_row = _to_row(f_col, eye)
        f_last = jnp.sum(lf_row, axis=1, keepdims=True)
        dm = jnp.where(incl, f_col - f_row + ig_row, -jnp.inf)
        m_intra = jnp.max(dm, axis=1, keepdims=True)
        w_col = f_last - f_col + _to_col(ig_row, eye)
        m_loc = jnp.max(w_col, axis=0, keepdims=True)
        return f_col, f_last, jnp.exp(dm - m_intra), m_intra, jnp.exp(w_col - m_loc), m_loc

    def local_group(gi, _):
        chunks = [gi * group + g for g in range(group)]
        loaded = []
        for c in chunks:
            rows = pl.ds(pl.multiple_of(c * CHUNK, CHUNK), CHUNK)
            gates = [(g_ref[0, r * MLSTM_HEADS + hd, pl.ds(c, 1), :],
                      g_ref[0, (2 + r) * MLSTM_HEADS + hd, pl.ds(c, 1), :]) for r in (0, 1)]
            v1 = jnp.concatenate([v_ref[0, rows, :].astype(BF16), ones_blk], axis=-1)
            loaded.append((q_ref[0, rows, :], k_ref[0, rows, :].astype(F32) * kscale, v1, gates))
        chains = [(g, r) for g in range(group) for r in (0, 1)]
        terms = [gate_terms(r, *loaded[g][3][r]) for g, r in chains]
        qks = [_dot_nt(q, k) for q, k, _, _ in loaded]
        c_locs = [_dot_tn(tm[4] * loaded[g][1], loaded[g][2]) for (g, r), tm in zip(chains, terms)]
        nums = [_dot(tm[2] * qks[g], loaded[g][2]) for (g, r), tm in zip(chains, terms)]
        for (g, r), tm, c_loc, num in zip(chains, terms, c_locs, nums):
            f_col, f_last, _, m_intra, _, m_loc = tm
            c = chunks[g]
            num_scr[r, c] = num
            cl_scr[r, c] = c_loc
            mi_scr[r, c] = jnp.broadcast_to(m_intra, (CHUNK, HEAD_DIM))
            fc_scr[r, c] = jnp.broadcast_to(f_col, (CHUNK, HEAD_DIM))
            sc_scr[r, c, :8, :] = jnp.broadcast_to(f_last, (8, HEAD_DIM))
            sc_scr[r, c, 8:, :] = jnp.broadcast_to(m_loc, (8, HEAD_DIM))
        return 0

    lax.fori_loop(0, n_chunks // group, local_group, 0)
    o_ref[...] = jnp.zeros_like(o_ref)

    def step(i, carry):
        cs = [_scan_chunk_index(i, r == 1, n_chunks, ctx_chunks) for r in (0, 1)]
        rows = [pl.ds(pl.multiple_of(c * CHUNK, CHUNK), CHUNK) for c in cs]
        qcs = [_dot(q_ref[0, rw, :], st[0]) for rw, st in zip(rows, carry)]
        new = []
        for r in (0, 1):
            s_st, m_st = carry[r]
            c = cs[r]
            mi, na = mi_scr[r, c], num_scr[r, c]
            inter = fc_scr[r, c] + m_st[0:1, :]
            m_t = jnp.maximum(mi, inter)
            a_loc_t, a_inter = jnp.exp(mi - m_t), jnp.exp(inter - m_t)
            num = a_loc_t * na[:, :HEAD_DIM] + a_inter * qcs[r][:, :HEAD_DIM]
            den = a_loc_t * na[:, HEAD_DIM:] + a_inter * qcs[r][:, HEAD_DIM:]
            o_ref[0, rows[r], :] += num / jnp.maximum(jnp.abs(den), jnp.exp(-m_t))
            f_last, m_loc = sc_scr[r, c, :8, :], sc_scr[r, c, 8:, :]
            m_new = jnp.maximum(f_last + m_st, m_loc)
            a_prev = jnp.exp(f_last + m_st - m_new)[0:1, :]
            a_loc = jnp.exp(m_loc - m_new)[0:1, :]
            a_prev = jnp.concatenate([a_prev, a_prev], axis=1)
            a_loc = jnp.concatenate([a_loc, a_loc], axis=1)
            new.append((a_prev * s_st + a_loc * cl_scr[r, c], m_new))
        return tuple(new)

    init = (jnp.zeros((HEAD_DIM, 2 * HEAD_DIM), F32), jnp.zeros((8, HEAD_DIM), F32))
    lax.fori_loop(0, n_chunks, step, (init, init), unroll=2)


def _mlstm(qkv, gates_row, bias, n_ctx):
    bsz, t, _ = qkv.shape
    n_chunks = t // CHUNK
    qkv_spec = lambda off: pl.BlockSpec((1, t, HEAD_DIM), lambda b, h: (b, 0, off + h))
    return pl.pallas_call(
        functools.partial(_mlstm_kernel, n_chunks=n_chunks, ctx_chunks=n_ctx // CHUNK,
                          group=math.gcd(6, n_chunks)),
        grid=(bsz, MLSTM_HEADS),
        in_specs=[pl.BlockSpec(memory_space=pltpu.SMEM),
                  qkv_spec(0), qkv_spec(MLSTM_HEADS), qkv_spec(2 * MLSTM_HEADS),
                  pl.BlockSpec((1, 4 * MLSTM_HEADS, n_chunks, CHUNK), lambda b, h: (b, 0, 0, 0))],
        out_specs=pl.BlockSpec((1, t, HEAD_DIM), lambda b, h: (b, 0, h)),
        out_shape=jax.ShapeDtypeStruct((bsz, t, MLSTM_WIDTH), F32),
        scratch_shapes=[pltpu.VMEM((2, n_chunks, CHUNK, 2 * HEAD_DIM), F32),
                        pltpu.VMEM((2, n_chunks, HEAD_DIM, 2 * HEAD_DIM), F32),
                        pltpu.VMEM((2, n_chunks, CHUNK, HEAD_DIM), F32),
                        pltpu.VMEM((2, n_chunks, CHUNK, HEAD_DIM), F32),
                        pltpu.VMEM((2, n_chunks, 16, HEAD_DIM), F32)],
        compiler_params=_cparams(("parallel", "parallel")),
        name="mlstm",
    )(bias, qkv, qkv, qkv, gates_row)


def _s5_prep_kernel(lr_ref, li_ref, ldt_ref, br_ref, bi_ref, cr_ref, ci_ref, bd_ref, cdt_ref, a_ref):
    lr, li = lr_ref[0, 0], li_ref[0, 0]
    dt = jnp.exp(ldt_ref[0, 0])
    mag, ang = jnp.exp(lr * dt), li * dt
    ab_re, ab_im = mag * jnp.cos(ang), mag * jnp.sin(ang)
    nr, ni = ab_re - 1.0, ab_im
    den = lr * lr + li * li
    co_re = (nr * lr + ni * li) / den
    co_im = (ni * lr - nr * li) / den
    b_re, b_im = br_ref[0, 0], bi_ref[0, 0]
    bb_re = co_re * b_re - co_im * b_im
    bb_im = co_re * b_im + co_im * b_re
    c_re, c_im = cr_ref[0, 0], ci_ref[0, 0]
    lane_group = lax.broadcasted_iota(jnp.int32, (S5_GROUP, S5_HALF_N), 1) // S5_STATE
    n = S5_HALF_N
    for g in range(S5_GROUPS // S5_HALVES):
        sel = lane_group == g
        rows = slice(g * S5_GROUP, (g + 1) * S5_GROUP)
        bd_ref[0, 0, rows, :n] = jnp.where(sel, bb_re, 0.0).astype(bd_ref.dtype)
        bd_ref[0, 0, rows, n:] = jnp.where(sel, bb_im, 0.0).astype(bd_ref.dtype)
        cdt_ref[0, 0, rows, :n] = jnp.where(sel, c_re, 0.0).astype(cdt_ref.dtype)
        cdt_ref[0, 0, rows, n:] = jnp.where(sel, -c_im, 0.0).astype(cdt_ref.dtype)
    a_ref[0, 0, :, :n] = jnp.broadcast_to(ab_re, (8, n))
    a_ref[0, 0, :, n:] = jnp.broadcast_to(ab_im, (8, n))


def _s5_prep(lam_re, lam_im, log_dt, b_re, b_im, c_re, c_im):
    gh = S5_GROUPS // S5_HALVES
    vec = lambda a: a.reshape(2, S5_HALVES, 1, S5_HALF_N)
    ldt = vec(jnp.broadcast_to(log_dt[:, :, None], (2, S5_GROUPS, S5_STATE)))
    bt = lambda a: a.reshape(2, S5_HALVES, gh, S5_STATE, S5_GROUP).transpose(0, 1, 4, 2, 3).reshape(
        2, S5_HALVES, S5_GROUP, S5_HALF_N)
    ct = lambda a: a.reshape(2, S5_HALVES, gh, S5_GROUP, S5_STATE).transpose(0, 1, 3, 2, 4).reshape(
        2, S5_HALVES, S5_GROUP, S5_HALF_N)
    vspec = pl.BlockSpec((1, 1, 1, S5_HALF_N), lambda r, h: (r, h, 0, 0))
    mspec = pl.BlockSpec((1, 1, S5_GROUP, S5_HALF_N), lambda r, h: (r, h, 0, 0))
    ospec = pl.BlockSpec((1, 1, S5_HALF_W, 2 * S5_HALF_N), lambda r, h: (r, h, 0, 0))
    return pl.pallas_call(
        _s5_prep_kernel,
        grid=(2, S5_HALVES),
        in_specs=[vspec, vspec, vspec, mspec, mspec, mspec, mspec],
        out_specs=[ospec, ospec, pl.BlockSpec((1, 1, 8, 2 * S5_HALF_N), lambda r, h: (r, h, 0, 0))],
        out_shape=[jax.ShapeDtypeStruct((2, S5_HALVES, S5_HALF_W, 2 * S5_HALF_N), BF16),
                   jax.ShapeDtypeStruct((2, S5_HALVES, S5_HALF_W, 2 * S5_HALF_N), BF16),
                   jax.ShapeDtypeStruct((2, S5_HALVES, 8, 2 * S5_HALF_N), F32)],
        compiler_params=_cparams(("parallel", "parallel")),
        name="s5_prep",
    )(vec(lam_re), vec(lam_im), ldt, bt(b_re), bt(b_im), ct(c_re), ct(c_im))


def _s5_kernel(u_ref, bd_ref, cdt_ref, a_ref, y_ref, bu_scr, s_scr, st_scr, *, tb, bsz):
    r = pl.program_id(0)
    n = S5_HALF_N

    @pl.when(pl.program_id(2) == 0)
    def _():
        st_scr[...] = jnp.zeros_like(st_scr)

    bu_scr[...] = _dot(u_ref[...], bd_ref[0, 0])
    a_re, a_im = a_ref[0, 0, :, :n], a_ref[0, 0, :, n:]

    def step(j, st):
        s_re, s_im = st
        t = jnp.where(r == 0, j, tb - 1 - j)
        rows = pl.ds(pl.multiple_of(t * bsz, bsz), bsz)
        n_re = a_re * s_re - a_im * s_im + bu_scr[rows, :n]
        n_im = a_re * s_im + a_im * s_re + bu_scr[rows, n:]
        s_scr[rows, :n] = n_re
        s_scr[rows, n:] = n_im
        return n_re, n_im

    s_re, s_im = lax.fori_loop(0, tb, step, (st_scr[:, :n], st_scr[:, n:]))
    st_scr[:, :n] = s_re
    st_scr[:, n:] = s_im
    y_ref[0] = _dot_nt(s_scr[...], cdt_ref[0, 0])


def _s5(u_tb, bd, cdt, avec, bsz, n_ctx, tb):
    rows_total = u_tb.shape[0]
    t = rows_total // bsz
    nb, ncb = t // tb, n_ctx // tb

    def blk(r, i):
        rev = jnp.where(i < ncb, ncb - 1 - i, nb + ncb - 1 - i)
        return jnp.where(r == 0, i, rev)

    wspec = pl.BlockSpec((1, 1, S5_HALF_W, 2 * S5_HALF_N), lambda r, h, i: (r, h, 0, 0))
    return pl.pallas_call(
        functools.partial(_s5_kernel, tb=tb, bsz=bsz),
        grid=(2, S5_HALVES, nb),
        in_specs=[pl.BlockSpec((tb * bsz, S5_HALF_W), lambda r, h, i: (blk(r, i), h)),
                  wspec, wspec,
                  pl.BlockSpec((1, 1, 8, 2 * S5_HALF_N), lambda r, h, i: (r, h, 0, 0))],
        out_specs=pl.BlockSpec((1, tb * bsz, S5_HALF_W), lambda r, h, i: (r, blk(r, i), h)),
        out_shape=jax.ShapeDtypeStruct((2, rows_total, S5_WIDTH), F32),
        scratch_shapes=[pltpu.VMEM((tb * bsz, 2 * S5_HALF_N), F32),
                        pltpu.VMEM((tb * bsz, 2 * S5_HALF_N), F32),
                        pltpu.VMEM((bsz, 2 * S5_HALF_N), F32)],
        compiler_params=_cparams(("parallel", "parallel", "arbitrary")),
        name="s5_scan",
    )(u_tb, bd, cdt, avec)


def _even_finish_kernel(x_ref, ha_ref, op_ref, yb_ref, u_ref, mhw_ref, ds_ref, wg_ref, wo_ref,
                        gc_ref, gl_ref, o_ref, *, n_top):
    ha = ha_ref[0]
    parts = []
    for h in range(MLSTM_HEADS):
        sl = slice(h * HEAD_DIM, (h + 1) * HEAD_DIM)
        parts.append(_rms(ha[:, sl], mhw_ref[:, sl]))
    a_out = jnp.concatenate(parts, axis=-1) * jax.nn.sigmoid(op_ref[0])
    yb = jax.nn.gelu(yb_ref[0] + yb_ref[1] + ds_ref[...] * u_ref[...])
    glu = _dot(yb, wg_ref[...])
    b_out = glu[:, :S5_WIDTH] * jax.nn.sigmoid(glu[:, S5_WIDTH:])
    y = _dot(a_out, wo_ref[:MLSTM_WIDTH, :]) + _dot(b_out, wo_ref[MLSTM_WIDTH:, :])
    for rows, (gate,) in _row_groups(n_top, y.shape[0], (gc_ref,), (gl_ref,)):
        o_ref[0, rows, :] = x_ref[0, rows, :] + gate * y[rows, :]


def _even_finish(xc, ha, o_pre, yb, u, mods, layer, mh_w, d_skip, wg, wo, tm, n_ctx):
    bsz, t, d = xc.shape
    assert t % tm == 0 and n_ctx <= tm
    yb3 = yb.reshape(2, t, bsz * S5_WIDTH)
    full = lambda shape: pl.BlockSpec(shape, lambda b, i: tuple(0 for _ in shape),
                                      pipeline_mode=pl.Buffered(1))
    return pl.pallas_call(
        functools.partial(_even_finish_kernel, n_top=n_ctx),
        grid=(bsz, t // tm),
        in_specs=[pl.BlockSpec((1, tm, d), lambda b, i: (b, i, 0)),
                  pl.BlockSpec((1, tm, MLSTM_WIDTH), lambda b, i: (b, i, 0)),
                  pl.BlockSpec((1, tm, MLSTM_WIDTH), lambda b, i: (b, i, 0)),
                  pl.BlockSpec((2, tm, S5_WIDTH), lambda b, i: (0, i, b)),
                  pl.BlockSpec((tm, S5_WIDTH), lambda b, i: (i, b)),
                  full((1, MLSTM_WIDTH)), full((1, S5_WIDTH)),
                  full((S5_WIDTH, 2 * S5_WIDTH)), full((d, d)),
                  _ctx_mod_spec(layer, 2, bsz), _lat_mod_spec(layer, 2)],
        out_specs=pl.BlockSpec((1, tm, d), lambda b, i: (b, i, 0)),
        out_shape=jax.ShapeDtypeStruct((bsz, t, d), F32),
        compiler_params=_cparams(("parallel", "parallel")),
        name="even_finish",
    )(xc, ha, o_pre, yb3, u, mh_w.reshape(1, -1), d_skip.reshape(1, -1), wg, wo, mods, mods)


def _conv_kernel(x_ref, w_ref, o_ref, *, n_ctx, seq):
    j = pl.program_id(1)
    w = w_ref[...]
    wrow = lambda k: w[k:k + 1, :]
    n_rows = seq // GRID_W
    tpos = lax.broadcasted_iota(jnp.int32, (GRID_W, LANES), 0)
    head_blocks = D_MODEL // LANES
    is_qk = j < 2 * head_blocks
    post_scale = jnp.where(j < head_blocks, 1.0 / math.sqrt(HEAD_DIM), 1.0)

    def finish(y):
        y = y * jax.nn.sigmoid(y)
        nrm = y * lax.rsqrt(jnp.sum(y * y, axis=-1, keepdims=True) + EPS) * post_scale
        return jnp.where(is_qk, nrm, y)

    def shifted(r0, r1, r2, first_mask, last_mask):
        prev = jnp.where(first_mask, 0.0, pltpu.roll(r0, 1, axis=0))
        nxt = jnp.where(last_mask, 0.0, pltpu.roll(r2, GRID_W - 1, axis=0))
        return r1 + prev + nxt

    first = tpos == 0
    last = tpos == GRID_W - 1

    def lat_row(r, _):
        base = n_ctx + r * GRID_W
        cur = x_ref[0, pl.ds(pl.multiple_of(base, GRID_W), GRID_W), :].astype(F32)
        up_i = jnp.maximum(r - 1, 0)
        dn_i = jnp.minimum(r + 1, n_rows - 1)
        up = x_ref[0, pl.ds(pl.multiple_of(n_ctx + up_i * GRID_W, GRID_W), GRID_W), :].astype(F32)
        dn = x_ref[0, pl.ds(pl.multiple_of(n_ctx + dn_i * GRID_W, GRID_W), GRID_W), :].astype(F32)
        w_up = jnp.where(r > 0, w[0:3, :], 0.0)
        w_dn = jnp.where(r < n_rows - 1, w[6:9, :], 0.0)
        rs = [w_up[dc:dc + 1, :] * up + wrow(3 + dc) * cur + w_dn[dc:dc + 1, :] * dn for dc in range(3)]
        o_ref[0, pl.ds(pl.multiple_of(base, GRID_W), GRID_W), :] = finish(
            shifted(rs[0], rs[1], rs[2], first, last)).astype(o_ref.dtype)
        return 0

    lax.fori_loop(0, n_rows, lat_row, 0, unroll=math.gcd(4, n_rows))

    n_seg = n_ctx // GRID_W

    def ctx_seg(s, _):
        base = pl.multiple_of(s * GRID_W, GRID_W)
        cur = x_ref[0, pl.ds(base, GRID_W), :].astype(F32)
        pb = jnp.maximum(s * GRID_W - 16, 0)
        prev_blk = x_ref[0, pl.ds(pl.multiple_of(pb, 16), 16), :].astype(F32)
        nb = jnp.minimum(s * GRID_W + GRID_W, n_ctx - 16)
        next_blk = x_ref[0, pl.ds(pl.multiple_of(nb, 16), 16), :].astype(F32)
        prev_tok = jnp.where(s > 0, prev_blk[15:16, :], 0.0)
        next_tok = jnp.where(s < n_seg - 1, next_blk[0:1, :], 0.0)
        left = jnp.where(first, prev_tok, pltpu.roll(cur, 1, axis=0))
        right = jnp.where(last, next_tok, pltpu.roll(cur, GRID_W - 1, axis=0))
        y = wrow(3) * left + wrow(4) * cur + wrow(5) * right
        o_ref[0, pl.ds(base, GRID_W), :] = finish(y).astype(o_ref.dtype)
        return 0

    lax.fori_loop(0, n_seg, ctx_seg, 0, unroll=math.gcd(4, n_seg))


def _gdn_conv(qkv_pre, conv_w, n_ctx):
    bsz, t, _ = qkv_pre.shape
    return pl.pallas_call(
        functools.partial(_conv_kernel, n_ctx=n_ctx, seq=t - n_ctx),
        grid=(bsz, GDN_QKV // LANES),
        in_specs=[pl.BlockSpec((1, t, LANES), lambda b, j: (b, 0, j)),
                  pl.BlockSpec((9, LANES), lambda b, j: (0, j))],
        out_specs=pl.BlockSpec((1, t, LANES), lambda b, j: (b, 0, j)),
        out_shape=jax.ShapeDtypeStruct((bsz, t, GDN_QKV), BF16),
        compiler_params=_cparams(("parallel", "parallel")),
        name="gdn_conv",
    )(qkv_pre, conv_w.reshape(9, GDN_QKV))


def _unit_tri_inverses(mats, eye_f):
    ps = [eye_f - a for a in mats]
    pws = [_dot(a, a) for a in mats]
    k = 2
    while 2 * k < CHUNK:
        res = [_dot(jnp.concatenate([p, pw], axis=0), pw) for p, pw in zip(ps, pws)]
        ps = [p + r[:CHUNK] for p, r in zip(ps, res)]
        pws = [r[CHUNK:] for r in res]
        k *= 2
    return [p + _dot(p, pw) for p, pw in zip(ps, pws)]


def _gdn_kernel(par_ref, q_ref, k_ref, v_ref, g_ref, o_ref, qk_scr, ol_scr, nl_scr, gl_scr, *,
                n_chunks, ctx_chunks, group):
    hd = pl.program_id(1)
    row, col = _chunk_masks()
    eye = row == col
    eye_f = eye.astype(F32)

    def load(c):
        rows = pl.ds(pl.multiple_of(c * CHUNK, CHUNK), CHUNK)
        gates = [(g_ref[0, r * GDN_HEADS + hd, pl.ds(c, 1), :],
                  g_ref[0, (2 + r) * GDN_HEADS + hd, pl.ds(c, 1), :]) for r in (0, 1)]
        return (q_ref[0, rows, :].astype(F32), k_ref[0, rows, :].astype(F32),
                v_ref[0, rows, :].astype(F32), gates)

    def gate_terms(r, a_pre, b_pre):
        incl = (col >= row) if r == 1 else (col <= row)
        a_log = jnp.full((1, CHUNK), par_ref[r * GDN_HEADS + hd], F32)
        dt_b = par_ref[(2 + r) * GDN_HEADS + hd]
        g_row = -jnp.exp(a_log) * jax.nn.softplus(a_pre + dt_b)
        beta_col = _to_col(jax.nn.sigmoid(b_pre), eye)
        gc_col = jnp.sum(jnp.where(incl, jnp.broadcast_to(g_row, incl.shape), 0.0),
                         axis=1, keepdims=True)
        gc_row = _to_row(gc_col, eye)
        g_last = jnp.sum(g_row, axis=1, keepdims=True)
        gamma = jnp.exp(jnp.where(incl, gc_col - gc_row, -jnp.inf))
        return beta_col, gc_col, g_last, gamma

    def local_group(gi, _):
        chunks = [gi * group + g for g in range(group)]
        loaded = [load(c) for c in chunks]
        kqs = [_dot_nt(jnp.concatenate([k, q], axis=0), k) for q, k, _, _ in loaded]
        chains = [(g, r) for g in range(group) for r in (0, 1)]
        terms = [gate_terms(r, *loaded[g][3][r]) for g, r in chains]
        a_mats = []
        for (g, r), (beta_col, _, _, gamma) in zip(chains, terms):
            strict = (col > row) if r == 1 else (col < row)
            a_mats.append(jnp.where(strict, beta_col * kqs[g][:CHUNK] * gamma, 0.0))
        t_invs = _unit_tri_inverses(a_mats, eye_f)
        sols = []
        for (g, r), (beta_col, gc_col, _, _), t_inv in zip(chains, terms, t_invs):
            _, k, v, _ = loaded[g]
            rhs = jnp.concatenate([v * beta_col, k * (beta_col * jnp.exp(gc_col))], axis=-1)
            sols.append(_dot(t_inv, rhs))
        auws = [_dot(kqs[g][CHUNK:] * tm[3], sol) for (g, r), tm, sol in zip(chains, terms, sols)]
        kuws = [_dot_tn(loaded[g][1] * jnp.exp(tm[2] - tm[1]), sol)
                for (g, r), tm, sol in zip(chains, terms, sols)]
        for (g, r), (_, gc_col, g_last, _), auw, kuw in zip(chains, terms, auws, kuws):
            c = chunks[g]
            q = loaded[g][0]
            qk_scr[r, c, :CHUNK, :] = (q * jnp.exp(gc_col) - auw[:, HEAD_DIM:]).astype(qk_scr.dtype)
            qk_scr[r, c, CHUNK:, :] = kuw[:, HEAD_DIM:].astype(qk_scr.dtype)
            ol_scr[r, c] = auw[:, :HEAD_DIM]
            nl_scr[r, c] = kuw[:, :HEAD_DIM]
            gl_scr[r, c] = jnp.broadcast_to(jnp.exp(g_last), (8, HEAD_DIM))
        return 0

    lax.fori_loop(0, n_chunks // group, local_group, 0)
    o_ref[...] = jnp.zeros_like(o_ref)

    def step(i, carry):
        new = []
        for r in (0, 1):
            c = _scan_chunk_index(i, r == 1, n_chunks, ctx_chunks)
            rows = pl.ds(pl.multiple_of(c * CHUNK, CHUNK), CHUNK)
            s = carry[r]
            res = jnp.dot(qk_scr[r, c], s.astype(BF16), preferred_element_type=F32)
            o_ref[0, rows, :] += ol_scr[r, c] + res[:CHUNK]
            new.append(gl_scr[r, c][0:1, :] * s - res[CHUNK:] + nl_scr[r, c])
        return tuple(new)

    init = jnp.zeros((HEAD_DIM, HEAD_DIM), F32)
    lax.fori_loop(0, n_chunks, step, (init, init), unroll=2)


def _gdn(qkv, gates_row, par, n_ctx):
    bsz, t, _ = qkv.shape
    n_chunks = t // CHUNK
    group = math.gcd(9, n_chunks)
    spec = lambda off: pl.BlockSpec((1, t, HEAD_DIM), lambda b, h: (b, 0, off + h))
    return pl.pallas_call(
        functools.partial(_gdn_kernel, n_chunks=n_chunks, ctx_chunks=n_ctx // CHUNK, group=group),
        grid=(bsz, GDN_HEADS),
        in_specs=[pl.BlockSpec(memory_space=pltpu.SMEM),
                  spec(0), spec(GDN_HEADS), spec(2 * GDN_HEADS),
                  pl.BlockSpec((1, 4 * GDN_HEADS, n_chunks, CHUNK), lambda b, h: (b, 0, 0, 0))],
        out_specs=pl.BlockSpec((1, t, HEAD_DIM), lambda b, h: (b, 0, h)),
        out_shape=jax.ShapeDtypeStruct((bsz, t, D_MODEL), F32),
        scratch_shapes=[pltpu.VMEM((2, n_chunks, CHUNK + HEAD_DIM, HEAD_DIM), BF16),
                        pltpu.VMEM((2, n_chunks, CHUNK, HEAD_DIM), F32),
                        pltpu.VMEM((2, n_chunks, HEAD_DIM, HEAD_DIM), F32),
                        pltpu.VMEM((2, n_chunks, 8, HEAD_DIM), F32)],
        compiler_params=_cparams(("parallel", "parallel")),
        name="gdn",
    )(par, qkv, qkv, qkv, gates_row)


def _odd_finish_kernel(x_ref, o_in_ref, z_ref, hw_ref, wo_ref, g_ref, o_ref):
    o = o_in_ref[0]
    parts = []
    for h in range(GDN_HEADS):
        sl = slice(h * HEAD_DIM, (h + 1) * HEAD_DIM)
        parts.append(_rms(o[:, sl], hw_ref[:, sl]))
    z = z_ref[0]
    y = jnp.concatenate(parts, axis=-1) * (z * jax.nn.sigmoid(z))
    o_ref[0] = x_ref[0] + g_ref[0, 0] * _dot(y, wo_ref[...])


def _odd_finish(xc, o, z, mods, layer, hw, wo, tm, n_ctx):
    bsz, t, d = xc.shape
    assert n_ctx % tm == 0 and (t - n_ctx) % tm == 0
    t0 = n_ctx // tm
    return pl.pallas_call(
        _odd_finish_kernel,
        grid=(bsz, (t - n_ctx) // tm),
        in_specs=[pl.BlockSpec((1, tm, d), lambda b, i: (b, i + t0, 0)),
                  pl.BlockSpec((1, tm, d), lambda b, i: (b, i + t0, 0)),
                  pl.BlockSpec((1, tm, d), lambda b, i: (b, i + t0, 0)),
                  pl.BlockSpec((1, d), lambda b, i: (0, 0)),
                  pl.BlockSpec((d, d), lambda b, i: (0, 0)),
                  _lat_mod_spec(layer, 2)],
        out_specs=pl.BlockSpec((1, tm, d), lambda b, i: (b, i, 0)),
        out_shape=jax.ShapeDtypeStruct((bsz, t - n_ctx, d), F32),
        compiler_params=_cparams(("parallel", "parallel")),
        name="odd_finish",
    )(xc, o, z, hw.reshape(1, d), wo, mods)


def _chunked(gates_t):
    bsz, g, t = gates_t.shape
    return gates_t.reshape(bsz, g, t // CHUNK, CHUNK)


def _pick_tile(total, target, at_least):
    for tile in range(min(target, total), 7, -1):
        if total % tile == 0 and tile % 8 == 0 and tile >= at_least:
            return tile
    raise ValueError(f"no row tile for {total} rows")


def _pad_cols(w, n):
    return jnp.pad(w, ((0, 0), (0, n - w.shape[1])))


def kernel(x, c, ctx, c_ctx, ada_w, ada_b, norm1_w, norm2_w, ffn_w1, ffn_w3, ffn_w2, final_norm_w,
           ev_w_in, ev_i_bias, ev_f_bias, ev_head_norm_w, ev_lam_re, ev_lam_im, ev_log_dt,
           ev_b_re, ev_b_im, ev_c_re, ev_c_im, ev_d, ev_w_glu, ev_w_out,
           od_w_in, od_conv_w, od_a_log, od_dt_bias, od_head_norm_w, od_w_out):
    bsz, seq, d = x.shape
    n_ctx = ctx.shape[1]
    assert d == D_MODEL and seq % GRID_W == 0 and n_ctx % CHUNK == 0 and bsz % 8 == 0
    assert ada_w.shape[0] == 2
    t_all = n_ctx + seq
    tm = _pick_tile(t_all, 768, n_ctx)
    assert tm % LANES == 0
    tm_lat = _pick_tile(math.gcd(seq, n_ctx), 512, 0)
    th = FFN_HIDDEN // 2
    tb = CHUNK

    xc = jnp.concatenate([ctx, x], axis=1)
    mod_rows = 16
    cvec = jnp.zeros((mod_rows, d), F32).at[:bsz].set(c).at[bsz].set(c_ctx)
    mods = _ada(cvec, ada_w, ada_b).reshape(2, mod_rows, 1, 6 * d)
    w1, w3, w2 = ffn_w1.astype(BF16), ffn_w3.astype(BF16), ffn_w2.astype(BF16)

    qkv_w, n_g = 3 * MLSTM_WIDTH, 4 * MLSTM_HEADS
    w_in = ev_w_in[0]
    g0 = qkv_w + MLSTM_WIDTH
    w_even = jnp.concatenate([w_in[:, :g0], w_in[:, g0 + n_g:], _pad_cols(w_in[:, g0:g0 + n_g], LANES)],
                             axis=1).astype(BF16)
    c1, c2, c3 = qkv_w, g0, g0 + S5_WIDTH
    qkv, o_pre, u, gates = _inproj(
        xc, norm1_w[0], mods, 0, w_even,
        [(0, c1, "row", BF16, c1), (c1, c2, "row", F32, c2 - c1), (c2, c3, "tok", F32, S5_WIDTH),
         (c3, c3 + LANES, "gate", F32, n_g)], tm, n_ctx)
    bias = jnp.concatenate([ev_i_bias[0].reshape(-1), ev_f_bias[0].reshape(-1)]).astype(F32)
    ha = _mlstm(qkv, _chunked(gates), bias, n_ctx)
    bd, cdt, avec = _s5_prep(ev_lam_re[0], ev_lam_im[0], ev_log_dt[0], ev_b_re[0], ev_b_im[0],
                             ev_c_re[0], ev_c_im[0])
    yb = _s5(u.reshape(-1, S5_WIDTH), bd, cdt, avec, bsz, n_ctx, tb)
    xc = _even_finish(xc, ha, o_pre, yb, u, mods, 0, ev_head_norm_w[0], ev_d[0],
                      ev_w_glu[0].astype(BF16), ev_w_out[0].astype(BF16), tm, n_ctx)
    xc = _ffn(xc, norm2_w[0], mods, 0, w1[0], w3[0], w2[0], final_norm_w, tm, th, n_ctx, False)

    n_g = 4 * GDN_HEADS
    w_odd = _pad_cols(od_w_in[0], 4 * d + LANES).astype(BF16)
    qkv_pre, z, gates = _inproj(
        xc, norm1_w[1], mods, 1, w_odd,
        [(0, GDN_QKV, "row", BF16, GDN_QKV), (GDN_QKV, 4 * d, "row", F32, d),
         (4 * d, 4 * d + LANES, "gate", F32, n_g)], tm, n_ctx)
    qkv = _gdn_conv(qkv_pre, od_conv_w[0], n_ctx)
    par = jnp.concatenate([od_a_log[0].reshape(-1), od_dt_bias[0].reshape(-1)]).astype(F32)
    o = _gdn(qkv, _chunked(gates), par, n_ctx)
    xl = _odd_finish(xc, o, z, mods, 1, od_head_norm_w[0], od_w_out[0].astype(BF16), tm_lat, n_ctx)
    return _ffn(xl, norm2_w[1], mods, 1, w1[1], w3[1], w2[1], final_norm_w, _pick_tile(seq, 512, 0),
                th, 0, True)
```

```python
import functools
import math

import jax
import jax.numpy as jnp
from jax import lax
from jax.experimental import pallas as pl
from jax.experimental.pallas import tpu as pltpu

F32 = jnp.float32
BF16 = jnp.bfloat16

D_MODEL = 1024
CHUNK = 64
GRID_W = 64
EPS = 1e-6
LANES = 128
HEAD_DIM = 128
MLSTM_WIDTH = D_MODEL // 2
MLSTM_HEADS = MLSTM_WIDTH // HEAD_DIM
S5_WIDTH = D_MODEL - MLSTM_WIDTH
S5_GROUP = 16
S5_GROUPS = S5_WIDTH // S5_GROUP
S5_STATE = 64
S5_HALVES = 2
S5_HALF_W = S5_WIDTH // S5_HALVES
S5_HALF_N = (S5_GROUPS // S5_HALVES) * S5_STATE
GDN_HEADS = D_MODEL // HEAD_DIM
GDN_QKV = 3 * D_MODEL
FFN_HIDDEN = ((8 * D_MODEL + 3 * 256 - 1) // (3 * 256)) * 256
VMEM_LIMIT = 56 * 1024 * 1024

_NT = (((1,), (1,)), ((), ()))
_TN = (((0,), (0,)), ((), ()))


def _cparams(sem):
    return pltpu.CompilerParams(dimension_semantics=sem, vmem_limit_bytes=VMEM_LIMIT)


def _dot(a, b):
    return jnp.dot(a.astype(BF16), b.astype(BF16), preferred_element_type=F32)


def _dot_nt(a, b):
    return lax.dot_general(a.astype(BF16), b.astype(BF16), _NT, preferred_element_type=F32)


def _dot_tn(a, b):
    return lax.dot_general(a.astype(BF16), b.astype(BF16), _TN, preferred_element_type=F32)


def _rms(x, w):
    return x * lax.rsqrt(jnp.mean(x * x, axis=-1, keepdims=True) + EPS) * w


def _ada_kernel(s_ref, w_ref, b_ref, o_ref):
    s = s_ref[...]
    s = s * jax.nn.sigmoid(s)
    o_ref[0] = _dot(s, w_ref[0]) + b_ref[0]


def _ada(cvec, ada_w, ada_b):
    depth, d, n = ada_w.shape
    tn = 1536
    rows = cvec.shape[0]
    return pl.pallas_call(
        _ada_kernel,
        grid=(depth, n // tn),
        in_specs=[pl.BlockSpec((rows, d), lambda l, j: (0, 0)),
                  pl.BlockSpec((1, d, tn), lambda l, j: (l, 0, j)),
                  pl.BlockSpec((1, 1, tn), lambda l, j: (l, 0, j))],
        out_specs=pl.BlockSpec((1, rows, tn), lambda l, j: (l, 0, j)),
        out_shape=jax.ShapeDtypeStruct((depth, rows, n), F32),
        compiler_params=_cparams(("parallel", "parallel")),
        name="ada_mod",
    )(cvec, ada_w, ada_b.reshape(depth, 1, n))


def _ctx_mod_spec(layer, chunk, ctx_row):
    return pl.BlockSpec((1, 1, 1, D_MODEL), lambda b, i: (layer, ctx_row, 0, chunk))


def _lat_mod_spec(layer, chunk):
    return pl.BlockSpec((1, 1, 1, D_MODEL), lambda b, i: (layer, b, 0, chunk))


def _row_groups(n_top, tm, ctx_refs, lat_refs):
    lat = [r[0, 0] for r in lat_refs]
    if n_top == 0:
        return [(slice(0, tm), lat)]
    first = pl.program_id(1) == 0
    top = [jnp.where(first, c[0, 0], l) for c, l in zip(ctx_refs, lat)]
    return [(slice(0, n_top), top), (slice(n_top, tm), lat)]


def _inproj_kernel(x_ref, nw_ref, shc_ref, scc_ref, shl_ref, scl_ref, w_ref, *rest, outs, n_top,
                   n_chunk):
    out_refs, h_scr, g_scr = rest[:-2], rest[-2], rest[-1]
    tm = h_scr.shape[0]
    for rows, (sh, sc) in _row_groups(n_top, tm, (shc_ref, scc_ref), (shl_ref, scl_ref)):
        h_scr[rows, :] = (_rms(x_ref[0, rows, :], nw_ref[...]) * (1.0 + sc) + sh).astype(BF16)
    for o_ref, (c0, c1, kind) in zip(out_refs, outs):
        for n0 in range(c0, c1, n_chunk):
            n1 = min(n0 + n_chunk, c1)
            val = jnp.dot(h_scr[...], w_ref[:, n0:n1], preferred_element_type=F32)
            if kind == "row":
                o_ref[0, :, n0 - c0:n1 - c0] = val.astype(o_ref.dtype)
            else:
                g_scr[...] = val
                o_ref[0] = g_scr[...].T[:o_ref.shape[1], :].astype(o_ref.dtype)


def _inproj(xc, norm_w, mods, layer, w_bf16, outs, tm, n_ctx):
    bsz, t, d = xc.shape
    n = w_bf16.shape[1]
    assert t % tm == 0 and (n_ctx == 0 or n_ctx <= tm)
    out_specs, out_shapes = [], []
    for c0, c1, kind, dtype, wd in outs:
        if kind == "row":
            out_specs.append(pl.BlockSpec((1, tm, wd), lambda b, i: (b, i, 0)))
            out_shapes.append(jax.ShapeDtypeStruct((bsz, t, wd), dtype))
        else:
            assert c1 - c0 == LANES and tm % LANES == 0
            out_specs.append(pl.BlockSpec((1, wd, tm), lambda b, i: (b, 0, i)))
            out_shapes.append(jax.ShapeDtypeStruct((bsz, wd, t), dtype))
    ctx_spec = lambda c: _ctx_mod_spec(layer, c, bsz)
    lat_spec = lambda c: _lat_mod_spec(layer, c)
    const = lambda shape: pl.BlockSpec(shape, lambda b, i: (0, 0), pipeline_mode=pl.Buffered(1))
    return pl.pallas_call(
        functools.partial(_inproj_kernel, outs=tuple(o[:3] for o in outs), n_top=n_ctx, n_chunk=512),
        grid=(bsz, t // tm),
        in_specs=[pl.BlockSpec((1, tm, d), lambda b, i: (b, i, 0)),
                  const((1, d)), ctx_spec(0), ctx_spec(1), lat_spec(0), lat_spec(1), const((d, n))],
        out_specs=out_specs,
        out_shape=out_shapes,
        scratch_shapes=[pltpu.VMEM((tm, d), BF16), pltpu.VMEM((tm, LANES), F32)],
        compiler_params=_cparams(("parallel", "parallel")),
        name=f"inproj_l{layer}",
    )(xc, norm_w.reshape(1, d), mods, mods, mods, mods, w_bf16)


def _ffn_kernel(x_ref, nw_ref, shc_ref, scc_ref, gc_ref, shl_ref, scl_ref, gl_ref,
                w1_ref, w3_ref, w2_ref, fw_ref, o_ref, h_scr, *, final, n_top, th):
    tm, hid = h_scr.shape[0], w1_ref.shape[1]
    groups = _row_groups(n_top, tm, (shc_ref, scc_ref, gc_ref), (shl_ref, scl_ref, gl_ref))
    for rows, (sh, sc, _) in groups:
        h_scr[rows, :] = (_rms(x_ref[0, rows, :], nw_ref[...]) * (1.0 + sc) + sh).astype(BF16)
    acc = None
    for c0 in range(0, hid, th):
        h = h_scr[...]
        a = jnp.dot(h, w1_ref[:, c0:c0 + th], preferred_element_type=F32)
        g = jnp.dot(h, w3_ref[:, c0:c0 + th], preferred_element_type=F32)
        t = ((a * jax.nn.sigmoid(a)) * g).astype(BF16)
        part = jnp.dot(t, w2_ref[c0:c0 + th, :], preferred_element_type=F32)
        acc = part if acc is None else acc + part
    for rows, (_, _, gate) in groups:
        y = x_ref[0, rows, :] + gate * acc[rows, :]
        if final:
            y = _rms(y, fw_ref[...])
        o_ref[0, rows, :] = y


def _ffn(x, norm_w, mods, layer, w1, w3, w2, final_w, tm, th, n_ctx, final):
    bsz, tx, d = x.shape
    hid = w1.shape[1]
    assert n_ctx <= tm and tx % tm == 0 and hid % th == 0
    ctx_spec = lambda c: _ctx_mod_spec(layer, c, bsz)
    lat_spec = lambda c: _lat_mod_spec(layer, c)
    const = lambda shape: pl.BlockSpec(shape, lambda b, i: (0, 0), pipeline_mode=pl.Buffered(1))
    return pl.pallas_call(
        functools.partial(_ffn_kernel, final=final, n_top=n_ctx, th=th),
        grid=(bsz, tx // tm),
        in_specs=[pl.BlockSpec((1, tm, d), lambda b, i: (b, i, 0)),
                  const((1, d)),
                  ctx_spec(3), ctx_spec(4), ctx_spec(5), lat_spec(3), lat_spec(4), lat_spec(5),
                  const((d, hid)), const((d, hid)), const((hid, d)), const((1, d))],
        out_specs=pl.BlockSpec((1, tm, d), lambda b, i: (b, i, 0)),
        out_shape=jax.ShapeDtypeStruct((bsz, tx, d), F32),
        scratch_shapes=[pltpu.VMEM((tm, d), BF16)],
        compiler_params=_cparams(("parallel", "parallel")),
        name=f"ffn_l{layer}",
    )(x, norm_w.reshape(1, d), mods, mods, mods, mods, mods, mods, w1, w3, w2, final_w.reshape(1, d))


def _chunk_masks():
    row = lax.broadcasted_iota(jnp.int32, (CHUNK, CHUNK), 0)
    col = lax.broadcasted_iota(jnp.int32, (CHUNK, CHUNK), 1)
    return row, col


def _to_col(row_vec, eye):
    return jnp.sum(jnp.where(eye, jnp.broadcast_to(row_vec, eye.shape), 0.0), axis=1, keepdims=True)


def _to_row(col_vec, eye):
    return jnp.sum(jnp.where(eye, jnp.broadcast_to(col_vec, eye.shape), 0.0), axis=0, keepdims=True)


def _scan_chunk_index(i, rev, n_chunks, ctx_chunks):
    if not rev:
        return i
    return jnp.where(i < ctx_chunks, ctx_chunks - 1 - i, n_chunks + ctx_chunks - 1 - i)


def _mlstm_kernel(bias_ref, q_ref, k_ref, v_ref, g_ref, o_ref, num_scr, cl_scr, mi_scr, fc_scr, sc_scr,
                  *, n_chunks, ctx_chunks, group):
    hd = pl.program_id(1)
    row, col = _chunk_masks()
    eye = row == col
    kscale = 1.0 / math.sqrt(HEAD_DIM)
    ones_blk = jnp.ones((CHUNK, HEAD_DIM), BF16)

    def gate_terms(r, ig_raw, f_raw):
        incl = (col >= row) if r == 1 else (col <= row)
        ig_row = ig_raw + bias_ref[r * MLSTM_HEADS + hd]
        lf_row = jax.nn.log_sigmoid(f_raw + bias_ref[(2 + r) * MLSTM_HEADS + hd])
        f_col = jnp.sum(jnp.where(incl, jnp.broadcast_to(lf_row, incl.shape), 0.0),
                        axis=1, keepdims=True)
        f_row = _to_row(f_col, eye)
        f_last = jnp.sum(lf_row, axis=1, keepdims=True)
        dm = jnp.where(incl, f_col - f_row + ig_row, -jnp.inf)
        m_intra = jnp.max(dm, axis=1, keepdims=True)
        w_col = f_last - f_col + _to_col(ig_row, eye)
        m_loc = jnp.max(w_col, axis=0, keepdims=True)
        return f_col, f_last, jnp.exp(dm - m_intra), m_intra, jnp.exp(w_col - m_loc), m_loc

    def local_group(gi, _):
        chunks = [gi * group + g for g in range(group)]
        loaded = []
        for c in chunks:
            rows = pl.ds(pl.multiple_of(c * CHUNK, CHUNK), CHUNK)
            gates = [(g_ref[0, r * MLSTM_HEADS + hd, pl.ds(c, 1), :],
                      g_ref[0, (2 + r) * MLSTM_HEADS + hd, pl.ds(c, 1), :]) for r in (0, 1)]
            v1 = jnp.concatenate([v_ref[0, rows, :].astype(BF16), ones_blk], axis=-1)
            loaded.append((q_ref[0, rows, :], k_ref[0, rows, :].astype(F32) * kscale, v1, gates))
        chains = [(g, r) for g in range(group) for r in (0, 1)]
        terms = [gate_terms(r, *loaded[g][3][r]) for g, r in chains]
        qks = [_dot_nt(q, k) for q, k, _, _ in loaded]
        c_locs = [_dot_tn(tm[4] * loaded[g][1], loaded[g][2]) for (g, r), tm in zip(chains, terms)]
        nums = [_dot(tm[2] * qks[g], loaded[g][2]) for (g, r), tm in zip(chains, terms)]
        for (g, r), tm, c_loc, num in zip(chains, terms, c_locs, nums):
            f_col, f_last, _, m_intra, _, m_loc = tm
            c = chunks[g]
            num_scr[r, c] = num
            cl_scr[r, c] = c_loc
            mi_scr[r, c] = jnp.broadcast_to(m_intra, (CHUNK, HEAD_DIM))
            fc_scr[r, c] = jnp.broadcast_to(f_col, (CHUNK, HEAD_DIM))
            sc_scr[r, c, :8, :] = jnp.broadcast_to(f_last, (8, HEAD_DIM))
            sc_scr[r, c, 8:, :] = jnp.broadcast_to(m_loc, (8, HEAD_DIM))
        return 0

    lax.fori_loop(0, n_chunks // group, local_group, 0)
    o_ref[...] = jnp.zeros_like(o_ref)

    def step(i, carry):
        cs = [_scan_chunk_index(i, r == 1, n_chunks, ctx_chunks) for r in (0, 1)]
        rows = [pl.ds(pl.multiple_of(c * CHUNK, CHUNK), CHUNK) for c in cs]
        qcs = [_dot(q_ref[0, rw, :], st[0]) for rw, st in zip(rows, carry)]
        new = []
        for r in (0, 1):
            s_st, m_st = carry[r]
            c = cs[r]
            mi, na = mi_scr[r, c], num_scr[r, c]
            inter = fc_scr[r, c] + m_st[0:1, :]
            m_t = jnp.maximum(mi, inter)
            a_loc_t, a_inter = jnp.exp(mi - m_t), jnp.exp(inter - m_t)
            num = a_loc_t * na[:, :HEAD_DIM] + a_inter * qcs[r][:, :HEAD_DIM]
            den = a_loc_t * na[:, HEAD_DIM:] + a_inter * qcs[r][:, HEAD_DIM:]
            o_ref[0, rows[r], :] += num / jnp.maximum(jnp.abs(den), jnp.exp(-m_t))
            f_last, m_loc = sc_scr[r, c, :8, :], sc_scr[r, c, 8:, :]
            m_new = jnp.maximum(f_last + m_st, m_loc)
            a_prev = jnp.exp(f_last + m_st - m_new)[0:1, :]
            a_loc = jnp.exp(m_loc - m_new)[0:1, :]
            a_prev = jnp.concatenate([a_prev, a_prev], axis=1)
            a_loc = jnp.concatenate([a_loc, a_loc], axis=1)
            new.append((a_prev * s_st + a_loc * cl_scr[r, c], m_new))
        return tuple(new)

    init = (jnp.zeros((HEAD_DIM, 2 * HEAD_DIM), F32), jnp.zeros((8, HEAD_DIM), F32))
    lax.fori_loop(0, n_chunks, step, (init, init), unroll=2)


def _mlstm(qkv, gates_row, bias, n_ctx):
    bsz, t, _ = qkv.shape
    n_chunks = t // CHUNK
    qkv_spec = lambda off: pl.BlockSpec((1, t, HEAD_DIM), lambda b, h: (b, 0, off + h))
    return pl.pallas_call(
        functools.partial(_mlstm_kernel, n_chunks=n_chunks, ctx_chunks=n_ctx // CHUNK,
                          group=math.gcd(9, n_chunks)),
        grid=(bsz, MLSTM_HEADS),
        in_specs=[pl.BlockSpec(memory_space=pltpu.SMEM),
                  qkv_spec(0), qkv_spec(MLSTM_HEADS), qkv_spec(2 * MLSTM_HEADS),
                  pl.BlockSpec((1, 4 * MLSTM_HEADS, n_chunks, CHUNK), lambda b, h: (b, 0, 0, 0))],
        out_specs=pl.BlockSpec((1, t, HEAD_DIM), lambda b, h: (b, 0, h)),
        out_shape=jax.ShapeDtypeStruct((bsz, t, MLSTM_WIDTH), F32),
        scratch_shapes=[pltpu.VMEM((2, n_chunks, CHUNK, 2 * HEAD_DIM), F32),
                        pltpu.VMEM((2, n_chunks, HEAD_DIM, 2 * HEAD_DIM), F32),
                        pltpu.VMEM((2, n_chunks, CHUNK, HEAD_DIM), F32),
                        pltpu.VMEM((2, n_chunks, CHUNK, HEAD_DIM), F32),
                        pltpu.VMEM((2, n_chunks, 16, HEAD_DIM), F32)],
        compiler_params=_cparams(("parallel", "parallel")),
        name="mlstm",
    )(bias, qkv, qkv, qkv, gates_row)


def _s5_prep_kernel(lr_ref, li_ref, ldt_ref, br_ref, bi_ref, cr_ref, ci_ref, bd_ref, cdt_ref, a_ref):
    lr, li = lr_ref[0, 0], li_ref[0, 0]
    dt = jnp.exp(ldt_ref[0, 0])
    mag, ang = jnp.exp(lr * dt), li * dt
    ab_re, ab_im = mag * jnp.cos(ang), mag * jnp.sin(ang)
    nr, ni = ab_re - 1.0, ab_im
    den = lr * lr + li * li
    co_re = (nr * lr + ni * li) / den
    co_im = (ni * lr - nr * li) / den
    b_re, b_im = br_ref[0, 0], bi_ref[0, 0]
    bb_re = co_re * b_re - co_im * b_im
    bb_im = co_re * b_im + co_im * b_re
    c_re, c_im = cr_ref[0, 0], ci_ref[0, 0]
    lane_group = lax.broadcasted_iota(jnp.int32, (S5_GROUP, S5_HALF_N), 1) // S5_STATE
    n = S5_HALF_N
    for g in range(S5_GROUPS // S5_HALVES):
        sel = lane_group == g
        rows = slice(g * S5_GROUP, (g + 1) * S5_GROUP)
        bd_ref[0, 0, rows, :n] = jnp.where(sel, bb_re, 0.0).astype(bd_ref.dtype)
        bd_ref[0, 0, rows, n:] = jnp.where(sel, bb_im, 0.0).astype(bd_ref.dtype)
        cdt_ref[0, 0, rows, :n] = jnp.where(sel, c_re, 0.0).astype(cdt_ref.dtype)
        cdt_ref[0, 0, rows, n:] = jnp.where(sel, -c_im, 0.0).astype(cdt_ref.dtype)
    a_ref[0, 0, :, :n] = jnp.broadcast_to(ab_re, (8, n))
    a_ref[0, 0, :, n:] = jnp.broadcast_to(ab_im, (8, n))


def _s5_prep(lam_re, lam_im, log_dt, b_re, b_im, c_re, c_im):
    gh = S5_GROUPS // S5_HALVES
    vec = lambda a: a.reshape(2, S5_HALVES, 1, S5_HALF_N)
    ldt = vec(jnp.broadcast_to(log_dt[:, :, None], (2, S5_GROUPS, S5_STATE)))
    bt = lambda a: a.reshape(2, S5_HALVES, gh, S5_STATE, S5_GROUP).transpose(0, 1, 4, 2, 3).reshape(
        2, S5_HALVES, S5_GROUP, S5_HALF_N)
    ct = lambda a: a.reshape(2, S5_HALVES, gh, S5_GROUP, S5_STATE).transpose(0, 1, 3, 2, 4).reshape(
        2, S5_HALVES, S5_GROUP, S5_HALF_N)
    vspec = pl.BlockSpec((1, 1, 1, S5_HALF_N), lambda r, h: (r, h, 0, 0))
    mspec = pl.BlockSpec((1, 1, S5_GROUP, S5_HALF_N), lambda r, h: (r, h, 0, 0))
    ospec = pl.BlockSpec((1, 1, S5_HALF_W, 2 * S5_HALF_N), lambda r, h: (r, h, 0, 0))
    return pl.pallas_call(
        _s5_prep_kernel,
        grid=(2, S5_HALVES),
        in_specs=[vspec, vspec, vspec, mspec, mspec, mspec, mspec],
        out_specs=[ospec, ospec, pl.BlockSpec((1, 1, 8, 2 * S5_HALF_N), lambda r, h: (r, h, 0, 0))],
        out_shape=[jax.ShapeDtypeStruct((2, S5_HALVES, S5_HALF_W, 2 * S5_HALF_N), BF16),
                   jax.ShapeDtypeStruct((2, S5_HALVES, S5_HALF_W, 2 * S5_HALF_N), BF16),
                   jax.ShapeDtypeStruct((2, S5_HALVES, 8, 2 * S5_HALF_N), F32)],
        compiler_params=_cparams(("parallel", "parallel")),
        name="s5_prep",
    )(vec(lam_re), vec(lam_im), ldt, bt(b_re), bt(b_im), ct(c_re), ct(c_im))


def _s5_kernel(uf0_ref, uf1_ref, ub0_ref, ub1_ref, bd_ref, cdt_ref, a_ref, yf_ref, yb_ref,
               lhs_scr, bu_scr, s_scr, st_scr, *, tb, bsz):
    n = S5_HALF_N
    rows = tb * bsz
    u_refs = ((uf0_ref, uf1_ref), (ub0_ref, ub1_ref))
    y_refs = (yf_ref, yb_ref)

    @pl.when(pl.program_id(1) == 0)
    def _():
        st_scr[...] = jnp.zeros_like(st_scr)

    for d in (0, 1):
        for j in (0, 1):
            lhs_scr[d, :, j * LANES:(j + 1) * LANES] = jnp.transpose(
                u_refs[d][j][...], (1, 0, 2)).reshape(rows, LANES)
    for d in (0, 1):
        bu_scr[d] = _dot(lhs_scr[d], bd_ref[d, 0])
    for d in (0, 1):
        a_re, a_im = a_ref[d, 0, :, :n], a_ref[d, 0, :, n:]
        s_re, s_im = st_scr[d, :, :n], st_scr[d, :, n:]
        for j in range(tb):
            t = j if d == 0 else tb - 1 - j
            sl = slice(t * bsz, (t + 1) * bsz)
            s_re, s_im = (a_re * s_re - a_im * s_im + bu_scr[d, sl, :n],
                          a_re * s_im + a_im * s_re + bu_scr[d, sl, n:])
            s_scr[d, sl, :n] = s_re
            s_scr[d, sl, n:] = s_im
        st_scr[d, :, :n] = s_re
        st_scr[d, :, n:] = s_im
        y = _dot_nt(s_scr[d], cdt_ref[d, 0])
        for j in (0, 1):
            y_refs[d][:, :, j * LANES:(j + 1) * LANES] = jnp.transpose(
                y[:, j * LANES:(j + 1) * LANES].reshape(tb, bsz, LANES), (1, 0, 2))


def _s5(u, bd, cdt, avec, n_ctx, tb):
    bsz, t, _ = u.shape
    nb, ncb = t // tb, n_ctx // tb
    assert bsz == 8 and S5_HALF_W == 2 * LANES

    def rev(i):
        return jnp.where(i < ncb, ncb - 1 - i, nb + ncb - 1 - i)

    fwd = lambda i: i
    uspec = lambda order, j: pl.BlockSpec((bsz, tb, LANES), lambda h, i: (0, order(i), 2 * h + j))
    yspec = lambda order: pl.BlockSpec((bsz, tb, S5_HALF_W), lambda h, i: (0, order(i), h))
    wspec = pl.BlockSpec((2, 1, S5_HALF_W, 2 * S5_HALF_N), lambda h, i: (0, h, 0, 0))
    rows = tb * bsz
    return pl.pallas_call(
        functools.partial(_s5_kernel, tb=tb, bsz=bsz),
        grid=(S5_HALVES, nb),
        in_specs=[uspec(fwd, 0), uspec(fwd, 1), uspec(rev, 0), uspec(rev, 1), wspec, wspec,
                  pl.BlockSpec((2, 1, 8, 2 * S5_HALF_N), lambda h, i: (0, h, 0, 0))],
        out_specs=[yspec(fwd), yspec(rev)],
        out_shape=[jax.ShapeDtypeStruct((bsz, t, S5_WIDTH), F32)] * 2,
        scratch_shapes=[pltpu.VMEM((2, rows, S5_HALF_W), F32),
                        pltpu.VMEM((2, rows, 2 * S5_HALF_N), F32),
                        pltpu.VMEM((2, rows, 2 * S5_HALF_N), F32),
                        pltpu.VMEM((2, bsz, 2 * S5_HALF_N), F32)],
        compiler_params=_cparams(("parallel", "arbitrary")),
        name="s5_scan",
    )(u, u, u, u, bd, cdt, avec)


def _even_finish_kernel(x_ref, ha_ref, op_ref, yf_ref, yb_ref, u_ref, mhw_ref, ds_ref, wg_ref, wo_ref,
                        gc_ref, gl_ref, o_ref, *, n_top):
    ha = ha_ref[0]
    parts = []
    for h in range(MLSTM_HEADS):
        sl = slice(h * HEAD_DIM, (h + 1) * HEAD_DIM)
        parts.append(_rms(ha[:, sl], mhw_ref[:, sl]))
    a_out = jnp.concatenate(parts, axis=-1) * jax.nn.sigmoid(op_ref[0])
    yb = jax.nn.gelu(yf_ref[0] + yb_ref[0] + ds_ref[...] * u_ref[0])
    glu = _dot(yb, wg_ref[...])
    b_out = glu[:, :S5_WIDTH] * jax.nn.sigmoid(glu[:, S5_WIDTH:])
    y = _dot(a_out, wo_ref[:MLSTM_WIDTH, :]) + _dot(b_out, wo_ref[MLSTM_WIDTH:, :])
    for rows, (gate,) in _row_groups(n_top, y.shape[0], (gc_ref,), (gl_ref,)):
        o_ref[0, rows, :] = x_ref[0, rows, :] + gate * y[rows, :]


def _even_finish(xc, ha, o_pre, yf, yb, u, mods, layer, mh_w, d_skip, wg, wo, tm, n_ctx):
    bsz, t, d = xc.shape
    assert t % tm == 0 and n_ctx <= tm
    full = lambda shape: pl.BlockSpec(shape, lambda b, i: tuple(0 for _ in shape),
                                      pipeline_mode=pl.Buffered(1))
    return pl.pallas_call(
        functools.partial(_even_finish_kernel, n_top=n_ctx),
        grid=(bsz, t // tm),
        in_specs=[pl.BlockSpec((1, tm, d), lambda b, i: (b, i, 0)),
                  pl.BlockSpec((1, tm, MLSTM_WIDTH), lambda b, i: (b, i, 0)),
                  pl.BlockSpec((1, tm, MLSTM_WIDTH), lambda b, i: (b, i, 0)),
                  pl.BlockSpec((1, tm, S5_WIDTH), lambda b, i: (b, i, 0)),
                  pl.BlockSpec((1, tm, S5_WIDTH), lambda b, i: (b, i, 0)),
                  pl.BlockSpec((1, tm, S5_WIDTH), lambda b, i: (b, i, 0)),
                  full((1, MLSTM_WIDTH)), full((1, S5_WIDTH)),
                  full((S5_WIDTH, 2 * S5_WIDTH)), full((d, d)),
                  _ctx_mod_spec(layer, 2, bsz), _lat_mod_spec(layer, 2)],
        out_specs=pl.BlockSpec((1, tm, d), lambda b, i: (b, i, 0)),
        out_shape=jax.ShapeDtypeStruct((bsz, t, d), F32),
        compiler_params=_cparams(("parallel", "parallel")),
        name="even_finish",
    )(xc, ha, o_pre, yf, yb, u, mh_w.reshape(1, -1), d_skip.reshape(1, -1), wg, wo, mods, mods)


def _conv_kernel(x_ref, w_ref, o_ref, *, n_ctx, seq):
    j = pl.program_id(1)
    w = w_ref[...]
    wrow = lambda k: w[k:k + 1, :]
    n_rows = seq // GRID_W
    tpos = lax.broadcasted_iota(jnp.int32, (GRID_W, LANES), 0)
    head_blocks = D_MODEL // LANES
    is_qk = j < 2 * head_blocks
    post_scale = jnp.where(j < head_blocks, 1.0 / math.sqrt(HEAD_DIM), 1.0)

    def finish(y):
        y = y * jax.nn.sigmoid(y)
        nrm = y * lax.rsqrt(jnp.sum(y * y, axis=-1, keepdims=True) + EPS) * post_scale
        return jnp.where(is_qk, nrm, y)

    def shifted(r0, r1, r2, first_mask, last_mask):
        prev = jnp.where(first_mask, 0.0, pltpu.roll(r0, 1, axis=0))
        nxt = jnp.where(last_mask, 0.0, pltpu.roll(r2, GRID_W - 1, axis=0))
        return r1 + prev + nxt

    first = tpos == 0
    last = tpos == GRID_W - 1

    def lat_row(r, _):
        base = n_ctx + r * GRID_W
        cur = x_ref[0, pl.ds(pl.multiple_of(base, GRID_W), GRID_W), :].astype(F32)
        up_i = jnp.maximum(r - 1, 0)
        dn_i = jnp.minimum(r + 1, n_rows - 1)
        up = x_ref[0, pl.ds(pl.multiple_of(n_ctx + up_i * GRID_W, GRID_W), GRID_W), :].astype(F32)
        dn = x_ref[0, pl.ds(pl.multiple_of(n_ctx + dn_i * GRID_W, GRID_W), GRID_W), :].astype(F32)
        w_up = jnp.where(r > 0, w[0:3, :], 0.0)
        w_dn = jnp.where(r < n_rows - 1, w[6:9, :], 0.0)
        rs = [w_up[dc:dc + 1, :] * up + wrow(3 + dc) * cur + w_dn[dc:dc + 1, :] * dn for dc in range(3)]
        o_ref[0, pl.ds(pl.multiple_of(base, GRID_W), GRID_W), :] = finish(
            shifted(rs[0], rs[1], rs[2], first, last)).astype(o_ref.dtype)
        return 0

    lax.fori_loop(0, n_rows, lat_row, 0, unroll=math.gcd(4, n_rows))

    n_seg = n_ctx // GRID_W

    def ctx_seg(s, _):
        base = pl.multiple_of(s * GRID_W, GRID_W)
        cur = x_ref[0, pl.ds(base, GRID_W), :].astype(F32)
        pb = jnp.maximum(s * GRID_W - 16, 0)
        prev_blk = x_ref[0, pl.ds(pl.multiple_of(pb, 16), 16), :].astype(F32)
        nb = jnp.minimum(s * GRID_W + GRID_W, n_ctx - 16)
        next_blk = x_ref[0, pl.ds(pl.multiple_of(nb, 16), 16), :].astype(F32)
        prev_tok = jnp.where(s > 0, prev_blk[15:16, :], 0.0)
        next_tok = jnp.where(s < n_seg - 1, next_blk[0:1, :], 0.0)
        left = jnp.where(first, prev_tok, pltpu.roll(cur, 1, axis=0))
        right = jnp.where(last, next_tok, pltpu.roll(cur, GRID_W - 1, axis=0))
        y = wrow(3) * left + wrow(4) * cur + wrow(5) * right
        o_ref[0, pl.ds(base, GRID_W), :] = finish(y).astype(o_ref.dtype)
        return 0

    lax.fori_loop(0, n_seg, ctx_seg, 0, unroll=math.gcd(4, n_seg))


def _gdn_conv(qkv_pre, conv_w, n_ctx):
    bsz, t, _ = qkv_pre.shape
    return pl.pallas_call(
        functools.partial(_conv_kernel, n_ctx=n_ctx, seq=t - n_ctx),
        grid=(bsz, GDN_QKV // LANES),
        in_specs=[pl.BlockSpec((1, t, LANES), lambda b, j: (b, 0, j)),
                  pl.BlockSpec((9, LANES), lambda b, j: (0, j))],
        out_specs=pl.BlockSpec((1, t, LANES), lambda b, j: (b, 0, j)),
        out_shape=jax.ShapeDtypeStruct((bsz, t, GDN_QKV), BF16),
        compiler_params=_cparams(("parallel", "parallel")),
        name="gdn_conv",
    )(qkv_pre, conv_w.reshape(9, GDN_QKV))


def _unit_tri_inverses(mats, eye_f):
    ps = [eye_f - a for a in mats]
    pws = [_dot(a, a) for a in mats]
    k = 2
    while 2 * k < CHUNK:
        res = [_dot(jnp.concatenate([p, pw], axis=0), pw) for p, pw in zip(ps, pws)]
        ps = [p + r[:CHUNK] for p, r in zip(ps, res)]
        pws = [r[CHUNK:] for r in res]
        k *= 2
    return [p + _dot(p, pw) for p, pw in zip(ps, pws)]


def _gdn_kernel(par_ref, q_ref, k_ref, v_ref, g_ref, o_ref, qk_scr, ol_scr, nl_scr, gl_scr, *,
                n_chunks, ctx_chunks, group, heads):
    row, col = _chunk_masks()
    eye = row == col
    eye_f = eye.astype(F32)

    def gate_terms(hd, r, a_pre, b_pre):
        incl = (col >= row) if r == 1 else (col <= row)
        a_log = jnp.full((1, CHUNK), par_ref[r * GDN_HEADS + hd], F32)
        dt_b = par_ref[(2 + r) * GDN_HEADS + hd]
        g_row = -jnp.exp(a_log) * jax.nn.softplus(a_pre + dt_b)
        beta_col = _to_col(jax.nn.sigmoid(b_pre), eye)
        gc_col = jnp.sum(jnp.where(incl, jnp.broadcast_to(g_row, incl.shape), 0.0),
                         axis=1, keepdims=True)
        gc_row = _to_row(gc_col, eye)
        g_last = jnp.sum(g_row, axis=1, keepdims=True)
        gamma = jnp.exp(jnp.where(incl, gc_col - gc_row, -jnp.inf))
        return beta_col, gc_col, g_last, gamma

    for hh in range(heads):
        hd = pl.program_id(1) * heads + hh
        lanes = slice(hh * HEAD_DIM, (hh + 1) * HEAD_DIM)

        def load(c, hd=hd, lanes=lanes):
            rows = pl.ds(pl.multiple_of(c * CHUNK, CHUNK), CHUNK)
            gates = [(g_ref[0, r * GDN_HEADS + hd, pl.ds(c, 1), :],
                      g_ref[0, (2 + r) * GDN_HEADS + hd, pl.ds(c, 1), :]) for r in (0, 1)]
            return (q_ref[0, rows, lanes].astype(F32), k_ref[0, rows, lanes].astype(F32),
                    v_ref[0, rows, lanes].astype(F32), gates)

        def local_group(gi, _, hh=hh, hd=hd, load=load):
            chunks = [gi * group + g for g in range(group)]
            loaded = [load(c) for c in chunks]
            kqs = [_dot_nt(jnp.concatenate([k, q], axis=0), k) for q, k, _, _ in loaded]
            chains = [(g, r) for g in range(group) for r in (0, 1)]
            terms = [gate_terms(hd, r, *loaded[g][3][r]) for g, r in chains]
            a_mats = []
            for (g, r), (beta_col, _, _, gamma) in zip(chains, terms):
                strict = (col > row) if r == 1 else (col < row)
                a_mats.append(jnp.where(strict, beta_col * kqs[g][:CHUNK] * gamma, 0.0))
            t_invs = _unit_tri_inverses(a_mats, eye_f)
            sols = []
            for (g, r), (beta_col, gc_col, _, _), t_inv in zip(chains, terms, t_invs):
                _, k, v, _ = loaded[g]
                rhs = jnp.concatenate([v * beta_col, k * (beta_col * jnp.exp(gc_col))], axis=-1)
                sols.append(_dot(t_inv, rhs))
            auws = [_dot(kqs[g][CHUNK:] * tm[3], sol) for (g, r), tm, sol in zip(chains, terms, sols)]
            kuws = [_dot_tn(loaded[g][1] * jnp.exp(tm[2] - tm[1]), sol)
                    for (g, r), tm, sol in zip(chains, terms, sols)]
            for (g, r), (_, gc_col, g_last, _), auw, kuw in zip(chains, terms, auws, kuws):
                c = chunks[g]
                q = loaded[g][0]
                qk_scr[hh, r, c, :CHUNK, :] = (q * jnp.exp(gc_col) - auw[:, HEAD_DIM:]).astype(qk_scr.dtype)
                qk_scr[hh, r, c, CHUNK:, :] = kuw[:, HEAD_DIM:].astype(qk_scr.dtype)
                ol_scr[hh, r, c] = auw[:, :HEAD_DIM]
                nl_scr[hh, r, c] = kuw[:, :HEAD_DIM]
                gl_scr[hh, r, c] = jnp.broadcast_to(jnp.exp(g_last), (8, HEAD_DIM))
            return 0

        lax.fori_loop(0, n_chunks // group, local_group, 0)

    o_ref[...] = jnp.zeros_like(o_ref)
    streams = [(hh, r) for hh in range(heads) for r in (0, 1)]

    def step(i, carry):
        cs = [_scan_chunk_index(i, r == 1, n_chunks, ctx_chunks) for _, r in streams]
        res = [jnp.dot(qk_scr[hh, r, c], s.astype(BF16), preferred_element_type=F32)
               for (hh, r), c, s in zip(streams, cs, carry)]
        new = []
        for (hh, r), c, s, rs in zip(streams, cs, carry, res):
            rows = pl.ds(pl.multiple_of(c * CHUNK, CHUNK), CHUNK)
            o_ref[0, rows, hh * HEAD_DIM:(hh + 1) * HEAD_DIM] += ol_scr[hh, r, c] + rs[:CHUNK]
            new.append(gl_scr[hh, r, c][0:1, :] * s - rs[CHUNK:] + nl_scr[hh, r, c])
        return tuple(new)

    init = jnp.zeros((HEAD_DIM, HEAD_DIM), F32)
    lax.fori_loop(0, n_chunks, step, (init,) * len(streams), unroll=2)


def _gdn(qkv, gates_row, par, n_ctx, heads=2):
    bsz, t, _ = qkv.shape
    n_chunks = t // CHUNK
    group = math.gcd(9, n_chunks)
    wd = heads * HEAD_DIM
    blocks = GDN_HEADS // heads
    spec = lambda off: pl.BlockSpec((1, t, wd), lambda b, h: (b, 0, off + h))
    return pl.pallas_call(
        functools.partial(_gdn_kernel, n_chunks=n_chunks, ctx_chunks=n_ctx // CHUNK, group=group,
                          heads=heads),
        grid=(bsz, blocks),
        in_specs=[pl.BlockSpec(memory_space=pltpu.SMEM),
                  spec(0), spec(blocks), spec(2 * blocks),
                  pl.BlockSpec((1, 4 * GDN_HEADS, n_chunks, CHUNK), lambda b, h: (b, 0, 0, 0))],
        out_specs=pl.BlockSpec((1, t, wd), lambda b, h: (b, 0, h)),
        out_shape=jax.ShapeDtypeStruct((bsz, t, D_MODEL), F32),
        scratch_shapes=[pltpu.VMEM((heads, 2, n_chunks, CHUNK + HEAD_DIM, HEAD_DIM), BF16),
                        pltpu.VMEM((heads, 2, n_chunks, CHUNK, HEAD_DIM), F32),
                        pltpu.VMEM((heads, 2, n_chunks, HEAD_DIM, HEAD_DIM), F32),
                        pltpu.VMEM((heads, 2, n_chunks, 8, HEAD_DIM), F32)],
        compiler_params=_cparams(("parallel", "parallel")),
        name="gdn",
    )(par, qkv, qkv, qkv, gates_row)


def _odd_finish_kernel(x_ref, o_in_ref, z_ref, hw_ref, wo_ref, g_ref, o_ref):
    o = o_in_ref[0]
    parts = []
    for h in range(GDN_HEADS):
        sl = slice(h * HEAD_DIM, (h + 1) * HEAD_DIM)
        parts.append(_rms(o[:, sl], hw_ref[:, sl]))
    z = z_ref[0]
    y = jnp.concatenate(parts, axis=-1) * (z * jax.nn.sigmoid(z))
    o_ref[0] = x_ref[0] + g_ref[0, 0] * _dot(y, wo_ref[...])


def _odd_finish(xc, o, z, mods, layer, hw, wo, tm, n_ctx):
    bsz, t, d = xc.shape
    assert n_ctx % tm == 0 and (t - n_ctx) % tm == 0
    t0 = n_ctx // tm
    return pl.pallas_call(
        _odd_finish_kernel,
        grid=(bsz, (t - n_ctx) // tm),
        in_specs=[pl.BlockSpec((1, tm, d), lambda b, i: (b, i + t0, 0)),
                  pl.BlockSpec((1, tm, d), lambda b, i: (b, i + t0, 0)),
                  pl.BlockSpec((1, tm, d), lambda b, i: (b, i + t0, 0)),
                  pl.BlockSpec((1, d), lambda b, i: (0, 0)),
                  pl.BlockSpec((d, d), lambda b, i: (0, 0)),
                  _lat_mod_spec(layer, 2)],
        out_specs=pl.BlockSpec((1, tm, d), lambda b, i: (b, i, 0)),
        out_shape=jax.ShapeDtypeStruct((bsz, t - n_ctx, d), F32),
        compiler_params=_cparams(("parallel", "parallel")),
        name="odd_finish",
    )(xc, o, z, hw.reshape(1, d), wo, mods)


def _chunked(gates_t):
    bsz, g, t = gates_t.shape
    return gates_t.reshape(bsz, g, t // CHUNK, CHUNK)


def _pick_tile(total, target, at_least):
    for tile in range(min(target, total), 7, -1):
        if total % tile == 0 and tile % 8 == 0 and tile >= at_least:
            return tile
    raise ValueError(f"no row tile for {total} rows")


def _pad_cols(w, n):
    return jnp.pad(w, ((0, 0), (0, n - w.shape[1])))


def kernel(x, c, ctx, c_ctx, ada_w, ada_b, norm1_w, norm2_w, ffn_w1, ffn_w3, ffn_w2, final_norm_w,
           ev_w_in, ev_i_bias, ev_f_bias, ev_head_norm_w, ev_lam_re, ev_lam_im, ev_log_dt,
           ev_b_re, ev_b_im, ev_c_re, ev_c_im, ev_d, ev_w_glu, ev_w_out,
           od_w_in, od_conv_w, od_a_log, od_dt_bias, od_head_norm_w, od_w_out):
    bsz, seq, d = x.shape
    n_ctx = ctx.shape[1]
    assert d == D_MODEL and seq % GRID_W == 0 and n_ctx % CHUNK == 0 and bsz % 8 == 0
    assert ada_w.shape[0] == 2
    t_all = n_ctx + seq
    tm = _pick_tile(t_all, 768, n_ctx)
    assert tm % LANES == 0
    tm_lat = _pick_tile(math.gcd(seq, n_ctx), 512, 0)
    th = FFN_HIDDEN // 2
    tb = CHUNK

    xc = jnp.concatenate([ctx, x], axis=1)
    mod_rows = 16
    cvec = jnp.zeros((mod_rows, d), F32).at[:bsz].set(c).at[bsz].set(c_ctx)
    mods = _ada(cvec, ada_w, ada_b).reshape(2, mod_rows, 1, 6 * d)
    w1, w3, w2 = ffn_w1.astype(BF16), ffn_w3.astype(BF16), ffn_w2.astype(BF16)

    qkv_w, n_g = 3 * MLSTM_WIDTH, 4 * MLSTM_HEADS
    w_in = ev_w_in[0]
    g0 = qkv_w + MLSTM_WIDTH
    w_even = jnp.concatenate([w_in[:, :g0], w_in[:, g0 + n_g:], _pad_cols(w_in[:, g0:g0 + n_g], LANES)],
                             axis=1).astype(BF16)
    c1, c2, c3 = qkv_w, g0, g0 + S5_WIDTH
    qkv, o_pre, u, gates = _inproj(
        xc, norm1_w[0], mods, 0, w_even,
        [(0, c1, "row", BF16, c1), (c1, c2, "row", F32, c2 - c1), (c2, c3, "row", F32, S5_WIDTH),
         (c3, c3 + LANES, "gate", F32, n_g)], tm, n_ctx)
    bias = jnp.concatenate([ev_i_bias[0].reshape(-1), ev_f_bias[0].reshape(-1)]).astype(F32)
    ha = _mlstm(qkv, _chunked(gates), bias, n_ctx)
    bd, cdt, avec = _s5_prep(ev_lam_re[0], ev_lam_im[0], ev_log_dt[0], ev_b_re[0], ev_b_im[0],
                             ev_c_re[0], ev_c_im[0])
    yf, yb = _s5(u, bd, cdt, avec, n_ctx, tb)
    xc = _even_finish(xc, ha, o_pre, yf, yb, u, mods, 0, ev_head_norm_w[0], ev_d[0],
                      ev_w_glu[0].astype(BF16), ev_w_out[0].astype(BF16), tm, n_ctx)
    xc = _ffn(xc, norm2_w[0], mods, 0, w1[0], w3[0], w2[0], final_norm_w, tm, th, n_ctx, False)

    n_g = 4 * GDN_HEADS
    w_odd = _pad_cols(od_w_in[0], 4 * d + LANES).astype(BF16)
    qkv_pre, z, gates = _inproj(
        xc, norm1_w[1], mods, 1, w_odd,
        [(0, GDN_QKV, "row", BF16, GDN_QKV), (GDN_QKV, 4 * d, "row", F32, d),
         (4 * d, 4 * d + LANES, "gate", F32, n_g)], tm, n_ctx)
    qkv = _gdn_conv(qkv_pre, od_conv_w[0], n_ctx)
    par = jnp.concatenate([od_a_log[0].reshape(-1), od_dt_bias[0].reshape(-1)]).astype(F32)
    o = _gdn(qkv, _chunked(gates), par, n_ctx)
    xl = _odd_finish(xc, o, z, mods, 1, od_head_norm_w[0], od_w_out[0].astype(BF16), tm_lat, n_ctx)
    return _ffn(xl, norm2_w[1], mods, 1, w1[1], w3[1], w2[1], final_norm_w, _pick_tile(seq, 512, 0),
                th, 0, True)
```

```python
import functools
import math

import jax
import jax.numpy as jnp
from jax import lax
from jax.experimental import pallas as pl
from jax.experimental.pallas import tpu as pltpu

F32 = jnp.float32
BF16 = jnp.bfloat16

D_MODEL = 1024
CHUNK = 64
GRID_W = 64
EPS = 1e-6
LANES = 128
HEAD_DIM = 128
MLSTM_WIDTH = D_MODEL // 2
MLSTM_HEADS = MLSTM_WIDTH // HEAD_DIM
S5_WIDTH = D_MODEL - MLSTM_WIDTH
S5_GROUP = 16
S5_GROUPS = S5_WIDTH // S5_GROUP
S5_STATE = 64
S5_HALVES = 2
S5_HALF_W = S5_WIDTH // S5_HALVES
S5_HALF_N = (S5_GROUPS // S5_HALVES) * S5_STATE
GDN_HEADS = D_MODEL // HEAD_DIM
GDN_QKV = 3 * D_MODEL
FFN_HIDDEN = ((8 * D_MODEL + 3 * 256 - 1) // (3 * 256)) * 256
VMEM_LIMIT = 56 * 1024 * 1024

_NT = (((1,), (1,)), ((), ()))
_TN = (((0,), (0,)), ((), ()))


def _cparams(sem):
    return pltpu.CompilerParams(dimension_semantics=sem, vmem_limit_bytes=VMEM_LIMIT)


def _dot(a, b):
    return jnp.dot(a.astype(BF16), b.astype(BF16), preferred_element_type=F32)


def _dot_nt(a, b):
    return lax.dot_general(a.astype(BF16), b.astype(BF16), _NT, preferred_element_type=F32)


def _dot_tn(a, b):
    return lax.dot_general(a.astype(BF16), b.astype(BF16), _TN, preferred_element_type=F32)


def _rms(x, w):
    return x * lax.rsqrt(jnp.mean(x * x, axis=-1, keepdims=True) + EPS) * w


def _ada_kernel(s_ref, w_ref, b_ref, o_ref):
    s = s_ref[...]
    s = s * jax.nn.sigmoid(s)
    o_ref[0] = _dot(s, w_ref[0]) + b_ref[0]


def _ada(cvec, ada_w, ada_b):
    depth, d, n = ada_w.shape
    tn = 1536
    rows = cvec.shape[0]
    return pl.pallas_call(
        _ada_kernel,
        grid=(depth, n // tn),
        in_specs=[pl.BlockSpec((rows, d), lambda l, j: (0, 0)),
                  pl.BlockSpec((1, d, tn), lambda l, j: (l, 0, j)),
                  pl.BlockSpec((1, 1, tn), lambda l, j: (l, 0, j))],
        out_specs=pl.BlockSpec((1, rows, tn), lambda l, j: (l, 0, j)),
        out_shape=jax.ShapeDtypeStruct((depth, rows, n), F32),
        compiler_params=_cparams(("parallel", "parallel")),
        name="ada_mod",
    )(cvec, ada_w, ada_b.reshape(depth, 1, n))


def _ctx_mod_spec(layer, chunk, ctx_row):
    return pl.BlockSpec((1, 1, 1, D_MODEL), lambda b, i: (layer, ctx_row, 0, chunk))


def _lat_mod_spec(layer, chunk):
    return pl.BlockSpec((1, 1, 1, D_MODEL), lambda b, i: (layer, b, 0, chunk))


def _row_groups(n_top, tm, ctx_refs, lat_refs):
    lat = [r[0, 0] for r in lat_refs]
    if n_top == 0:
        return [(slice(0, tm), lat)]
    first = pl.program_id(1) == 0
    top = [jnp.where(first, c[0, 0], l) for c, l in zip(ctx_refs, lat)]
    return [(slice(0, n_top), top), (slice(n_top, tm), lat)]


def _inproj_kernel(x_ref, nw_ref, shc_ref, scc_ref, shl_ref, scl_ref, w_ref, *rest, outs, n_top,
                   n_chunk):
    out_refs, h_scr, g_scr = rest[:-2], rest[-2], rest[-1]
    tm = h_scr.shape[0]
    for rows, (sh, sc) in _row_groups(n_top, tm, (shc_ref, scc_ref), (shl_ref, scl_ref)):
        h_scr[rows, :] = (_rms(x_ref[0, rows, :], nw_ref[...]) * (1.0 + sc) + sh).astype(BF16)
    for o_ref, (c0, c1, kind) in zip(out_refs, outs):
        for n0 in range(c0, c1, n_chunk):
            n1 = min(n0 + n_chunk, c1)
            val = jnp.dot(h_scr[...], w_ref[:, n0:n1], preferred_element_type=F32)
            if kind == "row":
                o_ref[0, :, n0 - c0:n1 - c0] = val.astype(o_ref.dtype)
            else:
                g_scr[...] = val
                o_ref[0] = g_scr[...].T[:o_ref.shape[1], :].astype(o_ref.dtype)


def _inproj(xc, norm_w, mods, layer, w_bf16, outs, tm, n_ctx):
    bsz, t, d = xc.shape
    n = w_bf16.shape[1]
    assert t % tm == 0 and (n_ctx == 0 or n_ctx <= tm)
    out_specs, out_shapes = [], []
    for c0, c1, kind, dtype, wd in outs:
        if kind == "row":
            out_specs.append(pl.BlockSpec((1, tm, wd), lambda b, i: (b, i, 0)))
            out_shapes.append(jax.ShapeDtypeStruct((bsz, t, wd), dtype))
        else:
            assert c1 - c0 == LANES and tm % LANES == 0
            out_specs.append(pl.BlockSpec((1, wd, tm), lambda b, i: (b, 0, i)))
            out_shapes.append(jax.ShapeDtypeStruct((bsz, wd, t), dtype))
    ctx_spec = lambda c: _ctx_mod_spec(layer, c, bsz)
    lat_spec = lambda c: _lat_mod_spec(layer, c)
    const = lambda shape: pl.BlockSpec(shape, lambda b, i: (0, 0), pipeline_mode=pl.Buffered(1))
    return pl.pallas_call(
        functools.partial(_inproj_kernel, outs=tuple(o[:3] for o in outs), n_top=n_ctx, n_chunk=512),
        grid=(bsz, t // tm),
        in_specs=[pl.BlockSpec((1, tm, d), lambda b, i: (b, i, 0)),
                  const((1, d)), ctx_spec(0), ctx_spec(1), lat_spec(0), lat_spec(1), const((d, n))],
        out_specs=out_specs,
        out_shape=out_shapes,
        scratch_shapes=[pltpu.VMEM((tm, d), BF16), pltpu.VMEM((tm, LANES), F32)],
        compiler_params=_cparams(("parallel", "parallel")),
        name=f"inproj_l{layer}",
    )(xc, norm_w.reshape(1, d), mods, mods, mods, mods, w_bf16)


def _ffn_kernel(x_ref, nw_ref, shc_ref, scc_ref, gc_ref, shl_ref, scl_ref, gl_ref,
                w1_ref, w3_ref, w2_ref, fw_ref, o_ref, h_scr, *, final, n_top, th):
    tm, hid = h_scr.shape[0], w1_ref.shape[1]
    groups = _row_groups(n_top, tm, (shc_ref, scc_ref, gc_ref), (shl_ref, scl_ref, gl_ref))
    for rows, (sh, sc, _) in groups:
        h_scr[rows, :] = (_rms(x_ref[0, rows, :], nw_ref[...]) * (1.0 + sc) + sh).astype(BF16)
    acc = None
    for c0 in range(0, hid, th):
        h = h_scr[...]
        a = jnp.dot(h, w1_ref[:, c0:c0 + th], preferred_element_type=F32)
        g = jnp.dot(h, w3_ref[:, c0:c0 + th], preferred_element_type=F32)
        t = ((a * jax.nn.sigmoid(a)) * g).astype(BF16)
        part = jnp.dot(t, w2_ref[c0:c0 + th, :], preferred_element_type=F32)
        acc = part if acc is None else acc + part
    for rows, (_, _, gate) in groups:
        y = x_ref[0, rows, :] + gate * acc[rows, :]
        if final:
            y = _rms(y, fw_ref[...])
        o_ref[0, rows, :] = y


def _ffn(x, norm_w, mods, layer, w1, w3, w2, final_w, tm, th, n_ctx, final):
    bsz, tx, d = x.shape
    hid = w1.shape[1]
    assert n_ctx <= tm and tx % tm == 0 and hid % th == 0
    ctx_spec = lambda c: _ctx_mod_spec(layer, c, bsz)
    lat_spec = lambda c: _lat_mod_spec(layer, c)
    const = lambda shape: pl.BlockSpec(shape, lambda b, i: (0, 0), pipeline_mode=pl.Buffered(1))
    return pl.pallas_call(
        functools.partial(_ffn_kernel, final=final, n_top=n_ctx, th=th),
        grid=(bsz, tx // tm),
        in_specs=[pl.BlockSpec((1, tm, d), lambda b, i: (b, i, 0)),
                  const((1, d)),
                  ctx_spec(3), ctx_spec(4), ctx_spec(5), lat_spec(3), lat_spec(4), lat_spec(5),
                  const((d, hid)), const((d, hid)), const((hid, d)), const((1, d))],
        out_specs=pl.BlockSpec((1, tm, d), lambda b, i: (b, i, 0)),
        out_shape=jax.ShapeDtypeStruct((bsz, tx, d), F32),
        scratch_shapes=[pltpu.VMEM((tm, d), BF16)],
        compiler_params=_cparams(("parallel", "parallel")),
        name=f"ffn_l{layer}",
    )(x, norm_w.reshape(1, d), mods, mods, mods, mods, mods, mods, w1, w3, w2, final_w.reshape(1, d))


def _chunk_masks():
    row = lax.broadcasted_iota(jnp.int32, (CHUNK, CHUNK), 0)
    col = lax.broadcasted_iota(jnp.int32, (CHUNK, CHUNK), 1)
    return row, col


def _to_col(row_vec, eye):
    return jnp.sum(jnp.where(eye, jnp.broadcast_to(row_vec, eye.shape), 0.0), axis=1, keepdims=True)


def _to_row(col_vec, eye):
    return jnp.sum(jnp.where(eye, jnp.broadcast_to(col_vec, eye.shape), 0.0), axis=0, keepdims=True)


def _scan_chunk_index(i, rev, n_chunks, ctx_chunks):
    if not rev:
        return i
    return jnp.where(i < ctx_chunks, ctx_chunks - 1 - i, n_chunks + ctx_chunks - 1 - i)


def _mlstm_kernel(bias_ref, q_ref, k_ref, v_ref, g_ref, o_ref, num_scr, cl_scr, mi_scr, fc_scr, sc_scr,
                  *, n_chunks, ctx_chunks, group):
    hd = pl.program_id(1)
    row, col = _chunk_masks()
    eye = row == col
    kscale = 1.0 / math.sqrt(HEAD_DIM)
    ones_blk = jnp.ones((CHUNK, HEAD_DIM), BF16)

    def gate_terms(r, ig_raw, f_raw):
        incl = (col >= row) if r == 1 else (col <= row)
        ig_row = ig_raw + bias_ref[r * MLSTM_HEADS + hd]
        lf_row = jax.nn.log_sigmoid(f_raw + bias_ref[(2 + r) * MLSTM_HEADS + hd])
        f_col = jnp.sum(jnp.where(incl, jnp.broadcast_to(lf_row, incl.shape), 0.0),
                        axis=1, keepdims=True)
        f_row = _to_row(f_col, eye)
        f_last = jnp.sum(lf_row, axis=1, keepdims=True)
        dm = jnp.where(incl, f_col - f_row + ig_row, -jnp.inf)
        m_intra = jnp.max(dm, axis=1, keepdims=True)
        w_col = f_last - f_col + _to_col(ig_row, eye)
        m_loc = jnp.max(w_col, axis=0, keepdims=True)
        return f_col, f_last, jnp.exp(dm - m_intra), m_intra, jnp.exp(w_col - m_loc), m_loc

    def local_group(gi, _):
        chunks = [gi * group + g for g in range(group)]
        loaded = []
        for c in chunks:
            rows = pl.ds(pl.multiple_of(c * CHUNK, CHUNK), CHUNK)
            gates = [(g_ref[0, r * MLSTM_HEADS + hd, pl.ds(c, 1), :],
                      g_ref[0, (2 + r) * MLSTM_HEADS + hd, pl.ds(c, 1), :]) for r in (0, 1)]
            v1 = jnp.concatenate([v_ref[0, rows, :].astype(BF16), ones_blk], axis=-1)
            loaded.append((q_ref[0, rows, :], k_ref[0, rows, :].astype(F32) * kscale, v1, gates))
        chains = [(g, r) for g in range(group) for r in (0, 1)]
        terms = [gate_terms(r, *loaded[g][3][r]) for g, r in chains]
        qks = [_dot_nt(q, k) for q, k, _, _ in loaded]
        c_locs = [_dot_tn(tm[4] * loaded[g][1], loaded[g][2]) for (g, r), tm in zip(chains, terms)]
        nums = [_dot(tm[2] * qks[g], loaded[g][2]) for (g, r), tm in zip(chains, terms)]
        for (g, r), tm, c_loc, num in zip(chains, terms, c_locs, nums):
            f_col, f_last, _, m_intra, _, m_loc = tm
            c = chunks[g]
            num_scr[r, c] = num
            cl_scr[r, c] = c_loc
            mi_scr[r, c] = jnp.broadcast_to(m_intra, (CHUNK, HEAD_DIM))
            fc_scr[r, c] = jnp.broadcast_to(f_col, (CHUNK, HEAD_DIM))
            sc_scr[r, c, :8, :] = jnp.broadcast_to(f_last, (8, HEAD_DIM))
            sc_scr[r, c, 8:, :] = jnp.broadcast_to(m_loc, (8, HEAD_DIM))
        return 0

    lax.fori_loop(0, n_chunks // group, local_group, 0)
    o_ref[...] = jnp.zeros_like(o_ref)

    def step(i, carry):
        cs = [_scan_chunk_index(i, r == 1, n_chunks, ctx_chunks) for r in (0, 1)]
        rows = [pl.ds(pl.multiple_of(c * CHUNK, CHUNK), CHUNK) for c in cs]
        qcs = [_dot(q_ref[0, rw, :], st[0]) for rw, st in zip(rows, carry)]
        new = []
        for r in (0, 1):
            s_st, m_st = carry[r]
            c = cs[r]
            mi, na = mi_scr[r, c], num_scr[r, c]
            inter = fc_scr[r, c] + m_st[0:1, :]
            m_t = jnp.maximum(mi, inter)
            a_loc_t, a_inter = jnp.exp(mi - m_t), jnp.exp(inter - m_t)
            num = a_loc_t * na[:, :HEAD_DIM] + a_inter * qcs[r][:, :HEAD_DIM]
            den = a_loc_t * na[:, HEAD_DIM:] + a_inter * qcs[r][:, HEAD_DIM:]
            o_ref[0, rows[r], :] += num / jnp.maximum(jnp.abs(den), jnp.exp(-m_t))
            f_last, m_loc = sc_scr[r, c, :8, :], sc_scr[r, c, 8:, :]
            m_new = jnp.maximum(f_last + m_st, m_loc)
            a_prev = jnp.exp(f_last + m_st - m_new)[0:1, :]
            a_loc = jnp.exp(m_loc - m_new)[0:1, :]
            a_prev = jnp.concatenate([a_prev, a_prev], axis=1)
            a_loc = jnp.concatenate([a_loc, a_loc], axis=1)
            new.append((a_prev * s_st + a_loc * cl_scr[r, c], m_new))
        return tuple(new)

    init = (jnp.zeros((HEAD_DIM, 2 * HEAD_DIM), F32), jnp.zeros((8, HEAD_DIM), F32))
    lax.fori_loop(0, n_chunks, step, (init, init), unroll=2)


def _mlstm(qkv, gates_row, bias, n_ctx):
    bsz, t, _ = qkv.shape
    n_chunks = t // CHUNK
    qkv_spec = lambda off: pl.BlockSpec((1, t, HEAD_DIM), lambda b, h: (b, 0, off + h))
    return pl.pallas_call(
        functools.partial(_mlstm_kernel, n_chunks=n_chunks, ctx_chunks=n_ctx // CHUNK,
                          group=math.gcd(9, n_chunks)),
        grid=(bsz, MLSTM_HEADS),
        in_specs=[pl.BlockSpec(memory_space=pltpu.SMEM),
                  qkv_spec(0), qkv_spec(MLSTM_HEADS), qkv_spec(2 * MLSTM_HEADS),
                  pl.BlockSpec((1, 4 * MLSTM_HEADS, n_chunks, CHUNK), lambda b, h: (b, 0, 0, 0))],
        out_specs=pl.BlockSpec((1, t, HEAD_DIM), lambda b, h: (b, 0, h)),
        out_shape=jax.ShapeDtypeStruct((bsz, t, MLSTM_WIDTH), F32),
        scratch_shapes=[pltpu.VMEM((2, n_chunks, CHUNK, 2 * HEAD_DIM), F32),
                        pltpu.VMEM((2, n_chunks, HEAD_DIM, 2 * HEAD_DIM), F32),
                        pltpu.VMEM((2, n_chunks, CHUNK, HEAD_DIM), F32),
                        pltpu.VMEM((2, n_chunks, CHUNK, HEAD_DIM), F32),
                        pltpu.VMEM((2, n_chunks, 16, HEAD_DIM), F32)],
        compiler_params=_cparams(("parallel", "parallel")),
        name="mlstm",
    )(bias, qkv, qkv, qkv, gates_row)


def _s5_prep_kernel(lr_ref, li_ref, ldt_ref, br_ref, bi_ref, cr_ref, ci_ref, bd_ref, cdt_ref, a_ref):
    lr, li = lr_ref[0, 0], li_ref[0, 0]
    dt = jnp.exp(ldt_ref[0, 0])
    mag, ang = jnp.exp(lr * dt), li * dt
    ab_re, ab_im = mag * jnp.cos(ang), mag * jnp.sin(ang)
    nr, ni = ab_re - 1.0, ab_im
    den = lr * lr + li * li
    co_re = (nr * lr + ni * li) / den
    co_im = (ni * lr - nr * li) / den
    b_re, b_im = br_ref[0, 0], bi_ref[0, 0]
    bb_re = co_re * b_re - co_im * b_im
    bb_im = co_re * b_im + co_im * b_re
    c_re, c_im = cr_ref[0, 0], ci_ref[0, 0]
    lane_group = lax.broadcasted_iota(jnp.int32, (S5_GROUP, S5_HALF_N), 1) // S5_STATE
    n = S5_HALF_N
    for g in range(S5_GROUPS // S5_HALVES):
        sel = lane_group == g
        rows = slice(g * S5_GROUP, (g + 1) * S5_GROUP)
        bd_ref[0, 0, rows, :n] = jnp.where(sel, bb_re, 0.0).astype(bd_ref.dtype)
        bd_ref[0, 0, rows, n:] = jnp.where(sel, bb_im, 0.0).astype(bd_ref.dtype)
        cdt_ref[0, 0, rows, :n] = jnp.where(sel, c_re, 0.0).astype(cdt_ref.dtype)
        cdt_ref[0, 0, rows, n:] = jnp.where(sel, -c_im, 0.0).astype(cdt_ref.dtype)
    a_ref[0, 0, :, :n] = jnp.broadcast_to(ab_re, (8, n))
    a_ref[0, 0, :, n:] = jnp.broadcast_to(ab_im, (8, n))


def _s5_prep(lam_re, lam_im, log_dt, b_re, b_im, c_re, c_im):
    gh = S5_GROUPS // S5_HALVES
    vec = lambda a: a.reshape(2, S5_HALVES, 1, S5_HALF_N)
    ldt = vec(jnp.broadcast_to(log_dt[:, :, None], (2, S5_GROUPS, S5_STATE)))
    bt = lambda a: a.reshape(2, S5_HALVES, gh, S5_STATE, S5_GROUP).transpose(0, 1, 4, 2, 3).reshape(
        2, S5_HALVES, S5_GROUP, S5_HALF_N)
    ct = lambda a: a.reshape(2, S5_HALVES, gh, S5_GROUP, S5_STATE).transpose(0, 1, 3, 2, 4).reshape(
        2, S5_HALVES, S5_GROUP, S5_HALF_N)
    vspec = pl.BlockSpec((1, 1, 1, S5_HALF_N), lambda r, h: (r, h, 0, 0))
    mspec = pl.BlockSpec((1, 1, S5_GROUP, S5_HALF_N), lambda r, h: (r, h, 0, 0))
    ospec = pl.BlockSpec((1, 1, S5_HALF_W, 2 * S5_HALF_N), lambda r, h: (r, h, 0, 0))
    return pl.pallas_call(
        _s5_prep_kernel,
        grid=(2, S5_HALVES),
        in_specs=[vspec, vspec, vspec, mspec, mspec, mspec, mspec],
        out_specs=[ospec, ospec, pl.BlockSpec((1, 1, 8, 2 * S5_HALF_N), lambda r, h: (r, h, 0, 0))],
        out_shape=[jax.ShapeDtypeStruct((2, S5_HALVES, S5_HALF_W, 2 * S5_HALF_N), BF16),
                   jax.ShapeDtypeStruct((2, S5_HALVES, S5_HALF_W, 2 * S5_HALF_N), BF16),
                   jax.ShapeDtypeStruct((2, S5_HALVES, 8, 2 * S5_HALF_N), F32)],
        compiler_params=_cparams(("parallel", "parallel")),
        name="s5_prep",
    )(vec(lam_re), vec(lam_im), ldt, bt(b_re), bt(b_im), ct(c_re), ct(c_im))


def _s5_kernel(uf0_ref, uf1_ref, ub0_ref, ub1_ref, bd_ref, cdt_ref, a_ref, yf_ref, yb_ref,
               lhs_scr, bu_scr, s_scr, st_scr, *, tb, bsz):
    n = S5_HALF_N
    rows = tb * bsz
    u_refs = ((uf0_ref, uf1_ref), (ub0_ref, ub1_ref))
    y_refs = (yf_ref, yb_ref)

    @pl.when(pl.program_id(1) == 0)
    def _():
        st_scr[...] = jnp.zeros_like(st_scr)

    for d in (0, 1):
        for j in (0, 1):
            lhs_scr[d, :, j * LANES:(j + 1) * LANES] = jnp.transpose(
                u_refs[d][j][...], (1, 0, 2)).reshape(rows, LANES)
    for d in (0, 1):
        bu_scr[d] = _dot(lhs_scr[d], bd_ref[d, 0])
    for d in (0, 1):
        a_re, a_im = a_ref[d, 0, :, :n], a_ref[d, 0, :, n:]
        s_re, s_im = st_scr[d, :, :n], st_scr[d, :, n:]
        for j in range(tb):
            t = j if d == 0 else tb - 1 - j
            sl = slice(t * bsz, (t + 1) * bsz)
            s_re, s_im = (a_re * s_re - a_im * s_im + bu_scr[d, sl, :n],
                          a_re * s_im + a_im * s_re + bu_scr[d, sl, n:])
            s_scr[d, sl, :n] = s_re
            s_scr[d, sl, n:] = s_im
        st_scr[d, :, :n] = s_re
        st_scr[d, :, n:] = s_im
        y = _dot_nt(s_scr[d], cdt_ref[d, 0])
        for j in (0, 1):
            y_refs[d][:, :, j * LANES:(j + 1) * LANES] = jnp.transpose(
                y[:, j * LANES:(j + 1) * LANES].reshape(tb, bsz, LANES), (1, 0, 2))


def _s5(u, bd, cdt, avec, n_ctx, tb):
    bsz, t, _ = u.shape
    nb, ncb = t // tb, n_ctx // tb
    assert bsz == 8 and S5_HALF_W == 2 * LANES

    def rev(i):
        return jnp.where(i < ncb, ncb - 1 - i, nb + ncb - 1 - i)

    fwd = lambda i: i
    uspec = lambda order, j: pl.BlockSpec((bsz, tb, LANES), lambda h, i: (0, order(i), 2 * h + j))
    yspec = lambda order: pl.BlockSpec((bsz, tb, S5_HALF_W), lambda h, i: (0, order(i), h))
    wspec = pl.BlockSpec((2, 1, S5_HALF_W, 2 * S5_HALF_N), lambda h, i: (0, h, 0, 0))
    rows = tb * bsz
    return pl.pallas_call(
        functools.partial(_s5_kernel, tb=tb, bsz=bsz),
        grid=(S5_HALVES, nb),
        in_specs=[uspec(fwd, 0), uspec(fwd, 1), uspec(rev, 0), uspec(rev, 1), wspec, wspec,
                  pl.BlockSpec((2, 1, 8, 2 * S5_HALF_N), lambda h, i: (0, h, 0, 0))],
        out_specs=[yspec(fwd), yspec(rev)],
        out_shape=[jax.ShapeDtypeStruct((bsz, t, S5_WIDTH), F32)] * 2,
        scratch_shapes=[pltpu.VMEM((2, rows, S5_HALF_W), F32),
                        pltpu.VMEM((2, rows, 2 * S5_HALF_N), F32),
                        pltpu.VMEM((2, rows, 2 * S5_HALF_N), F32),
                        pltpu.VMEM((2, bsz, 2 * S5_HALF_N), F32)],
        compiler_params=_cparams(("parallel", "arbitrary")),
        name="s5_scan",
    )(u, u, u, u, bd, cdt, avec)


def _even_finish_kernel(x_ref, ha_ref, op_ref, yf_ref, yb_ref, u_ref, mhw_ref, ds_ref, wg_ref, wo_ref,
                        gc_ref, gl_ref, o_ref, *, n_top):
    ha = ha_ref[0]
    parts = []
    for h in range(MLSTM_HEADS):
        sl = slice(h * HEAD_DIM, (h + 1) * HEAD_DIM)
        parts.append(_rms(ha[:, sl], mhw_ref[:, sl]))
    a_out = jnp.concatenate(parts, axis=-1) * jax.nn.sigmoid(op_ref[0])
    yb = jax.nn.gelu(yf_ref[0] + yb_ref[0] + ds_ref[...] * u_ref[0])
    glu = _dot(yb, wg_ref[...])
    b_out = glu[:, :S5_WIDTH] * jax.nn.sigmoid(glu[:, S5_WIDTH:])
    y = _dot(a_out, wo_ref[:MLSTM_WIDTH, :]) + _dot(b_out, wo_ref[MLSTM_WIDTH:, :])
    for rows, (gate,) in _row_groups(n_top, y.shape[0], (gc_ref,), (gl_ref,)):
        o_ref[0, rows, :] = x_ref[0, rows, :] + gate * y[rows, :]


def _even_finish(xc, ha, o_pre, yf, yb, u, mods, layer, mh_w, d_skip, wg, wo, tm, n_ctx):
    bsz, t, d = xc.shape
    assert t % tm == 0 and n_ctx <= tm
    full = lambda shape: pl.BlockSpec(shape, lambda b, i: tuple(0 for _ in shape),
                                      pipeline_mode=pl.Buffered(1))
    return pl.pallas_call(
        functools.partial(_even_finish_kernel, n_top=n_ctx),
        grid=(bsz, t // tm),
        in_specs=[pl.BlockSpec((1, tm, d), lambda b, i: (b, i, 0)),
                  pl.BlockSpec((1, tm, MLSTM_WIDTH), lambda b, i: (b, i, 0)),
                  pl.BlockSpec((1, tm, MLSTM_WIDTH), lambda b, i: (b, i, 0)),
                  pl.BlockSpec((1, tm, S5_WIDTH), lambda b, i: (b, i, 0)),
                  pl.BlockSpec((1, tm, S5_WIDTH), lambda b, i: (b, i, 0)),
                  pl.BlockSpec((1, tm, S5_WIDTH), lambda b, i: (b, i, 0)),
                  full((1, MLSTM_WIDTH)), full((1, S5_WIDTH)),
                  full((S5_WIDTH, 2 * S5_WIDTH)), full((d, d)),
                  _ctx_mod_spec(layer, 2, bsz), _lat_mod_spec(layer, 2)],
        out_specs=pl.BlockSpec((1, tm, d), lambda b, i: (b, i, 0)),
        out_shape=jax.ShapeDtypeStruct((bsz, t, d), F32),
        compiler_params=_cparams(("parallel", "parallel")),
        name="even_finish",
    )(xc, ha, o_pre, yf, yb, u, mh_w.reshape(1, -1), d_skip.reshape(1, -1), wg, wo, mods, mods)


def _conv_chunk(x_ref, w, lanes, c, static, n_chunks, ctx_chunks):
    tpos = lax.broadcasted_iota(jnp.int32, (CHUNK, HEAD_DIM), 0)
    first, last = tpos == 0, tpos == CHUNK - 1
    wrow = lambda k: w[k:k + 1, :]

    def rows(ci):
        start = ci * CHUNK if isinstance(ci, int) else pl.multiple_of(ci * CHUNK, CHUNK)
        return x_ref[0, pl.ds(start, CHUNK), lanes].astype(F32)

    if static and c < ctx_chunks:
        cur = rows(c)
        zero = jnp.zeros((1, HEAD_DIM), F32)
        prev_tok = rows(c - 1)[CHUNK - 1:CHUNK, :] if c > 0 else zero
        next_tok = rows(c + 1)[0:1, :] if c < ctx_chunks - 1 else zero
        left = jnp.where(first, prev_tok, pltpu.roll(cur, 1, axis=0))
        right = jnp.where(last, next_tok, pltpu.roll(cur, CHUNK - 1, axis=0))
        return wrow(3) * left + wrow(4) * cur + wrow(5) * right

    if static:
        up_i, dn_i = max(c - 1, ctx_chunks), min(c + 1, n_chunks - 1)
        w_up = w[0:3, :] if c > ctx_chunks else jnp.zeros((3, HEAD_DIM), F32)
        w_dn = w[6:9, :] if c < n_chunks - 1 else jnp.zeros((3, HEAD_DIM), F32)
    else:
        up_i, dn_i = jnp.maximum(c - 1, ctx_chunks), jnp.minimum(c + 1, n_chunks - 1)
        w_up = jnp.where(c > ctx_chunks, w[0:3, :], 0.0)
        w_dn = jnp.where(c < n_chunks - 1, w[6:9, :], 0.0)
    up, cur, dn = rows(up_i), rows(c), rows(dn_i)
    r0, r1, r2 = (w_up[dc:dc + 1, :] * up + wrow(3 + dc) * cur + w_dn[dc:dc + 1, :] * dn for dc in range(3))
    prev = jnp.where(first, 0.0, pltpu.roll(r0, 1, axis=0))
    nxt = jnp.where(last, 0.0, pltpu.roll(r2, CHUNK - 1, axis=0))
    return r1 + prev + nxt


def _unit_tri_inverses(mats, eye_f):
    ps = [eye_f - a for a in mats]
    pws = [_dot(a, a) for a in mats]
    k = 2
    while 2 * k < CHUNK:
        res = [_dot(jnp.concatenate([p, pw], axis=0), pw) for p, pw in zip(ps, pws)]
        ps = [p + r[:CHUNK] for p, r in zip(ps, res)]
        pws = [r[CHUNK:] for r in res]
        k *= 2
    return [p + _dot(p, pw) for p, pw in zip(ps, pws)]


def _gdn_kernel(par_ref, q_ref, k_ref, v_ref, wq_ref, wk_ref, wv_ref, g_ref, o_ref,
                qk_scr, ol_scr, nl_scr, gl_scr, *, n_chunks, ctx_chunks, group, heads):
    row, col = _chunk_masks()
    eye = row == col
    eye_f = eye.astype(F32)

    def gate_terms(hd, r, a_pre, b_pre):
        incl = (col >= row) if r == 1 else (col <= row)
        a_log = jnp.full((1, CHUNK), par_ref[r * GDN_HEADS + hd], F32)
        dt_b = par_ref[(2 + r) * GDN_HEADS + hd]
        g_row = -jnp.exp(a_log) * jax.nn.softplus(a_pre + dt_b)
        beta_col = _to_col(jax.nn.sigmoid(b_pre), eye)
        gc_col = jnp.sum(jnp.where(incl, jnp.broadcast_to(g_row, incl.shape), 0.0),
                         axis=1, keepdims=True)
        gc_row = _to_row(gc_col, eye)
        g_last = jnp.sum(g_row, axis=1, keepdims=True)
        gamma = jnp.exp(jnp.where(incl, gc_col - gc_row, -jnp.inf))
        return beta_col, gc_col, g_last, gamma

    for hh in range(heads):
        hd = pl.program_id(1) * heads + hh
        lanes = slice(hh * HEAD_DIM, (hh + 1) * HEAD_DIM)

        def load(c, static, hd=hd, lanes=lanes):
            gates = [(g_ref[0, r * GDN_HEADS + hd, pl.ds(c, 1), :],
                      g_ref[0, (2 + r) * GDN_HEADS + hd, pl.ds(c, 1), :]) for r in (0, 1)]
            q, k, v = (_conv_chunk(x_ref, w_ref[:, lanes], lanes, c, static, n_chunks, ctx_chunks)
                       for x_ref, w_ref in ((q_ref, wq_ref), (k_ref, wk_ref), (v_ref, wv_ref)))
            q, k, v = q * jax.nn.sigmoid(q), k * jax.nn.sigmoid(k), v * jax.nn.sigmoid(v)
            q = q * (lax.rsqrt(jnp.sum(q * q, axis=-1, keepdims=True) + EPS) * (1.0 / math.sqrt(HEAD_DIM)))
            k = k * lax.rsqrt(jnp.sum(k * k, axis=-1, keepdims=True) + EPS)
            return q, k, v, gates

        def local_group(gi, _, hh=hh, hd=hd, load=load):
            static = isinstance(gi, int)
            chunks = [gi * group + g for g in range(group)]
            loaded = [load(c, static) for c in chunks]
            kqs = [_dot_nt(jnp.concatenate([k, q], axis=0), k) for q, k, _, _ in loaded]
            chains = [(g, r) for g in range(group) for r in (0, 1)]
            terms = [gate_terms(hd, r, *loaded[g][3][r]) for g, r in chains]
            a_mats = []
            for (g, r), (beta_col, _, _, gamma) in zip(chains, terms):
                strict = (col > row) if r == 1 else (col < row)
                a_mats.append(jnp.where(strict, beta_col * kqs[g][:CHUNK] * gamma, 0.0))
            t_invs = _unit_tri_inverses(a_mats, eye_f)
            sols = []
            for (g, r), (beta_col, gc_col, _, _), t_inv in zip(chains, terms, t_invs):
                _, k, v, _ = loaded[g]
                rhs = jnp.concatenate([v * beta_col, k * (beta_col * jnp.exp(gc_col))], axis=-1)
                sols.append(_dot(t_inv, rhs))
            auws = [_dot(kqs[g][CHUNK:] * tm[3], sol) for (g, r), tm, sol in zip(chains, terms, sols)]
            kuws = [_dot_tn(loaded[g][1] * jnp.exp(tm[2] - tm[1]), sol)
                    for (g, r), tm, sol in zip(chains, terms, sols)]
            for (g, r), (_, gc_col, g_last, _), auw, kuw in zip(chains, terms, auws, kuws):
                c = chunks[g]
                q = loaded[g][0]
                qk_scr[hh, r, c, :CHUNK, :] = (q * jnp.exp(gc_col) - auw[:, HEAD_DIM:]).astype(qk_scr.dtype)
                qk_scr[hh, r, c, CHUNK:, :] = kuw[:, HEAD_DIM:].astype(qk_scr.dtype)
                ol_scr[hh, r, c] = auw[:, :HEAD_DIM]
                nl_scr[hh, r, c] = kuw[:, :HEAD_DIM]
                gl_scr[hh, r, c] = jnp.broadcast_to(jnp.exp(g_last), (8, HEAD_DIM))
            return 0

        n_static = -(-ctx_chunks // group)
        for gi in range(n_static):
            local_group(gi, 0)
        lax.fori_loop(n_static, n_chunks // group, local_group, 0)

    o_ref[...] = jnp.zeros_like(o_ref)
    streams = [(hh, r) for hh in range(heads) for r in (0, 1)]

    def step(i, carry):
        cs = [_scan_chunk_index(i, r == 1, n_chunks, ctx_chunks) for _, r in streams]
        res = [jnp.dot(qk_scr[hh, r, c], s.astype(BF16), preferred_element_type=F32)
               for (hh, r), c, s in zip(streams, cs, carry)]
        new = []
        for (hh, r), c, s, rs in zip(streams, cs, carry, res):
            rows = pl.ds(pl.multiple_of(c * CHUNK, CHUNK), CHUNK)
            o_ref[0, rows, hh * HEAD_DIM:(hh + 1) * HEAD_DIM] += ol_scr[hh, r, c] + rs[:CHUNK]
            new.append(gl_scr[hh, r, c][0:1, :] * s - rs[CHUNK:] + nl_scr[hh, r, c])
        return tuple(new)

    init = jnp.zeros((HEAD_DIM, HEAD_DIM), F32)
    lax.fori_loop(0, n_chunks, step, (init,) * len(streams), unroll=2)


def _gdn(qkv, conv_w, gates_row, par, n_ctx, heads=2):
    bsz, t, _ = qkv.shape
    assert GRID_W == CHUNK and n_ctx % CHUNK == 0
    n_chunks = t // CHUNK
    group = math.gcd(9, n_chunks)
    wd = heads * HEAD_DIM
    blocks = GDN_HEADS // heads
    spec = lambda off: pl.BlockSpec((1, t, wd), lambda b, h: (b, 0, off + h))
    wspec = lambda off: pl.BlockSpec((9, wd), lambda b, h: (0, off + h))
    conv_w9 = conv_w.reshape(9, GDN_QKV)
    return pl.pallas_call(
        functools.partial(_gdn_kernel, n_chunks=n_chunks, ctx_chunks=n_ctx // CHUNK, group=group,
                          heads=heads),
        grid=(bsz, blocks),
        in_specs=[pl.BlockSpec(memory_space=pltpu.SMEM),
                  spec(0), spec(blocks), spec(2 * blocks),
                  wspec(0), wspec(blocks), wspec(2 * blocks),
                  pl.BlockSpec((1, 4 * GDN_HEADS, n_chunks, CHUNK), lambda b, h: (b, 0, 0, 0))],
        out_specs=pl.BlockSpec((1, t, wd), lambda b, h: (b, 0, h)),
        out_shape=jax.ShapeDtypeStruct((bsz, t, D_MODEL), F32),
        scratch_shapes=[pltpu.VMEM((heads, 2, n_chunks, CHUNK + HEAD_DIM, HEAD_DIM), BF16),
                        pltpu.VMEM((heads, 2, n_chunks, CHUNK, HEAD_DIM), F32),
                        pltpu.VMEM((heads, 2, n_chunks, HEAD_DIM, HEAD_DIM), F32),
                        pltpu.VMEM((heads, 2, n_chunks, 8, HEAD_DIM), F32)],
        compiler_params=_cparams(("parallel", "parallel")),
        name="gdn",
    )(par, qkv, qkv, qkv, conv_w9, conv_w9, conv_w9, gates_row)


def _odd_finish_kernel(x_ref, o_in_ref, z_ref, hw_ref, wo_ref, g_ref, o_ref):
    o = o_in_ref[0]
    parts = []
    for h in range(GDN_HEADS):
        sl = slice(h * HEAD_DIM, (h + 1) * HEAD_DIM)
        parts.append(_rms(o[:, sl], hw_ref[:, sl]))
    z = z_ref[0]
    y = jnp.concatenate(parts, axis=-1) * (z * jax.nn.sigmoid(z))
    o_ref[0] = x_ref[0] + g_ref[0, 0] * _dot(y, wo_ref[...])


def _odd_finish(xc, o, z, mods, layer, hw, wo, tm, n_ctx):
    bsz, t, d = xc.shape
    assert n_ctx % tm == 0 and (t - n_ctx) % tm == 0
    t0 = n_ctx // tm
    return pl.pallas_call(
        _odd_finish_kernel,
        grid=(bsz, (t - n_ctx) // tm),
        in_specs=[pl.BlockSpec((1, tm, d), lambda b, i: (b, i + t0, 0)),
                  pl.BlockSpec((1, tm, d), lambda b, i: (b, i + t0, 0)),
                  pl.BlockSpec((1, tm, d), lambda b, i: (b, i + t0, 0)),
                  pl.BlockSpec((1, d), lambda b, i: (0, 0)),
                  pl.BlockSpec((d, d), lambda b, i: (0, 0)),
                  _lat_mod_spec(layer, 2)],
        out_specs=pl.BlockSpec((1, tm, d), lambda b, i: (b, i, 0)),
        out_shape=jax.ShapeDtypeStruct((bsz, t - n_ctx, d), F32),
        compiler_params=_cparams(("parallel", "parallel")),
        name="odd_finish",
    )(xc, o, z, hw.reshape(1, d), wo, mods)


def _chunked(gates_t):
    bsz, g, t = gates_t.shape
    return gates_t.reshape(bsz, g, t // CHUNK, CHUNK)


def _pick_tile(total, target, at_least, multiple=8):
    for tile in range(min(target, total), multiple - 1, -1):
        if total % tile == 0 and tile % multiple == 0 and tile >= at_least:
            return tile
    raise ValueError(f"no row tile for {total} rows")


def _pad_cols(w, n):
    return jnp.pad(w, ((0, 0), (0, n - w.shape[1])))


def kernel(x, c, ctx, c_ctx, ada_w, ada_b, norm1_w, norm2_w, ffn_w1, ffn_w3, ffn_w2, final_norm_w,
           ev_w_in, ev_i_bias, ev_f_bias, ev_head_norm_w, ev_lam_re, ev_lam_im, ev_log_dt,
           ev_b_re, ev_b_im, ev_c_re, ev_c_im, ev_d, ev_w_glu, ev_w_out,
           od_w_in, od_conv_w, od_a_log, od_dt_bias, od_head_norm_w, od_w_out):
    bsz, seq, d = x.shape
    n_ctx = ctx.shape[1]
    assert d == D_MODEL and seq % GRID_W == 0 and n_ctx % CHUNK == 0 and bsz % 8 == 0
    assert ada_w.shape[0] == 2
    t_all = n_ctx + seq
    tm = _pick_tile(t_all, 768, n_ctx, LANES)
    tm_lat = _pick_tile(math.gcd(seq, n_ctx), 512, 0)
    th = FFN_HIDDEN // 2
    tb = CHUNK

    xc = jnp.concatenate([ctx, x], axis=1)
    mod_rows = 16
    cvec = jnp.zeros((mod_rows, d), F32).at[:bsz].set(c).at[bsz].set(c_ctx)
    mods = _ada(cvec, ada_w, ada_b).reshape(2, mod_rows, 1, 6 * d)
    w1, w3, w2 = ffn_w1.astype(BF16), ffn_w3.astype(BF16), ffn_w2.astype(BF16)

    qkv_w, n_g = 3 * MLSTM_WIDTH, 4 * MLSTM_HEADS
    w_in = ev_w_in[0]
    g0 = qkv_w + MLSTM_WIDTH
    w_even = jnp.concatenate([w_in[:, :g0], w_in[:, g0 + n_g:], _pad_cols(w_in[:, g0:g0 + n_g], LANES)],
                             axis=1).astype(BF16)
    c1, c2, c3 = qkv_w, g0, g0 + S5_WIDTH
    qkv, o_pre, u, gates = _inproj(
        xc, norm1_w[0], mods, 0, w_even,
        [(0, c1, "row", BF16, c1), (c1, c2, "row", F32, c2 - c1), (c2, c3, "row", F32, S5_WIDTH),
         (c3, c3 + LANES, "gate", F32, n_g)], tm, n_ctx)
    bias = jnp.concatenate([ev_i_bias[0].reshape(-1), ev_f_bias[0].reshape(-1)]).astype(F32)
    ha = _mlstm(qkv, _chunked(gates), bias, n_ctx)
    bd, cdt, avec = _s5_prep(ev_lam_re[0], ev_lam_im[0], ev_log_dt[0], ev_b_re[0], ev_b_im[0],
                             ev_c_re[0], ev_c_im[0])
    yf, yb = _s5(u, bd, cdt, avec, n_ctx, tb)
    xc = _even_finish(xc, ha, o_pre, yf, yb, u, mods, 0, ev_head_norm_w[0], ev_d[0],
                      ev_w_glu[0].astype(BF16), ev_w_out[0].astype(BF16), tm, n_ctx)
    xc = _ffn(xc, norm2_w[0], mods, 0, w1[0], w3[0], w2[0], final_norm_w, tm, th, n_ctx, False)

    n_g = 4 * GDN_HEADS
    w_odd = _pad_cols(od_w_in[0], 4 * d + LANES).astype(BF16)
    qkv_pre, z, gates = _inproj(
        xc, norm1_w[1], mods, 1, w_odd,
        [(0, GDN_QKV, "row", BF16, GDN_QKV), (GDN_QKV, 4 * d, "row", F32, d),
         (4 * d, 4 * d + LANES, "gate", F32, n_g)], tm, n_ctx)
    par = jnp.concatenate([od_a_log[0].reshape(-1), od_dt_bias[0].reshape(-1)]).astype(F32)
    o = _gdn(qkv_pre, od_conv_w[0], _chunked(gates), par, n_ctx)
    xl = _odd_finish(xc, o, z, mods, 1, od_head_norm_w[0], od_w_out[0].astype(BF16), tm_lat, n_ctx)
    return _ffn(xl, norm2_w[1], mods, 1, w1[1], w3[1], w2[1], final_norm_w, _pick_tile(seq, 512, 0),
                th, 0, True)
```

```python
import functools
import math

import jax
import jax.numpy as jnp
from jax import lax
from jax.experimental import pallas as pl
from jax.experimental.pallas import tpu as pltpu

F32 = jnp.float32
BF16 = jnp.bfloat16

D_MODEL = 1024
CHUNK = 64
GRID_W = 64
EPS = 1e-6
LANES = 128
HEAD_DIM = 128
MLSTM_WIDTH = D_MODEL // 2
MLSTM_HEADS = MLSTM_WIDTH // HEAD_DIM
S5_WIDTH = D_MODEL - MLSTM_WIDTH
S5_GROUP = 16
S5_GROUPS = S5_WIDTH // S5_GROUP
S5_STATE = 64
S5_HALVES = 2
S5_HALF_W = S5_WIDTH // S5_HALVES
S5_HALF_N = (S5_GROUPS // S5_HALVES) * S5_STATE
GDN_HEADS = D_MODEL // HEAD_DIM
GDN_QKV = 3 * D_MODEL
FFN_HIDDEN = ((8 * D_MODEL + 3 * 256 - 1) // (3 * 256)) * 256
VMEM_LIMIT = 56 * 1024 * 1024

_NT = (((1,), (1,)), ((), ()))
_TN = (((0,), (0,)), ((), ()))


def _cparams(sem):
    return pltpu.CompilerParams(dimension_semantics=sem, vmem_limit_bytes=VMEM_LIMIT)


def _dot(a, b):
    return jnp.dot(a.astype(BF16), b.astype(BF16), preferred_element_type=F32)


def _dot_nt(a, b):
    return lax.dot_general(a.astype(BF16), b.astype(BF16), _NT, preferred_element_type=F32)


def _dot_tn(a, b):
    return lax.dot_general(a.astype(BF16), b.astype(BF16), _TN, preferred_element_type=F32)


def _rms(x, w):
    return x * lax.rsqrt(jnp.mean(x * x, axis=-1, keepdims=True) + EPS) * w


def _ada_kernel(s_ref, w_ref, b_ref, o_ref):
    s = s_ref[...]
    s = s * jax.nn.sigmoid(s)
    o_ref[0] = _dot(s, w_ref[0]) + b_ref[0]


def _ada(cvec, ada_w, ada_b):
    depth, d, n = ada_w.shape
    tn = 1536
    rows = cvec.shape[0]
    return pl.pallas_call(
        _ada_kernel,
        grid=(depth, n // tn),
        in_specs=[pl.BlockSpec((rows, d), lambda l, j: (0, 0)),
                  pl.BlockSpec((1, d, tn), lambda l, j: (l, 0, j)),
                  pl.BlockSpec((1, 1, tn), lambda l, j: (l, 0, j))],
        out_specs=pl.BlockSpec((1, rows, tn), lambda l, j: (l, 0, j)),
        out_shape=jax.ShapeDtypeStruct((depth, rows, n), F32),
        compiler_params=_cparams(("parallel", "parallel")),
        name="ada_mod",
    )(cvec, ada_w, ada_b.reshape(depth, 1, n))


def _ctx_mod_spec(layer, chunk, ctx_row):
    return pl.BlockSpec((1, 1, 1, D_MODEL), lambda b, i: (layer, ctx_row, 0, chunk))


def _lat_mod_spec(layer, chunk):
    return pl.BlockSpec((1, 1, 1, D_MODEL), lambda b, i: (layer, b, 0, chunk))


def _row_groups(n_top, tm, x_refs, ctx_refs, lat_refs):
    lat = [r[0, 0] for r in lat_refs]
    if n_top == 0:
        return [(slice(0, tm), x_refs[0][0], lat)]
    first = pl.program_id(1) == 0
    top = [jnp.where(first, c[0, 0], l) for c, l in zip(ctx_refs, lat)]
    if len(x_refs) == 1:
        x_ref = x_refs[0]
        return [(slice(0, n_top), x_ref[0, :n_top, :], top), (slice(n_top, tm), x_ref[0, n_top:, :], lat)]
    ctx_ref, subs = x_refs[0], x_refs[1:]
    groups = [(slice(0, n_top), jnp.where(first, ctx_ref[0], subs[0][0]), top)]
    return groups + [(slice(k * n_top, (k + 1) * n_top), subs[k][0], lat) for k in range(1, len(subs))]


def _x_operands(xs, tm, n_ctx):
    d = xs[0].shape[-1]
    if len(xs) == 1:
        return list(xs), [pl.BlockSpec((1, tm, d), lambda b, i: (b, i, 0))]
    ctx, lat = xs
    nsub = tm // n_ctx
    assert tm % n_ctx == 0 and lat.shape[1] % n_ctx == 0
    specs = [pl.BlockSpec((1, n_ctx, d), lambda b, i: (b, 0, 0))]
    for k in range(nsub):
        specs.append(pl.BlockSpec((1, n_ctx, d), lambda b, i, k=k: (b, jnp.maximum(i * nsub + k - 1, 0), 0)))
    return [ctx] + [lat] * nsub, specs


def _inproj_kernel(*refs, outs, n_top, n_chunk, n_x):
    x_refs, (nw_ref, shc_ref, scc_ref, shl_ref, scl_ref, w_ref) = refs[:n_x], refs[n_x:n_x + 6]
    out_refs, h_scr, g_scr = refs[n_x + 6:-2], refs[-2], refs[-1]
    tm = h_scr.shape[0]
    for rows, x, (sh, sc) in _row_groups(n_top, tm, x_refs, (shc_ref, scc_ref), (shl_ref, scl_ref)):
        h_scr[rows, :] = (_rms(x, nw_ref[...]) * (1.0 + sc) + sh).astype(BF16)
    for o_ref, (c0, c1, kind) in zip(out_refs, outs):
        for n0 in range(c0, c1, n_chunk):
            n1 = min(n0 + n_chunk, c1)
            val = jnp.dot(h_scr[...], w_ref[:, n0:n1], preferred_element_type=F32)
            if kind == "row":
                o_ref[0, :, n0 - c0:n1 - c0] = val.astype(o_ref.dtype)
            else:
                g_scr[...] = val
                o_ref[0] = g_scr[...].T[:o_ref.shape[1], :].astype(o_ref.dtype)


def _inproj(xs, norm_w, mods, layer, w_bf16, outs, tm, n_ctx):
    bsz, d = xs[0].shape[0], xs[0].shape[-1]
    t = sum(a.shape[1] for a in xs)
    n = w_bf16.shape[1]
    x_args, x_specs = _x_operands(xs, tm, n_ctx)
    assert t % tm == 0 and (n_ctx == 0 or n_ctx <= tm)
    out_specs, out_shapes = [], []
    for c0, c1, kind, dtype, wd in outs:
        if kind == "row":
            out_specs.append(pl.BlockSpec((1, tm, wd), lambda b, i: (b, i, 0)))
            out_shapes.append(jax.ShapeDtypeStruct((bsz, t, wd), dtype))
        else:
            assert c1 - c0 == LANES and tm % LANES == 0
            out_specs.append(pl.BlockSpec((1, wd, tm), lambda b, i: (b, 0, i)))
            out_shapes.append(jax.ShapeDtypeStruct((bsz, wd, t), dtype))
    ctx_spec = lambda c: _ctx_mod_spec(layer, c, bsz)
    lat_spec = lambda c: _lat_mod_spec(layer, c)
    const = lambda shape: pl.BlockSpec(shape, lambda b, i: (0, 0), pipeline_mode=pl.Buffered(1))
    return pl.pallas_call(
        functools.partial(_inproj_kernel, outs=tuple(o[:3] for o in outs), n_top=n_ctx, n_chunk=512,
                          n_x=len(x_args)),
        grid=(bsz, t // tm),
        in_specs=x_specs + [const((1, d)), ctx_spec(0), ctx_spec(1), lat_spec(0), lat_spec(1),
                            const((d, n))],
        out_specs=out_specs,
        out_shape=out_shapes,
        scratch_shapes=[pltpu.VMEM((tm, d), BF16), pltpu.VMEM((tm, LANES), F32)],
        compiler_params=_cparams(("parallel", "parallel")),
        name=f"inproj_l{layer}",
    )(*x_args, norm_w.reshape(1, d), mods, mods, mods, mods, w_bf16)


def _ffn_kernel(x_ref, nw_ref, shc_ref, scc_ref, gc_ref, shl_ref, scl_ref, gl_ref,
                w1_ref, w3_ref, w2_ref, fw_ref, o_ref, h_scr, *, final, n_top, th):
    tm, hid = h_scr.shape[0], w1_ref.shape[1]
    groups = _row_groups(n_top, tm, (x_ref,), (shc_ref, scc_ref, gc_ref), (shl_ref, scl_ref, gl_ref))
    for rows, x, (sh, sc, _) in groups:
        h_scr[rows, :] = (_rms(x, nw_ref[...]) * (1.0 + sc) + sh).astype(BF16)
    acc = None
    for c0 in range(0, hid, th):
        h = h_scr[...]
        a = jnp.dot(h, w1_ref[:, c0:c0 + th], preferred_element_type=F32)
        g = jnp.dot(h, w3_ref[:, c0:c0 + th], preferred_element_type=F32)
        t = ((a * jax.nn.sigmoid(a)) * g).astype(BF16)
        part = jnp.dot(t, w2_ref[c0:c0 + th, :], preferred_element_type=F32)
        acc = part if acc is None else acc + part
    for rows, x, (_, _, gate) in groups:
        y = x + gate * acc[rows, :]
        if final:
            y = _rms(y, fw_ref[...])
        o_ref[0, rows, :] = y


def _ffn(x, norm_w, mods, layer, w1, w3, w2, final_w, tm, th, n_ctx, final):
    bsz, tx, d = x.shape
    hid = w1.shape[1]
    assert n_ctx <= tm and tx % tm == 0 and hid % th == 0
    ctx_spec = lambda c: _ctx_mod_spec(layer, c, bsz)
    lat_spec = lambda c: _lat_mod_spec(layer, c)
    const = lambda shape: pl.BlockSpec(shape, lambda b, i: (0, 0), pipeline_mode=pl.Buffered(1))
    return pl.pallas_call(
        functools.partial(_ffn_kernel, final=final, n_top=n_ctx, th=th),
        grid=(bsz, tx // tm),
        in_specs=[pl.BlockSpec((1, tm, d), lambda b, i: (b, i, 0)),
                  const((1, d)),
                  ctx_spec(3), ctx_spec(4), ctx_spec(5), lat_spec(3), lat_spec(4), lat_spec(5),
                  const((d, hid)), const((d, hid)), const((hid, d)), const((1, d))],
        out_specs=pl.BlockSpec((1, tm, d), lambda b, i: (b, i, 0)),
        out_shape=jax.ShapeDtypeStruct((bsz, tx, d), F32),
        scratch_shapes=[pltpu.VMEM((tm, d), BF16)],
        compiler_params=_cparams(("parallel", "parallel")),
        name=f"ffn_l{layer}",
    )(x, norm_w.reshape(1, d), mods, mods, mods, mods, mods, mods, w1, w3, w2, final_w.reshape(1, d))


def _chunk_masks():
    row = lax.broadcasted_iota(jnp.int32, (CHUNK, CHUNK), 0)
    col = lax.broadcasted_iota(jnp.int32, (CHUNK, CHUNK), 1)
    return row, col


def _to_col(row_vec, eye):
    return jnp.sum(jnp.where(eye, jnp.broadcast_to(row_vec, eye.shape), 0.0), axis=1, keepdims=True)


def _to_row(col_vec, eye):
    return jnp.sum(jnp.where(eye, jnp.broadcast_to(col_vec, eye.shape), 0.0), axis=0, keepdims=True)


def _scan_chunk_index(i, rev, n_chunks, ctx_chunks):
    if not rev:
        return i
    return jnp.where(i < ctx_chunks, ctx_chunks - 1 - i, n_chunks + ctx_chunks - 1 - i)


def _mlstm_kernel(bias_ref, q_ref, k_ref, v_ref, g_ref, o_ref, num_scr, cl_scr, mi_scr, fc_scr, sc_scr,
                  *, n_chunks, ctx_chunks, group):
    hd = pl.program_id(1)
    row, col = _chunk_masks()
    eye = row == col
    kscale = 1.0 / math.sqrt(HEAD_DIM)
    ones_blk = jnp.ones((CHUNK, HEAD_DIM), BF16)

    def gate_terms(r, ig_raw, f_raw):
        incl = (col >= row) if r == 1 else (col <= row)
        ig_row = ig_raw + bias_ref[r * MLSTM_HEADS + hd]
        lf_row = jax.nn.log_sigmoid(f_raw + bias_ref[(2 + r) * MLSTM_HEADS + hd])
        f_col = jnp.sum(jnp.where(incl, jnp.broadcast_to(lf_row, incl.shape), 0.0),
                        axis=1, keepdims=True)
        f_row = _to_row(f_col, eye)
        f_last = jnp.sum(lf_row, axis=1, keepdims=True)
        dm = jnp.where(incl, f_col - f_row + ig_row, -jnp.inf)
        m_intra = jnp.max(dm, axis=1, keepdims=True)
        w_col = f_last - f_col + _to_col(ig_row, eye)
        m_loc = jnp.max(w_col, axis=0, keepdims=True)
        return f_col, f_last, jnp.exp(dm - m_intra), m_intra, jnp.exp(w_col - m_loc), m_loc

    def local_group(gi, _):
        chunks = [gi * group + g for g in range(group)]
        loaded = []
        for c in chunks:
            rows = pl.ds(pl.multiple_of(c * CHUNK, CHUNK), CHUNK)
            gates = [(g_ref[0, r * MLSTM_HEADS + hd, pl.ds(c, 1), :],
                      g_ref[0, (2 + r) * MLSTM_HEADS + hd, pl.ds(c, 1), :]) for r in (0, 1)]
            v1 = jnp.concatenate([v_ref[0, rows, :].astype(BF16), ones_blk], axis=-1)
            loaded.append((q_ref[0, rows, :], k_ref[0, rows, :].astype(F32) * kscale, v1, gates))
        chains = [(g, r) for g in range(group) for r in (0, 1)]
        terms = [gate_terms(r, *loaded[g][3][r]) for g, r in chains]
        qks = [_dot_nt(q, k) for q, k, _, _ in loaded]
        c_locs = [_dot_tn(tm[4] * loaded[g][1], loaded[g][2]) for (g, r), tm in zip(chains, terms)]
        nums = [_dot(tm[2] * qks[g], loaded[g][2]) for (g, r), tm in zip(chains, terms)]
        for (g, r), tm, c_loc, num in zip(chains, terms, c_locs, nums):
            f_col, f_last, _, m_intra, _, m_loc = tm
            c = chunks[g]
            num_scr[r, c] = num
            cl_scr[r, c] = c_loc
            mi_scr[r, c] = jnp.broadcast_to(m_intra, (CHUNK, HEAD_DIM))
            fc_scr[r, c] = jnp.broadcast_to(f_col, (CHUNK, HEAD_DIM))
            sc_scr[r, c, :8, :] = jnp.broadcast_to(f_last, (8, HEAD_DIM))
            sc_scr[r, c, 8:, :] = jnp.broadcast_to(m_loc, (8, HEAD_DIM))
        return 0

    lax.fori_loop(0, n_chunks // group, local_group, 0)
    o_ref[...] = jnp.zeros_like(o_ref)

    def step(i, carry):
        cs = [_scan_chunk_index(i, r == 1, n_chunks, ctx_chunks) for r in (0, 1)]
        rows = [pl.ds(pl.multiple_of(c * CHUNK, CHUNK), CHUNK) for c in cs]
        qcs = [_dot(q_ref[0, rw, :], st[0]) for rw, st in zip(rows, carry)]
        new = []
        for r in (0, 1):
            s_st, m_st = carry[r]
            c = cs[r]
            mi, na = mi_scr[r, c], num_scr[r, c]
            inter = fc_scr[r, c] + m_st[0:1, :]
            m_t = jnp.maximum(mi, inter)
            a_loc_t, a_inter = jnp.exp(mi - m_t), jnp.exp(inter - m_t)
            num = a_loc_t * na[:, :HEAD_DIM] + a_inter * qcs[r][:, :HEAD_DIM]
            den = a_loc_t * na[:, HEAD_DIM:] + a_inter * qcs[r][:, HEAD_DIM:]
            out = num / jnp.maximum(jnp.abs(den), jnp.exp(-m_t))
            o_ref[0, rows[r], :] = (o_ref[0, rows[r], :].astype(F32) + out).astype(o_ref.dtype)
            f_last, m_loc = sc_scr[r, c, :8, :], sc_scr[r, c, 8:, :]
            m_new = jnp.maximum(f_last + m_st, m_loc)
            a_prev = jnp.exp(f_last + m_st - m_new)[0:1, :]
            a_loc = jnp.exp(m_loc - m_new)[0:1, :]
            a_prev = jnp.concatenate([a_prev, a_prev], axis=1)
            a_loc = jnp.concatenate([a_loc, a_loc], axis=1)
            new.append((a_prev * s_st + a_loc * cl_scr[r, c], m_new))
        return tuple(new)

    init = (jnp.zeros((HEAD_DIM, 2 * HEAD_DIM), F32), jnp.zeros((8, HEAD_DIM), F32))
    lax.fori_loop(0, n_chunks, step, (init, init), unroll=2)


def _mlstm(qkv, gates_row, bias, n_ctx):
    bsz, t, _ = qkv.shape
    n_chunks = t // CHUNK
    qkv_spec = lambda off: pl.BlockSpec((1, t, HEAD_DIM), lambda b, h: (b, 0, off + h))
    return pl.pallas_call(
        functools.partial(_mlstm_kernel, n_chunks=n_chunks, ctx_chunks=n_ctx // CHUNK,
                          group=math.gcd(9, n_chunks)),
        grid=(bsz, MLSTM_HEADS),
        in_specs=[pl.BlockSpec(memory_space=pltpu.SMEM),
                  qkv_spec(0), qkv_spec(MLSTM_HEADS), qkv_spec(2 * MLSTM_HEADS),
                  pl.BlockSpec((1, 4 * MLSTM_HEADS, n_chunks, CHUNK), lambda b, h: (b, 0, 0, 0))],
        out_specs=pl.BlockSpec((1, t, HEAD_DIM), lambda b, h: (b, 0, h)),
        out_shape=jax.ShapeDtypeStruct((bsz, t, MLSTM_WIDTH), BF16),
        scratch_shapes=[pltpu.VMEM((2, n_chunks, CHUNK, 2 * HEAD_DIM), F32),
                        pltpu.VMEM((2, n_chunks, HEAD_DIM, 2 * HEAD_DIM), F32),
                        pltpu.VMEM((2, n_chunks, CHUNK, HEAD_DIM), F32),
                        pltpu.VMEM((2, n_chunks, CHUNK, HEAD_DIM), F32),
                        pltpu.VMEM((2, n_chunks, 16, HEAD_DIM), F32)],
        compiler_params=_cparams(("parallel", "parallel")),
        name="mlstm",
    )(bias, qkv, qkv, qkv, gates_row)


def _s5_prep_kernel(lr_ref, li_ref, ldt_ref, br_ref, bi_ref, cr_ref, ci_ref, bd_ref, cdt_ref, a_ref):
    lr, li = lr_ref[0, 0], li_ref[0, 0]
    dt = jnp.exp(ldt_ref[0, 0])
    mag, ang = jnp.exp(lr * dt), li * dt
    ab_re, ab_im = mag * jnp.cos(ang), mag * jnp.sin(ang)
    nr, ni = ab_re - 1.0, ab_im
    den = lr * lr + li * li
    co_re = (nr * lr + ni * li) / den
    co_im = (ni * lr - nr * li) / den
    b_re, b_im = br_ref[0, 0], bi_ref[0, 0]
    bb_re = co_re * b_re - co_im * b_im
    bb_im = co_re * b_im + co_im * b_re
    c_re, c_im = cr_ref[0, 0], ci_ref[0, 0]
    lane_group = lax.broadcasted_iota(jnp.int32, (S5_GROUP, S5_HALF_N), 1) // S5_STATE
    n = S5_HALF_N
    for g in range(S5_GROUPS // S5_HALVES):
        sel = lane_group == g
        rows = slice(g * S5_GROUP, (g + 1) * S5_GROUP)
        bd_ref[0, 0, rows, :n] = jnp.where(sel, bb_re, 0.0).astype(bd_ref.dtype)
        bd_ref[0, 0, rows, n:] = jnp.where(sel, bb_im, 0.0).astype(bd_ref.dtype)
        cdt_ref[0, 0, rows, :n] = jnp.where(sel, c_re, 0.0).astype(cdt_ref.dtype)
        cdt_ref[0, 0, rows, n:] = jnp.where(sel, -c_im, 0.0).astype(cdt_ref.dtype)
    a_ref[0, 0, :, :n] = jnp.broadcast_to(ab_re, (8, n))
    a_ref[0, 0, :, n:] = jnp.broadcast_to(ab_im, (8, n))


def _s5_prep(lam_re, lam_im, log_dt, b_re, b_im, c_re, c_im):
    gh = S5_GROUPS // S5_HALVES
    vec = lambda a: a.reshape(2, S5_HALVES, 1, S5_HALF_N)
    ldt = vec(jnp.broadcast_to(log_dt[:, :, None], (2, S5_GROUPS, S5_STATE)))
    bt = lambda a: a.reshape(2, S5_HALVES, gh, S5_STATE, S5_GROUP).transpose(0, 1, 4, 2, 3).reshape(
        2, S5_HALVES, S5_GROUP, S5_HALF_N)
    ct = lambda a: a.reshape(2, S5_HALVES, gh, S5_GROUP, S5_STATE).transpose(0, 1, 3, 2, 4).reshape(
        2, S5_HALVES, S5_GROUP, S5_HALF_N)
    vspec = pl.BlockSpec((1, 1, 1, S5_HALF_N), lambda r, h: (r, h, 0, 0))
    mspec = pl.BlockSpec((1, 1, S5_GROUP, S5_HALF_N), lambda r, h: (r, h, 0, 0))
    ospec = pl.BlockSpec((1, 1, S5_HALF_W, 2 * S5_HALF_N), lambda r, h: (r, h, 0, 0))
    return pl.pallas_call(
        _s5_prep_kernel,
        grid=(2, S5_HALVES),
        in_specs=[vspec, vspec, vspec, mspec, mspec, mspec, mspec],
        out_specs=[ospec, ospec, pl.BlockSpec((1, 1, 8, 2 * S5_HALF_N), lambda r, h: (r, h, 0, 0))],
        out_shape=[jax.ShapeDtypeStruct((2, S5_HALVES, S5_HALF_W, 2 * S5_HALF_N), BF16),
                   jax.ShapeDtypeStruct((2, S5_HALVES, S5_HALF_W, 2 * S5_HALF_N), BF16),
                   jax.ShapeDtypeStruct((2, S5_HALVES, 8, 2 * S5_HALF_N), F32)],
        compiler_params=_cparams(("parallel", "parallel")),
        name="s5_prep",
    )(vec(lam_re), vec(lam_im), ldt, bt(b_re), bt(b_im), ct(c_re), ct(c_im))


def _s5_kernel(uf0_ref, uf1_ref, ub0_ref, ub1_ref, bd_ref, cdt_ref, a_ref, yf_ref, yb_ref,
               lhs_scr, bu_scr, s_scr, st_scr, *, tb, bsz):
    n = S5_HALF_N
    rows = tb * bsz
    u_refs = ((uf0_ref, uf1_ref), (ub0_ref, ub1_ref))
    y_refs = (yf_ref, yb_ref)

    @pl.when(pl.program_id(1) == 0)
    def _():
        st_scr[...] = jnp.zeros_like(st_scr)

    for d in (0, 1):
        for j in (0, 1):
            lhs_scr[d, :, j * LANES:(j + 1) * LANES] = jnp.transpose(
                u_refs[d][j][...].astype(F32), (1, 0, 2)).reshape(rows, LANES)
    for d in (0, 1):
        bu_scr[d] = _dot(lhs_scr[d], bd_ref[d, 0])
    for d in (0, 1):
        a_re, a_im = a_ref[d, 0, :, :n], a_ref[d, 0, :, n:]
        s_re, s_im = st_scr[d, :, :n], st_scr[d, :, n:]
        for j in range(tb):
            t = j if d == 0 else tb - 1 - j
            sl = slice(t * bsz, (t + 1) * bsz)
            s_re, s_im = (a_re * s_re - a_im * s_im + bu_scr[d, sl, :n],
                          a_re * s_im + a_im * s_re + bu_scr[d, sl, n:])
            s_scr[d, sl, :n] = s_re
            s_scr[d, sl, n:] = s_im
        st_scr[d, :, :n] = s_re
        st_scr[d, :, n:] = s_im
        y = _dot_nt(s_scr[d], cdt_ref[d, 0])
        for j in (0, 1):
            y_refs[d][:, :, j * LANES:(j + 1) * LANES] = jnp.transpose(
                y[:, j * LANES:(j + 1) * LANES].reshape(tb, bsz, LANES), (1, 0, 2)).astype(y_refs[d].dtype)


def _s5(u, bd, cdt, avec, n_ctx, tb):
    bsz, t, _ = u.shape
    nb, ncb = t // tb, n_ctx // tb
    assert bsz == 8 and S5_HALF_W == 2 * LANES

    def rev(i):
        return jnp.where(i < ncb, ncb - 1 - i, nb + ncb - 1 - i)

    fwd = lambda i: i
    uspec = lambda order, j: pl.BlockSpec((bsz, tb, LANES), lambda h, i: (0, order(i), 2 * h + j))
    yspec = lambda order: pl.BlockSpec((bsz, tb, S5_HALF_W), lambda h, i: (0, order(i), h))
    wspec = pl.BlockSpec((2, 1, S5_HALF_W, 2 * S5_HALF_N), lambda h, i: (0, h, 0, 0))
    rows = tb * bsz
    return pl.pallas_call(
        functools.partial(_s5_kernel, tb=tb, bsz=bsz),
        grid=(S5_HALVES, nb),
        in_specs=[uspec(fwd, 0), uspec(fwd, 1), uspec(rev, 0), uspec(rev, 1), wspec, wspec,
                  pl.BlockSpec((2, 1, 8, 2 * S5_HALF_N), lambda h, i: (0, h, 0, 0))],
        out_specs=[yspec(fwd), yspec(rev)],
        out_shape=[jax.ShapeDtypeStruct((bsz, t, S5_WIDTH), BF16)] * 2,
        scratch_shapes=[pltpu.VMEM((2, rows, S5_HALF_W), F32),
                        pltpu.VMEM((2, rows, 2 * S5_HALF_N), F32),
                        pltpu.VMEM((2, rows, 2 * S5_HALF_N), F32),
                        pltpu.VMEM((2, bsz, 2 * S5_HALF_N), F32)],
        compiler_params=_cparams(("parallel", "arbitrary")),
        name="s5_scan",
    )(u, u, u, u, bd, cdt, avec)


def _even_finish_kernel(*refs, n_top, n_x):
    x_refs = refs[:n_x]
    ha_ref, op_ref, yf_ref, yb_ref, u_ref, mhw_ref, ds_ref, wg_ref, wo_ref, gc_ref, gl_ref, o_ref = refs[n_x:]
    ha = ha_ref[0].astype(F32)
    parts = []
    for h in range(MLSTM_HEADS):
        sl = slice(h * HEAD_DIM, (h + 1) * HEAD_DIM)
        parts.append(_rms(ha[:, sl], mhw_ref[:, sl]))
    a_out = jnp.concatenate(parts, axis=-1) * jax.nn.sigmoid(op_ref[0].astype(F32))
    yb = jax.nn.gelu(yf_ref[0].astype(F32) + yb_ref[0].astype(F32) + ds_ref[...] * u_ref[0].astype(F32))
    glu = _dot(yb, wg_ref[...])
    b_out = glu[:, :S5_WIDTH] * jax.nn.sigmoid(glu[:, S5_WIDTH:])
    y = _dot(a_out, wo_ref[:MLSTM_WIDTH, :]) + _dot(b_out, wo_ref[MLSTM_WIDTH:, :])
    for rows, x, (gate,) in _row_groups(n_top, y.shape[0], x_refs, (gc_ref,), (gl_ref,)):
        o_ref[0, rows, :] = x + gate * y[rows, :]


def _even_finish(xs, ha, o_pre, yf, yb, u, mods, layer, mh_w, d_skip, wg, wo, tm, n_ctx):
    bsz, t, _ = ha.shape
    d = xs[0].shape[-1]
    assert t % tm == 0 and n_ctx <= tm
    x_args, x_specs = _x_operands(xs, tm, n_ctx)
    full = lambda shape: pl.BlockSpec(shape, lambda b, i: tuple(0 for _ in shape),
                                      pipeline_mode=pl.Buffered(1))
    return pl.pallas_call(
        functools.partial(_even_finish_kernel, n_top=n_ctx, n_x=len(x_args)),
        grid=(bsz, t // tm),
        in_specs=x_specs + [
                  pl.BlockSpec((1, tm, MLSTM_WIDTH), lambda b, i: (b, i, 0)),
                  pl.BlockSpec((1, tm, MLSTM_WIDTH), lambda b, i: (b, i, 0)),
                  pl.BlockSpec((1, tm, S5_WIDTH), lambda b, i: (b, i, 0)),
                  pl.BlockSpec((1, tm, S5_WIDTH), lambda b, i: (b, i, 0)),
                  pl.BlockSpec((1, tm, S5_WIDTH), lambda b, i: (b, i, 0)),
                  full((1, MLSTM_WIDTH)), full((1, S5_WIDTH)),
                  full((S5_WIDTH, 2 * S5_WIDTH)), full((d, d)),
                  _ctx_mod_spec(layer, 2, bsz), _lat_mod_spec(layer, 2)],
        out_specs=pl.BlockSpec((1, tm, d), lambda b, i: (b, i, 0)),
        out_shape=jax.ShapeDtypeStruct((bsz, t, d), F32),
        compiler_params=_cparams(("parallel", "parallel")),
        name="even_finish",
    )(*x_args, ha, o_pre, yf, yb, u, mh_w.reshape(1, -1), d_skip.reshape(1, -1), wg, wo, mods, mods)


def _conv_chunk(x_ref, w, lanes, c, static, n_chunks, ctx_chunks):
    tpos = lax.broadcasted_iota(jnp.int32, (CHUNK, HEAD_DIM), 0)
    first, last = tpos == 0, tpos == CHUNK - 1
    wrow = lambda k: w[k:k + 1, :]

    def rows(ci):
        start = ci * CHUNK if isinstance(ci, int) else pl.multiple_of(ci * CHUNK, CHUNK)
        return x_ref[0, pl.ds(start, CHUNK), lanes].astype(F32)

    if static and c < ctx_chunks:
        cur = rows(c)
        zero = jnp.zeros((1, HEAD_DIM), F32)
        prev_tok = rows(c - 1)[CHUNK - 1:CHUNK, :] if c > 0 else zero
        next_tok = rows(c + 1)[0:1, :] if c < ctx_chunks - 1 else zero
        left = jnp.where(first, prev_tok, pltpu.roll(cur, 1, axis=0))
        right = jnp.where(last, next_tok, pltpu.roll(cur, CHUNK - 1, axis=0))
        return wrow(3) * left + wrow(4) * cur + wrow(5) * right

    if static:
        up_i, dn_i = max(c - 1, ctx_chunks), min(c + 1, n_chunks - 1)
        w_up = w[0:3, :] if c > ctx_chunks else jnp.zeros((3, HEAD_DIM), F32)
        w_dn = w[6:9, :] if c < n_chunks - 1 else jnp.zeros((3, HEAD_DIM), F32)
    else:
        up_i, dn_i = jnp.maximum(c - 1, ctx_chunks), jnp.minimum(c + 1, n_chunks - 1)
        w_up = jnp.where(c > ctx_chunks, w[0:3, :], 0.0)
        w_dn = jnp.where(c < n_chunks - 1, w[6:9, :], 0.0)
    up, cur, dn = rows(up_i), rows(c), rows(dn_i)
    r0, r1, r2 = (w_up[dc:dc + 1, :] * up + wrow(3 + dc) * cur + w_dn[dc:dc + 1, :] * dn for dc in range(3))
    prev = jnp.where(first, 0.0, pltpu.roll(r0, 1, axis=0))
    nxt = jnp.where(last, 0.0, pltpu.roll(r2, CHUNK - 1, axis=0))
    return r1 + prev + nxt


def _unit_tri_inverses(mats, eye_f):
    ps = [eye_f - a for a in mats]
    pws = [_dot(a, a) for a in mats]
    k = 2
    while 2 * k < CHUNK:
        res = [_dot(jnp.concatenate([p, pw], axis=0), pw) for p, pw in zip(ps, pws)]
        ps = [p + r[:CHUNK] for p, r in zip(ps, res)]
        pws = [r[CHUNK:] for r in res]
        k *= 2
    return [p + _dot(p, pw) for p, pw in zip(ps, pws)]


def _gdn_kernel(par_ref, q_ref, k_ref, v_ref, wq_ref, wk_ref, wv_ref, g_ref, o_ref,
                qk_scr, ol_scr, nl_scr, gl_scr, *, n_chunks, ctx_chunks, group, heads):
    row, col = _chunk_masks()
    eye = row == col
    eye_f = eye.astype(F32)

    def gate_terms(hd, r, a_pre, b_pre):
        incl = (col >= row) if r == 1 else (col <= row)
        a_log = jnp.full((1, CHUNK), par_ref[r * GDN_HEADS + hd], F32)
        dt_b = par_ref[(2 + r) * GDN_HEADS + hd]
        g_row = -jnp.exp(a_log) * jax.nn.softplus(a_pre + dt_b)
        beta_col = _to_col(jax.nn.sigmoid(b_pre), eye)
        gc_col = jnp.sum(jnp.where(incl, jnp.broadcast_to(g_row, incl.shape), 0.0),
                         axis=1, keepdims=True)
        gc_row = _to_row(gc_col, eye)
        g_last = jnp.sum(g_row, axis=1, keepdims=True)
        gamma = jnp.exp(jnp.where(incl, gc_col - gc_row, -jnp.inf))
        return beta_col, gc_col, g_last, gamma

    for hh in range(heads):
        hd = pl.program_id(1) * heads + hh
        lanes = slice(hh * HEAD_DIM, (hh + 1) * HEAD_DIM)

        def load(c, static, hd=hd, lanes=lanes):
            gates = [(g_ref[0, r * GDN_HEADS + hd, pl.ds(c, 1), :],
                      g_ref[0, (2 + r) * GDN_HEADS + hd, pl.ds(c, 1), :]) for r in (0, 1)]
            q, k, v = (_conv_chunk(x_ref, w_ref[:, lanes], lanes, c, static, n_chunks, ctx_chunks)
                       for x_ref, w_ref in ((q_ref, wq_ref), (k_ref, wk_ref), (v_ref, wv_ref)))
            q, k, v = q * jax.nn.sigmoid(q), k * jax.nn.sigmoid(k), v * jax.nn.sigmoid(v)
            q = q * (lax.rsqrt(jnp.sum(q * q, axis=-1, keepdims=True) + EPS) * (1.0 / math.sqrt(HEAD_DIM)))
            k = k * lax.rsqrt(jnp.sum(k * k, axis=-1, keepdims=True) + EPS)
            return q, k, v, gates

        def local_group(gi, _, hh=hh, hd=hd, load=load):
            static = isinstance(gi, int)
            chunks = [gi * group + g for g in range(group)]
            loaded = [load(c, static) for c in chunks]
            kqs = [_dot_nt(jnp.concatenate([k, q], axis=0), k) for q, k, _, _ in loaded]
            chains = [(g, r) for g in range(group) for r in (0, 1)]
            terms = [gate_terms(hd, r, *loaded[g][3][r]) for g, r in chains]
            a_mats = []
            for (g, r), (beta_col, _, _, gamma) in zip(chains, terms):
                strict = (col > row) if r == 1 else (col < row)
                a_mats.append(jnp.where(strict, beta_col * kqs[g][:CHUNK] * gamma, 0.0))
            t_invs = _unit_tri_inverses(a_mats, eye_f)
            sols = []
            for (g, r), (beta_col, gc_col, _, _), t_inv in zip(chains, terms, t_invs):
                _, k, v, _ = loaded[g]
                rhs = jnp.concatenate([v * beta_col, k * (beta_col * jnp.exp(gc_col))], axis=-1)
                sols.append(_dot(t_inv, rhs))
            auws = [_dot(kqs[g][CHUNK:] * tm[3], sol) for (g, r), tm, sol in zip(chains, terms, sols)]
            kuws = [_dot_tn(loaded[g][1] * jnp.exp(tm[2] - tm[1]), sol)
                    for (g, r), tm, sol in zip(chains, terms, sols)]
            for (g, r), (_, gc_col, g_last, _), auw, kuw in zip(chains, terms, auws, kuws):
                c = chunks[g]
                q = loaded[g][0]
                qk_scr[hh, r, c, :CHUNK, :] = (q * jnp.exp(gc_col) - auw[:, HEAD_DIM:]).astype(qk_scr.dtype)
                qk_scr[hh, r, c, CHUNK:, :] = kuw[:, HEAD_DIM:].astype(qk_scr.dtype)
                ol_scr[hh, r, c] = auw[:, :HEAD_DIM]
                nl_scr[hh, r, c] = kuw[:, :HEAD_DIM]
                gl_scr[hh, r, c] = jnp.broadcast_to(jnp.exp(g_last), (8, HEAD_DIM))
            return 0

        n_static = -(-ctx_chunks // group)
        for gi in range(n_static):
            local_group(gi, 0)
        lax.fori_loop(n_static, n_chunks // group, local_group, 0)

    o_ref[...] = jnp.zeros_like(o_ref)
    streams = [(hh, r) for hh in range(heads) for r in (0, 1)]

    def step(i, carry):
        cs = [_scan_chunk_index(i, r == 1, n_chunks, ctx_chunks) for _, r in streams]
        res = [jnp.dot(qk_scr[hh, r, c], s.astype(BF16), preferred_element_type=F32)
               for (hh, r), c, s in zip(streams, cs, carry)]
        new = []
        for (hh, r), c, s, rs in zip(streams, cs, carry, res):
            rows = pl.ds(pl.multiple_of(c * CHUNK, CHUNK), CHUNK)
            lanes = slice(hh * HEAD_DIM, (hh + 1) * HEAD_DIM)
            o_ref[0, rows, lanes] = (o_ref[0, rows, lanes].astype(F32) + ol_scr[hh, r, c]
                                     + rs[:CHUNK]).astype(o_ref.dtype)
            new.append(gl_scr[hh, r, c][0:1, :] * s - rs[CHUNK:] + nl_scr[hh, r, c])
        return tuple(new)

    init = jnp.zeros((HEAD_DIM, HEAD_DIM), F32)
    lax.fori_loop(0, n_chunks, step, (init,) * len(streams), unroll=2)


def _gdn(qkv, conv_w, gates_row, par, n_ctx, heads=2):
    bsz, t, _ = qkv.shape
    assert GRID_W == CHUNK and n_ctx % CHUNK == 0
    n_chunks = t // CHUNK
    group = math.gcd(9, n_chunks)
    wd = heads * HEAD_DIM
    blocks = GDN_HEADS // heads
    spec = lambda off: pl.BlockSpec((1, t, wd), lambda b, h: (b, 0, off + h))
    wspec = lambda off: pl.BlockSpec((9, wd), lambda b, h: (0, off + h))
    conv_w9 = conv_w.reshape(9, GDN_QKV)
    return pl.pallas_call(
        functools.partial(_gdn_kernel, n_chunks=n_chunks, ctx_chunks=n_ctx // CHUNK, group=group,
                          heads=heads),
        grid=(bsz, blocks),
        in_specs=[pl.BlockSpec(memory_space=pltpu.SMEM),
                  spec(0), spec(blocks), spec(2 * blocks),
                  wspec(0), wspec(blocks), wspec(2 * blocks),
                  pl.BlockSpec((1, 4 * GDN_HEADS, n_chunks, CHUNK), lambda b, h: (b, 0, 0, 0))],
        out_specs=pl.BlockSpec((1, t, wd), lambda b, h: (b, 0, h)),
        out_shape=jax.ShapeDtypeStruct((bsz, t, D_MODEL), BF16),
        scratch_shapes=[pltpu.VMEM((heads, 2, n_chunks, CHUNK + HEAD_DIM, HEAD_DIM), BF16),
                        pltpu.VMEM((heads, 2, n_chunks, CHUNK, HEAD_DIM), F32),
                        pltpu.VMEM((heads, 2, n_chunks, HEAD_DIM, HEAD_DIM), F32),
                        pltpu.VMEM((heads, 2, n_chunks, 8, HEAD_DIM), F32)],
        compiler_params=_cparams(("parallel", "parallel")),
        name="gdn",
    )(par, qkv, qkv, qkv, conv_w9, conv_w9, conv_w9, gates_row)


def _odd_finish_kernel(x_ref, o_in_ref, z_ref, hw_ref, wo_ref, g_ref, o_ref):
    o = o_in_ref[0].astype(F32)
    parts = []
    for h in range(GDN_HEADS):
        sl = slice(h * HEAD_DIM, (h + 1) * HEAD_DIM)
        parts.append(_rms(o[:, sl], hw_ref[:, sl]))
    z = z_ref[0].astype(F32)
    y = jnp.concatenate(parts, axis=-1) * (z * jax.nn.sigmoid(z))
    o_ref[0] = x_ref[0] + g_ref[0, 0] * _dot(y, wo_ref[...])


def _odd_finish(xc, o, z, mods, layer, hw, wo, tm, n_ctx):
    bsz, t, d = xc.shape
    assert n_ctx % tm == 0 and (t - n_ctx) % tm == 0
    t0 = n_ctx // tm
    return pl.pallas_call(
        _odd_finish_kernel,
        grid=(bsz, (t - n_ctx) // tm),
        in_specs=[pl.BlockSpec((1, tm, d), lambda b, i: (b, i + t0, 0)),
                  pl.BlockSpec((1, tm, d), lambda b, i: (b, i + t0, 0)),
                  pl.BlockSpec((1, tm, d), lambda b, i: (b, i + t0, 0)),
                  pl.BlockSpec((1, d), lambda b, i: (0, 0)),
                  pl.BlockSpec((d, d), lambda b, i: (0, 0)),
                  _lat_mod_spec(layer, 2)],
        out_specs=pl.BlockSpec((1, tm, d), lambda b, i: (b, i, 0)),
        out_shape=jax.ShapeDtypeStruct((bsz, t - n_ctx, d), F32),
        compiler_params=_cparams(("parallel", "parallel")),
        name="odd_finish",
    )(xc, o, z, hw.reshape(1, d), wo, mods)


def _chunked(gates_t):
    bsz, g, t = gates_t.shape
    return gates_t.reshape(bsz, g, t // CHUNK, CHUNK)


def _pick_tile(total, target, at_least, multiple=8):
    for tile in range(min(target, total), multiple - 1, -1):
        if total % tile == 0 and tile % multiple == 0 and tile >= at_least:
            return tile
    raise ValueError(f"no row tile for {total} rows")


def _pad_cols(w, n):
    return jnp.pad(w, ((0, 0), (0, n - w.shape[1])))


def kernel(x, c, ctx, c_ctx, ada_w, ada_b, norm1_w, norm2_w, ffn_w1, ffn_w3, ffn_w2, final_norm_w,
           ev_w_in, ev_i_bias, ev_f_bias, ev_head_norm_w, ev_lam_re, ev_lam_im, ev_log_dt,
           ev_b_re, ev_b_im, ev_c_re, ev_c_im, ev_d, ev_w_glu, ev_w_out,
           od_w_in, od_conv_w, od_a_log, od_dt_bias, od_head_norm_w, od_w_out):
    bsz, seq, d = x.shape
    n_ctx = ctx.shape[1]
    assert d == D_MODEL and seq % GRID_W == 0 and n_ctx % CHUNK == 0 and bsz % 8 == 0
    assert ada_w.shape[0] == 2
    t_all = n_ctx + seq
    tm = _pick_tile(t_all, 768, n_ctx, LANES)
    tm_lat = _pick_tile(math.gcd(seq, n_ctx), 512, 0)
    th = FFN_HIDDEN // 2
    tb = CHUNK

    if tm % n_ctx == 0 and seq % n_ctx == 0:
        xs = (ctx, x)
    else:
        xs = (jnp.concatenate([ctx, x], axis=1),)
    mod_rows = 16
    cvec = jnp.zeros((mod_rows, d), F32).at[:bsz].set(c).at[bsz].set(c_ctx)
    mods = _ada(cvec, ada_w, ada_b).reshape(2, mod_rows, 1, 6 * d)
    w1, w3, w2 = ffn_w1.astype(BF16), ffn_w3.astype(BF16), ffn_w2.astype(BF16)

    qkv_w, n_g = 3 * MLSTM_WIDTH, 4 * MLSTM_HEADS
    w_in = ev_w_in[0]
    g0 = qkv_w + MLSTM_WIDTH
    w_even = jnp.concatenate([w_in[:, :g0], w_in[:, g0 + n_g:], _pad_cols(w_in[:, g0:g0 + n_g], LANES)],
                             axis=1).astype(BF16)
    c1, c2, c3 = qkv_w, g0, g0 + S5_WIDTH
    qkv, o_pre, u, gates = _inproj(
        xs, norm1_w[0], mods, 0, w_even,
        [(0, c1, "row", BF16, c1), (c1, c2, "row", BF16, c2 - c1), (c2, c3, "row", BF16, S5_WIDTH),
         (c3, c3 + LANES, "gate", F32, n_g)], tm, n_ctx)
    bias = jnp.concatenate([ev_i_bias[0].reshape(-1), ev_f_bias[0].reshape(-1)]).astype(F32)
    ha = _mlstm(qkv, _chunked(gates), bias, n_ctx)
    bd, cdt, avec = _s5_prep(ev_lam_re[0], ev_lam_im[0], ev_log_dt[0], ev_b_re[0], ev_b_im[0],
                             ev_c_re[0], ev_c_im[0])
    yf, yb = _s5(u, bd, cdt, avec, n_ctx, tb)
    xc = _even_finish(xs, ha, o_pre, yf, yb, u, mods, 0, ev_head_norm_w[0], ev_d[0],
                      ev_w_glu[0].astype(BF16), ev_w_out[0].astype(BF16), tm, n_ctx)
    xc = _ffn(xc, norm2_w[0], mods, 0, w1[0], w3[0], w2[0], final_norm_w, tm, th, n_ctx, False)

    n_g = 4 * GDN_HEADS
    w_odd = _pad_cols(od_w_in[0], 4 * d + LANES).astype(BF16)
    qkv_pre, z, gates = _inproj(
        (xc,), norm1_w[1], mods, 1, w_odd,
        [(0, GDN_QKV, "row", BF16, GDN_QKV), (GDN_QKV, 4 * d, "row", BF16, d),
         (4 * d, 4 * d + LANES, "gate", F32, n_g)], tm, n_ctx)
    par = jnp.concatenate([od_a_log[0].reshape(-1), od_dt_bias[0].reshape(-1)]).astype(F32)
    o = _gdn(qkv_pre, od_conv_w[0], _chunked(gates), par, n_ctx)
    xl = _odd_finish(xc, o, z, mods, 1, od_head_norm_w[0], od_w_out[0].astype(BF16), tm_lat, n_ctx)
    return _ffn(xl, norm2_w[1], mods, 1, w1[1], w3[1], w2[1], final_norm_w, _pick_tile(seq, 512, 0),
                th, 0, True)
```

```python
import functools
import math

import jax
import jax.numpy as jnp
from jax import lax
from jax.experimental import pallas as pl
from jax.experimental.pallas import tpu as pltpu

F32 = jnp.float32
BF16 = jnp.bfloat16

D_MODEL = 1024
CHUNK = 64
GRID_W = 64
EPS = 1e-6
LANES = 128
HEAD_DIM = 128
MLSTM_WIDTH = D_MODEL // 2
MLSTM_HEADS = MLSTM_WIDTH // HEAD_DIM
S5_WIDTH = D_MODEL - MLSTM_WIDTH
S5_GROUP = 16
S5_GROUPS = S5_WIDTH // S5_GROUP
S5_STATE = 64
S5_HALVES = 2
S5_HALF_W = S5_WIDTH // S5_HALVES
S5_HALF_N = (S5_GROUPS // S5_HALVES) * S5_STATE
GDN_HEADS = D_MODEL // HEAD_DIM
GDN_QKV = 3 * D_MODEL
FFN_HIDDEN = ((8 * D_MODEL + 3 * 256 - 1) // (3 * 256)) * 256
VMEM_LIMIT = 56 * 1024 * 1024

_NT = (((1,), (1,)), ((), ()))
_TN = (((0,), (0,)), ((), ()))


def _cparams(sem):
    return pltpu.CompilerParams(dimension_semantics=sem, vmem_limit_bytes=VMEM_LIMIT)


def _dot(a, b):
    return jnp.dot(a.astype(BF16), b.astype(BF16), preferred_element_type=F32)


def _dot_nt(a, b):
    return lax.dot_general(a.astype(BF16), b.astype(BF16), _NT, preferred_element_type=F32)


def _dot_tn(a, b):
    return lax.dot_general(a.astype(BF16), b.astype(BF16), _TN, preferred_element_type=F32)


def _rms(x, w):
    return x * lax.rsqrt(jnp.mean(x * x, axis=-1, keepdims=True) + EPS) * w


def _ada_kernel(s_ref, w_ref, b_ref, o_ref):
    s = s_ref[...]
    s = s * jax.nn.sigmoid(s)
    o_ref[0] = _dot(s, w_ref[0]) + b_ref[0]


def _ada(cvec, ada_w, ada_b):
    depth, d, n = ada_w.shape
    tn = 1536
    rows = cvec.shape[0]
    return pl.pallas_call(
        _ada_kernel,
        grid=(depth, n // tn),
        in_specs=[pl.BlockSpec((rows, d), lambda l, j: (0, 0)),
                  pl.BlockSpec((1, d, tn), lambda l, j: (l, 0, j)),
                  pl.BlockSpec((1, 1, tn), lambda l, j: (l, 0, j))],
        out_specs=pl.BlockSpec((1, rows, tn), lambda l, j: (l, 0, j)),
        out_shape=jax.ShapeDtypeStruct((depth, rows, n), F32),
        compiler_params=_cparams(("parallel", "parallel")),
        name="ada_mod",
    )(cvec, ada_w, ada_b.reshape(depth, 1, n))


def _ctx_mod_spec(layer, chunk, ctx_row):
    return pl.BlockSpec((1, 1, 1, D_MODEL), lambda b, i: (layer, ctx_row, 0, chunk))


def _lat_mod_spec(layer, chunk):
    return pl.BlockSpec((1, 1, 1, D_MODEL), lambda b, i: (layer, b, 0, chunk))


def _row_groups(n_top, tm, x_refs, ctx_refs, lat_refs):
    lat = [r[0, 0] for r in lat_refs]
    if n_top == 0:
        return [(slice(0, tm), x_refs[0][0], lat)]
    first = pl.program_id(1) == 0
    top = [jnp.where(first, c[0, 0], l) for c, l in zip(ctx_refs, lat)]
    if len(x_refs) == 1:
        x_ref = x_refs[0]
        return [(slice(0, n_top), x_ref[0, :n_top, :], top), (slice(n_top, tm), x_ref[0, n_top:, :], lat)]
    ctx_ref, subs = x_refs[0], x_refs[1:]
    groups = [(slice(0, n_top), jnp.where(first, ctx_ref[0], subs[0][0]), top)]
    return groups + [(slice(k * n_top, (k + 1) * n_top), subs[k][0], lat) for k in range(1, len(subs))]


def _x_operands(xs, tm, n_ctx):
    d = xs[0].shape[-1]
    if len(xs) == 1:
        return list(xs), [pl.BlockSpec((1, tm, d), lambda b, i: (b, i, 0))]
    ctx, lat = xs
    nsub = tm // n_ctx
    assert tm % n_ctx == 0 and lat.shape[1] % n_ctx == 0
    specs = [pl.BlockSpec((1, n_ctx, d), lambda b, i: (b, 0, 0))]
    for k in range(nsub):
        specs.append(pl.BlockSpec((1, n_ctx, d), lambda b, i, k=k: (b, jnp.maximum(i * nsub + k - 1, 0), 0)))
    return [ctx] + [lat] * nsub, specs


def _inproj_kernel(*refs, outs, n_top, n_chunk, n_x):
    x_refs, (nw_ref, shc_ref, scc_ref, shl_ref, scl_ref, w_ref) = refs[:n_x], refs[n_x:n_x + 6]
    out_refs, h_scr, g_scr = refs[n_x + 6:-2], refs[-2], refs[-1]
    tm = h_scr.shape[0]
    for rows, x, (sh, sc) in _row_groups(n_top, tm, x_refs, (shc_ref, scc_ref), (shl_ref, scl_ref)):
        h_scr[rows, :] = (_rms(x, nw_ref[...]) * (1.0 + sc) + sh).astype(BF16)
    for o_ref, (c0, c1, kind) in zip(out_refs, outs):
        if kind == "row" and (c0 % LANES or c1 % LANES):
            a0, a1 = c0 // LANES * LANES, -(-c1 // LANES) * LANES
            val = jnp.dot(h_scr[...], w_ref[:, a0:a1], preferred_element_type=F32)
            o_ref[0] = val[:, c0 - a0:c1 - a0].astype(o_ref.dtype)
            continue
        for n0 in range(c0, c1, n_chunk):
            n1 = min(n0 + n_chunk, c1)
            val = jnp.dot(h_scr[...], w_ref[:, n0:n1], preferred_element_type=F32)
            if kind == "row":
                o_ref[0, :, n0 - c0:n1 - c0] = val.astype(o_ref.dtype)
            else:
                g_scr[...] = val
                o_ref[0] = g_scr[...].T[:o_ref.shape[1], :].astype(o_ref.dtype)


def _inproj(xs, norm_w, mods, layer, w_bf16, outs, tm, n_ctx):
    bsz, d = xs[0].shape[0], xs[0].shape[-1]
    t = sum(a.shape[1] for a in xs)
    n = w_bf16.shape[1]
    x_args, x_specs = _x_operands(xs, tm, n_ctx)
    assert t % tm == 0 and (n_ctx == 0 or n_ctx <= tm)
    out_specs, out_shapes = [], []
    for c0, c1, kind, dtype, wd in outs:
        if kind == "row":
            out_specs.append(pl.BlockSpec((1, tm, wd), lambda b, i: (b, i, 0)))
            out_shapes.append(jax.ShapeDtypeStruct((bsz, t, wd), dtype))
        else:
            assert c1 - c0 == LANES and tm % LANES == 0
            out_specs.append(pl.BlockSpec((1, wd, tm), lambda b, i: (b, 0, i)))
            out_shapes.append(jax.ShapeDtypeStruct((bsz, wd, t), dtype))
    ctx_spec = lambda c: _ctx_mod_spec(layer, c, bsz)
    lat_spec = lambda c: _lat_mod_spec(layer, c)
    const = lambda shape: pl.BlockSpec(shape, lambda b, i: (0, 0), pipeline_mode=pl.Buffered(1))
    return pl.pallas_call(
        functools.partial(_inproj_kernel, outs=tuple(o[:3] for o in outs), n_top=n_ctx, n_chunk=512,
                          n_x=len(x_args)),
        grid=(bsz, t // tm),
        in_specs=x_specs + [const((1, d)), ctx_spec(0), ctx_spec(1), lat_spec(0), lat_spec(1),
                            const((d, n))],
        out_specs=out_specs,
        out_shape=out_shapes,
        scratch_shapes=[pltpu.VMEM((tm, d), BF16), pltpu.VMEM((tm, LANES), F32)],
        compiler_params=_cparams(("parallel", "parallel")),
        name=f"inproj_l{layer}",
    )(*x_args, norm_w.reshape(1, d), mods, mods, mods, mods, w_bf16)


def _ffn_kernel(x_ref, nw_ref, shc_ref, scc_ref, gc_ref, shl_ref, scl_ref, gl_ref,
                w1_ref, w3_ref, w2_ref, fw_ref, o_ref, h_scr, *, final, n_top, th):
    tm, hid = h_scr.shape[0], w1_ref.shape[1]
    groups = _row_groups(n_top, tm, (x_ref,), (shc_ref, scc_ref, gc_ref), (shl_ref, scl_ref, gl_ref))
    for rows, x, (sh, sc, _) in groups:
        h_scr[rows, :] = (_rms(x, nw_ref[...]) * (1.0 + sc) + sh).astype(BF16)
    acc = None
    for c0 in range(0, hid, th):
        h = h_scr[...]
        a = jnp.dot(h, w1_ref[:, c0:c0 + th], preferred_element_type=F32)
        g = jnp.dot(h, w3_ref[:, c0:c0 + th], preferred_element_type=F32)
        t = ((a * jax.nn.sigmoid(a)) * g).astype(BF16)
        part = jnp.dot(t, w2_ref[c0:c0 + th, :], preferred_element_type=F32)
        acc = part if acc is None else acc + part
    for rows, x, (_, _, gate) in groups:
        y = x + gate * acc[rows, :]
        if final:
            y = _rms(y, fw_ref[...])
        o_ref[0, rows, :] = y


def _ffn(x, norm_w, mods, layer, w1, w3, w2, final_w, tm, th, n_ctx, final):
    bsz, tx, d = x.shape
    hid = w1.shape[1]
    assert n_ctx <= tm and tx % tm == 0 and hid % th == 0
    ctx_spec = lambda c: _ctx_mod_spec(layer, c, bsz)
    lat_spec = lambda c: _lat_mod_spec(layer, c)
    const = lambda shape: pl.BlockSpec(shape, lambda b, i: (0, 0), pipeline_mode=pl.Buffered(1))
    return pl.pallas_call(
        functools.partial(_ffn_kernel, final=final, n_top=n_ctx, th=th),
        grid=(bsz, tx // tm),
        in_specs=[pl.BlockSpec((1, tm, d), lambda b, i: (b, i, 0)),
                  const((1, d)),
                  ctx_spec(3), ctx_spec(4), ctx_spec(5), lat_spec(3), lat_spec(4), lat_spec(5),
                  const((d, hid)), const((d, hid)), const((hid, d)), const((1, d))],
        out_specs=pl.BlockSpec((1, tm, d), lambda b, i: (b, i, 0)),
        out_shape=jax.ShapeDtypeStruct((bsz, tx, d), F32),
        scratch_shapes=[pltpu.VMEM((tm, d), BF16)],
        compiler_params=_cparams(("parallel", "parallel")),
        name=f"ffn_l{layer}",
    )(x, norm_w.reshape(1, d), mods, mods, mods, mods, mods, mods, w1, w3, w2, final_w.reshape(1, d))


def _chunk_masks():
    row = lax.broadcasted_iota(jnp.int32, (CHUNK, CHUNK), 0)
    col = lax.broadcasted_iota(jnp.int32, (CHUNK, CHUNK), 1)
    return row, col


def _to_col(row_vec, eye):
    return jnp.sum(jnp.where(eye, jnp.broadcast_to(row_vec, eye.shape), 0.0), axis=1, keepdims=True)


def _to_row(col_vec, eye):
    return jnp.sum(jnp.where(eye, jnp.broadcast_to(col_vec, eye.shape), 0.0), axis=0, keepdims=True)


def _scan_chunk_index(i, rev, n_chunks, ctx_chunks):
    if not rev:
        return i
    return jnp.where(i < ctx_chunks, ctx_chunks - 1 - i, n_chunks + ctx_chunks - 1 - i)


def _mlstm_kernel(bias_ref, q_ref, k_ref, v_ref, g_ref, o_ref, num_scr, cl_scr, mi_scr, fc_scr, sc_scr,
                  *, n_chunks, ctx_chunks, group):
    hd = pl.program_id(1)
    row, col = _chunk_masks()
    eye = row == col
    kscale = 1.0 / math.sqrt(HEAD_DIM)
    ones_blk = jnp.ones((CHUNK, HEAD_DIM), BF16)

    def gate_terms(r, ig_raw, f_raw):
        incl = (col >= row) if r == 1 else (col <= row)
        ig_row = ig_raw + bias_ref[r * MLSTM_HEADS + hd]
        lf_row = jax.nn.log_sigmoid(f_raw + bias_ref[(2 + r) * MLSTM_HEADS + hd])
        f_col = jnp.sum(jnp.where(incl, jnp.broadcast_to(lf_row, incl.shape), 0.0),
                        axis=1, keepdims=True)
        f_row = _to_row(f_col, eye)
        f_last = jnp.sum(lf_row, axis=1, keepdims=True)
        dm = jnp.where(incl, f_col - f_row + ig_row, -jnp.inf)
        m_intra = jnp.max(dm, axis=1, keepdims=True)
        w_col = f_last - f_col + _to_col(ig_row, eye)
        m_loc = jnp.max(w_col, axis=0, keepdims=True)
        return f_col, f_last, jnp.exp(dm - m_intra), m_intra, jnp.exp(w_col - m_loc), m_loc

    def local_group(gi, _):
        chunks = [gi * group + g for g in range(group)]
        loaded = []
        for c in chunks:
            rows = pl.ds(pl.multiple_of(c * CHUNK, CHUNK), CHUNK)
            gates = [(g_ref[0, r * MLSTM_HEADS + hd, pl.ds(c, 1), :],
                      g_ref[0, (2 + r) * MLSTM_HEADS + hd, pl.ds(c, 1), :]) for r in (0, 1)]
            v1 = jnp.concatenate([v_ref[0, rows, :].astype(BF16), ones_blk], axis=-1)
            loaded.append((q_ref[0, rows, :], k_ref[0, rows, :].astype(F32) * kscale, v1, gates))
        chains = [(g, r) for g in range(group) for r in (0, 1)]
        terms = [gate_terms(r, *loaded[g][3][r]) for g, r in chains]
        qks = [_dot_nt(q, k) for q, k, _, _ in loaded]
        c_locs = [_dot_tn(tm[4] * loaded[g][1], loaded[g][2]) for (g, r), tm in zip(chains, terms)]
        nums = [_dot(tm[2] * qks[g], loaded[g][2]) for (g, r), tm in zip(chains, terms)]
        for (g, r), tm, c_loc, num in zip(chains, terms, c_locs, nums):
            f_col, f_last, _, m_intra, _, m_loc = tm
            c = chunks[g]
            num_scr[r, c] = num
            cl_scr[r, c] = c_loc
            mi_scr[r, c] = jnp.broadcast_to(m_intra, (CHUNK, HEAD_DIM))
            fc_scr[r, c] = jnp.broadcast_to(f_col, (CHUNK, HEAD_DIM))
            sc_scr[r, c, :8, :] = jnp.broadcast_to(f_last, (8, HEAD_DIM))
            sc_scr[r, c, 8:, :] = jnp.broadcast_to(m_loc, (8, HEAD_DIM))
        return 0

    lax.fori_loop(0, n_chunks // group, local_group, 0)
    o_ref[...] = jnp.zeros_like(o_ref)

    def step(i, carry):
        cs = [_scan_chunk_index(i, r == 1, n_chunks, ctx_chunks) for r in (0, 1)]
        rows = [pl.ds(pl.multiple_of(c * CHUNK, CHUNK), CHUNK) for c in cs]
        qcs = [_dot(q_ref[0, rw, :], st[0]) for rw, st in zip(rows, carry)]
        new = []
        for r in (0, 1):
            s_st, m_st = carry[r]
            c = cs[r]
            mi, na = mi_scr[r, c], num_scr[r, c]
            inter = fc_scr[r, c] + m_st[0:1, :]
            m_t = jnp.maximum(mi, inter)
            a_loc_t, a_inter = jnp.exp(mi - m_t), jnp.exp(inter - m_t)
            num = a_loc_t * na[:, :HEAD_DIM] + a_inter * qcs[r][:, :HEAD_DIM]
            den = a_loc_t * na[:, HEAD_DIM:] + a_inter * qcs[r][:, HEAD_DIM:]
            out = num / jnp.maximum(jnp.abs(den), jnp.exp(-m_t))
            o_ref[0, rows[r], :] = (o_ref[0, rows[r], :].astype(F32) + out).astype(o_ref.dtype)
            f_last, m_loc = sc_scr[r, c, :8, :], sc_scr[r, c, 8:, :]
            m_new = jnp.maximum(f_last + m_st, m_loc)
            a_prev = jnp.exp(f_last + m_st - m_new)[0:1, :]
            a_loc = jnp.exp(m_loc - m_new)[0:1, :]
            a_prev = jnp.concatenate([a_prev, a_prev], axis=1)
            a_loc = jnp.concatenate([a_loc, a_loc], axis=1)
            new.append((a_prev * s_st + a_loc * cl_scr[r, c], m_new))
        return tuple(new)

    init = (jnp.zeros((HEAD_DIM, 2 * HEAD_DIM), F32), jnp.zeros((8, HEAD_DIM), F32))
    lax.fori_loop(0, n_chunks, step, (init, init), unroll=2)


def _mlstm(qkv, gates_row, bias, n_ctx):
    bsz, t, _ = qkv.shape
    n_chunks = t // CHUNK
    qkv_spec = lambda off: pl.BlockSpec((1, t, HEAD_DIM), lambda b, h: (b, 0, off + h))
    return pl.pallas_call(
        functools.partial(_mlstm_kernel, n_chunks=n_chunks, ctx_chunks=n_ctx // CHUNK,
                          group=math.gcd(9, n_chunks)),
        grid=(bsz, MLSTM_HEADS),
        in_specs=[pl.BlockSpec(memory_space=pltpu.SMEM),
                  qkv_spec(0), qkv_spec(MLSTM_HEADS), qkv_spec(2 * MLSTM_HEADS),
                  pl.BlockSpec((1, 4 * MLSTM_HEADS, n_chunks, CHUNK), lambda b, h: (b, 0, 0, 0))],
        out_specs=pl.BlockSpec((1, t, HEAD_DIM), lambda b, h: (b, 0, h)),
        out_shape=jax.ShapeDtypeStruct((bsz, t, MLSTM_WIDTH), BF16),
        scratch_shapes=[pltpu.VMEM((2, n_chunks, CHUNK, 2 * HEAD_DIM), F32),
                        pltpu.VMEM((2, n_chunks, HEAD_DIM, 2 * HEAD_DIM), F32),
                        pltpu.VMEM((2, n_chunks, CHUNK, HEAD_DIM), F32),
                        pltpu.VMEM((2, n_chunks, CHUNK, HEAD_DIM), F32),
                        pltpu.VMEM((2, n_chunks, 16, HEAD_DIM), F32)],
        compiler_params=_cparams(("parallel", "parallel")),
        name="mlstm",
    )(bias, qkv, qkv, qkv, gates_row)


def _s5_prep_kernel(lr_ref, li_ref, ldt_ref, br_ref, bi_ref, cr_ref, ci_ref, bd_ref, cdt_ref, a_ref):
    lr, li = lr_ref[0, 0], li_ref[0, 0]
    dt = jnp.exp(ldt_ref[0, 0])
    mag, ang = jnp.exp(lr * dt), li * dt
    ab_re, ab_im = mag * jnp.cos(ang), mag * jnp.sin(ang)
    nr, ni = ab_re - 1.0, ab_im
    den = lr * lr + li * li
    co_re = (nr * lr + ni * li) / den
    co_im = (ni * lr - nr * li) / den
    b_re, b_im = br_ref[0, 0], bi_ref[0, 0]
    bb_re = co_re * b_re - co_im * b_im
    bb_im = co_re * b_im + co_im * b_re
    c_re, c_im = cr_ref[0, 0], ci_ref[0, 0]
    lane_group = lax.broadcasted_iota(jnp.int32, (S5_GROUP, S5_HALF_N), 1) // S5_STATE
    n = S5_HALF_N
    for g in range(S5_GROUPS // S5_HALVES):
        sel = lane_group == g
        rows = slice(g * S5_GROUP, (g + 1) * S5_GROUP)
        bd_ref[0, 0, rows, :n] = jnp.where(sel, bb_re, 0.0).astype(bd_ref.dtype)
        bd_ref[0, 0, rows, n:] = jnp.where(sel, bb_im, 0.0).astype(bd_ref.dtype)
        cdt_ref[0, 0, rows, :n] = jnp.where(sel, c_re, 0.0).astype(cdt_ref.dtype)
        cdt_ref[0, 0, rows, n:] = jnp.where(sel, -c_im, 0.0).astype(cdt_ref.dtype)
    a_ref[0, 0, :, :n] = jnp.broadcast_to(ab_re, (8, n))
    a_ref[0, 0, :, n:] = jnp.broadcast_to(ab_im, (8, n))


def _s5_prep(lam_re, lam_im, log_dt, b_re, b_im, c_re, c_im):
    gh = S5_GROUPS // S5_HALVES
    vec = lambda a: a.reshape(2, S5_HALVES, 1, S5_HALF_N)
    ldt = vec(jnp.broadcast_to(log_dt[:, :, None], (2, S5_GROUPS, S5_STATE)))
    bt = lambda a: a.reshape(2, S5_HALVES, gh, S5_STATE, S5_GROUP).transpose(0, 1, 4, 2, 3).reshape(
        2, S5_HALVES, S5_GROUP, S5_HALF_N)
    ct = lambda a: a.reshape(2, S5_HALVES, gh, S5_GROUP, S5_STATE).transpose(0, 1, 3, 2, 4).reshape(
        2, S5_HALVES, S5_GROUP, S5_HALF_N)
    vspec = pl.BlockSpec((1, 1, 1, S5_HALF_N), lambda r, h: (r, h, 0, 0))
    mspec = pl.BlockSpec((1, 1, S5_GROUP, S5_HALF_N), lambda r, h: (r, h, 0, 0))
    ospec = pl.BlockSpec((1, 1, S5_HALF_W, 2 * S5_HALF_N), lambda r, h: (r, h, 0, 0))
    return pl.pallas_call(
        _s5_prep_kernel,
        grid=(2, S5_HALVES),
        in_specs=[vspec, vspec, vspec, mspec, mspec, mspec, mspec],
        out_specs=[ospec, ospec, pl.BlockSpec((1, 1, 8, 2 * S5_HALF_N), lambda r, h: (r, h, 0, 0))],
        out_shape=[jax.ShapeDtypeStruct((2, S5_HALVES, S5_HALF_W, 2 * S5_HALF_N), BF16),
                   jax.ShapeDtypeStruct((2, S5_HALVES, S5_HALF_W, 2 * S5_HALF_N), BF16),
                   jax.ShapeDtypeStruct((2, S5_HALVES, 8, 2 * S5_HALF_N), F32)],
        compiler_params=_cparams(("parallel", "parallel")),
        name="s5_prep",
    )(vec(lam_re), vec(lam_im), ldt, bt(b_re), bt(b_im), ct(c_re), ct(c_im))


def _s5_kernel(uf0_ref, uf1_ref, ub0_ref, ub1_ref, bd_ref, cdt_ref, a_ref, yf_ref, yb_ref,
               lhs_scr, bu_scr, s_scr, st_scr, *, tb, bsz):
    n = S5_HALF_N
    rows = tb * bsz
    u_refs = ((uf0_ref, uf1_ref), (ub0_ref, ub1_ref))
    y_refs = (yf_ref, yb_ref)

    @pl.when(pl.program_id(1) == 0)
    def _():
        st_scr[...] = jnp.zeros_like(st_scr)

    for d in (0, 1):
        for j in (0, 1):
            lhs_scr[d, :, j * LANES:(j + 1) * LANES] = jnp.transpose(
                u_refs[d][j][...].astype(F32), (1, 0, 2)).reshape(rows, LANES)
    for d in (0, 1):
        bu_scr[d] = _dot(lhs_scr[d], bd_ref[d, 0])
    for d in (0, 1):
        a_re, a_im = a_ref[d, 0, :, :n], a_ref[d, 0, :, n:]
        s_re, s_im = st_scr[d, :, :n], st_scr[d, :, n:]
        for j in range(tb):
            t = j if d == 0 else tb - 1 - j
            sl = slice(t * bsz, (t + 1) * bsz)
            s_re, s_im = (a_re * s_re - a_im * s_im + bu_scr[d, sl, :n],
                          a_re * s_im + a_im * s_re + bu_scr[d, sl, n:])
            s_scr[d, sl, :n] = s_re
            s_scr[d, sl, n:] = s_im
        st_scr[d, :, :n] = s_re
        st_scr[d, :, n:] = s_im
        y = _dot_nt(s_scr[d], cdt_ref[d, 0])
        for j in (0, 1):
            y_refs[d][:, :, j * LANES:(j + 1) * LANES] = jnp.transpose(
                y[:, j * LANES:(j + 1) * LANES].reshape(tb, bsz, LANES), (1, 0, 2)).astype(y_refs[d].dtype)


def _s5(u, bd, cdt, avec, n_ctx, tb):
    bsz, t, _ = u.shape
    nb, ncb = t // tb, n_ctx // tb
    assert bsz == 8 and S5_HALF_W == 2 * LANES

    def rev(i):
        return jnp.where(i < ncb, ncb - 1 - i, nb + ncb - 1 - i)

    fwd = lambda i: i
    uspec = lambda order, j: pl.BlockSpec((bsz, tb, LANES), lambda h, i: (0, order(i), 2 * h + j))
    yspec = lambda order: pl.BlockSpec((bsz, tb, S5_HALF_W), lambda h, i: (0, order(i), h))
    wspec = pl.BlockSpec((2, 1, S5_HALF_W, 2 * S5_HALF_N), lambda h, i: (0, h, 0, 0))
    rows = tb * bsz
    return pl.pallas_call(
        functools.partial(_s5_kernel, tb=tb, bsz=bsz),
        grid=(S5_HALVES, nb),
        in_specs=[uspec(fwd, 0), uspec(fwd, 1), uspec(rev, 0), uspec(rev, 1), wspec, wspec,
                  pl.BlockSpec((2, 1, 8, 2 * S5_HALF_N), lambda h, i: (0, h, 0, 0))],
        out_specs=[yspec(fwd), yspec(rev)],
        out_shape=[jax.ShapeDtypeStruct((bsz, t, S5_WIDTH), BF16)] * 2,
        scratch_shapes=[pltpu.VMEM((2, rows, S5_HALF_W), F32),
                        pltpu.VMEM((2, rows, 2 * S5_HALF_N), F32),
                        pltpu.VMEM((2, rows, 2 * S5_HALF_N), F32),
                        pltpu.VMEM((2, bsz, 2 * S5_HALF_N), F32)],
        compiler_params=_cparams(("parallel", "arbitrary")),
        name="s5_scan",
    )(u, u, u, u, bd, cdt, avec)


def _even_finish_kernel(*refs, n_top, n_x):
    x_refs = refs[:n_x]
    ha_ref, op_ref, yf_ref, yb_ref, u_ref, mhw_ref, ds_ref, wg_ref, wo_ref, gc_ref, gl_ref, o_ref = refs[n_x:]
    ha = ha_ref[0].astype(F32)
    parts = []
    for h in range(MLSTM_HEADS):
        sl = slice(h * HEAD_DIM, (h + 1) * HEAD_DIM)
        parts.append(_rms(ha[:, sl], mhw_ref[:, sl]))
    a_out = jnp.concatenate(parts, axis=-1) * jax.nn.sigmoid(op_ref[0].astype(F32))
    yb = jax.nn.gelu(yf_ref[0].astype(F32) + yb_ref[0].astype(F32) + ds_ref[...] * u_ref[0].astype(F32))
    glu = _dot(yb, wg_ref[...])
    b_out = glu[:, :S5_WIDTH] * jax.nn.sigmoid(glu[:, S5_WIDTH:])
    y = _dot(a_out, wo_ref[:MLSTM_WIDTH, :]) + _dot(b_out, wo_ref[MLSTM_WIDTH:, :])
    for rows, x, (gate,) in _row_groups(n_top, y.shape[0], x_refs, (gc_ref,), (gl_ref,)):
        o_ref[0, rows, :] = x + gate * y[rows, :]


def _even_finish(xs, ha, o_pre, yf, yb, u, mods, layer, mh_w, d_skip, wg, wo, tm, n_ctx):
    bsz, t, _ = ha.shape
    d = xs[0].shape[-1]
    assert t % tm == 0 and n_ctx <= tm
    x_args, x_specs = _x_operands(xs, tm, n_ctx)
    full = lambda shape: pl.BlockSpec(shape, lambda b, i: tuple(0 for _ in shape),
                                      pipeline_mode=pl.Buffered(1))
    return pl.pallas_call(
        functools.partial(_even_finish_kernel, n_top=n_ctx, n_x=len(x_args)),
        grid=(bsz, t // tm),
        in_specs=x_specs + [
                  pl.BlockSpec((1, tm, MLSTM_WIDTH), lambda b, i: (b, i, 0)),
                  pl.BlockSpec((1, tm, MLSTM_WIDTH), lambda b, i: (b, i, 0)),
                  pl.BlockSpec((1, tm, S5_WIDTH), lambda b, i: (b, i, 0)),
                  pl.BlockSpec((1, tm, S5_WIDTH), lambda b, i: (b, i, 0)),
                  pl.BlockSpec((1, tm, S5_WIDTH), lambda b, i: (b, i, 0)),
                  full((1, MLSTM_WIDTH)), full((1, S5_WIDTH)),
                  full((S5_WIDTH, 2 * S5_WIDTH)), full((d, d)),
                  _ctx_mod_spec(layer, 2, bsz), _lat_mod_spec(layer, 2)],
        out_specs=pl.BlockSpec((1, tm, d), lambda b, i: (b, i, 0)),
        out_shape=jax.ShapeDtypeStruct((bsz, t, d), F32),
        compiler_params=_cparams(("parallel", "parallel")),
        name="even_finish",
    )(*x_args, ha, o_pre, yf, yb, u, mh_w.reshape(1, -1), d_skip.reshape(1, -1), wg, wo, mods, mods)


def _conv_chunk(x_ref, w, lanes, c, static, n_chunks, ctx_chunks):
    tpos = lax.broadcasted_iota(jnp.int32, (CHUNK, HEAD_DIM), 0)
    first, last = tpos == 0, tpos == CHUNK - 1
    wrow = lambda k: w[k:k + 1, :]

    def rows(ci):
        start = ci * CHUNK if isinstance(ci, int) else pl.multiple_of(ci * CHUNK, CHUNK)
        return x_ref[0, pl.ds(start, CHUNK), lanes].astype(F32)

    if static and c < ctx_chunks:
        cur = rows(c)
        zero = jnp.zeros((1, HEAD_DIM), F32)
        prev_tok = rows(c - 1)[CHUNK - 1:CHUNK, :] if c > 0 else zero
        next_tok = rows(c + 1)[0:1, :] if c < ctx_chunks - 1 else zero
        left = jnp.where(first, prev_tok, pltpu.roll(cur, 1, axis=0))
        right = jnp.where(last, next_tok, pltpu.roll(cur, CHUNK - 1, axis=0))
        return wrow(3) * left + wrow(4) * cur + wrow(5) * right

    if static:
        up_i, dn_i = max(c - 1, ctx_chunks), min(c + 1, n_chunks - 1)
        w_up = w[0:3, :] if c > ctx_chunks else jnp.zeros((3, HEAD_DIM), F32)
        w_dn = w[6:9, :] if c < n_chunks - 1 else jnp.zeros((3, HEAD_DIM), F32)
    else:
        up_i, dn_i = jnp.maximum(c - 1, ctx_chunks), jnp.minimum(c + 1, n_chunks - 1)
        w_up = jnp.where(c > ctx_chunks, w[0:3, :], 0.0)
        w_dn = jnp.where(c < n_chunks - 1, w[6:9, :], 0.0)
    up, cur, dn = rows(up_i), rows(c), rows(dn_i)
    r0, r1, r2 = (w_up[dc:dc + 1, :] * up + wrow(3 + dc) * cur + w_dn[dc:dc + 1, :] * dn for dc in range(3))
    prev = jnp.where(first, 0.0, pltpu.roll(r0, 1, axis=0))
    nxt = jnp.where(last, 0.0, pltpu.roll(r2, CHUNK - 1, axis=0))
    return r1 + prev + nxt


def _unit_tri_inverses(mats, eye_f):
    ps = [eye_f - a for a in mats]
    pws = [_dot(a, a) for a in mats]
    k = 2
    while 2 * k < CHUNK:
        res = [_dot(jnp.concatenate([p, pw], axis=0), pw) for p, pw in zip(ps, pws)]
        ps = [p + r[:CHUNK] for p, r in zip(ps, res)]
        pws = [r[CHUNK:] for r in res]
        k *= 2
    return [p + _dot(p, pw) for p, pw in zip(ps, pws)]


def _gdn_kernel(par_ref, q_ref, k_ref, v_ref, wq_ref, wk_ref, wv_ref, g_ref, o_ref,
                qk_scr, ol_scr, nl_scr, gl_scr, *, n_chunks, ctx_chunks, group, heads):
    row, col = _chunk_masks()
    eye = row == col
    eye_f = eye.astype(F32)

    def gate_terms(hd, r, a_pre, b_pre):
        incl = (col >= row) if r == 1 else (col <= row)
        a_log = jnp.full((1, CHUNK), par_ref[r * GDN_HEADS + hd], F32)
        dt_b = par_ref[(2 + r) * GDN_HEADS + hd]
        g_row = -jnp.exp(a_log) * jax.nn.softplus(a_pre + dt_b)
        beta_col = _to_col(jax.nn.sigmoid(b_pre), eye)
        gc_col = jnp.sum(jnp.where(incl, jnp.broadcast_to(g_row, incl.shape), 0.0),
                         axis=1, keepdims=True)
        gc_row = _to_row(gc_col, eye)
        g_last = jnp.sum(g_row, axis=1, keepdims=True)
        gamma = jnp.exp(jnp.where(incl, gc_col - gc_row, -jnp.inf))
        return beta_col, gc_col, g_last, gamma

    for hh in range(heads):
        hd = pl.program_id(1) * heads + hh
        lanes = slice(hh * HEAD_DIM, (hh + 1) * HEAD_DIM)

        def load(c, static, hd=hd, lanes=lanes):
            gates = [(g_ref[0, r * GDN_HEADS + hd, pl.ds(c, 1), :],
                      g_ref[0, (2 + r) * GDN_HEADS + hd, pl.ds(c, 1), :]) for r in (0, 1)]
            q, k, v = (_conv_chunk(x_ref, w_ref[:, lanes], lanes, c, static, n_chunks, ctx_chunks)
                       for x_ref, w_ref in ((q_ref, wq_ref), (k_ref, wk_ref), (v_ref, wv_ref)))
            q, k, v = q * jax.nn.sigmoid(q), k * jax.nn.sigmoid(k), v * jax.nn.sigmoid(v)
            q = q * (lax.rsqrt(jnp.sum(q * q, axis=-1, keepdims=True) + EPS) * (1.0 / math.sqrt(HEAD_DIM)))
            k = k * lax.rsqrt(jnp.sum(k * k, axis=-1, keepdims=True) + EPS)
            return q, k, v, gates

        def local_group(gi, _, hh=hh, hd=hd, load=load):
            static = isinstance(gi, int)
            chunks = [gi * group + g for g in range(group)]
            loaded = [load(c, static) for c in chunks]
            kqs = [_dot_nt(jnp.concatenate([k, q], axis=0), k) for q, k, _, _ in loaded]
            chains = [(g, r) for g in range(group) for r in (0, 1)]
            terms = [gate_terms(hd, r, *loaded[g][3][r]) for g, r in chains]
            a_mats = []
            for (g, r), (beta_col, _, _, gamma) in zip(chains, terms):
                strict = (col > row) if r == 1 else (col < row)
                a_mats.append(jnp.where(strict, beta_col * kqs[g][:CHUNK] * gamma, 0.0))
            t_invs = _unit_tri_inverses(a_mats, eye_f)
            sols = []
            for (g, r), (beta_col, gc_col, _, _), t_inv in zip(chains, terms, t_invs):
                _, k, v, _ = loaded[g]
                rhs = jnp.concatenate([v * beta_col, k * (beta_col * jnp.exp(gc_col))], axis=-1)
                sols.append(_dot(t_inv, rhs))
            auws = [_dot(kqs[g][CHUNK:] * tm[3], sol) for (g, r), tm, sol in zip(chains, terms, sols)]
            kuws = [_dot_tn(loaded[g][1] * jnp.exp(tm[2] - tm[1]), sol)
                    for (g, r), tm, sol in zip(chains, terms, sols)]
            for (g, r), (_, gc_col, g_last, _), auw, kuw in zip(chains, terms, auws, kuws):
                c = chunks[g]
                q = loaded[g][0]
                qk_scr[hh, r, c, :CHUNK, :] = (q * jnp.exp(gc_col) - auw[:, HEAD_DIM:]).astype(qk_scr.dtype)
                qk_scr[hh, r, c, CHUNK:, :] = kuw[:, HEAD_DIM:].astype(qk_scr.dtype)
                ol_scr[hh, r, c] = auw[:, :HEAD_DIM].astype(ol_scr.dtype)
                nl_scr[hh, r, c] = kuw[:, :HEAD_DIM].astype(nl_scr.dtype)
                gl_scr[hh, r, c] = jnp.broadcast_to(jnp.exp(g_last), (8, HEAD_DIM))
            return 0

        n_static = -(-ctx_chunks // group)
        for gi in range(n_static):
            local_group(gi, 0)
        lax.fori_loop(n_static, n_chunks // group, local_group, 0)

    o_ref[...] = jnp.zeros_like(o_ref)
    streams = [(hh, r) for hh in range(heads) for r in (0, 1)]

    def step(i, carry):
        cs = [_scan_chunk_index(i, r == 1, n_chunks, ctx_chunks) for _, r in streams]
        res = [jnp.dot(qk_scr[hh, r, c], s.astype(BF16), preferred_element_type=F32)
               for (hh, r), c, s in zip(streams, cs, carry)]
        new = []
        for (hh, r), c, s, rs in zip(streams, cs, carry, res):
            rows = pl.ds(pl.multiple_of(c * CHUNK, CHUNK), CHUNK)
            lanes = slice(hh * HEAD_DIM, (hh + 1) * HEAD_DIM)
            o_ref[0, rows, lanes] = (o_ref[0, rows, lanes].astype(F32) + ol_scr[hh, r, c]
                                     + rs[:CHUNK]).astype(o_ref.dtype)
            new.append(gl_scr[hh, r, c][0:1, :] * s - rs[CHUNK:] + nl_scr[hh, r, c])
        return tuple(new)

    init = jnp.zeros((HEAD_DIM, HEAD_DIM), F32)
    lax.fori_loop(0, n_chunks, step, (init,) * len(streams), unroll=2)


def _gdn(qkv, conv_w, gates_row, par, n_ctx, heads=4):
    bsz, t, _ = qkv.shape
    assert GRID_W == CHUNK and n_ctx % CHUNK == 0
    n_chunks = t // CHUNK
    group = math.gcd(9, n_chunks)
    wd = heads * HEAD_DIM
    blocks = GDN_HEADS // heads
    spec = lambda off: pl.BlockSpec((1, t, wd), lambda b, h: (b, 0, off + h), pipeline_mode=pl.Buffered(1))
    wspec = lambda off: pl.BlockSpec((9, wd), lambda b, h: (0, off + h))
    conv_w9 = conv_w.reshape(9, GDN_QKV)
    return pl.pallas_call(
        functools.partial(_gdn_kernel, n_chunks=n_chunks, ctx_chunks=n_ctx // CHUNK, group=group,
                          heads=heads),
        grid=(bsz, blocks),
        in_specs=[pl.BlockSpec(memory_space=pltpu.SMEM),
                  spec(0), spec(blocks), spec(2 * blocks),
                  wspec(0), wspec(blocks), wspec(2 * blocks),
                  pl.BlockSpec((1, 4 * GDN_HEADS, n_chunks, CHUNK), lambda b, h: (b, 0, 0, 0))],
        out_specs=pl.BlockSpec((1, t, wd), lambda b, h: (b, 0, h)),
        out_shape=jax.ShapeDtypeStruct((bsz, t, D_MODEL), BF16),
        scratch_shapes=[pltpu.VMEM((heads, 2, n_chunks, CHUNK + HEAD_DIM, HEAD_DIM), BF16),
                        pltpu.VMEM((heads, 2, n_chunks, CHUNK, HEAD_DIM), BF16),
                        pltpu.VMEM((heads, 2, n_chunks, HEAD_DIM, HEAD_DIM), BF16),
                        pltpu.VMEM((heads, 2, n_chunks, 8, HEAD_DIM), F32)],
        compiler_params=_cparams(("parallel", "parallel")),
        name="gdn",
    )(par, qkv, qkv, qkv, conv_w9, conv_w9, conv_w9, gates_row)


def _odd_finish_kernel(x_ref, o_in_ref, z_ref, hw_ref, wo_ref, g_ref, o_ref):
    o = o_in_ref[0].astype(F32)
    parts = []
    for h in range(GDN_HEADS):
        sl = slice(h * HEAD_DIM, (h + 1) * HEAD_DIM)
        parts.append(_rms(o[:, sl], hw_ref[:, sl]))
    z = z_ref[0].astype(F32)
    y = jnp.concatenate(parts, axis=-1) * (z * jax.nn.sigmoid(z))
    o_ref[0] = x_ref[0] + g_ref[0, 0] * _dot(y, wo_ref[...])


def _odd_finish(xc, o, z, mods, layer, hw, wo, tm, n_ctx):
    bsz, t, d = xc.shape
    assert n_ctx % tm == 0 and (t - n_ctx) % tm == 0
    t0 = n_ctx // tm
    return pl.pallas_call(
        _odd_finish_kernel,
        grid=(bsz, (t - n_ctx) // tm),
        in_specs=[pl.BlockSpec((1, tm, d), lambda b, i: (b, i + t0, 0)),
                  pl.BlockSpec((1, tm, d), lambda b, i: (b, i + t0, 0)),
                  pl.BlockSpec((1, tm, d), lambda b, i: (b, i + t0, 0)),
                  pl.BlockSpec((1, d), lambda b, i: (0, 0)),
                  pl.BlockSpec((d, d), lambda b, i: (0, 0)),
                  _lat_mod_spec(layer, 2)],
        out_specs=pl.BlockSpec((1, tm, d), lambda b, i: (b, i, 0)),
        out_shape=jax.ShapeDtypeStruct((bsz, t - n_ctx, d), F32),
        compiler_params=_cparams(("parallel", "parallel")),
        name="odd_finish",
    )(xc, o, z, hw.reshape(1, d), wo, mods)


def _chunked(gates_t):
    bsz, g, t = gates_t.shape
    return gates_t.reshape(bsz, g, t // CHUNK, CHUNK)


def _pick_tile(total, target, at_least, multiple=8):
    for tile in range(min(target, total), multiple - 1, -1):
        if total % tile == 0 and tile % multiple == 0 and tile >= at_least:
            return tile
    raise ValueError(f"no row tile for {total} rows")


def _pad_cols(w, n):
    return jnp.pad(w, ((0, 0), (0, n - w.shape[1])))


def kernel(x, c, ctx, c_ctx, ada_w, ada_b, norm1_w, norm2_w, ffn_w1, ffn_w3, ffn_w2, final_norm_w,
           ev_w_in, ev_i_bias, ev_f_bias, ev_head_norm_w, ev_lam_re, ev_lam_im, ev_log_dt,
           ev_b_re, ev_b_im, ev_c_re, ev_c_im, ev_d, ev_w_glu, ev_w_out,
           od_w_in, od_conv_w, od_a_log, od_dt_bias, od_head_norm_w, od_w_out):
    bsz, seq, d = x.shape
    n_ctx = ctx.shape[1]
    assert d == D_MODEL and seq % GRID_W == 0 and n_ctx % CHUNK == 0 and bsz % 8 == 0
    assert ada_w.shape[0] == 2
    t_all = n_ctx + seq
    tm = _pick_tile(t_all, 768, n_ctx, LANES)
    tm_lat = _pick_tile(math.gcd(seq, n_ctx), 512, 0)
    th = FFN_HIDDEN // 2
    tb = CHUNK

    if tm % n_ctx == 0 and seq % n_ctx == 0:
        xs = (ctx, x)
    else:
        xs = (jnp.concatenate([ctx, x], axis=1),)
    mod_rows = 16
    cvec = jnp.zeros((mod_rows, d), F32).at[:bsz].set(c).at[bsz].set(c_ctx)
    mods = _ada(cvec, ada_w, ada_b).reshape(2, mod_rows, 1, 6 * d)
    w1, w3, w2 = ffn_w1.astype(BF16), ffn_w3.astype(BF16), ffn_w2.astype(BF16)

    qkv_w, n_g = 3 * MLSTM_WIDTH, 4 * MLSTM_HEADS
    w_in = ev_w_in[0]
    g0 = qkv_w + MLSTM_WIDTH
    n_even = w_in.shape[1]
    w_even = _pad_cols(w_in, -(-n_even // LANES) * LANES).astype(BF16)
    qkv, o_pre, u, gates = _inproj(
        xs, norm1_w[0], mods, 0, w_even,
        [(0, qkv_w, "row", BF16, qkv_w), (qkv_w, g0, "row", BF16, g0 - qkv_w),
         (g0 + n_g, n_even, "row", BF16, S5_WIDTH), (g0, g0 + LANES, "gate", F32, n_g)], tm, n_ctx)
    bias = jnp.concatenate([ev_i_bias[0].reshape(-1), ev_f_bias[0].reshape(-1)]).astype(F32)
    ha = _mlstm(qkv, _chunked(gates), bias, n_ctx)
    bd, cdt, avec = _s5_prep(ev_lam_re[0], ev_lam_im[0], ev_log_dt[0], ev_b_re[0], ev_b_im[0],
                             ev_c_re[0], ev_c_im[0])
    yf, yb = _s5(u, bd, cdt, avec, n_ctx, tb)
    xc = _even_finish(xs, ha, o_pre, yf, yb, u, mods, 0, ev_head_norm_w[0], ev_d[0],
                      ev_w_glu[0].astype(BF16), ev_w_out[0].astype(BF16), tm, n_ctx)
    xc = _ffn(xc, norm2_w[0], mods, 0, w1[0], w3[0], w2[0], final_norm_w, tm, th, n_ctx, False)

    n_g = 4 * GDN_HEADS
    w_odd = _pad_cols(od_w_in[0], 4 * d + LANES).astype(BF16)
    qkv_pre, z, gates = _inproj(
        (xc,), norm1_w[1], mods, 1, w_odd,
        [(0, GDN_QKV, "row", BF16, GDN_QKV), (GDN_QKV, 4 * d, "row", BF16, d),
         (4 * d, 4 * d + LANES, "gate", F32, n_g)], tm, n_ctx)
    par = jnp.concatenate([od_a_log[0].reshape(-1), od_dt_bias[0].reshape(-1)]).astype(F32)
    o = _gdn(qkv_pre, od_conv_w[0], _chunked(gates), par, n_ctx)
    xl = _odd_finish(xc, o, z, mods, 1, od_head_norm_w[0], od_w_out[0].astype(BF16), tm_lat, n_ctx)
    return _ffn(xl, norm2_w[1], mods, 1, w1[1], w3[1], w2[1], final_norm_w, _pick_tile(seq, 512, 0),
                th, 0, True)
```

```python
import functools
import math

import jax
import jax.numpy as jnp
from jax import lax
from jax.experimental import pallas as pl
from jax.experimental.pallas import tpu as pltpu

F32 = jnp.float32
BF16 = jnp.bfloat16

D_MODEL = 1024
CHUNK = 64
MLSTM_CHUNK = 128
GRID_W = 64
EPS = 1e-6
LANES = 128
HEAD_DIM = 128
MLSTM_WIDTH = D_MODEL // 2
MLSTM_HEADS = MLSTM_WIDTH // HEAD_DIM
S5_WIDTH = D_MODEL - MLSTM_WIDTH
S5_GROUP = 16
S5_GROUPS = S5_WIDTH // S5_GROUP
S5_STATE = 64
S5_HALVES = 2
S5_HALF_W = S5_WIDTH // S5_HALVES
S5_HALF_N = (S5_GROUPS // S5_HALVES) * S5_STATE
GDN_HEADS = D_MODEL // HEAD_DIM
GDN_QKV = 3 * D_MODEL
FFN_HIDDEN = ((8 * D_MODEL + 3 * 256 - 1) // (3 * 256)) * 256
VMEM_LIMIT = 56 * 1024 * 1024

_NT = (((1,), (1,)), ((), ()))
_TN = (((0,), (0,)), ((), ()))


def _cparams(sem):
    return pltpu.CompilerParams(dimension_semantics=sem, vmem_limit_bytes=VMEM_LIMIT)


def _dot(a, b):
    return jnp.dot(a.astype(BF16), b.astype(BF16), preferred_element_type=F32)


def _dot_nt(a, b):
    return lax.dot_general(a.astype(BF16), b.astype(BF16), _NT, preferred_element_type=F32)


def _dot_tn(a, b):
    return lax.dot_general(a.astype(BF16), b.astype(BF16), _TN, preferred_element_type=F32)


def _rms(x, w):
    return x * lax.rsqrt(jnp.mean(x * x, axis=-1, keepdims=True) + EPS) * w


def _ada_kernel(s_ref, w_ref, b_ref, o_ref):
    s = s_ref[...]
    s = s * jax.nn.sigmoid(s)
    o_ref[0] = _dot(s, w_ref[0]) + b_ref[0]


def _ada(cvec, ada_w, ada_b):
    depth, d, n = ada_w.shape
    tn = 1536
    rows = cvec.shape[0]
    return pl.pallas_call(
        _ada_kernel,
        grid=(depth, n // tn),
        in_specs=[pl.BlockSpec((rows, d), lambda l, j: (0, 0)),
                  pl.BlockSpec((1, d, tn), lambda l, j: (l, 0, j)),
                  pl.BlockSpec((1, 1, tn), lambda l, j: (l, 0, j))],
        out_specs=pl.BlockSpec((1, rows, tn), lambda l, j: (l, 0, j)),
        out_shape=jax.ShapeDtypeStruct((depth, rows, n), F32),
        compiler_params=_cparams(("parallel", "parallel")),
        name="ada_mod",
    )(cvec, ada_w, ada_b.reshape(depth, 1, n))


def _ctx_mod_spec(layer, chunk, ctx_row):
    return pl.BlockSpec((1, 1, 1, D_MODEL), lambda b, i: (layer, ctx_row, 0, chunk))


def _lat_mod_spec(layer, chunk):
    return pl.BlockSpec((1, 1, 1, D_MODEL), lambda b, i: (layer, b, 0, chunk))


def _row_groups(n_top, tm, x_refs, ctx_refs, lat_refs):
    lat = [r[0, 0] for r in lat_refs]
    if n_top == 0:
        return [(slice(0, tm), x_refs[0][0], lat)]
    first = pl.program_id(1) == 0
    top = [jnp.where(first, c[0, 0], l) for c, l in zip(ctx_refs, lat)]
    if len(x_refs) == 1:
        x_ref = x_refs[0]
        return [(slice(0, n_top), x_ref[0, :n_top, :], top), (slice(n_top, tm), x_ref[0, n_top:, :], lat)]
    ctx_ref, subs = x_refs[0], x_refs[1:]
    groups = [(slice(0, n_top), jnp.where(first, ctx_ref[0], subs[0][0]), top)]
    return groups + [(slice(k * n_top, (k + 1) * n_top), subs[k][0], lat) for k in range(1, len(subs))]


def _x_operands(xs, tm, n_ctx):
    d = xs[0].shape[-1]
    if len(xs) == 1:
        return list(xs), [pl.BlockSpec((1, tm, d), lambda b, i: (b, i, 0))]
    ctx, lat = xs
    nsub = tm // n_ctx
    assert tm % n_ctx == 0 and lat.shape[1] % n_ctx == 0
    specs = [pl.BlockSpec((1, n_ctx, d), lambda b, i: (b, 0, 0))]
    for k in range(nsub):
        specs.append(pl.BlockSpec((1, n_ctx, d), lambda b, i, k=k: (b, jnp.maximum(i * nsub + k - 1, 0), 0)))
    return [ctx] + [lat] * nsub, specs


def _inproj_kernel(*refs, outs, n_top, n_chunk, n_x):
    x_refs, (nw_ref, shc_ref, scc_ref, shl_ref, scl_ref, w_ref) = refs[:n_x], refs[n_x:n_x + 6]
    out_refs, h_scr, g_scr = refs[n_x + 6:-2], refs[-2], refs[-1]
    tm = h_scr.shape[0]
    for rows, x, (sh, sc) in _row_groups(n_top, tm, x_refs, (shc_ref, scc_ref), (shl_ref, scl_ref)):
        h_scr[rows, :] = (_rms(x, nw_ref[...]) * (1.0 + sc) + sh).astype(BF16)
    for o_ref, (c0, c1, kind) in zip(out_refs, outs):
        if kind == "row" and (c0 % LANES or c1 % LANES):
            a0, a1 = c0 // LANES * LANES, -(-c1 // LANES) * LANES
            val = jnp.dot(h_scr[...], w_ref[:, a0:a1], preferred_element_type=F32)
            o_ref[0] = val[:, c0 - a0:c1 - a0].astype(o_ref.dtype)
            continue
        for n0 in range(c0, c1, n_chunk):
            n1 = min(n0 + n_chunk, c1)
            val = jnp.dot(h_scr[...], w_ref[:, n0:n1], preferred_element_type=F32)
            if kind == "row":
                o_ref[0, :, n0 - c0:n1 - c0] = val.astype(o_ref.dtype)
            else:
                g_scr[...] = val
                o_ref[0] = g_scr[...].T[:o_ref.shape[1], :].astype(o_ref.dtype)


def _inproj(xs, norm_w, mods, layer, w_bf16, outs, tm, n_ctx):
    bsz, d = xs[0].shape[0], xs[0].shape[-1]
    t = sum(a.shape[1] for a in xs)
    n = w_bf16.shape[1]
    x_args, x_specs = _x_operands(xs, tm, n_ctx)
    assert t % tm == 0 and (n_ctx == 0 or n_ctx <= tm)
    out_specs, out_shapes = [], []
    for c0, c1, kind, dtype, wd in outs:
        if kind == "row":
            out_specs.append(pl.BlockSpec((1, tm, wd), lambda b, i: (b, i, 0)))
            out_shapes.append(jax.ShapeDtypeStruct((bsz, t, wd), dtype))
        else:
            assert c1 - c0 == LANES and tm % LANES == 0
            out_specs.append(pl.BlockSpec((1, wd, tm), lambda b, i: (b, 0, i)))
            out_shapes.append(jax.ShapeDtypeStruct((bsz, wd, t), dtype))
    ctx_spec = lambda c: _ctx_mod_spec(layer, c, bsz)
    lat_spec = lambda c: _lat_mod_spec(layer, c)
    const = lambda shape: pl.BlockSpec(shape, lambda b, i: (0, 0), pipeline_mode=pl.Buffered(1))
    return pl.pallas_call(
        functools.partial(_inproj_kernel, outs=tuple(o[:3] for o in outs), n_top=n_ctx, n_chunk=512,
                          n_x=len(x_args)),
        grid=(bsz, t // tm),
        in_specs=x_specs + [const((1, d)), ctx_spec(0), ctx_spec(1), lat_spec(0), lat_spec(1),
                            const((d, n))],
        out_specs=out_specs,
        out_shape=out_shapes,
        scratch_shapes=[pltpu.VMEM((tm, d), BF16), pltpu.VMEM((tm, LANES), F32)],
        compiler_params=_cparams(("parallel", "parallel")),
        name=f"inproj_l{layer}",
    )(*x_args, norm_w.reshape(1, d), mods, mods, mods, mods, w_bf16)


def _ffn_kernel(x_ref, nw_ref, shc_ref, scc_ref, gc_ref, shl_ref, scl_ref, gl_ref,
                w1_ref, w3_ref, w2_ref, fw_ref, o_ref, h_scr, *, final, n_top, th):
    tm, hid = h_scr.shape[0], w1_ref.shape[1]
    groups = _row_groups(n_top, tm, (x_ref,), (shc_ref, scc_ref, gc_ref), (shl_ref, scl_ref, gl_ref))
    for rows, x, (sh, sc, _) in groups:
        h_scr[rows, :] = (_rms(x, nw_ref[...]) * (1.0 + sc) + sh).astype(BF16)
    acc = None
    for c0 in range(0, hid, th):
        h = h_scr[...]
        a = jnp.dot(h, w1_ref[:, c0:c0 + th], preferred_element_type=F32)
        g = jnp.dot(h, w3_ref[:, c0:c0 + th], preferred_element_type=F32)
        t = ((a * jax.nn.sigmoid(a)) * g).astype(BF16)
        part = jnp.dot(t, w2_ref[c0:c0 + th, :], preferred_element_type=F32)
        acc = part if acc is None else acc + part
    for rows, x, (_, _, gate) in groups:
        y = x + gate * acc[rows, :]
        if final:
            y = _rms(y, fw_ref[...])
        o_ref[0, rows, :] = y


def _ffn(x, norm_w, mods, layer, w1, w3, w2, final_w, tm, th, n_ctx, final):
    bsz, tx, d = x.shape
    hid = w1.shape[2]
    assert n_ctx <= tm and tx % tm == 0 and hid % th == 0
    ctx_spec = lambda c: _ctx_mod_spec(layer, c, bsz)
    lat_spec = lambda c: _lat_mod_spec(layer, c)
    const = lambda shape: pl.BlockSpec(shape, lambda b, i: (0, 0), pipeline_mode=pl.Buffered(1))
    layer_w = lambda shape: pl.BlockSpec((None,) + shape, lambda b, i: (layer, 0, 0),
                                         pipeline_mode=pl.Buffered(1))
    return pl.pallas_call(
        functools.partial(_ffn_kernel, final=final, n_top=n_ctx, th=th),
        grid=(bsz, tx // tm),
        in_specs=[pl.BlockSpec((1, tm, d), lambda b, i: (b, i, 0)),
                  const((1, d)),
                  ctx_spec(3), ctx_spec(4), ctx_spec(5), lat_spec(3), lat_spec(4), lat_spec(5),
                  layer_w((d, hid)), layer_w((d, hid)), layer_w((hid, d)), const((1, d))],
        out_specs=pl.BlockSpec((1, tm, d), lambda b, i: (b, i, 0)),
        out_shape=jax.ShapeDtypeStruct((bsz, tx, d), F32),
        scratch_shapes=[pltpu.VMEM((tm, d), BF16)],
        compiler_params=_cparams(("parallel", "parallel")),
        name=f"ffn_l{layer}",
    )(x, norm_w.reshape(1, d), mods, mods, mods, mods, mods, mods, w1, w3, w2, final_w.reshape(1, d))


def _chunk_masks(size=CHUNK):
    row = lax.broadcasted_iota(jnp.int32, (size, size), 0)
    col = lax.broadcasted_iota(jnp.int32, (size, size), 1)
    return row, col


def _to_col(row_vec, eye):
    return jnp.sum(jnp.where(eye, jnp.broadcast_to(row_vec, eye.shape), 0.0), axis=1, keepdims=True)


def _to_row(col_vec, eye):
    return jnp.sum(jnp.where(eye, jnp.broadcast_to(col_vec, eye.shape), 0.0), axis=0, keepdims=True)


def _scan_chunk_index(i, rev, n_chunks, ctx_chunks):
    if not rev:
        return i
    return jnp.where(i < ctx_chunks, ctx_chunks - 1 - i, n_chunks + ctx_chunks - 1 - i)


def _mlstm_kernel(bias_ref, q_ref, k_ref, v_ref, g_ref, o_ref, num_scr, cl_scr, mi_scr, fc_scr, sc_scr,
                  *, chunk, n_chunks, ctx_chunks, group):
    hd = pl.program_id(1)
    row, col = _chunk_masks(chunk)
    eye = row == col
    kscale = 1.0 / math.sqrt(HEAD_DIM)
    ones_blk = jnp.ones((chunk, HEAD_DIM), BF16)

    def gate_terms(r, ig_raw, f_raw):
        incl = (col >= row) if r == 1 else (col <= row)
        ig_row = ig_raw + bias_ref[r * MLSTM_HEADS + hd]
        lf_row = jax.nn.log_sigmoid(f_raw + bias_ref[(2 + r) * MLSTM_HEADS + hd])
        f_col = jnp.sum(jnp.where(incl, jnp.broadcast_to(lf_row, incl.shape), 0.0),
                        axis=1, keepdims=True)
        f_row = _to_row(f_col, eye)
        f_last = jnp.sum(lf_row, axis=1, keepdims=True)
        dm = jnp.where(incl, f_col - f_row + ig_row, -jnp.inf)
        m_intra = jnp.max(dm, axis=1, keepdims=True)
        w_col = f_last - f_col + _to_col(ig_row, eye)
        m_loc = jnp.max(w_col, axis=0, keepdims=True)
        return f_col, f_last, jnp.exp(dm - m_intra), m_intra, jnp.exp(w_col - m_loc), m_loc

    def local_group(gi, _):
        chunks = [gi * group + g for g in range(group)]
        loaded = []
        for c in chunks:
            rows = pl.ds(pl.multiple_of(c * chunk, chunk), chunk)
            gates = [(g_ref[0, r * MLSTM_HEADS + hd, pl.ds(c, 1), :],
                      g_ref[0, (2 + r) * MLSTM_HEADS + hd, pl.ds(c, 1), :]) for r in (0, 1)]
            v1 = jnp.concatenate([v_ref[0, rows, :].astype(BF16), ones_blk], axis=-1)
            loaded.append((q_ref[0, rows, :], k_ref[0, rows, :].astype(F32) * kscale, v1, gates))
        chains = [(g, r) for g in range(group) for r in (0, 1)]
        terms = [gate_terms(r, *loaded[g][3][r]) for g, r in chains]
        qks = [_dot_nt(q, k) for q, k, _, _ in loaded]
        c_locs = [_dot_tn(tm[4] * loaded[g][1], loaded[g][2]) for (g, r), tm in zip(chains, terms)]
        nums = [_dot(tm[2] * qks[g], loaded[g][2]) for (g, r), tm in zip(chains, terms)]
        for (g, r), tm, c_loc, num in zip(chains, terms, c_locs, nums):
            f_col, f_last, _, m_intra, _, m_loc = tm
            c = chunks[g]
            num_scr[r, c] = num
            cl_scr[r, c] = c_loc
            mi_scr[r, c] = jnp.broadcast_to(m_intra, (chunk, HEAD_DIM))
            fc_scr[r, c] = jnp.broadcast_to(f_col, (chunk, HEAD_DIM))
            sc_scr[r, c, :8, :] = jnp.broadcast_to(f_last, (8, HEAD_DIM))
            sc_scr[r, c, 8:, :] = jnp.broadcast_to(m_loc, (8, HEAD_DIM))
        return 0

    lax.fori_loop(0, n_chunks // group, local_group, 0)
    o_ref[...] = jnp.zeros_like(o_ref)

    def step(i, carry):
        cs = [_scan_chunk_index(i, r == 1, n_chunks, ctx_chunks) for r in (0, 1)]
        rows = [pl.ds(pl.multiple_of(c * chunk, chunk), chunk) for c in cs]
        qcs = [_dot(q_ref[0, rw, :], st[0]) for rw, st in zip(rows, carry)]
        new = []
        for r in (0, 1):
            s_st, m_st = carry[r]
            c = cs[r]
            mi, na = mi_scr[r, c], num_scr[r, c]
            inter = fc_scr[r, c] + m_st[0:1, :]
            m_t = jnp.maximum(mi, inter)
            a_loc_t, a_inter = jnp.exp(mi - m_t), jnp.exp(inter - m_t)
            num = a_loc_t * na[:, :HEAD_DIM] + a_inter * qcs[r][:, :HEAD_DIM]
            den = a_loc_t * na[:, HEAD_DIM:] + a_inter * qcs[r][:, HEAD_DIM:]
            out = num / jnp.maximum(jnp.abs(den), jnp.exp(-m_t))
            o_ref[0, rows[r], :] = (o_ref[0, rows[r], :].astype(F32) + out).astype(o_ref.dtype)
            f_last, m_loc = sc_scr[r, c, :8, :], sc_scr[r, c, 8:, :]
            m_new = jnp.maximum(f_last + m_st, m_loc)
            a_prev = jnp.exp(f_last + m_st - m_new)[0:1, :]
            a_loc = jnp.exp(m_loc - m_new)[0:1, :]
            a_prev = jnp.concatenate([a_prev, a_prev], axis=1)
            a_loc = jnp.concatenate([a_loc, a_loc], axis=1)
            new.append((a_prev * s_st + a_loc * cl_scr[r, c], m_new))
        return tuple(new)

    init = (jnp.zeros((HEAD_DIM, 2 * HEAD_DIM), F32), jnp.zeros((8, HEAD_DIM), F32))
    lax.fori_loop(0, n_chunks, step, (init, init), unroll=2)


def _mlstm(qkv, gates, bias, n_ctx, chunk):
    bsz, t, _ = qkv.shape
    assert t % chunk == 0 and n_ctx % chunk == 0
    n_chunks = t // chunk
    gates_row = _chunked(gates, chunk)
    qkv_spec = lambda off: pl.BlockSpec((1, t, HEAD_DIM), lambda b, h: (b, 0, off + h))
    return pl.pallas_call(
        functools.partial(_mlstm_kernel, chunk=chunk, n_chunks=n_chunks, ctx_chunks=n_ctx // chunk,
                          group=math.gcd(9 if chunk <= CHUNK else 3, n_chunks)),
        grid=(bsz, MLSTM_HEADS),
        in_specs=[pl.BlockSpec(memory_space=pltpu.SMEM),
                  qkv_spec(0), qkv_spec(MLSTM_HEADS), qkv_spec(2 * MLSTM_HEADS),
                  pl.BlockSpec((1, 4 * MLSTM_HEADS, n_chunks, chunk), lambda b, h: (b, 0, 0, 0))],
        out_specs=pl.BlockSpec((1, t, HEAD_DIM), lambda b, h: (b, 0, h)),
        out_shape=jax.ShapeDtypeStruct((bsz, t, MLSTM_WIDTH), BF16),
        scratch_shapes=[pltpu.VMEM((2, n_chunks, chunk, 2 * HEAD_DIM), F32),
                        pltpu.VMEM((2, n_chunks, HEAD_DIM, 2 * HEAD_DIM), F32),
                        pltpu.VMEM((2, n_chunks, chunk, HEAD_DIM), F32),
                        pltpu.VMEM((2, n_chunks, chunk, HEAD_DIM), F32),
                        pltpu.VMEM((2, n_chunks, 16, HEAD_DIM), F32)],
        compiler_params=_cparams(("parallel", "parallel")),
        name="mlstm",
    )(bias, qkv, qkv, qkv, gates_row)


def _s5_prep_kernel(lr_ref, li_ref, ldt_ref, br_ref, bi_ref, cr_ref, ci_ref, bd_ref, cdt_ref, a_ref):
    lr, li = lr_ref[0, 0], li_ref[0, 0]
    dt = jnp.exp(ldt_ref[0, 0])
    mag, ang = jnp.exp(lr * dt), li * dt
    ab_re, ab_im = mag * jnp.cos(ang), mag * jnp.sin(ang)
    nr, ni = ab_re - 1.0, ab_im
    den = lr * lr + li * li
    co_re = (nr * lr + ni * li) / den
    co_im = (ni * lr - nr * li) / den
    b_re, b_im = br_ref[0, 0], bi_ref[0, 0]
    bb_re = co_re * b_re - co_im * b_im
    bb_im = co_re * b_im + co_im * b_re
    c_re, c_im = cr_ref[0, 0], ci_ref[0, 0]
    lane_group = lax.broadcasted_iota(jnp.int32, (S5_GROUP, S5_HALF_N), 1) // S5_STATE
    n = S5_HALF_N
    for g in range(S5_GROUPS // S5_HALVES):
        sel = lane_group == g
        rows = slice(g * S5_GROUP, (g + 1) * S5_GROUP)
        bd_ref[0, 0, rows, :n] = jnp.where(sel, bb_re, 0.0).astype(bd_ref.dtype)
        bd_ref[0, 0, rows, n:] = jnp.where(sel, bb_im, 0.0).astype(bd_ref.dtype)
        cdt_ref[0, 0, rows, :n] = jnp.where(sel, c_re, 0.0).astype(cdt_ref.dtype)
        cdt_ref[0, 0, rows, n:] = jnp.where(sel, -c_im, 0.0).astype(cdt_ref.dtype)
    a_ref[0, 0, :, :n] = jnp.broadcast_to(ab_re, (8, n))
    a_ref[0, 0, :, n:] = jnp.broadcast_to(ab_im, (8, n))


def _s5_prep(lam_re, lam_im, log_dt, b_re, b_im, c_re, c_im):
    gh = S5_GROUPS // S5_HALVES
    vec = lambda a: a.reshape(2, S5_HALVES, 1, S5_HALF_N)
    ldt = vec(jnp.broadcast_to(log_dt[:, :, None], (2, S5_GROUPS, S5_STATE)))
    bt = lambda a: a.reshape(2, S5_HALVES, gh, S5_STATE, S5_GROUP).transpose(0, 1, 4, 2, 3).reshape(
        2, S5_HALVES, S5_GROUP, S5_HALF_N)
    ct = lambda a: a.reshape(2, S5_HALVES, gh, S5_GROUP, S5_STATE).transpose(0, 1, 3, 2, 4).reshape(
        2, S5_HALVES, S5_GROUP, S5_HALF_N)
    vspec = pl.BlockSpec((1, 1, 1, S5_HALF_N), lambda r, h: (r, h, 0, 0))
    mspec = pl.BlockSpec((1, 1, S5_GROUP, S5_HALF_N), lambda r, h: (r, h, 0, 0))
    ospec = pl.BlockSpec((1, 1, S5_HALF_W, 2 * S5_HALF_N), lambda r, h: (r, h, 0, 0))
    return pl.pallas_call(
        _s5_prep_kernel,
        grid=(2, S5_HALVES),
        in_specs=[vspec, vspec, vspec, mspec, mspec, mspec, mspec],
        out_specs=[ospec, ospec, pl.BlockSpec((1, 1, 8, 2 * S5_HALF_N), lambda r, h: (r, h, 0, 0))],
        out_shape=[jax.ShapeDtypeStruct((2, S5_HALVES, S5_HALF_W, 2 * S5_HALF_N), BF16),
                   jax.ShapeDtypeStruct((2, S5_HALVES, S5_HALF_W, 2 * S5_HALF_N), BF16),
                   jax.ShapeDtypeStruct((2, S5_HALVES, 8, 2 * S5_HALF_N), F32)],
        compiler_params=_cparams(("parallel", "parallel")),
        name="s5_prep",
    )(vec(lam_re), vec(lam_im), ldt, bt(b_re), bt(b_im), ct(c_re), ct(c_im))


def _s5_kernel(uf0_ref, uf1_ref, ub0_ref, ub1_ref, bd_ref, cdt_ref, a_ref, yf_ref, yb_ref,
               lhs_scr, bu_scr, s_scr, st_scr, *, tb, bsz):
    n = S5_HALF_N
    rows = tb * bsz
    u_refs = ((uf0_ref, uf1_ref), (ub0_ref, ub1_ref))
    y_refs = (yf_ref, yb_ref)

    @pl.when(pl.program_id(1) == 0)
    def _():
        st_scr[...] = jnp.zeros_like(st_scr)

    for d in (0, 1):
        for j in (0, 1):
            lhs_scr[d, :, j * LANES:(j + 1) * LANES] = jnp.transpose(
                u_refs[d][j][...].astype(F32), (1, 0, 2)).reshape(rows, LANES)
    for d in (0, 1):
        bu_scr[d] = _dot(lhs_scr[d], bd_ref[d, 0])
    for d in (0, 1):
        a_re, a_im = a_ref[d, 0, :, :n], a_ref[d, 0, :, n:]
        s_re, s_im = st_scr[d, :, :n], st_scr[d, :, n:]
        for j in range(tb):
            t = j if d == 0 else tb - 1 - j
            sl = slice(t * bsz, (t + 1) * bsz)
            s_re, s_im = (a_re * s_re - a_im * s_im + bu_scr[d, sl, :n],
                          a_re * s_im + a_im * s_re + bu_scr[d, sl, n:])
            s_scr[d, sl, :n] = s_re
            s_scr[d, sl, n:] = s_im
        st_scr[d, :, :n] = s_re
        st_scr[d, :, n:] = s_im
        y = _dot_nt(s_scr[d], cdt_ref[d, 0])
        for j in (0, 1):
            y_refs[d][:, :, j * LANES:(j + 1) * LANES] = jnp.transpose(
                y[:, j * LANES:(j + 1) * LANES].reshape(tb, bsz, LANES), (1, 0, 2)).astype(y_refs[d].dtype)


def _s5(u, bd, cdt, avec, n_ctx, tb):
    bsz, t, _ = u.shape
    nb, ncb = t // tb, n_ctx // tb
    assert bsz == 8 and S5_HALF_W == 2 * LANES

    def rev(i):
        return jnp.where(i < ncb, ncb - 1 - i, nb + ncb - 1 - i)

    fwd = lambda i: i
    uspec = lambda order, j: pl.BlockSpec((bsz, tb, LANES), lambda h, i: (0, order(i), 2 * h + j))
    yspec = lambda order: pl.BlockSpec((bsz, tb, S5_HALF_W), lambda h, i: (0, order(i), h))
    wspec = pl.BlockSpec((2, 1, S5_HALF_W, 2 * S5_HALF_N), lambda h, i: (0, h, 0, 0))
    rows = tb * bsz
    return pl.pallas_call(
        functools.partial(_s5_kernel, tb=tb, bsz=bsz),
        grid=(S5_HALVES, nb),
        in_specs=[uspec(fwd, 0), uspec(fwd, 1), uspec(rev, 0), uspec(rev, 1), wspec, wspec,
                  pl.BlockSpec((2, 1, 8, 2 * S5_HALF_N), lambda h, i: (0, h, 0, 0))],
        out_specs=[yspec(fwd), yspec(rev)],
        out_shape=[jax.ShapeDtypeStruct((bsz, t, S5_WIDTH), BF16)] * 2,
        scratch_shapes=[pltpu.VMEM((2, rows, S5_HALF_W), F32),
                        pltpu.VMEM((2, rows, 2 * S5_HALF_N), F32),
                        pltpu.VMEM((2, rows, 2 * S5_HALF_N), F32),
                        pltpu.VMEM((2, bsz, 2 * S5_HALF_N), F32)],
        compiler_params=_cparams(("parallel", "arbitrary")),
        name="s5_scan",
    )(u, u, u, u, bd, cdt, avec)


def _even_finish_kernel(*refs, n_top, n_x):
    x_refs = refs[:n_x]
    ha_ref, op_ref, yf_ref, yb_ref, u_ref, mhw_ref, ds_ref, wg_ref, wo_ref, gc_ref, gl_ref, o_ref = refs[n_x:]
    ha = ha_ref[0].astype(F32)
    parts = []
    for h in range(MLSTM_HEADS):
        sl = slice(h * HEAD_DIM, (h + 1) * HEAD_DIM)
        parts.append(_rms(ha[:, sl], mhw_ref[:, sl]))
    a_out = jnp.concatenate(parts, axis=-1) * jax.nn.sigmoid(op_ref[0].astype(F32))
    yb = jax.nn.gelu(yf_ref[0].astype(F32) + yb_ref[0].astype(F32) + ds_ref[...] * u_ref[0].astype(F32))
    glu = _dot(yb, wg_ref[...])
    b_out = glu[:, :S5_WIDTH] * jax.nn.sigmoid(glu[:, S5_WIDTH:])
    y = _dot(a_out, wo_ref[:MLSTM_WIDTH, :]) + _dot(b_out, wo_ref[MLSTM_WIDTH:, :])
    for rows, x, (gate,) in _row_groups(n_top, y.shape[0], x_refs, (gc_ref,), (gl_ref,)):
        o_ref[0, rows, :] = x + gate * y[rows, :]


def _even_finish(xs, ha, o_pre, yf, yb, u, mods, layer, mh_w, d_skip, wg, wo, tm, n_ctx):
    bsz, t, _ = ha.shape
    d = xs[0].shape[-1]
    assert t % tm == 0 and n_ctx <= tm
    x_args, x_specs = _x_operands(xs, tm, n_ctx)
    full = lambda shape: pl.BlockSpec(shape, lambda b, i: tuple(0 for _ in shape),
                                      pipeline_mode=pl.Buffered(1))
    return pl.pallas_call(
        functools.partial(_even_finish_kernel, n_top=n_ctx, n_x=len(x_args)),
        grid=(bsz, t // tm),
        in_specs=x_specs + [
                  pl.BlockSpec((1, tm, MLSTM_WIDTH), lambda b, i: (b, i, 0)),
                  pl.BlockSpec((1, tm, MLSTM_WIDTH), lambda b, i: (b, i, 0)),
                  pl.BlockSpec((1, tm, S5_WIDTH), lambda b, i: (b, i, 0)),
                  pl.BlockSpec((1, tm, S5_WIDTH), lambda b, i: (b, i, 0)),
                  pl.BlockSpec((1, tm, S5_WIDTH), lambda b, i: (b, i, 0)),
                  full((1, MLSTM_WIDTH)), full((1, S5_WIDTH)),
                  full((S5_WIDTH, 2 * S5_WIDTH)), full((d, d)),
                  _ctx_mod_spec(layer, 2, bsz), _lat_mod_spec(layer, 2)],
        out_specs=pl.BlockSpec((1, tm, d), lambda b, i: (b, i, 0)),
        out_shape=jax.ShapeDtypeStruct((bsz, t, d), F32),
        compiler_params=_cparams(("parallel", "parallel")),
        name="even_finish",
    )(*x_args, ha, o_pre, yf, yb, u, mh_w.reshape(1, -1), d_skip.reshape(1, -1), wg, wo, mods, mods)


def _conv_chunk(x_ref, w, lanes, c, static, n_chunks, ctx_chunks):
    tpos = lax.broadcasted_iota(jnp.int32, (CHUNK, HEAD_DIM), 0)
    first, last = tpos == 0, tpos == CHUNK - 1
    wrow = lambda k: w[k:k + 1, :]

    def rows(ci):
        start = ci * CHUNK if isinstance(ci, int) else pl.multiple_of(ci * CHUNK, CHUNK)
        return x_ref[0, pl.ds(start, CHUNK), lanes].astype(F32)

    if static and c < ctx_chunks:
        cur = rows(c)
        zero = jnp.zeros((1, HEAD_DIM), F32)
        prev_tok = rows(c - 1)[CHUNK - 1:CHUNK, :] if c > 0 else zero
        next_tok = rows(c + 1)[0:1, :] if c < ctx_chunks - 1 else zero
        left = jnp.where(first, prev_tok, pltpu.roll(cur, 1, axis=0))
        right = jnp.where(last, next_tok, pltpu.roll(cur, CHUNK - 1, axis=0))
        return wrow(3) * left + wrow(4) * cur + wrow(5) * right

    if static:
        up_i, dn_i = max(c - 1, ctx_chunks), min(c + 1, n_chunks - 1)
        w_up = w[0:3, :] if c > ctx_chunks else jnp.zeros((3, HEAD_DIM), F32)
        w_dn = w[6:9, :] if c < n_chunks - 1 else jnp.zeros((3, HEAD_DIM), F32)
    else:
        up_i, dn_i = jnp.maximum(c - 1, ctx_chunks), jnp.minimum(c + 1, n_chunks - 1)
        w_up = jnp.where(c > ctx_chunks, w[0:3, :], 0.0)
        w_dn = jnp.where(c < n_chunks - 1, w[6:9, :], 0.0)
    up, cur, dn = rows(up_i), rows(c), rows(dn_i)
    r0, r1, r2 = (w_up[dc:dc + 1, :] * up + wrow(3 + dc) * cur + w_dn[dc:dc + 1, :] * dn for dc in range(3))
    prev = jnp.where(first, 0.0, pltpu.roll(r0, 1, axis=0))
    nxt = jnp.where(last, 0.0, pltpu.roll(r2, CHUNK - 1, axis=0))
    return r1 + prev + nxt


def _unit_tri_inverses(mats, eye_f):
    ps = [eye_f - a for a in mats]
    pws = [_dot(a, a) for a in mats]
    k = 2
    while 2 * k < CHUNK:
        res = [_dot(jnp.concatenate([p, pw], axis=0), pw) for p, pw in zip(ps, pws)]
        ps = [p + r[:CHUNK] for p, r in zip(ps, res)]
        pws = [r[CHUNK:] for r in res]
        k *= 2
    return [p + _dot(p, pw) for p, pw in zip(ps, pws)]


def _gdn_kernel(par_ref, q_ref, k_ref, v_ref, wq_ref, wk_ref, wv_ref, g_ref, o_ref,
                qk_scr, ol_scr, nl_scr, gl_scr, *, n_chunks, ctx_chunks, group, heads):
    row, col = _chunk_masks()
    eye = row == col
    eye_f = eye.astype(F32)

    def gate_terms(hd, r, a_pre, b_pre):
        incl = (col >= row) if r == 1 else (col <= row)
        a_log = jnp.full((1, CHUNK), par_ref[r * GDN_HEADS + hd], F32)
        dt_b = par_ref[(2 + r) * GDN_HEADS + hd]
        g_row = -jnp.exp(a_log) * jax.nn.softplus(a_pre + dt_b)
        beta_col = _to_col(jax.nn.sigmoid(b_pre), eye)
        gc_col = jnp.sum(jnp.where(incl, jnp.broadcast_to(g_row, incl.shape), 0.0),
                         axis=1, keepdims=True)
        gc_row = _to_row(gc_col, eye)
        g_last = jnp.sum(g_row, axis=1, keepdims=True)
        gamma = jnp.exp(jnp.where(incl, gc_col - gc_row, -jnp.inf))
        return beta_col, gc_col, g_last, gamma

    for hh in range(heads):
        hd = pl.program_id(1) * heads + hh
        lanes = slice(hh * HEAD_DIM, (hh + 1) * HEAD_DIM)

        def load(c, static, hd=hd, lanes=lanes):
            gates = [(g_ref[0, r * GDN_HEADS + hd, pl.ds(c, 1), :],
                      g_ref[0, (2 + r) * GDN_HEADS + hd, pl.ds(c, 1), :]) for r in (0, 1)]
            q, k, v = (_conv_chunk(x_ref, w_ref[:, lanes], lanes, c, static, n_chunks, ctx_chunks)
                       for x_ref, w_ref in ((q_ref, wq_ref), (k_ref, wk_ref), (v_ref, wv_ref)))
            q, k, v = q * jax.nn.sigmoid(q), k * jax.nn.sigmoid(k), v * jax.nn.sigmoid(v)
            q = q * (lax.rsqrt(jnp.sum(q * q, axis=-1, keepdims=True) + EPS) * (1.0 / math.sqrt(HEAD_DIM)))
            k = k * lax.rsqrt(jnp.sum(k * k, axis=-1, keepdims=True) + EPS)
            return q, k, v, gates

        def local_group(gi, _, hh=hh, hd=hd, load=load):
            static = isinstance(gi, int)
            chunks = [gi * group + g for g in range(group)]
            loaded = [load(c, static) for c in chunks]
            kqs = [_dot_nt(jnp.concatenate([k, q], axis=0), k) for q, k, _, _ in loaded]
            chains = [(g, r) for g in range(group) for r in (0, 1)]
            terms = [gate_terms(hd, r, *loaded[g][3][r]) for g, r in chains]
            a_mats = []
            for (g, r), (beta_col, _, _, gamma) in zip(chains, terms):
                strict = (col > row) if r == 1 else (col < row)
                a_mats.append(jnp.where(strict, beta_col * kqs[g][:CHUNK] * gamma, 0.0))
            t_invs = _unit_tri_inverses(a_mats, eye_f)
            sols = []
            for (g, r), (beta_col, gc_col, _, _), t_inv in zip(chains, terms, t_invs):
                _, k, v, _ = loaded[g]
                rhs = jnp.concatenate([v * beta_col, k * (beta_col * jnp.exp(gc_col))], axis=-1)
                sols.append(_dot(t_inv, rhs))
            auws = [_dot(kqs[g][CHUNK:] * tm[3], sol) for (g, r), tm, sol in zip(chains, terms, sols)]
            kuws = [_dot_tn(loaded[g][1] * jnp.exp(tm[2] - tm[1]), sol)
                    for (g, r), tm, sol in zip(chains, terms, sols)]
            for (g, r), (_, gc_col, g_last, _), auw, kuw in zip(chains, terms, auws, kuws):
                c = chunks[g]
                q = loaded[g][0]
                qk_scr[hh, r, c, :CHUNK, :] = (q * jnp.exp(gc_col) - auw[:, HEAD_DIM:]).astype(qk_scr.dtype)
                qk_scr[hh, r, c, CHUNK:, :] = kuw[:, HEAD_DIM:].astype(qk_scr.dtype)
                ol_scr[hh, r, c] = auw[:, :HEAD_DIM].astype(ol_scr.dtype)
                nl_scr[hh, r, c] = kuw[:, :HEAD_DIM].astype(nl_scr.dtype)
                gl_scr[hh, r, c] = jnp.broadcast_to(jnp.exp(g_last), (8, HEAD_DIM))
            return 0

        n_static = -(-ctx_chunks // group)
        for gi in range(n_static):
            local_group(gi, 0)
        lax.fori_loop(n_static, n_chunks // group, local_group, 0)

    o_ref[...] = jnp.zeros_like(o_ref)
    streams = [(hh, r) for hh in range(heads) for r in (0, 1)]

    def step(i, carry):
        cs = [_scan_chunk_index(i, r == 1, n_chunks, ctx_chunks) for _, r in streams]
        res = [jnp.dot(qk_scr[hh, r, c], s.astype(BF16), preferred_element_type=F32)
               for (hh, r), c, s in zip(streams, cs, carry)]
        new = []
        for (hh, r), c, s, rs in zip(streams, cs, carry, res):
            rows = pl.ds(pl.multiple_of(c * CHUNK, CHUNK), CHUNK)
            lanes = slice(hh * HEAD_DIM, (hh + 1) * HEAD_DIM)
            o_ref[0, rows, lanes] = (o_ref[0, rows, lanes].astype(F32) + ol_scr[hh, r, c]
                                     + rs[:CHUNK]).astype(o_ref.dtype)
            new.append(gl_scr[hh, r, c][0:1, :] * s - rs[CHUNK:] + nl_scr[hh, r, c])
        return tuple(new)

    init = jnp.zeros((HEAD_DIM, HEAD_DIM), F32)
    lax.fori_loop(0, n_chunks, step, (init,) * len(streams), unroll=2)


def _gdn(qkv, conv_w, gates_row, par, n_ctx, heads=4):
    bsz, t, _ = qkv.shape
    assert GRID_W == CHUNK and n_ctx % CHUNK == 0
    n_chunks = t // CHUNK
    group = math.gcd(9, n_chunks)
    wd = heads * HEAD_DIM
    blocks = GDN_HEADS // heads
    spec = lambda off: pl.BlockSpec((1, t, wd), lambda b, h: (b, 0, off + h), pipeline_mode=pl.Buffered(1))
    wspec = lambda off: pl.BlockSpec((9, wd), lambda b, h: (0, off + h))
    conv_w9 = conv_w.reshape(9, GDN_QKV)
    return pl.pallas_call(
        functools.partial(_gdn_kernel, n_chunks=n_chunks, ctx_chunks=n_ctx // CHUNK, group=group,
                          heads=heads),
        grid=(bsz, blocks),
        in_specs=[pl.BlockSpec(memory_space=pltpu.SMEM),
                  spec(0), spec(blocks), spec(2 * blocks),
                  wspec(0), wspec(blocks), wspec(2 * blocks),
                  pl.BlockSpec((1, 4 * GDN_HEADS, n_chunks, CHUNK), lambda b, h: (b, 0, 0, 0))],
        out_specs=pl.BlockSpec((1, t, wd), lambda b, h: (b, 0, h)),
        out_shape=jax.ShapeDtypeStruct((bsz, t, D_MODEL), BF16),
        scratch_shapes=[pltpu.VMEM((heads, 2, n_chunks, CHUNK + HEAD_DIM, HEAD_DIM), BF16),
                        pltpu.VMEM((heads, 2, n_chunks, CHUNK, HEAD_DIM), BF16),
                        pltpu.VMEM((heads, 2, n_chunks, HEAD_DIM, HEAD_DIM), BF16),
                        pltpu.VMEM((heads, 2, n_chunks, 8, HEAD_DIM), F32)],
        compiler_params=_cparams(("parallel", "parallel")),
        name="gdn",
    )(par, qkv, qkv, qkv, conv_w9, conv_w9, conv_w9, gates_row)


def _odd_finish_kernel(x_ref, o_in_ref, z_ref, hw_ref, wo_ref, g_ref, o_ref):
    o = o_in_ref[0].astype(F32)
    parts = []
    for h in range(GDN_HEADS):
        sl = slice(h * HEAD_DIM, (h + 1) * HEAD_DIM)
        parts.append(_rms(o[:, sl], hw_ref[:, sl]))
    z = z_ref[0].astype(F32)
    y = jnp.concatenate(parts, axis=-1) * (z * jax.nn.sigmoid(z))
    o_ref[0] = x_ref[0] + g_ref[0, 0] * _dot(y, wo_ref[...])


def _odd_finish(xc, o, z, mods, layer, hw, wo, tm, n_ctx):
    bsz, t, d = xc.shape
    assert n_ctx % tm == 0 and (t - n_ctx) % tm == 0
    t0 = n_ctx // tm
    return pl.pallas_call(
        _odd_finish_kernel,
        grid=(bsz, (t - n_ctx) // tm),
        in_specs=[pl.BlockSpec((1, tm, d), lambda b, i: (b, i + t0, 0)),
                  pl.BlockSpec((1, tm, d), lambda b, i: (b, i + t0, 0)),
                  pl.BlockSpec((1, tm, d), lambda b, i: (b, i + t0, 0)),
                  pl.BlockSpec((1, d), lambda b, i: (0, 0)),
                  pl.BlockSpec((d, d), lambda b, i: (0, 0)),
                  _lat_mod_spec(layer, 2)],
        out_specs=pl.BlockSpec((1, tm, d), lambda b, i: (b, i, 0)),
        out_shape=jax.ShapeDtypeStruct((bsz, t - n_ctx, d), F32),
        compiler_params=_cparams(("parallel", "parallel")),
        name="odd_finish",
    )(xc, o, z, hw.reshape(1, d), wo, mods)


def _chunked(gates_t, chunk=CHUNK):
    bsz, g, t = gates_t.shape
    return gates_t.reshape(bsz, g, t // chunk, chunk)


def _pick_tile(total, target, at_least, multiple=8):
    for tile in range(min(target, total), multiple - 1, -1):
        if total % tile == 0 and tile % multiple == 0 and tile >= at_least:
            return tile
    raise ValueError(f"no row tile for {total} rows")


def _pad_cols(w, n):
    return jnp.pad(w, ((0, 0), (0, n - w.shape[1])))


def kernel(x, c, ctx, c_ctx, ada_w, ada_b, norm1_w, norm2_w, ffn_w1, ffn_w3, ffn_w2, final_norm_w,
           ev_w_in, ev_i_bias, ev_f_bias, ev_head_norm_w, ev_lam_re, ev_lam_im, ev_log_dt,
           ev_b_re, ev_b_im, ev_c_re, ev_c_im, ev_d, ev_w_glu, ev_w_out,
           od_w_in, od_conv_w, od_a_log, od_dt_bias, od_head_norm_w, od_w_out):
    bsz, seq, d = x.shape
    n_ctx = ctx.shape[1]
    assert d == D_MODEL and seq % GRID_W == 0 and n_ctx % CHUNK == 0 and bsz % 8 == 0
    assert ada_w.shape[0] == 2
    t_all = n_ctx + seq
    tm = _pick_tile(t_all, 768, n_ctx, LANES)
    tm_lat = _pick_tile(math.gcd(seq, n_ctx), 512, 0)
    th = FFN_HIDDEN // 2
    tb = CHUNK

    if tm % n_ctx == 0 and seq % n_ctx == 0:
        xs = (ctx, x)
    else:
        xs = (jnp.concatenate([ctx, x], axis=1),)
    mod_rows = 16
    cvec = jnp.zeros((mod_rows, d), F32).at[:bsz].set(c).at[bsz].set(c_ctx)
    mods = _ada(cvec, ada_w, ada_b).reshape(2, mod_rows, 1, 6 * d)
    w1, w3, w2 = ffn_w1.astype(BF16), ffn_w3.astype(BF16), ffn_w2.astype(BF16)

    qkv_w, n_g = 3 * MLSTM_WIDTH, 4 * MLSTM_HEADS
    w_in = ev_w_in[0]
    g0 = qkv_w + MLSTM_WIDTH
    n_even = w_in.shape[1]
    w_even = _pad_cols(w_in, -(-n_even // LANES) * LANES).astype(BF16)
    qkv, o_pre, u, gates = _inproj(
        xs, norm1_w[0], mods, 0, w_even,
        [(0, qkv_w, "row", BF16, qkv_w), (qkv_w, g0, "row", BF16, g0 - qkv_w),
         (g0 + n_g, n_even, "row", BF16, S5_WIDTH), (g0, g0 + LANES, "gate", F32, n_g)], tm, n_ctx)
    bias = jnp.concatenate([ev_i_bias[0].reshape(-1), ev_f_bias[0].reshape(-1)]).astype(F32)
    mlstm_chunk = MLSTM_CHUNK if n_ctx % MLSTM_CHUNK == 0 and seq % MLSTM_CHUNK == 0 else CHUNK
    ha = _mlstm(qkv, gates, bias, n_ctx, mlstm_chunk)
    bd, cdt, avec = _s5_prep(ev_lam_re[0], ev_lam_im[0], ev_log_dt[0], ev_b_re[0], ev_b_im[0],
                             ev_c_re[0], ev_c_im[0])
    yf, yb = _s5(u, bd, cdt, avec, n_ctx, tb)
    xc = _even_finish(xs, ha, o_pre, yf, yb, u, mods, 0, ev_head_norm_w[0], ev_d[0],
                      ev_w_glu[0].astype(BF16), ev_w_out[0].astype(BF16), tm, n_ctx)
    xc = _ffn(xc, norm2_w[0], mods, 0, w1, w3, w2, final_norm_w, tm, th, n_ctx, False)

    n_g = 4 * GDN_HEADS
    w_odd = _pad_cols(od_w_in[0], 4 * d + LANES).astype(BF16)
    qkv_pre, z, gates = _inproj(
        (xc,), norm1_w[1], mods, 1, w_odd,
        [(0, GDN_QKV, "row", BF16, GDN_QKV), (GDN_QKV, 4 * d, "row", BF16, d),
         (4 * d, 4 * d + LANES, "gate", F32, n_g)], tm, n_ctx)
    par = jnp.concatenate([od_a_log[0].reshape(-1), od_dt_bias[0].reshape(-1)]).astype(F32)
    o = _gdn(qkv_pre, od_conv_w[0], _chunked(gates), par, n_ctx)
    xl = _odd_finish(xc, o, z, mods, 1, od_head_norm_w[0], od_w_out[0].astype(BF16), tm_lat, n_ctx)
    return _ffn(xl, norm2_w[1], mods, 1, w1, w3, w2, final_norm_w, _pick_tile(seq, 512, 0),
                th, 0, True)
```

```python
import functools
import math

import jax
import jax.numpy as jnp
from jax import lax
from jax.experimental import pallas as pl
from jax.experimental.pallas import tpu as pltpu

F32 = jnp.float32
BF16 = jnp.bfloat16

D_MODEL = 1024
CHUNK = 64
MLSTM_CHUNK = 128
GRID_W = 64
EPS = 1e-6
LANES = 128
HEAD_DIM = 128
MLSTM_WIDTH = D_MODEL // 2
MLSTM_HEADS = MLSTM_WIDTH // HEAD_DIM
S5_WIDTH = D_MODEL - MLSTM_WIDTH
S5_GROUP = 16
S5_GROUPS = S5_WIDTH // S5_GROUP
S5_STATE = 64
S5_HALVES = 2
S5_HALF_W = S5_WIDTH // S5_HALVES
S5_HALF_N = (S5_GROUPS // S5_HALVES) * S5_STATE
GDN_HEADS = D_MODEL // HEAD_DIM
GDN_QKV = 3 * D_MODEL
FFN_HIDDEN = ((8 * D_MODEL + 3 * 256 - 1) // (3 * 256)) * 256
VMEM_LIMIT = 56 * 1024 * 1024

_NT = (((1,), (1,)), ((), ()))
_TN = (((0,), (0,)), ((), ()))


def _cparams(sem):
    return pltpu.CompilerParams(dimension_semantics=sem, vmem_limit_bytes=VMEM_LIMIT)


def _dot(a, b):
    return jnp.dot(a.astype(BF16), b.astype(BF16), preferred_element_type=F32)


def _dot_nt(a, b):
    return lax.dot_general(a.astype(BF16), b.astype(BF16), _NT, preferred_element_type=F32)


def _dot_tn(a, b):
    return lax.dot_general(a.astype(BF16), b.astype(BF16), _TN, preferred_element_type=F32)


def _rms(x, w):
    return x * lax.rsqrt(jnp.mean(x * x, axis=-1, keepdims=True) + EPS) * w


def _ada_kernel(s_ref, w_ref, b_ref, o_ref):
    s = s_ref[...]
    s = s * jax.nn.sigmoid(s)
    o_ref[0] = _dot(s, w_ref[0]) + b_ref[0]


def _ada(cvec, ada_w, ada_b):
    depth, d, n = ada_w.shape
    tn = 1536
    rows = cvec.shape[0]
    return pl.pallas_call(
        _ada_kernel,
        grid=(depth, n // tn),
        in_specs=[pl.BlockSpec((rows, d), lambda l, j: (0, 0)),
                  pl.BlockSpec((1, d, tn), lambda l, j: (l, 0, j)),
                  pl.BlockSpec((1, 1, tn), lambda l, j: (l, 0, j))],
        out_specs=pl.BlockSpec((1, rows, tn), lambda l, j: (l, 0, j)),
        out_shape=jax.ShapeDtypeStruct((depth, rows, n), F32),
        compiler_params=_cparams(("parallel", "parallel")),
        name="ada_mod",
    )(cvec, ada_w, ada_b.reshape(depth, 1, n))


def _ctx_mod_spec(layer, chunk, ctx_row):
    return pl.BlockSpec((1, 1, 1, D_MODEL), lambda b, i: (layer, ctx_row, 0, chunk))


def _lat_mod_spec(layer, chunk):
    return pl.BlockSpec((1, 1, 1, D_MODEL), lambda b, i: (layer, b, 0, chunk))


def _row_groups(n_top, tm, x_refs, ctx_refs, lat_refs):
    lat = [r[0, 0] for r in lat_refs]
    if n_top == 0:
        return [(slice(0, tm), x_refs[0][0], lat)]
    first = pl.program_id(1) == 0
    top = [jnp.where(first, c[0, 0], l) for c, l in zip(ctx_refs, lat)]
    if len(x_refs) == 1:
        x_ref = x_refs[0]
        return [(slice(0, n_top), x_ref[0, :n_top, :], top), (slice(n_top, tm), x_ref[0, n_top:, :], lat)]
    ctx_ref, subs = x_refs[0], x_refs[1:]
    groups = [(slice(0, n_top), jnp.where(first, ctx_ref[0], subs[0][0]), top)]
    return groups + [(slice(k * n_top, (k + 1) * n_top), subs[k][0], lat) for k in range(1, len(subs))]


def _x_operands(xs, tm, n_ctx):
    d = xs[0].shape[-1]
    if len(xs) == 1:
        return list(xs), [pl.BlockSpec((1, tm, d), lambda b, i: (b, i, 0))]
    ctx, lat = xs
    nsub = tm // n_ctx
    assert tm % n_ctx == 0 and lat.shape[1] % n_ctx == 0
    specs = [pl.BlockSpec((1, n_ctx, d), lambda b, i: (b, 0, 0))]
    for k in range(nsub):
        specs.append(pl.BlockSpec((1, n_ctx, d), lambda b, i, k=k: (b, jnp.maximum(i * nsub + k - 1, 0), 0)))
    return [ctx] + [lat] * nsub, specs


def _inproj_kernel(*refs, outs, n_top, n_chunk, n_x):
    x_refs, (nw_ref, shc_ref, scc_ref, shl_ref, scl_ref, w_ref) = refs[:n_x], refs[n_x:n_x + 6]
    out_refs, h_scr, g_scr = refs[n_x + 6:-2], refs[-2], refs[-1]
    tm = h_scr.shape[0]
    for rows, x, (sh, sc) in _row_groups(n_top, tm, x_refs, (shc_ref, scc_ref), (shl_ref, scl_ref)):
        h_scr[rows, :] = (_rms(x, nw_ref[...]) * (1.0 + sc) + sh).astype(BF16)
    for o_ref, (c0, c1, kind) in zip(out_refs, outs):
        if kind == "row" and (c0 % LANES or c1 % LANES):
            a0, a1 = c0 // LANES * LANES, min(-(-c1 // LANES) * LANES, w_ref.shape[1])
            val = jnp.dot(h_scr[...], w_ref[:, a0:a1], preferred_element_type=F32)
            o_ref[0] = val[:, c0 - a0:c1 - a0].astype(o_ref.dtype)
            continue
        for n0 in range(c0, c1, n_chunk):
            n1 = min(n0 + n_chunk, c1)
            val = jnp.dot(h_scr[...], w_ref[:, n0:n1], preferred_element_type=F32)
            if kind == "row":
                o_ref[0, :, n0 - c0:n1 - c0] = val.astype(o_ref.dtype)
            else:
                if val.shape[1] < LANES:
                    val = jnp.concatenate([val, jnp.zeros((tm, LANES - val.shape[1]), F32)], axis=1)
                g_scr[...] = val
                o_ref[0] = g_scr[...].T[:o_ref.shape[1], :].astype(o_ref.dtype)


def _inproj(xs, norm_w, mods, layer, w_bf16, outs, tm, n_ctx):
    bsz, d = xs[0].shape[0], xs[0].shape[-1]
    t = sum(a.shape[1] for a in xs)
    n = w_bf16.shape[1]
    x_args, x_specs = _x_operands(xs, tm, n_ctx)
    assert t % tm == 0 and (n_ctx == 0 or n_ctx <= tm)
    out_specs, out_shapes = [], []
    for c0, c1, kind, dtype, wd in outs:
        if kind == "row":
            out_specs.append(pl.BlockSpec((1, tm, wd), lambda b, i: (b, i, 0)))
            out_shapes.append(jax.ShapeDtypeStruct((bsz, t, wd), dtype))
        else:
            assert c1 - c0 <= LANES and c0 % LANES == 0 and tm % LANES == 0
            out_specs.append(pl.BlockSpec((1, wd, tm), lambda b, i: (b, 0, i)))
            out_shapes.append(jax.ShapeDtypeStruct((bsz, wd, t), dtype))
    ctx_spec = lambda c: _ctx_mod_spec(layer, c, bsz)
    lat_spec = lambda c: _lat_mod_spec(layer, c)
    const = lambda shape: pl.BlockSpec(shape, lambda b, i: (0, 0), pipeline_mode=pl.Buffered(1))
    return pl.pallas_call(
        functools.partial(_inproj_kernel, outs=tuple(o[:3] for o in outs), n_top=n_ctx, n_chunk=512,
                          n_x=len(x_args)),
        grid=(bsz, t // tm),
        in_specs=x_specs + [const((1, d)), ctx_spec(0), ctx_spec(1), lat_spec(0), lat_spec(1),
                            const((d, n))],
        out_specs=out_specs,
        out_shape=out_shapes,
        scratch_shapes=[pltpu.VMEM((tm, d), BF16), pltpu.VMEM((tm, LANES), F32)],
        compiler_params=_cparams(("parallel", "parallel")),
        name=f"inproj_l{layer}",
    )(*x_args, norm_w.reshape(1, d), mods, mods, mods, mods, w_bf16)


def _ffn_kernel(x_ref, nw_ref, shc_ref, scc_ref, gc_ref, shl_ref, scl_ref, gl_ref,
                w1_ref, w3_ref, w2_ref, fw_ref, o_ref, h_scr, *, final, n_top, th):
    tm, hid = h_scr.shape[0], w1_ref.shape[1]
    groups = _row_groups(n_top, tm, (x_ref,), (shc_ref, scc_ref, gc_ref), (shl_ref, scl_ref, gl_ref))
    for rows, x, (sh, sc, _) in groups:
        h_scr[rows, :] = (_rms(x, nw_ref[...]) * (1.0 + sc) + sh).astype(BF16)
    acc = None
    for c0 in range(0, hid, th):
        h = h_scr[...]
        a = jnp.dot(h, w1_ref[:, c0:c0 + th], preferred_element_type=F32)
        g = jnp.dot(h, w3_ref[:, c0:c0 + th], preferred_element_type=F32)
        t = ((a * jax.nn.sigmoid(a)) * g).astype(BF16)
        part = jnp.dot(t, w2_ref[c0:c0 + th, :], preferred_element_type=F32)
        acc = part if acc is None else acc + part
    for rows, x, (_, _, gate) in groups:
        y = x + gate * acc[rows, :]
        if final:
            y = _rms(y, fw_ref[...])
        o_ref[0, rows, :] = y


def _ffn(x, norm_w, mods, layer, w1, w3, w2, final_w, tm, th, n_ctx, final):
    bsz, tx, d = x.shape
    hid = w1.shape[2]
    assert n_ctx <= tm and tx % tm == 0 and hid % th == 0
    ctx_spec = lambda c: _ctx_mod_spec(layer, c, bsz)
    lat_spec = lambda c: _lat_mod_spec(layer, c)
    const = lambda shape: pl.BlockSpec(shape, lambda b, i: (0, 0), pipeline_mode=pl.Buffered(1))
    layer_w = lambda shape: pl.BlockSpec((None,) + shape, lambda b, i: (layer, 0, 0),
                                         pipeline_mode=pl.Buffered(1))
    return pl.pallas_call(
        functools.partial(_ffn_kernel, final=final, n_top=n_ctx, th=th),
        grid=(bsz, tx // tm),
        in_specs=[pl.BlockSpec((1, tm, d), lambda b, i: (b, i, 0)),
                  const((1, d)),
                  ctx_spec(3), ctx_spec(4), ctx_spec(5), lat_spec(3), lat_spec(4), lat_spec(5),
                  layer_w((d, hid)), layer_w((d, hid)), layer_w((hid, d)), const((1, d))],
        out_specs=pl.BlockSpec((1, tm, d), lambda b, i: (b, i, 0)),
        out_shape=jax.ShapeDtypeStruct((bsz, tx, d), F32),
        scratch_shapes=[pltpu.VMEM((tm, d), BF16)],
        compiler_params=_cparams(("parallel", "parallel")),
        name=f"ffn_l{layer}",
    )(x, norm_w.reshape(1, d), mods, mods, mods, mods, mods, mods, w1, w3, w2, final_w.reshape(1, d))


def _chunk_masks(size=CHUNK):
    row = lax.broadcasted_iota(jnp.int32, (size, size), 0)
    col = lax.broadcasted_iota(jnp.int32, (size, size), 1)
    return row, col


def _to_col(row_vec, eye):
    return jnp.sum(jnp.where(eye, jnp.broadcast_to(row_vec, eye.shape), 0.0), axis=1, keepdims=True)


def _to_row(col_vec, eye):
    return jnp.sum(jnp.where(eye, jnp.broadcast_to(col_vec, eye.shape), 0.0), axis=0, keepdims=True)


def _scan_chunk_index(i, rev, n_chunks, ctx_chunks):
    if not rev:
        return i
    return jnp.where(i < ctx_chunks, ctx_chunks - 1 - i, n_chunks + ctx_chunks - 1 - i)


def _mlstm_kernel(bias_ref, q_ref, k_ref, v_ref, g_ref, o_ref, num_scr, cl_scr, mi_scr, fc_scr, sc_scr,
                  *, chunk, n_chunks, ctx_chunks, group):
    hd = pl.program_id(1)
    row, col = _chunk_masks(chunk)
    eye = row == col
    kscale = 1.0 / math.sqrt(HEAD_DIM)
    ones_blk = jnp.ones((chunk, HEAD_DIM), BF16)

    def gate_terms(r, ig_raw, f_raw):
        incl = (col >= row) if r == 1 else (col <= row)
        ig_row = ig_raw + bias_ref[r * MLSTM_HEADS + hd]
        lf_row = jax.nn.log_sigmoid(f_raw + bias_ref[(2 + r) * MLSTM_HEADS + hd])
        f_col = jnp.sum(jnp.where(incl, jnp.broadcast_to(lf_row, incl.shape), 0.0),
                        axis=1, keepdims=True)
        f_row = _to_row(f_col, eye)
        f_last = jnp.sum(lf_row, axis=1, keepdims=True)
        dm = jnp.where(incl, f_col - f_row + ig_row, -jnp.inf)
        m_intra = jnp.max(dm, axis=1, keepdims=True)
        w_col = f_last - f_col + _to_col(ig_row, eye)
        m_loc = jnp.max(w_col, axis=0, keepdims=True)
        return f_col, f_last, jnp.exp(dm - m_intra), m_intra, jnp.exp(w_col - m_loc), m_loc

    def local_group(gi, _):
        chunks = [gi * group + g for g in range(group)]
        loaded = []
        for c in chunks:
            rows = pl.ds(pl.multiple_of(c * chunk, chunk), chunk)
            gates = [(g_ref[0, r * MLSTM_HEADS + hd, pl.ds(c, 1), :],
                      g_ref[0, (2 + r) * MLSTM_HEADS + hd, pl.ds(c, 1), :]) for r in (0, 1)]
            v1 = jnp.concatenate([v_ref[0, rows, :].astype(BF16), ones_blk], axis=-1)
            loaded.append((q_ref[0, rows, :], k_ref[0, rows, :].astype(F32) * kscale, v1, gates))
        chains = [(g, r) for g in range(group) for r in (0, 1)]
        terms = [gate_terms(r, *loaded[g][3][r]) for g, r in chains]
        qks = [_dot_nt(q, k) for q, k, _, _ in loaded]
        c_locs = [_dot_tn(tm[4] * loaded[g][1], loaded[g][2]) for (g, r), tm in zip(chains, terms)]
        nums = [_dot(tm[2] * qks[g], loaded[g][2]) for (g, r), tm in zip(chains, terms)]
        for (g, r), tm, c_loc, num in zip(chains, terms, c_locs, nums):
            f_col, f_last, _, m_intra, _, m_loc = tm
            c = chunks[g]
            num_scr[r, c] = num
            cl_scr[r, c] = c_loc
            mi_scr[r, c] = jnp.broadcast_to(m_intra, (chunk, HEAD_DIM))
            fc_scr[r, c] = jnp.broadcast_to(f_col, (chunk, HEAD_DIM))
            sc_scr[r, c, :8, :] = jnp.broadcast_to(f_last, (8, HEAD_DIM))
            sc_scr[r, c, 8:, :] = jnp.broadcast_to(m_loc, (8, HEAD_DIM))
        return 0

    lax.fori_loop(0, n_chunks // group, local_group, 0)
    o_ref[...] = jnp.zeros_like(o_ref)

    def step(i, carry):
        cs = [_scan_chunk_index(i, r == 1, n_chunks, ctx_chunks) for r in (0, 1)]
        rows = [pl.ds(pl.multiple_of(c * chunk, chunk), chunk) for c in cs]
        qcs = [_dot(q_ref[0, rw, :], st[0]) for rw, st in zip(rows, carry)]
        new = []
        for r in (0, 1):
            s_st, m_st = carry[r]
            c = cs[r]
            mi, na = mi_scr[r, c], num_scr[r, c]
            inter = fc_scr[r, c] + m_st[0:1, :]
            m_t = jnp.maximum(mi, inter)
            a_loc_t, a_inter = jnp.exp(mi - m_t), jnp.exp(inter - m_t)
            num = a_loc_t * na[:, :HEAD_DIM] + a_inter * qcs[r][:, :HEAD_DIM]
            den = a_loc_t * na[:, HEAD_DIM:] + a_inter * qcs[r][:, HEAD_DIM:]
            out = num / jnp.maximum(jnp.abs(den), jnp.exp(-m_t))
            o_ref[0, rows[r], :] = (o_ref[0, rows[r], :].astype(F32) + out).astype(o_ref.dtype)
            f_last, m_loc = sc_scr[r, c, :8, :], sc_scr[r, c, 8:, :]
            m_new = jnp.maximum(f_last + m_st, m_loc)
            a_prev = jnp.exp(f_last + m_st - m_new)[0:1, :]
            a_loc = jnp.exp(m_loc - m_new)[0:1, :]
            a_prev = jnp.concatenate([a_prev, a_prev], axis=1)
            a_loc = jnp.concatenate([a_loc, a_loc], axis=1)
            new.append((a_prev * s_st + a_loc * cl_scr[r, c], m_new))
        return tuple(new)

    init = (jnp.zeros((HEAD_DIM, 2 * HEAD_DIM), F32), jnp.zeros((8, HEAD_DIM), F32))
    lax.fori_loop(0, n_chunks, step, (init, init), unroll=2)


def _mlstm(qkv, gates, bias, n_ctx, chunk):
    bsz, t, _ = qkv.shape
    assert t % chunk == 0 and n_ctx % chunk == 0
    n_chunks = t // chunk
    gates_row = _chunked(gates, chunk)
    qkv_spec = lambda off: pl.BlockSpec((1, t, HEAD_DIM), lambda b, h: (b, 0, off + h))
    return pl.pallas_call(
        functools.partial(_mlstm_kernel, chunk=chunk, n_chunks=n_chunks, ctx_chunks=n_ctx // chunk,
                          group=math.gcd(9 if chunk <= CHUNK else 3, n_chunks)),
        grid=(bsz, MLSTM_HEADS),
        in_specs=[pl.BlockSpec(memory_space=pltpu.SMEM),
                  qkv_spec(0), qkv_spec(MLSTM_HEADS), qkv_spec(2 * MLSTM_HEADS),
                  pl.BlockSpec((1, 4 * MLSTM_HEADS, n_chunks, chunk), lambda b, h: (b, 0, 0, 0))],
        out_specs=pl.BlockSpec((1, t, HEAD_DIM), lambda b, h: (b, 0, h)),
        out_shape=jax.ShapeDtypeStruct((bsz, t, MLSTM_WIDTH), BF16),
        scratch_shapes=[pltpu.VMEM((2, n_chunks, chunk, 2 * HEAD_DIM), F32),
                        pltpu.VMEM((2, n_chunks, HEAD_DIM, 2 * HEAD_DIM), F32),
                        pltpu.VMEM((2, n_chunks, chunk, HEAD_DIM), F32),
                        pltpu.VMEM((2, n_chunks, chunk, HEAD_DIM), F32),
                        pltpu.VMEM((2, n_chunks, 16, HEAD_DIM), F32)],
        compiler_params=_cparams(("parallel", "parallel")),
        name="mlstm",
    )(bias, qkv, qkv, qkv, gates_row)


def _s5_prep_kernel(lr_ref, li_ref, ldt_ref, br_ref, bi_ref, cr_ref, ci_ref, bd_ref, cdt_ref, a_ref):
    lr, li = lr_ref[0, 0], li_ref[0, 0]
    dt = jnp.exp(ldt_ref[0, 0])
    mag, ang = jnp.exp(lr * dt), li * dt
    ab_re, ab_im = mag * jnp.cos(ang), mag * jnp.sin(ang)
    nr, ni = ab_re - 1.0, ab_im
    den = lr * lr + li * li
    co_re = (nr * lr + ni * li) / den
    co_im = (ni * lr - nr * li) / den
    b_re, b_im = br_ref[0, 0], bi_ref[0, 0]
    bb_re = co_re * b_re - co_im * b_im
    bb_im = co_re * b_im + co_im * b_re
    c_re, c_im = cr_ref[0, 0], ci_ref[0, 0]
    lane_group = lax.broadcasted_iota(jnp.int32, (S5_GROUP, S5_HALF_N), 1) // S5_STATE
    n = S5_HALF_N
    for g in range(S5_GROUPS // S5_HALVES):
        sel = lane_group == g
        rows = slice(g * S5_GROUP, (g + 1) * S5_GROUP)
        bd_ref[0, 0, rows, :n] = jnp.where(sel, bb_re, 0.0).astype(bd_ref.dtype)
        bd_ref[0, 0, rows, n:] = jnp.where(sel, bb_im, 0.0).astype(bd_ref.dtype)
        cdt_ref[0, 0, rows, :n] = jnp.where(sel, c_re, 0.0).astype(cdt_ref.dtype)
        cdt_ref[0, 0, rows, n:] = jnp.where(sel, -c_im, 0.0).astype(cdt_ref.dtype)
    a_ref[0, 0, :, :n] = jnp.broadcast_to(ab_re, (8, n))
    a_ref[0, 0, :, n:] = jnp.broadcast_to(ab_im, (8, n))


def _s5_prep(lam_re, lam_im, log_dt, b_re, b_im, c_re, c_im):
    gh = S5_GROUPS // S5_HALVES
    vec = lambda a: a.reshape(2, S5_HALVES, 1, S5_HALF_N)
    ldt = vec(jnp.broadcast_to(log_dt[:, :, None], (2, S5_GROUPS, S5_STATE)))
    bt = lambda a: a.reshape(2, S5_HALVES, gh, S5_STATE, S5_GROUP).transpose(0, 1, 4, 2, 3).reshape(
        2, S5_HALVES, S5_GROUP, S5_HALF_N)
    ct = lambda a: a.reshape(2, S5_HALVES, gh, S5_GROUP, S5_STATE).transpose(0, 1, 3, 2, 4).reshape(
        2, S5_HALVES, S5_GROUP, S5_HALF_N)
    vspec = pl.BlockSpec((1, 1, 1, S5_HALF_N), lambda r, h: (r, h, 0, 0))
    mspec = pl.BlockSpec((1, 1, S5_GROUP, S5_HALF_N), lambda r, h: (r, h, 0, 0))
    ospec = pl.BlockSpec((1, 1, S5_HALF_W, 2 * S5_HALF_N), lambda r, h: (r, h, 0, 0))
    return pl.pallas_call(
        _s5_prep_kernel,
        grid=(2, S5_HALVES),
        in_specs=[vspec, vspec, vspec, mspec, mspec, mspec, mspec],
        out_specs=[ospec, ospec, pl.BlockSpec((1, 1, 8, 2 * S5_HALF_N), lambda r, h: (r, h, 0, 0))],
        out_shape=[jax.ShapeDtypeStruct((2, S5_HALVES, S5_HALF_W, 2 * S5_HALF_N), BF16),
                   jax.ShapeDtypeStruct((2, S5_HALVES, S5_HALF_W, 2 * S5_HALF_N), BF16),
                   jax.ShapeDtypeStruct((2, S5_HALVES, 8, 2 * S5_HALF_N), F32)],
        compiler_params=_cparams(("parallel", "parallel")),
        name="s5_prep",
    )(vec(lam_re), vec(lam_im), ldt, bt(b_re), bt(b_im), ct(c_re), ct(c_im))


def _s5_kernel(uf0_ref, uf1_ref, ub0_ref, ub1_ref, bd_ref, cdt_ref, a_ref, yf_ref, yb_ref,
               lhs_scr, bu_scr, s_scr, st_scr, *, tb, bsz):
    n = S5_HALF_N
    rows = tb * bsz
    u_refs = ((uf0_ref, uf1_ref), (ub0_ref, ub1_ref))
    y_refs = (yf_ref, yb_ref)

    @pl.when(pl.program_id(1) == 0)
    def _():
        st_scr[...] = jnp.zeros_like(st_scr)

    for d in (0, 1):
        for j in (0, 1):
            lhs_scr[d, :, j * LANES:(j + 1) * LANES] = jnp.transpose(
                u_refs[d][j][...].astype(F32), (1, 0, 2)).reshape(rows, LANES)
    for d in (0, 1):
        bu_scr[d] = _dot(lhs_scr[d], bd_ref[d, 0])
    for d in (0, 1):
        a_re, a_im = a_ref[d, 0, :, :n], a_ref[d, 0, :, n:]
        s_re, s_im = st_scr[d, :, :n], st_scr[d, :, n:]
        for j in range(tb):
            t = j if d == 0 else tb - 1 - j
            sl = slice(t * bsz, (t + 1) * bsz)
            s_re, s_im = (a_re * s_re - a_im * s_im + bu_scr[d, sl, :n],
                          a_re * s_im + a_im * s_re + bu_scr[d, sl, n:])
            s_scr[d, sl, :n] = s_re
            s_scr[d, sl, n:] = s_im
        st_scr[d, :, :n] = s_re
        st_scr[d, :, n:] = s_im
        y = _dot_nt(s_scr[d], cdt_ref[d, 0])
        for j in (0, 1):
            y_refs[d][:, :, j * LANES:(j + 1) * LANES] = jnp.transpose(
                y[:, j * LANES:(j + 1) * LANES].reshape(tb, bsz, LANES), (1, 0, 2)).astype(y_refs[d].dtype)


def _s5(u, bd, cdt, avec, n_ctx, tb):
    bsz, t, _ = u.shape
    nb, ncb = t // tb, n_ctx // tb
    assert bsz == 8 and S5_HALF_W == 2 * LANES

    def rev(i):
        return jnp.where(i < ncb, ncb - 1 - i, nb + ncb - 1 - i)

    fwd = lambda i: i
    uspec = lambda order, j: pl.BlockSpec((bsz, tb, LANES), lambda h, i: (0, order(i), 2 * h + j))
    yspec = lambda order: pl.BlockSpec((bsz, tb, S5_HALF_W), lambda h, i: (0, order(i), h))
    wspec = pl.BlockSpec((2, 1, S5_HALF_W, 2 * S5_HALF_N), lambda h, i: (0, h, 0, 0))
    rows = tb * bsz
    return pl.pallas_call(
        functools.partial(_s5_kernel, tb=tb, bsz=bsz),
        grid=(S5_HALVES, nb),
        in_specs=[uspec(fwd, 0), uspec(fwd, 1), uspec(rev, 0), uspec(rev, 1), wspec, wspec,
                  pl.BlockSpec((2, 1, 8, 2 * S5_HALF_N), lambda h, i: (0, h, 0, 0))],
        out_specs=[yspec(fwd), yspec(rev)],
        out_shape=[jax.ShapeDtypeStruct((bsz, t, S5_WIDTH), BF16)] * 2,
        scratch_shapes=[pltpu.VMEM((2, rows, S5_HALF_W), F32),
                        pltpu.VMEM((2, rows, 2 * S5_HALF_N), F32),
                        pltpu.VMEM((2, rows, 2 * S5_HALF_N), F32),
                        pltpu.VMEM((2, bsz, 2 * S5_HALF_N), F32)],
        compiler_params=_cparams(("parallel", "arbitrary")),
        name="s5_scan",
    )(u, u, u, u, bd, cdt, avec)


def _even_finish_kernel(*refs, n_top, n_x):
    x_refs = refs[:n_x]
    ha_ref, op_ref, yf_ref, yb_ref, u_ref, mhw_ref, ds_ref, wg_ref, wo_ref, gc_ref, gl_ref, o_ref = refs[n_x:]
    ha = ha_ref[0].astype(F32)
    parts = []
    for h in range(MLSTM_HEADS):
        sl = slice(h * HEAD_DIM, (h + 1) * HEAD_DIM)
        parts.append(_rms(ha[:, sl], mhw_ref[:, sl]))
    a_out = jnp.concatenate(parts, axis=-1) * jax.nn.sigmoid(op_ref[0].astype(F32))
    yb = jax.nn.gelu(yf_ref[0].astype(F32) + yb_ref[0].astype(F32) + ds_ref[...] * u_ref[0].astype(F32))
    glu = _dot(yb, wg_ref[...])
    b_out = glu[:, :S5_WIDTH] * jax.nn.sigmoid(glu[:, S5_WIDTH:])
    y = _dot(a_out, wo_ref[:MLSTM_WIDTH, :]) + _dot(b_out, wo_ref[MLSTM_WIDTH:, :])
    for rows, x, (gate,) in _row_groups(n_top, y.shape[0], x_refs, (gc_ref,), (gl_ref,)):
        o_ref[0, rows, :] = x + gate * y[rows, :]


def _even_finish(xs, ha, o_pre, yf, yb, u, mods, layer, mh_w, d_skip, wg, wo, tm, n_ctx):
    bsz, t, _ = ha.shape
    d = xs[0].shape[-1]
    assert t % tm == 0 and n_ctx <= tm
    x_args, x_specs = _x_operands(xs, tm, n_ctx)
    full = lambda shape: pl.BlockSpec(shape, lambda b, i: tuple(0 for _ in shape),
                                      pipeline_mode=pl.Buffered(1))
    return pl.pallas_call(
        functools.partial(_even_finish_kernel, n_top=n_ctx, n_x=len(x_args)),
        grid=(bsz, t // tm),
        in_specs=x_specs + [
                  pl.BlockSpec((1, tm, MLSTM_WIDTH), lambda b, i: (b, i, 0)),
                  pl.BlockSpec((1, tm, MLSTM_WIDTH), lambda b, i: (b, i, 0)),
                  pl.BlockSpec((1, tm, S5_WIDTH), lambda b, i: (b, i, 0)),
                  pl.BlockSpec((1, tm, S5_WIDTH), lambda b, i: (b, i, 0)),
                  pl.BlockSpec((1, tm, S5_WIDTH), lambda b, i: (b, i, 0)),
                  full((1, MLSTM_WIDTH)), full((1, S5_WIDTH)),
                  full((S5_WIDTH, 2 * S5_WIDTH)), full((d, d)),
                  _ctx_mod_spec(layer, 2, bsz), _lat_mod_spec(layer, 2)],
        out_specs=pl.BlockSpec((1, tm, d), lambda b, i: (b, i, 0)),
        out_shape=jax.ShapeDtypeStruct((bsz, t, d), F32),
        compiler_params=_cparams(("parallel", "parallel")),
        name="even_finish",
    )(*x_args, ha, o_pre, yf, yb, u, mh_w.reshape(1, -1), d_skip.reshape(1, -1), wg, wo, mods, mods)


def _conv_chunk(x_ref, w, lanes, c, static, n_chunks, ctx_chunks):
    tpos = lax.broadcasted_iota(jnp.int32, (CHUNK, HEAD_DIM), 0)
    first, last = tpos == 0, tpos == CHUNK - 1
    wrow = lambda k: w[k:k + 1, :]

    def rows(ci, dtype=F32):
        start = ci * CHUNK if isinstance(ci, int) else pl.multiple_of(ci * CHUNK, CHUNK)
        return x_ref[0, pl.ds(start, CHUNK), lanes].astype(dtype)

    if static and c < ctx_chunks:
        cur = rows(c)
        zero = jnp.zeros((1, HEAD_DIM), F32)
        prev_tok = rows(c - 1)[CHUNK - 1:CHUNK, :] if c > 0 else zero
        next_tok = rows(c + 1)[0:1, :] if c < ctx_chunks - 1 else zero
        left = jnp.where(first, prev_tok, pltpu.roll(cur, 1, axis=0))
        right = jnp.where(last, next_tok, pltpu.roll(cur, CHUNK - 1, axis=0))
        return wrow(3) * left + wrow(4) * cur + wrow(5) * right

    if static:
        up_i, dn_i = max(c - 1, ctx_chunks), min(c + 1, n_chunks - 1)
        w_up = w[0:3, :] if c > ctx_chunks else jnp.zeros((3, HEAD_DIM), F32)
        w_dn = w[6:9, :] if c < n_chunks - 1 else jnp.zeros((3, HEAD_DIM), F32)
    else:
        up_i, dn_i = jnp.maximum(c - 1, ctx_chunks), jnp.minimum(c + 1, n_chunks - 1)
        w_up = jnp.where(c > ctx_chunks, w[0:3, :], 0.0)
        w_dn = jnp.where(c < n_chunks - 1, w[6:9, :], 0.0)
    up, cur, dn = rows(up_i, BF16), rows(c, BF16), rows(dn_i, BF16)
    w_up, w_mid, w_dn = w_up.astype(BF16), w[3:6, :].astype(BF16), w_dn.astype(BF16)
    r0, r1, r2 = ((w_up[dc:dc + 1, :] * up + w_mid[dc:dc + 1, :] * cur + w_dn[dc:dc + 1, :] * dn).astype(F32)
                  for dc in range(3))
    prev = jnp.where(first, 0.0, pltpu.roll(r0, 1, axis=0))
    nxt = jnp.where(last, 0.0, pltpu.roll(r2, CHUNK - 1, axis=0))
    return r1 + prev + nxt


def _unit_tri_inverses(mats, eye_f):
    ps = [eye_f - a for a in mats]
    pws = [_dot(a, a) for a in mats]
    k = 2
    while 2 * k < CHUNK:
        res = [_dot(jnp.concatenate([p, pw], axis=0), pw) for p, pw in zip(ps, pws)]
        ps = [p + r[:CHUNK] for p, r in zip(ps, res)]
        pws = [r[CHUNK:] for r in res]
        k *= 2
    return [p + _dot(p, pw) for p, pw in zip(ps, pws)]


def _gdn_kernel(par_ref, q_ref, k_ref, v_ref, wq_ref, wk_ref, wv_ref, g_ref, o_ref,
                qk_scr, ol_scr, nl_scr, gl_scr, *, n_chunks, ctx_chunks, group, heads):
    row, col = _chunk_masks()
    eye = row == col
    eye_f = eye.astype(F32)

    def gate_terms(hd, r, a_pre, b_pre):
        incl = (col >= row) if r == 1 else (col <= row)
        a_log = jnp.full((1, CHUNK), par_ref[r * GDN_HEADS + hd], F32)
        dt_b = par_ref[(2 + r) * GDN_HEADS + hd]
        g_row = -jnp.exp(a_log) * jax.nn.softplus(a_pre + dt_b)
        beta_col = _to_col(jax.nn.sigmoid(b_pre), eye)
        gc_col = jnp.sum(jnp.where(incl, jnp.broadcast_to(g_row, incl.shape), 0.0),
                         axis=1, keepdims=True)
        gc_row = _to_row(gc_col, eye)
        g_last = jnp.sum(g_row, axis=1, keepdims=True)
        gamma = jnp.exp(jnp.where(incl, gc_col - gc_row, -jnp.inf))
        return beta_col, gc_col, g_last, gamma

    for hh in range(heads):
        hd = pl.program_id(1) * heads + hh
        lanes = slice(hh * HEAD_DIM, (hh + 1) * HEAD_DIM)

        def load(c, static, hd=hd, lanes=lanes):
            gates = [(g_ref[0, r * GDN_HEADS + hd, pl.ds(c, 1), :],
                      g_ref[0, (2 + r) * GDN_HEADS + hd, pl.ds(c, 1), :]) for r in (0, 1)]
            q, k, v = (_conv_chunk(x_ref, w_ref[:, lanes], lanes, c, static, n_chunks, ctx_chunks)
                       for x_ref, w_ref in ((q_ref, wq_ref), (k_ref, wk_ref), (v_ref, wv_ref)))
            q, k, v = q * jax.nn.sigmoid(q), k * jax.nn.sigmoid(k), v * jax.nn.sigmoid(v)
            q = q * (lax.rsqrt(jnp.sum(q * q, axis=-1, keepdims=True) + EPS) * (1.0 / math.sqrt(HEAD_DIM)))
            k = k * lax.rsqrt(jnp.sum(k * k, axis=-1, keepdims=True) + EPS)
            return q, k, v, gates

        def local_group(gi, _, hh=hh, hd=hd, load=load):
            static = isinstance(gi, int)
            chunks = [gi * group + g for g in range(group)]
            loaded = [load(c, static) for c in chunks]
            kqs = [_dot_nt(jnp.concatenate([k, q], axis=0), k) for q, k, _, _ in loaded]
            chains = [(g, r) for g in range(group) for r in (0, 1)]
            terms = [gate_terms(hd, r, *loaded[g][3][r]) for g, r in chains]
            a_mats = []
            for (g, r), (beta_col, _, _, gamma) in zip(chains, terms):
                strict = (col > row) if r == 1 else (col < row)
                a_mats.append(jnp.where(strict, beta_col * kqs[g][:CHUNK] * gamma, 0.0))
            t_invs = _unit_tri_inverses(a_mats, eye_f)
            sols = []
            for (g, r), (beta_col, gc_col, _, _), t_inv in zip(chains, terms, t_invs):
                _, k, v, _ = loaded[g]
                rhs = jnp.concatenate([v * beta_col, k * (beta_col * jnp.exp(gc_col))], axis=-1)
                sols.append(_dot(t_inv, rhs))
            auws = [_dot(kqs[g][CHUNK:] * tm[3], sol) for (g, r), tm, sol in zip(chains, terms, sols)]
            kuws = [_dot_tn(loaded[g][1] * jnp.exp(tm[2] - tm[1]), sol)
                    for (g, r), tm, sol in zip(chains, terms, sols)]
            for (g, r), (_, gc_col, g_last, _), auw, kuw in zip(chains, terms, auws, kuws):
                c = chunks[g]
                q = loaded[g][0]
                qk_scr[hh, r, c, :CHUNK, :] = (q * jnp.exp(gc_col) - auw[:, HEAD_DIM:]).astype(qk_scr.dtype)
                qk_scr[hh, r, c, CHUNK:, :] = kuw[:, HEAD_DIM:].astype(qk_scr.dtype)
                ol_scr[hh, r, c] = auw[:, :HEAD_DIM].astype(ol_scr.dtype)
                nl_scr[hh, r, c] = kuw[:, :HEAD_DIM].astype(nl_scr.dtype)
                gl_scr[hh, r, c] = jnp.broadcast_to(jnp.exp(g_last), (8, HEAD_DIM))
            return 0

        n_static = -(-ctx_chunks // group)
        for gi in range(n_static):
            local_group(gi, 0)
        lax.fori_loop(n_static, n_chunks // group, local_group, 0)

    o_ref[...] = jnp.zeros_like(o_ref)
    streams = [(hh, r) for hh in range(heads) for r in (0, 1)]

    def step(i, carry):
        cs = [_scan_chunk_index(i, r == 1, n_chunks, ctx_chunks) for _, r in streams]
        res = [jnp.dot(qk_scr[hh, r, c], s.astype(BF16), preferred_element_type=F32)
               for (hh, r), c, s in zip(streams, cs, carry)]
        new = []
        for (hh, r), c, s, rs in zip(streams, cs, carry, res):
            rows = pl.ds(pl.multiple_of(c * CHUNK, CHUNK), CHUNK)
            lanes = slice(hh * HEAD_DIM, (hh + 1) * HEAD_DIM)
            o_ref[0, rows, lanes] = (o_ref[0, rows, lanes].astype(F32) + ol_scr[hh, r, c]
                                     + rs[:CHUNK]).astype(o_ref.dtype)
            new.append(gl_scr[hh, r, c][0:1, :] * s - rs[CHUNK:] + nl_scr[hh, r, c])
        return tuple(new)

    init = jnp.zeros((HEAD_DIM, HEAD_DIM), F32)
    lax.fori_loop(0, n_chunks, step, (init,) * len(streams), unroll=2)


def _gdn(qkv, conv_w, gates_row, par, n_ctx, heads=4):
    bsz, t, _ = qkv.shape
    assert GRID_W == CHUNK and n_ctx % CHUNK == 0
    n_chunks = t // CHUNK
    group = math.gcd(9, n_chunks)
    wd = heads * HEAD_DIM
    blocks = GDN_HEADS // heads
    spec = lambda off: pl.BlockSpec((1, t, wd), lambda b, h: (b, 0, off + h), pipeline_mode=pl.Buffered(1))
    wspec = lambda off: pl.BlockSpec((9, wd), lambda b, h: (0, off + h))
    conv_w9 = conv_w.reshape(9, GDN_QKV)
    return pl.pallas_call(
        functools.partial(_gdn_kernel, n_chunks=n_chunks, ctx_chunks=n_ctx // CHUNK, group=group,
                          heads=heads),
        grid=(bsz, blocks),
        in_specs=[pl.BlockSpec(memory_space=pltpu.SMEM),
                  spec(0), spec(blocks), spec(2 * blocks),
                  wspec(0), wspec(blocks), wspec(2 * blocks),
                  pl.BlockSpec((1, 4 * GDN_HEADS, n_chunks, CHUNK), lambda b, h: (b, 0, 0, 0))],
        out_specs=pl.BlockSpec((1, t, wd), lambda b, h: (b, 0, h)),
        out_shape=jax.ShapeDtypeStruct((bsz, t, D_MODEL), BF16),
        scratch_shapes=[pltpu.VMEM((heads, 2, n_chunks, CHUNK + HEAD_DIM, HEAD_DIM), BF16),
                        pltpu.VMEM((heads, 2, n_chunks, CHUNK, HEAD_DIM), BF16),
                        pltpu.VMEM((heads, 2, n_chunks, HEAD_DIM, HEAD_DIM), BF16),
                        pltpu.VMEM((heads, 2, n_chunks, 8, HEAD_DIM), F32)],
        compiler_params=_cparams(("parallel", "parallel")),
        name="gdn",
    )(par, qkv, qkv, qkv, conv_w9, conv_w9, conv_w9, gates_row)


def _odd_finish_kernel(x_ref, o_in_ref, z_ref, hw_ref, wo_ref, g_ref, o_ref):
    o = o_in_ref[0].astype(F32)
    parts = []
    for h in range(GDN_HEADS):
        sl = slice(h * HEAD_DIM, (h + 1) * HEAD_DIM)
        parts.append(_rms(o[:, sl], hw_ref[:, sl]))
    z = z_ref[0].astype(F32)
    y = jnp.concatenate(parts, axis=-1) * (z * jax.nn.sigmoid(z))
    o_ref[0] = x_ref[0] + g_ref[0, 0] * _dot(y, wo_ref[...])


def _odd_finish(xc, o, z, mods, layer, hw, wo, tm, n_ctx):
    bsz, t, d = xc.shape
    assert n_ctx % tm == 0 and (t - n_ctx) % tm == 0
    t0 = n_ctx // tm
    return pl.pallas_call(
        _odd_finish_kernel,
        grid=(bsz, (t - n_ctx) // tm),
        in_specs=[pl.BlockSpec((1, tm, d), lambda b, i: (b, i + t0, 0)),
                  pl.BlockSpec((1, tm, d), lambda b, i: (b, i + t0, 0)),
                  pl.BlockSpec((1, tm, d), lambda b, i: (b, i + t0, 0)),
                  pl.BlockSpec((1, d), lambda b, i: (0, 0)),
                  pl.BlockSpec((d, d), lambda b, i: (0, 0)),
                  _lat_mod_spec(layer, 2)],
        out_specs=pl.BlockSpec((1, tm, d), lambda b, i: (b, i, 0)),
        out_shape=jax.ShapeDtypeStruct((bsz, t - n_ctx, d), F32),
        compiler_params=_cparams(("parallel", "parallel")),
        name="odd_finish",
    )(xc, o, z, hw.reshape(1, d), wo, mods)


def _chunked(gates_t, chunk=CHUNK):
    bsz, g, t = gates_t.shape
    return gates_t.reshape(bsz, g, t // chunk, chunk)


def _pick_tile(total, target, at_least, multiple=8):
    for tile in range(min(target, total), multiple - 1, -1):
        if total % tile == 0 and tile % multiple == 0 and tile >= at_least:
            return tile
    raise ValueError(f"no row tile for {total} rows")


def kernel(x, c, ctx, c_ctx, ada_w, ada_b, norm1_w, norm2_w, ffn_w1, ffn_w3, ffn_w2, final_norm_w,
           ev_w_in, ev_i_bias, ev_f_bias, ev_head_norm_w, ev_lam_re, ev_lam_im, ev_log_dt,
           ev_b_re, ev_b_im, ev_c_re, ev_c_im, ev_d, ev_w_glu, ev_w_out,
           od_w_in, od_conv_w, od_a_log, od_dt_bias, od_head_norm_w, od_w_out):
    bsz, seq, d = x.shape
    n_ctx = ctx.shape[1]
    assert d == D_MODEL and seq % GRID_W == 0 and n_ctx % CHUNK == 0 and bsz % 8 == 0
    assert ada_w.shape[0] == 2
    t_all = n_ctx + seq
    tm = _pick_tile(t_all, 768, n_ctx, LANES)
    tm_lat = _pick_tile(math.gcd(seq, n_ctx), 512, 0)
    th = FFN_HIDDEN // 2
    tb = CHUNK

    if tm % n_ctx == 0 and seq % n_ctx == 0:
        xs = (ctx, x)
    else:
        xs = (jnp.concatenate([ctx, x], axis=1),)
    mod_rows = 16
    cvec = jnp.zeros((mod_rows, d), F32).at[:bsz].set(c).at[bsz].set(c_ctx)
    mods = _ada(cvec, ada_w, ada_b).reshape(2, mod_rows, 1, 6 * d)
    w1, w3, w2 = ffn_w1.astype(BF16), ffn_w3.astype(BF16), ffn_w2.astype(BF16)

    qkv_w, n_g = 3 * MLSTM_WIDTH, 4 * MLSTM_HEADS
    w_in = ev_w_in[0]
    g0 = qkv_w + MLSTM_WIDTH
    n_even = w_in.shape[1]
    qkv, o_pre, u, gates = _inproj(
        xs, norm1_w[0], mods, 0, w_in.astype(BF16),
        [(0, qkv_w, "row", BF16, qkv_w), (qkv_w, g0, "row", BF16, g0 - qkv_w),
         (g0 + n_g, n_even, "row", BF16, S5_WIDTH), (g0, g0 + n_g, "gate", F32, n_g)], tm, n_ctx)
    bias = jnp.concatenate([ev_i_bias[0].reshape(-1), ev_f_bias[0].reshape(-1)]).astype(F32)
    mlstm_chunk = MLSTM_CHUNK if n_ctx % MLSTM_CHUNK == 0 and seq % MLSTM_CHUNK == 0 else CHUNK
    ha = _mlstm(qkv, gates, bias, n_ctx, mlstm_chunk)
    bd, cdt, avec = _s5_prep(ev_lam_re[0], ev_lam_im[0], ev_log_dt[0], ev_b_re[0], ev_b_im[0],
                             ev_c_re[0], ev_c_im[0])
    yf, yb = _s5(u, bd, cdt, avec, n_ctx, tb)
    xc = _even_finish(xs, ha, o_pre, yf, yb, u, mods, 0, ev_head_norm_w[0], ev_d[0],
                      ev_w_glu[0].astype(BF16), ev_w_out[0].astype(BF16), tm, n_ctx)
    xc = _ffn(xc, norm2_w[0], mods, 0, w1, w3, w2, final_norm_w, tm, th, n_ctx, False)

    n_g = 4 * GDN_HEADS
    qkv_pre, z, gates = _inproj(
        (xc,), norm1_w[1], mods, 1, od_w_in[0].astype(BF16),
        [(0, GDN_QKV, "row", BF16, GDN_QKV), (GDN_QKV, 4 * d, "row", BF16, d),
         (4 * d, 4 * d + n_g, "gate", F32, n_g)], tm, n_ctx)
    par = jnp.concatenate([od_a_log[0].reshape(-1), od_dt_bias[0].reshape(-1)]).astype(F32)
    o = _gdn(qkv_pre, od_conv_w[0], _chunked(gates), par, n_ctx)
    xl = _odd_finish(xc, o, z, mods, 1, od_head_norm_w[0], od_w_out[0].astype(BF16), tm_lat, n_ctx)
    return _ffn(xl, norm2_w[1], mods, 1, w1, w3, w2, final_norm_w, _pick_tile(seq, 512, 0),
                th, 0, True)
```

```python
import functools
import math

import jax
import jax.numpy as jnp
from jax import lax
from jax.experimental import pallas as pl
from jax.experimental.pallas import tpu as pltpu

F32 = jnp.float32
BF16 = jnp.bfloat16

D_MODEL = 1024
CHUNK = 64
MLSTM_CHUNK = 128
GRID_W = 64
EPS = 1e-6
LANES = 128
HEAD_DIM = 128
MLSTM_WIDTH = D_MODEL // 2
MLSTM_HEADS = MLSTM_WIDTH // HEAD_DIM
S5_WIDTH = D_MODEL - MLSTM_WIDTH
S5_GROUP = 16
S5_GROUPS = S5_WIDTH // S5_GROUP
S5_STATE = 64
S5_HALVES = 2
S5_HALF_W = S5_WIDTH // S5_HALVES
S5_HALF_N = (S5_GROUPS // S5_HALVES) * S5_STATE
GDN_HEADS = D_MODEL // HEAD_DIM
GDN_QKV = 3 * D_MODEL
FFN_HIDDEN = ((8 * D_MODEL + 3 * 256 - 1) // (3 * 256)) * 256
VMEM_LIMIT = 56 * 1024 * 1024

_NT = (((1,), (1,)), ((), ()))
_TN = (((0,), (0,)), ((), ()))


def _cparams(sem):
    return pltpu.CompilerParams(dimension_semantics=sem, vmem_limit_bytes=VMEM_LIMIT)


def _dot(a, b):
    return jnp.dot(a.astype(BF16), b.astype(BF16), preferred_element_type=F32)


def _dot_nt(a, b):
    return lax.dot_general(a.astype(BF16), b.astype(BF16), _NT, preferred_element_type=F32)


def _dot_tn(a, b):
    return lax.dot_general(a.astype(BF16), b.astype(BF16), _TN, preferred_element_type=F32)


def _rms(x, w):
    return x * lax.rsqrt(jnp.mean(x * x, axis=-1, keepdims=True) + EPS) * w


def _ada_kernel(s_ref, w_ref, b_ref, o_ref):
    s = s_ref[...]
    s = s * jax.nn.sigmoid(s)
    o_ref[0] = _dot(s, w_ref[0]) + b_ref[0]


def _ada(cvec, ada_w, ada_b):
    depth, d, n = ada_w.shape
    tn = 1536
    rows = cvec.shape[0]
    return pl.pallas_call(
        _ada_kernel,
        grid=(depth, n // tn),
        in_specs=[pl.BlockSpec((rows, d), lambda l, j: (0, 0)),
                  pl.BlockSpec((1, d, tn), lambda l, j: (l, 0, j)),
                  pl.BlockSpec((1, 1, tn), lambda l, j: (l, 0, j))],
        out_specs=pl.BlockSpec((1, rows, tn), lambda l, j: (l, 0, j)),
        out_shape=jax.ShapeDtypeStruct((depth, rows, n), F32),
        compiler_params=_cparams(("parallel", "parallel")),
        name="ada_mod",
    )(cvec, ada_w, ada_b.reshape(depth, 1, n))


def _ctx_mod_spec(layer, chunk, ctx_row):
    return pl.BlockSpec((1, 1, 1, D_MODEL), lambda b, i: (layer, ctx_row, 0, chunk))


def _lat_mod_spec(layer, chunk):
    return pl.BlockSpec((1, 1, 1, D_MODEL), lambda b, i: (layer, b, 0, chunk))


def _row_groups(n_top, tm, x_refs, ctx_refs, lat_refs):
    lat = [r[0, 0] for r in lat_refs]
    if n_top == 0:
        return [(slice(0, tm), x_refs[0][0], lat)]
    first = pl.program_id(1) == 0
    top = [jnp.where(first, c[0, 0], l) for c, l in zip(ctx_refs, lat)]
    if len(x_refs) == 1:
        x_ref = x_refs[0]
        return [(slice(0, n_top), x_ref[0, :n_top, :], top), (slice(n_top, tm), x_ref[0, n_top:, :], lat)]
    ctx_ref, subs = x_refs[0], x_refs[1:]
    groups = [(slice(0, n_top), jnp.where(first, ctx_ref[0], subs[0][0]), top)]
    return groups + [(slice(k * n_top, (k + 1) * n_top), subs[k][0], lat) for k in range(1, len(subs))]


def _x_operands(xs, tm, n_ctx):
    d = xs[0].shape[-1]
    if len(xs) == 1:
        return list(xs), [pl.BlockSpec((1, tm, d), lambda b, i: (b, i, 0))]
    ctx, lat = xs
    nsub = tm // n_ctx
    assert tm % n_ctx == 0 and lat.shape[1] % n_ctx == 0
    specs = [pl.BlockSpec((1, n_ctx, d), lambda b, i: (b, 0, 0))]
    for k in range(nsub):
        specs.append(pl.BlockSpec((1, n_ctx, d), lambda b, i, k=k: (b, jnp.maximum(i * nsub + k - 1, 0), 0)))
    return [ctx] + [lat] * nsub, specs


def _inproj_kernel(*refs, outs, n_top, n_chunk, n_x):
    x_refs, (nw_ref, shc_ref, scc_ref, shl_ref, scl_ref, w_ref) = refs[:n_x], refs[n_x:n_x + 6]
    out_refs, h_scr, g_scr = refs[n_x + 6:-2], refs[-2], refs[-1]
    tm = h_scr.shape[0]
    for rows, x, (sh, sc) in _row_groups(n_top, tm, x_refs, (shc_ref, scc_ref), (shl_ref, scl_ref)):
        h_scr[rows, :] = (_rms(x, nw_ref[...]) * (1.0 + sc) + sh).astype(BF16)
    for o_ref, (c0, c1, kind) in zip(out_refs, outs):
        if kind == "row" and (c0 % LANES or c1 % LANES):
            a0, a1 = c0 // LANES * LANES, min(-(-c1 // LANES) * LANES, w_ref.shape[1])
            val = jnp.dot(h_scr[...], w_ref[:, a0:a1].astype(BF16), preferred_element_type=F32)
            o_ref[0] = val[:, c0 - a0:c1 - a0].astype(o_ref.dtype)
            continue
        for n0 in range(c0, c1, n_chunk):
            n1 = min(n0 + n_chunk, c1)
            val = jnp.dot(h_scr[...], w_ref[:, n0:n1].astype(BF16), preferred_element_type=F32)
            if kind == "row":
                o_ref[0, :, n0 - c0:n1 - c0] = val.astype(o_ref.dtype)
            else:
                if val.shape[1] < LANES:
                    val = jnp.concatenate([val, jnp.zeros((tm, LANES - val.shape[1]), F32)], axis=1)
                g_scr[...] = val
                o_ref[0] = g_scr[...].T[:o_ref.shape[1], :].astype(o_ref.dtype)


def _inproj(xs, norm_w, mods, layer, w, outs, tm, n_ctx):
    bsz, d = xs[0].shape[0], xs[0].shape[-1]
    t = sum(a.shape[1] for a in xs)
    n = w.shape[1]
    x_args, x_specs = _x_operands(xs, tm, n_ctx)
    assert t % tm == 0 and (n_ctx == 0 or n_ctx <= tm)
    out_specs, out_shapes = [], []
    for c0, c1, kind, dtype, wd in outs:
        if kind == "row":
            out_specs.append(pl.BlockSpec((1, tm, wd), lambda b, i: (b, i, 0)))
            out_shapes.append(jax.ShapeDtypeStruct((bsz, t, wd), dtype))
        else:
            assert c1 - c0 <= LANES and c0 % LANES == 0 and tm % LANES == 0
            out_specs.append(pl.BlockSpec((1, wd, tm), lambda b, i: (b, 0, i)))
            out_shapes.append(jax.ShapeDtypeStruct((bsz, wd, t), dtype))
    ctx_spec = lambda c: _ctx_mod_spec(layer, c, bsz)
    lat_spec = lambda c: _lat_mod_spec(layer, c)
    const = lambda shape: pl.BlockSpec(shape, lambda b, i: (0, 0), pipeline_mode=pl.Buffered(1))
    return pl.pallas_call(
        functools.partial(_inproj_kernel, outs=tuple(o[:3] for o in outs), n_top=n_ctx, n_chunk=512,
                          n_x=len(x_args)),
        grid=(bsz, t // tm),
        in_specs=x_specs + [const((1, d)), ctx_spec(0), ctx_spec(1), lat_spec(0), lat_spec(1),
                            const((d, n))],
        out_specs=out_specs,
        out_shape=out_shapes,
        scratch_shapes=[pltpu.VMEM((tm, d), BF16), pltpu.VMEM((tm, LANES), F32)],
        compiler_params=_cparams(("parallel", "parallel")),
        name=f"inproj_l{layer}",
    )(*x_args, norm_w.reshape(1, d), mods, mods, mods, mods, w)


def _ffn_kernel(x_ref, nw_ref, shc_ref, scc_ref, gc_ref, shl_ref, scl_ref, gl_ref,
                w1_ref, w3_ref, w2_ref, fw_ref, o_ref, h_scr, *, final, n_top, th):
    tm, hid = h_scr.shape[0], w1_ref.shape[1]
    groups = _row_groups(n_top, tm, (x_ref,), (shc_ref, scc_ref, gc_ref), (shl_ref, scl_ref, gl_ref))
    for rows, x, (sh, sc, _) in groups:
        h_scr[rows, :] = (_rms(x, nw_ref[...]) * (1.0 + sc) + sh).astype(BF16)
    acc = None
    for c0 in range(0, hid, th):
        h = h_scr[...]
        a = jnp.dot(h, w1_ref[:, c0:c0 + th], preferred_element_type=F32)
        g = jnp.dot(h, w3_ref[:, c0:c0 + th], preferred_element_type=F32)
        t = ((a * jax.nn.sigmoid(a)) * g).astype(BF16)
        part = jnp.dot(t, w2_ref[c0:c0 + th, :], preferred_element_type=F32)
        acc = part if acc is None else acc + part
    for rows, x, (_, _, gate) in groups:
        y = x + gate * acc[rows, :]
        if final:
            y = _rms(y, fw_ref[...])
        o_ref[0, rows, :] = y


def _ffn(x, norm_w, mods, layer, w1, w3, w2, final_w, tm, th, n_ctx, final):
    bsz, tx, d = x.shape
    hid = w1.shape[2]
    assert n_ctx <= tm and tx % tm == 0 and hid % th == 0
    ctx_spec = lambda c: _ctx_mod_spec(layer, c, bsz)
    lat_spec = lambda c: _lat_mod_spec(layer, c)
    const = lambda shape: pl.BlockSpec(shape, lambda b, i: (0, 0), pipeline_mode=pl.Buffered(1))
    layer_w = lambda shape: pl.BlockSpec((None,) + shape, lambda b, i: (layer, 0, 0),
                                         pipeline_mode=pl.Buffered(1))
    return pl.pallas_call(
        functools.partial(_ffn_kernel, final=final, n_top=n_ctx, th=th),
        grid=(bsz, tx // tm),
        in_specs=[pl.BlockSpec((1, tm, d), lambda b, i: (b, i, 0)),
                  const((1, d)),
                  ctx_spec(3), ctx_spec(4), ctx_spec(5), lat_spec(3), lat_spec(4), lat_spec(5),
                  layer_w((d, hid)), layer_w((d, hid)), layer_w((hid, d)), const((1, d))],
        out_specs=pl.BlockSpec((1, tm, d), lambda b, i: (b, i, 0)),
        out_shape=jax.ShapeDtypeStruct((bsz, tx, d), F32),
        scratch_shapes=[pltpu.VMEM((tm, d), BF16)],
        compiler_params=_cparams(("parallel", "parallel")),
        name=f"ffn_l{layer}",
    )(x, norm_w.reshape(1, d), mods, mods, mods, mods, mods, mods, w1, w3, w2, final_w.reshape(1, d))


def _chunk_masks(size=CHUNK):
    row = lax.broadcasted_iota(jnp.int32, (size, size), 0)
    col = lax.broadcasted_iota(jnp.int32, (size, size), 1)
    return row, col


def _to_col(row_vec, eye):
    return jnp.sum(jnp.where(eye, jnp.broadcast_to(row_vec, eye.shape), 0.0), axis=1, keepdims=True)


def _to_row(col_vec, eye):
    return jnp.sum(jnp.where(eye, jnp.broadcast_to(col_vec, eye.shape), 0.0), axis=0, keepdims=True)


def _scan_chunk_index(i, rev, n_chunks, ctx_chunks):
    if not rev:
        return i
    return jnp.where(i < ctx_chunks, ctx_chunks - 1 - i, n_chunks + ctx_chunks - 1 - i)


def _mlstm_kernel(bias_ref, q_ref, k_ref, v_ref, g_ref, o_ref, num_scr, cl_scr, mi_scr, fc_scr, sc_scr,
                  *, chunk, n_chunks, ctx_chunks, group):
    hd = pl.program_id(1)
    row, col = _chunk_masks(chunk)
    eye = row == col
    kscale = 1.0 / math.sqrt(HEAD_DIM)
    ones_blk = jnp.ones((chunk, HEAD_DIM), BF16)

    def gate_terms(r, ig_raw, f_raw):
        incl = (col >= row) if r == 1 else (col <= row)
        ig_row = ig_raw + bias_ref[r * MLSTM_HEADS + hd]
        lf_row = jax.nn.log_sigmoid(f_raw + bias_ref[(2 + r) * MLSTM_HEADS + hd])
        f_col = jnp.sum(jnp.where(incl, jnp.broadcast_to(lf_row, incl.shape), 0.0),
                        axis=1, keepdims=True)
        f_row = _to_row(f_col, eye)
        f_last = jnp.sum(lf_row, axis=1, keepdims=True)
        dm = jnp.where(incl, f_col - f_row + ig_row, -jnp.inf)
        m_intra = jnp.max(dm, axis=1, keepdims=True)
        w_col = f_last - f_col + _to_col(ig_row, eye)
        m_loc = jnp.max(w_col, axis=0, keepdims=True)
        return f_col, f_last, jnp.exp(dm - m_intra), m_intra, jnp.exp(w_col - m_loc), m_loc

    def local_group(gi, _):
        chunks = [gi * group + g for g in range(group)]
        loaded = []
        for c in chunks:
            rows = pl.ds(pl.multiple_of(c * chunk, chunk), chunk)
            gates = [(g_ref[0, r * MLSTM_HEADS + hd, pl.ds(c, 1), :],
                      g_ref[0, (2 + r) * MLSTM_HEADS + hd, pl.ds(c, 1), :]) for r in (0, 1)]
            v1 = jnp.concatenate([v_ref[0, rows, :].astype(BF16), ones_blk], axis=-1)
            loaded.append((q_ref[0, rows, :], k_ref[0, rows, :].astype(F32) * kscale, v1, gates))
        chains = [(g, r) for g in range(group) for r in (0, 1)]
        terms = [gate_terms(r, *loaded[g][3][r]) for g, r in chains]
        qks = [_dot_nt(q, k) for q, k, _, _ in loaded]
        c_locs = [_dot_tn(tm[4] * loaded[g][1], loaded[g][2]) for (g, r), tm in zip(chains, terms)]
        nums = [_dot(tm[2] * qks[g], loaded[g][2]) for (g, r), tm in zip(chains, terms)]
        for (g, r), tm, c_loc, num in zip(chains, terms, c_locs, nums):
            f_col, f_last, _, m_intra, _, m_loc = tm
            c = chunks[g]
            num_scr[r, c] = num
            cl_scr[r, c] = c_loc
            mi_scr[r, c] = jnp.broadcast_to(m_intra, (chunk, HEAD_DIM))
            fc_scr[r, c] = jnp.broadcast_to(f_col, (chunk, HEAD_DIM))
            sc_scr[r, c, :8, :] = jnp.broadcast_to(f_last, (8, HEAD_DIM))
            sc_scr[r, c, 8:, :] = jnp.broadcast_to(m_loc, (8, HEAD_DIM))
        return 0

    lax.fori_loop(0, n_chunks // group, local_group, 0)
    o_ref[...] = jnp.zeros_like(o_ref)

    def step(i, carry):
        cs = [_scan_chunk_index(i, r == 1, n_chunks, ctx_chunks) for r in (0, 1)]
        rows = [pl.ds(pl.multiple_of(c * chunk, chunk), chunk) for c in cs]
        qcs = [_dot(q_ref[0, rw, :], st[0]) for rw, st in zip(rows, carry)]
        new = []
        for r in (0, 1):
            s_st, m_st = carry[r]
            c = cs[r]
            mi, na = mi_scr[r, c], num_scr[r, c]
            inter = fc_scr[r, c] + m_st[0:1, :]
            m_t = jnp.maximum(mi, inter)
            a_loc_t, a_inter = jnp.exp(mi - m_t), jnp.exp(inter - m_t)
            num = a_loc_t * na[:, :HEAD_DIM] + a_inter * qcs[r][:, :HEAD_DIM]
            den = a_loc_t * na[:, HEAD_DIM:] + a_inter * qcs[r][:, HEAD_DIM:]
            out = num / jnp.maximum(jnp.abs(den), jnp.exp(-m_t))
            o_ref[0, rows[r], :] = (o_ref[0, rows[r], :].astype(F32) + out).astype(o_ref.dtype)
            f_last, m_loc = sc_scr[r, c, :8, :], sc_scr[r, c, 8:, :]
            m_new = jnp.maximum(f_last + m_st, m_loc)
            a_prev = jnp.exp(f_last + m_st - m_new)[0:1, :]
            a_loc = jnp.exp(m_loc - m_new)[0:1, :]
            a_prev = jnp.concatenate([a_prev, a_prev], axis=1)
            a_loc = jnp.concatenate([a_loc, a_loc], axis=1)
            new.append((a_prev * s_st + a_loc * cl_scr[r, c], m_new))
        return tuple(new)

    init = (jnp.zeros((HEAD_DIM, 2 * HEAD_DIM), F32), jnp.zeros((8, HEAD_DIM), F32))
    lax.fori_loop(0, n_chunks, step, (init, init), unroll=2)


def _mlstm(qkv, gates, bias, n_ctx, chunk):
    bsz, t, _ = qkv.shape
    assert t % chunk == 0 and n_ctx % chunk == 0
    n_chunks = t // chunk
    gates_row = _chunked(gates, chunk)
    qkv_spec = lambda off: pl.BlockSpec((1, t, HEAD_DIM), lambda b, h: (b, 0, off + h))
    return pl.pallas_call(
        functools.partial(_mlstm_kernel, chunk=chunk, n_chunks=n_chunks, ctx_chunks=n_ctx // chunk,
                          group=math.gcd(9 if chunk <= CHUNK else 3, n_chunks)),
        grid=(bsz, MLSTM_HEADS),
        in_specs=[pl.BlockSpec(memory_space=pltpu.SMEM),
                  qkv_spec(0), qkv_spec(MLSTM_HEADS), qkv_spec(2 * MLSTM_HEADS),
                  pl.BlockSpec((1, 4 * MLSTM_HEADS, n_chunks, chunk), lambda b, h: (b, 0, 0, 0))],
        out_specs=pl.BlockSpec((1, t, HEAD_DIM), lambda b, h: (b, 0, h)),
        out_shape=jax.ShapeDtypeStruct((bsz, t, MLSTM_WIDTH), BF16),
        scratch_shapes=[pltpu.VMEM((2, n_chunks, chunk, 2 * HEAD_DIM), F32),
                        pltpu.VMEM((2, n_chunks, HEAD_DIM, 2 * HEAD_DIM), F32),
                        pltpu.VMEM((2, n_chunks, chunk, HEAD_DIM), F32),
                        pltpu.VMEM((2, n_chunks, chunk, HEAD_DIM), F32),
                        pltpu.VMEM((2, n_chunks, 16, HEAD_DIM), F32)],
        compiler_params=_cparams(("parallel", "parallel")),
        name="mlstm",
    )(bias, qkv, qkv, qkv, gates_row)


def _s5_prep_kernel(lr_ref, li_ref, ldt_ref, br_ref, bi_ref, cr_ref, ci_ref, bd_ref, cdt_ref, a_ref):
    lr, li = lr_ref[0, 0], li_ref[0, 0]
    dt = jnp.exp(ldt_ref[0, 0])
    mag, ang = jnp.exp(lr * dt), li * dt
    ab_re, ab_im = mag * jnp.cos(ang), mag * jnp.sin(ang)
    nr, ni = ab_re - 1.0, ab_im
    den = lr * lr + li * li
    co_re = (nr * lr + ni * li) / den
    co_im = (ni * lr - nr * li) / den
    b_re, b_im = br_ref[0, 0], bi_ref[0, 0]
    bb_re = co_re * b_re - co_im * b_im
    bb_im = co_re * b_im + co_im * b_re
    c_re, c_im = cr_ref[0, 0], ci_ref[0, 0]
    lane_group = lax.broadcasted_iota(jnp.int32, (S5_GROUP, S5_HALF_N), 1) // S5_STATE
    n = S5_HALF_N
    for g in range(S5_GROUPS // S5_HALVES):
        sel = lane_group == g
        rows = slice(g * S5_GROUP, (g + 1) * S5_GROUP)
        bd_ref[0, 0, rows, :n] = jnp.where(sel, bb_re, 0.0).astype(bd_ref.dtype)
        bd_ref[0, 0, rows, n:] = jnp.where(sel, bb_im, 0.0).astype(bd_ref.dtype)
        cdt_ref[0, 0, rows, :n] = jnp.where(sel, c_re, 0.0).astype(cdt_ref.dtype)
        cdt_ref[0, 0, rows, n:] = jnp.where(sel, -c_im, 0.0).astype(cdt_ref.dtype)
    a_ref[0, 0, :, :n] = jnp.broadcast_to(ab_re, (8, n))
    a_ref[0, 0, :, n:] = jnp.broadcast_to(ab_im, (8, n))


def _s5_prep(lam_re, lam_im, log_dt, b_re, b_im, c_re, c_im):
    gh = S5_GROUPS // S5_HALVES
    vec = lambda a: a.reshape(2, S5_HALVES, 1, S5_HALF_N)
    ldt = vec(jnp.broadcast_to(log_dt[:, :, None], (2, S5_GROUPS, S5_STATE)))
    bt = lambda a: a.reshape(2, S5_HALVES, gh, S5_STATE, S5_GROUP).transpose(0, 1, 4, 2, 3).reshape(
        2, S5_HALVES, S5_GROUP, S5_HALF_N)
    ct = lambda a: a.reshape(2, S5_HALVES, gh, S5_GROUP, S5_STATE).transpose(0, 1, 3, 2, 4).reshape(
        2, S5_HALVES, S5_GROUP, S5_HALF_N)
    vspec = pl.BlockSpec((1, 1, 1, S5_HALF_N), lambda r, h: (r, h, 0, 0))
    mspec = pl.BlockSpec((1, 1, S5_GROUP, S5_HALF_N), lambda r, h: (r, h, 0, 0))
    ospec = pl.BlockSpec((1, 1, S5_HALF_W, 2 * S5_HALF_N), lambda r, h: (r, h, 0, 0))
    return pl.pallas_call(
        _s5_prep_kernel,
        grid=(2, S5_HALVES),
        in_specs=[vspec, vspec, vspec, mspec, mspec, mspec, mspec],
        out_specs=[ospec, ospec, pl.BlockSpec((1, 1, 8, 2 * S5_HALF_N), lambda r, h: (r, h, 0, 0))],
        out_shape=[jax.ShapeDtypeStruct((2, S5_HALVES, S5_HALF_W, 2 * S5_HALF_N), BF16),
                   jax.ShapeDtypeStruct((2, S5_HALVES, S5_HALF_W, 2 * S5_HALF_N), BF16),
                   jax.ShapeDtypeStruct((2, S5_HALVES, 8, 2 * S5_HALF_N), F32)],
        compiler_params=_cparams(("parallel", "parallel")),
        name="s5_prep",
    )(vec(lam_re), vec(lam_im), ldt, bt(b_re), bt(b_im), ct(c_re), ct(c_im))


def _s5_kernel(uf0_ref, uf1_ref, ub0_ref, ub1_ref, bd_ref, cdt_ref, a_ref, yf_ref, yb_ref,
               lhs_scr, bu_scr, s_scr, st_scr, *, tb, bsz):
    n = S5_HALF_N
    rows = tb * bsz
    u_refs = ((uf0_ref, uf1_ref), (ub0_ref, ub1_ref))
    y_refs = (yf_ref, yb_ref)

    @pl.when(pl.program_id(1) == 0)
    def _():
        st_scr[...] = jnp.zeros_like(st_scr)

    for d in (0, 1):
        for j in (0, 1):
            lhs_scr[d, :, j * LANES:(j + 1) * LANES] = jnp.transpose(
                u_refs[d][j][...].astype(F32), (1, 0, 2)).reshape(rows, LANES)
    for d in (0, 1):
        bu_scr[d] = _dot(lhs_scr[d], bd_ref[d, 0])
    for d in (0, 1):
        a_re, a_im = a_ref[d, 0, :, :n], a_ref[d, 0, :, n:]
        s_re, s_im = st_scr[d, :, :n], st_scr[d, :, n:]
        for j in range(tb):
            t = j if d == 0 else tb - 1 - j
            sl = slice(t * bsz, (t + 1) * bsz)
            s_re, s_im = (a_re * s_re - a_im * s_im + bu_scr[d, sl, :n],
                          a_re * s_im + a_im * s_re + bu_scr[d, sl, n:])
            s_scr[d, sl, :n] = s_re
            s_scr[d, sl, n:] = s_im
        st_scr[d, :, :n] = s_re
        st_scr[d, :, n:] = s_im
        y = _dot_nt(s_scr[d], cdt_ref[d, 0])
        for j in (0, 1):
            y_refs[d][:, :, j * LANES:(j + 1) * LANES] = jnp.transpose(
                y[:, j * LANES:(j + 1) * LANES].reshape(tb, bsz, LANES), (1, 0, 2)).astype(y_refs[d].dtype)


def _s5(u, bd, cdt, avec, n_ctx, tb):
    bsz, t, _ = u.shape
    nb, ncb = t // tb, n_ctx // tb
    assert bsz == 8 and S5_HALF_W == 2 * LANES

    def rev(i):
        return jnp.where(i < ncb, ncb - 1 - i, nb + ncb - 1 - i)

    fwd = lambda i: i
    uspec = lambda order, j: pl.BlockSpec((bsz, tb, LANES), lambda h, i: (0, order(i), 2 * h + j))
    yspec = lambda order: pl.BlockSpec((bsz, tb, S5_HALF_W), lambda h, i: (0, order(i), h))
    wspec = pl.BlockSpec((2, 1, S5_HALF_W, 2 * S5_HALF_N), lambda h, i: (0, h, 0, 0))
    rows = tb * bsz
    return pl.pallas_call(
        functools.partial(_s5_kernel, tb=tb, bsz=bsz),
        grid=(S5_HALVES, nb),
        in_specs=[uspec(fwd, 0), uspec(fwd, 1), uspec(rev, 0), uspec(rev, 1), wspec, wspec,
                  pl.BlockSpec((2, 1, 8, 2 * S5_HALF_N), lambda h, i: (0, h, 0, 0))],
        out_specs=[yspec(fwd), yspec(rev)],
        out_shape=[jax.ShapeDtypeStruct((bsz, t, S5_WIDTH), BF16)] * 2,
        scratch_shapes=[pltpu.VMEM((2, rows, S5_HALF_W), F32),
                        pltpu.VMEM((2, rows, 2 * S5_HALF_N), F32),
                        pltpu.VMEM((2, rows, 2 * S5_HALF_N), F32),
                        pltpu.VMEM((2, bsz, 2 * S5_HALF_N), F32)],
        compiler_params=_cparams(("parallel", "arbitrary")),
        name="s5_scan",
    )(u, u, u, u, bd, cdt, avec)


def _even_finish_kernel(*refs, n_top, n_x):
    x_refs = refs[:n_x]
    ha_ref, op_ref, yf_ref, yb_ref, u_ref, mhw_ref, ds_ref, wg_ref, wo_ref, gc_ref, gl_ref, o_ref = refs[n_x:]
    ha = ha_ref[0].astype(F32)
    parts = []
    for h in range(MLSTM_HEADS):
        sl = slice(h * HEAD_DIM, (h + 1) * HEAD_DIM)
        parts.append(_rms(ha[:, sl], mhw_ref[:, sl]))
    a_out = jnp.concatenate(parts, axis=-1) * jax.nn.sigmoid(op_ref[0].astype(F32))
    yb = jax.nn.gelu(yf_ref[0].astype(F32) + yb_ref[0].astype(F32) + ds_ref[...] * u_ref[0].astype(F32))
    glu = _dot(yb, wg_ref[...])
    b_out = glu[:, :S5_WIDTH] * jax.nn.sigmoid(glu[:, S5_WIDTH:])
    y = _dot(a_out, wo_ref[:MLSTM_WIDTH, :]) + _dot(b_out, wo_ref[MLSTM_WIDTH:, :])
    for rows, x, (gate,) in _row_groups(n_top, y.shape[0], x_refs, (gc_ref,), (gl_ref,)):
        o_ref[0, rows, :] = x + gate * y[rows, :]


def _even_finish(xs, ha, o_pre, yf, yb, u, mods, layer, mh_w, d_skip, wg, wo, tm, n_ctx):
    bsz, t, _ = ha.shape
    d = xs[0].shape[-1]
    assert t % tm == 0 and n_ctx <= tm
    x_args, x_specs = _x_operands(xs, tm, n_ctx)
    full = lambda shape: pl.BlockSpec(shape, lambda b, i: tuple(0 for _ in shape),
                                      pipeline_mode=pl.Buffered(1))
    return pl.pallas_call(
        functools.partial(_even_finish_kernel, n_top=n_ctx, n_x=len(x_args)),
        grid=(bsz, t // tm),
        in_specs=x_specs + [
                  pl.BlockSpec((1, tm, MLSTM_WIDTH), lambda b, i: (b, i, 0)),
                  pl.BlockSpec((1, tm, MLSTM_WIDTH), lambda b, i: (b, i, 0)),
                  pl.BlockSpec((1, tm, S5_WIDTH), lambda b, i: (b, i, 0)),
                  pl.BlockSpec((1, tm, S5_WIDTH), lambda b, i: (b, i, 0)),
                  pl.BlockSpec((1, tm, S5_WIDTH), lambda b, i: (b, i, 0)),
                  full((1, MLSTM_WIDTH)), full((1, S5_WIDTH)),
                  full((S5_WIDTH, 2 * S5_WIDTH)), full((d, d)),
                  _ctx_mod_spec(layer, 2, bsz), _lat_mod_spec(layer, 2)],
        out_specs=pl.BlockSpec((1, tm, d), lambda b, i: (b, i, 0)),
        out_shape=jax.ShapeDtypeStruct((bsz, t, d), F32),
        compiler_params=_cparams(("parallel", "parallel")),
        name="even_finish",
    )(*x_args, ha, o_pre, yf, yb, u, mh_w.reshape(1, -1), d_skip.reshape(1, -1), wg, wo, mods, mods)


def _conv_chunk(x_ref, w, lanes, c, static, n_chunks, ctx_chunks):
    tpos = lax.broadcasted_iota(jnp.int32, (CHUNK, HEAD_DIM), 0)
    first, last = tpos == 0, tpos == CHUNK - 1
    wrow = lambda k: w[k:k + 1, :]

    def rows(ci, dtype=F32):
        start = ci * CHUNK if isinstance(ci, int) else pl.multiple_of(ci * CHUNK, CHUNK)
        return x_ref[0, pl.ds(start, CHUNK), lanes].astype(dtype)

    if static and c < ctx_chunks:
        cur = rows(c)
        zero = jnp.zeros((1, HEAD_DIM), F32)
        prev_tok = rows(c - 1)[CHUNK - 1:CHUNK, :] if c > 0 else zero
        next_tok = rows(c + 1)[0:1, :] if c < ctx_chunks - 1 else zero
        left = jnp.where(first, prev_tok, pltpu.roll(cur, 1, axis=0))
        right = jnp.where(last, next_tok, pltpu.roll(cur, CHUNK - 1, axis=0))
        return wrow(3) * left + wrow(4) * cur + wrow(5) * right

    if static:
        up_i, dn_i = max(c - 1, ctx_chunks), min(c + 1, n_chunks - 1)
        w_up = w[0:3, :] if c > ctx_chunks else jnp.zeros((3, HEAD_DIM), F32)
        w_dn = w[6:9, :] if c < n_chunks - 1 else jnp.zeros((3, HEAD_DIM), F32)
    else:
        up_i, dn_i = jnp.maximum(c - 1, ctx_chunks), jnp.minimum(c + 1, n_chunks - 1)
        w_up = jnp.where(c > ctx_chunks, w[0:3, :], 0.0)
        w_dn = jnp.where(c < n_chunks - 1, w[6:9, :], 0.0)
    up, cur, dn = rows(up_i, BF16), rows(c, BF16), rows(dn_i, BF16)
    w_up, w_mid, w_dn = w_up.astype(BF16), w[3:6, :].astype(BF16), w_dn.astype(BF16)
    r0, r1, r2 = ((w_up[dc:dc + 1, :] * up + w_mid[dc:dc + 1, :] * cur + w_dn[dc:dc + 1, :] * dn).astype(F32)
                  for dc in range(3))
    prev = jnp.where(first, 0.0, pltpu.roll(r0, 1, axis=0))
    nxt = jnp.where(last, 0.0, pltpu.roll(r2, CHUNK - 1, axis=0))
    return r1 + prev + nxt


def _unit_tri_inverses(mats, eye_f):
    ps = [eye_f - a for a in mats]
    pws = [_dot(a, a) for a in mats]
    k = 2
    while 2 * k < CHUNK:
        res = [_dot(jnp.concatenate([p, pw], axis=0), pw) for p, pw in zip(ps, pws)]
        ps = [p + r[:CHUNK] for p, r in zip(ps, res)]
        pws = [r[CHUNK:] for r in res]
        k *= 2
    return [p + _dot(p, pw) for p, pw in zip(ps, pws)]


def _gdn_kernel(par_ref, q_ref, k_ref, v_ref, wq_ref, wk_ref, wv_ref, g_ref, o_ref,
                qk_scr, ol_scr, nl_scr, gl_scr, *, n_chunks, ctx_chunks, group, heads):
    row, col = _chunk_masks()
    eye = row == col
    eye_f = eye.astype(F32)

    def gate_terms(hd, r, a_pre, b_pre):
        incl = (col >= row) if r == 1 else (col <= row)
        a_log = jnp.full((1, CHUNK), par_ref[r * GDN_HEADS + hd], F32)
        dt_b = par_ref[(2 + r) * GDN_HEADS + hd]
        g_row = -jnp.exp(a_log) * jax.nn.softplus(a_pre + dt_b)
        beta_col = _to_col(jax.nn.sigmoid(b_pre), eye)
        gc_col = jnp.sum(jnp.where(incl, jnp.broadcast_to(g_row, incl.shape), 0.0),
                         axis=1, keepdims=True)
        gc_row = _to_row(gc_col, eye)
        g_last = jnp.sum(g_row, axis=1, keepdims=True)
        gamma = jnp.exp(jnp.where(incl, gc_col - gc_row, -jnp.inf))
        return beta_col, gc_col, g_last, gamma

    for hh in range(heads):
        hd = pl.program_id(1) * heads + hh
        lanes = slice(hh * HEAD_DIM, (hh + 1) * HEAD_DIM)

        def load(c, static, hd=hd, lanes=lanes):
            gates = [(g_ref[0, r * GDN_HEADS + hd, pl.ds(c, 1), :],
                      g_ref[0, (2 + r) * GDN_HEADS + hd, pl.ds(c, 1), :]) for r in (0, 1)]
            q, k, v = (_conv_chunk(x_ref, w_ref[:, lanes], lanes, c, static, n_chunks, ctx_chunks)
                       for x_ref, w_ref in ((q_ref, wq_ref), (k_ref, wk_ref), (v_ref, wv_ref)))
            q, k, v = q * jax.nn.sigmoid(q), k * jax.nn.sigmoid(k), v * jax.nn.sigmoid(v)
            q = q * (lax.rsqrt(jnp.sum(q * q, axis=-1, keepdims=True) + EPS) * (1.0 / math.sqrt(HEAD_DIM)))
            k = k * lax.rsqrt(jnp.sum(k * k, axis=-1, keepdims=True) + EPS)
            return q, k, v, gates

        def local_group(gi, _, hh=hh, hd=hd, load=load):
            static = isinstance(gi, int)
            chunks = [gi * group + g for g in range(group)]
            loaded = [load(c, static) for c in chunks]
            kqs = [_dot_nt(jnp.concatenate([k, q], axis=0), k) for q, k, _, _ in loaded]
            chains = [(g, r) for g in range(group) for r in (0, 1)]
            terms = [gate_terms(hd, r, *loaded[g][3][r]) for g, r in chains]
            a_mats = []
            for (g, r), (beta_col, _, _, gamma) in zip(chains, terms):
                strict = (col > row) if r == 1 else (col < row)
                a_mats.append(jnp.where(strict, beta_col * kqs[g][:CHUNK] * gamma, 0.0))
            t_invs = _unit_tri_inverses(a_mats, eye_f)
            sols = []
            for (g, r), (beta_col, gc_col, _, _), t_inv in zip(chains, terms, t_invs):
                _, k, v, _ = loaded[g]
                rhs = jnp.concatenate([v * beta_col, k * (beta_col * jnp.exp(gc_col))], axis=-1)
                sols.append(_dot(t_inv, rhs))
            auws = [_dot(kqs[g][CHUNK:] * tm[3], sol) for (g, r), tm, sol in zip(chains, terms, sols)]
            kuws = [_dot_tn(loaded[g][1] * jnp.exp(tm[2] - tm[1]), sol)
                    for (g, r), tm, sol in zip(chains, terms, sols)]
            for (g, r), (_, gc_col, g_last, _), auw, kuw in zip(chains, terms, auws, kuws):
                c = chunks[g]
                q = loaded[g][0]
                qk_scr[hh, r, c, :CHUNK, :] = (q * jnp.exp(gc_col) - auw[:, HEAD_DIM:]).astype(qk_scr.dtype)
                qk_scr[hh, r, c, CHUNK:, :] = kuw[:, HEAD_DIM:].astype(qk_scr.dtype)
                ol_scr[hh, r, c] = auw[:, :HEAD_DIM].astype(ol_scr.dtype)
                nl_scr[hh, r, c] = kuw[:, :HEAD_DIM].astype(nl_scr.dtype)
                gl_scr[hh, r, c] = jnp.broadcast_to(jnp.exp(g_last), (8, HEAD_DIM))
            return 0

        n_static = -(-ctx_chunks // group)
        for gi in range(n_static):
            local_group(gi, 0)
        lax.fori_loop(n_static, n_chunks // group, local_group, 0)

    o_ref[...] = jnp.zeros_like(o_ref)
    streams = [(hh, r) for hh in range(heads) for r in (0, 1)]

    def step(i, carry):
        cs = [_scan_chunk_index(i, r == 1, n_chunks, ctx_chunks) for _, r in streams]
        res = [jnp.dot(qk_scr[hh, r, c], s.astype(BF16), preferred_element_type=F32)
               for (hh, r), c, s in zip(streams, cs, carry)]
        new = []
        for (hh, r), c, s, rs in zip(streams, cs, carry, res):
            rows = pl.ds(pl.multiple_of(c * CHUNK, CHUNK), CHUNK)
            lanes = slice(hh * HEAD_DIM, (hh + 1) * HEAD_DIM)
            o_ref[0, rows, lanes] = (o_ref[0, rows, lanes].astype(F32) + ol_scr[hh, r, c]
                                     + rs[:CHUNK]).astype(o_ref.dtype)
            new.append(gl_scr[hh, r, c][0:1, :] * s - rs[CHUNK:] + nl_scr[hh, r, c])
        return tuple(new)

    init = jnp.zeros((HEAD_DIM, HEAD_DIM), F32)
    lax.fori_loop(0, n_chunks, step, (init,) * len(streams), unroll=2)


def _gdn(qkv, conv_w, gates_row, par, n_ctx, heads=4):
    bsz, t, _ = qkv.shape
    assert GRID_W == CHUNK and n_ctx % CHUNK == 0
    n_chunks = t // CHUNK
    group = math.gcd(9, n_chunks)
    wd = heads * HEAD_DIM
    blocks = GDN_HEADS // heads
    spec = lambda off: pl.BlockSpec((1, t, wd), lambda b, h: (b, 0, off + h), pipeline_mode=pl.Buffered(1))
    wspec = lambda off: pl.BlockSpec((9, wd), lambda b, h: (0, off + h))
    conv_w9 = conv_w.reshape(9, GDN_QKV)
    return pl.pallas_call(
        functools.partial(_gdn_kernel, n_chunks=n_chunks, ctx_chunks=n_ctx // CHUNK, group=group,
                          heads=heads),
        grid=(bsz, blocks),
        in_specs=[pl.BlockSpec(memory_space=pltpu.SMEM),
                  spec(0), spec(blocks), spec(2 * blocks),
                  wspec(0), wspec(blocks), wspec(2 * blocks),
                  pl.BlockSpec((1, 4 * GDN_HEADS, n_chunks, CHUNK), lambda b, h: (b, 0, 0, 0))],
        out_specs=pl.BlockSpec((1, t, wd), lambda b, h: (b, 0, h)),
        out_shape=jax.ShapeDtypeStruct((bsz, t, D_MODEL), BF16),
        scratch_shapes=[pltpu.VMEM((heads, 2, n_chunks, CHUNK + HEAD_DIM, HEAD_DIM), BF16),
                        pltpu.VMEM((heads, 2, n_chunks, CHUNK, HEAD_DIM), BF16),
                        pltpu.VMEM((heads, 2, n_chunks, HEAD_DIM, HEAD_DIM), BF16),
                        pltpu.VMEM((heads, 2, n_chunks, 8, HEAD_DIM), F32)],
        compiler_params=_cparams(("parallel", "parallel")),
        name="gdn",
    )(par, qkv, qkv, qkv, conv_w9, conv_w9, conv_w9, gates_row)


def _odd_finish_kernel(x_ref, o_in_ref, z_ref, hw_ref, wo_ref, g_ref, o_ref):
    o = o_in_ref[0].astype(F32)
    parts = []
    for h in range(GDN_HEADS):
        sl = slice(h * HEAD_DIM, (h + 1) * HEAD_DIM)
        parts.append(_rms(o[:, sl], hw_ref[:, sl]))
    z = z_ref[0].astype(F32)
    y = jnp.concatenate(parts, axis=-1) * (z * jax.nn.sigmoid(z))
    o_ref[0] = x_ref[0] + g_ref[0, 0] * _dot(y, wo_ref[...])


def _odd_finish(xc, o, z, mods, layer, hw, wo, tm, n_ctx):
    bsz, t, d = xc.shape
    assert n_ctx % tm == 0 and (t - n_ctx) % tm == 0
    t0 = n_ctx // tm
    return pl.pallas_call(
        _odd_finish_kernel,
        grid=(bsz, (t - n_ctx) // tm),
        in_specs=[pl.BlockSpec((1, tm, d), lambda b, i: (b, i + t0, 0)),
                  pl.BlockSpec((1, tm, d), lambda b, i: (b, i + t0, 0)),
                  pl.BlockSpec((1, tm, d), lambda b, i: (b, i + t0, 0)),
                  pl.BlockSpec((1, d), lambda b, i: (0, 0)),
                  pl.BlockSpec((d, d), lambda b, i: (0, 0)),
                  _lat_mod_spec(layer, 2)],
        out_specs=pl.BlockSpec((1, tm, d), lambda b, i: (b, i, 0)),
        out_shape=jax.ShapeDtypeStruct((bsz, t - n_ctx, d), F32),
        compiler_params=_cparams(("parallel", "parallel")),
        name="odd_finish",
    )(xc, o, z, hw.reshape(1, d), wo, mods)


def _chunked(gates_t, chunk=CHUNK):
    bsz, g, t = gates_t.shape
    return gates_t.reshape(bsz, g, t // chunk, chunk)


def _pick_tile(total, target, at_least, multiple=8):
    for tile in range(min(target, total), multiple - 1, -1):
        if total % tile == 0 and tile % multiple == 0 and tile >= at_least:
            return tile
    raise ValueError(f"no row tile for {total} rows")


def kernel(x, c, ctx, c_ctx, ada_w, ada_b, norm1_w, norm2_w, ffn_w1, ffn_w3, ffn_w2, final_norm_w,
           ev_w_in, ev_i_bias, ev_f_bias, ev_head_norm_w, ev_lam_re, ev_lam_im, ev_log_dt,
           ev_b_re, ev_b_im, ev_c_re, ev_c_im, ev_d, ev_w_glu, ev_w_out,
           od_w_in, od_conv_w, od_a_log, od_dt_bias, od_head_norm_w, od_w_out):
    bsz, seq, d = x.shape
    n_ctx = ctx.shape[1]
    assert d == D_MODEL and seq % GRID_W == 0 and n_ctx % CHUNK == 0 and bsz % 8 == 0
    assert ada_w.shape[0] == 2
    t_all = n_ctx + seq
    tm = _pick_tile(t_all, 768, n_ctx, LANES)
    tm_lat = _pick_tile(math.gcd(seq, n_ctx), 512, 0)
    th = FFN_HIDDEN // 2
    tb = 2 * CHUNK if n_ctx % (2 * CHUNK) == 0 and seq % (2 * CHUNK) == 0 else CHUNK

    if tm % n_ctx == 0 and seq % n_ctx == 0:
        xs = (ctx, x)
    else:
        xs = (jnp.concatenate([ctx, x], axis=1),)
    mod_rows = 16
    cvec = jnp.zeros((mod_rows, d), F32).at[:bsz].set(c).at[bsz].set(c_ctx)
    mods = _ada(cvec, ada_w, ada_b).reshape(2, mod_rows, 1, 6 * d)
    w1, w3, w2 = ffn_w1.astype(BF16), ffn_w3.astype(BF16), ffn_w2.astype(BF16)

    qkv_w, n_g = 3 * MLSTM_WIDTH, 4 * MLSTM_HEADS
    w_in = ev_w_in[0]
    g0 = qkv_w + MLSTM_WIDTH
    n_even = w_in.shape[1]
    qkv, o_pre, u, gates = _inproj(
        xs, norm1_w[0], mods, 0, w_in,
        [(0, qkv_w, "row", BF16, qkv_w), (qkv_w, g0, "row", BF16, g0 - qkv_w),
         (g0 + n_g, n_even, "row", BF16, S5_WIDTH), (g0, g0 + n_g, "gate", F32, n_g)], tm, n_ctx)
    bias = jnp.concatenate([ev_i_bias[0].reshape(-1), ev_f_bias[0].reshape(-1)]).astype(F32)
    mlstm_chunk = MLSTM_CHUNK if n_ctx % MLSTM_CHUNK == 0 and seq % MLSTM_CHUNK == 0 else CHUNK
    ha = _mlstm(qkv, gates, bias, n_ctx, mlstm_chunk)
    bd, cdt, avec = _s5_prep(ev_lam_re[0], ev_lam_im[0], ev_log_dt[0], ev_b_re[0], ev_b_im[0],
                             ev_c_re[0], ev_c_im[0])
    yf, yb = _s5(u, bd, cdt, avec, n_ctx, tb)
    xc = _even_finish(xs, ha, o_pre, yf, yb, u, mods, 0, ev_head_norm_w[0], ev_d[0],
                      ev_w_glu[0].astype(BF16), ev_w_out[0].astype(BF16), tm, n_ctx)
    xc = _ffn(xc, norm2_w[0], mods, 0, w1, w3, w2, final_norm_w, tm, th, n_ctx, False)

    n_g = 4 * GDN_HEADS
    qkv_pre, z, gates = _inproj(
        (xc,), norm1_w[1], mods, 1, od_w_in[0],
        [(0, GDN_QKV, "row", BF16, GDN_QKV), (GDN_QKV, 4 * d, "row", BF16, d),
         (4 * d, 4 * d + n_g, "gate", F32, n_g)], tm, n_ctx)
    par = jnp.concatenate([od_a_log[0].reshape(-1), od_dt_bias[0].reshape(-1)]).astype(F32)
    o = _gdn(qkv_pre, od_conv_w[0], _chunked(gates), par, n_ctx)
    xl = _odd_finish(xc, o, z, mods, 1, od_head_norm_w[0], od_w_out[0].astype(BF16), tm_lat, n_ctx)
    return _ffn(xl, norm2_w[1], mods, 1, w1, w3, w2, final_norm_w, _pick_tile(seq, 512, 0),
                th, 0, True)
```

```python
import functools
import math

import jax
import jax.numpy as jnp
from jax import lax
from jax.experimental import pallas as pl
from jax.experimental.pallas import tpu as pltpu

F32 = jnp.float32
BF16 = jnp.bfloat16

D_MODEL = 1024
CHUNK = 64
MLSTM_CHUNK = 128
GRID_W = 64
EPS = 1e-6
LANES = 128
HEAD_DIM = 128
MLSTM_WIDTH = D_MODEL // 2
MLSTM_HEADS = MLSTM_WIDTH // HEAD_DIM
S5_WIDTH = D_MODEL - MLSTM_WIDTH
S5_GROUP = 16
S5_GROUPS = S5_WIDTH // S5_GROUP
S5_STATE = 64
S5_HALVES = 2
S5_HALF_W = S5_WIDTH // S5_HALVES
S5_HALF_N = (S5_GROUPS // S5_HALVES) * S5_STATE
GDN_HEADS = D_MODEL // HEAD_DIM
GDN_QKV = 3 * D_MODEL
FFN_HIDDEN = ((8 * D_MODEL + 3 * 256 - 1) // (3 * 256)) * 256
VMEM_LIMIT = 56 * 1024 * 1024

_NT = (((1,), (1,)), ((), ()))
_TN = (((0,), (0,)), ((), ()))


def _cparams(sem):
    return pltpu.CompilerParams(dimension_semantics=sem, vmem_limit_bytes=VMEM_LIMIT)


def _dot(a, b):
    return jnp.dot(a.astype(BF16), b.astype(BF16), preferred_element_type=F32)


def _dot_nt(a, b):
    return lax.dot_general(a.astype(BF16), b.astype(BF16), _NT, preferred_element_type=F32)


def _dot_tn(a, b):
    return lax.dot_general(a.astype(BF16), b.astype(BF16), _TN, preferred_element_type=F32)


def _rms(x, w):
    return x * lax.rsqrt(jnp.mean(x * x, axis=-1, keepdims=True) + EPS) * w


def _ada_kernel(s_ref, w_ref, b_ref, o_ref):
    s = s_ref[...]
    s = s * jax.nn.sigmoid(s)
    o_ref[0] = _dot(s, w_ref[0]) + b_ref[0]


def _ada(cvec, ada_w, ada_b):
    depth, d, n = ada_w.shape
    tn = 1536
    rows = cvec.shape[0]
    return pl.pallas_call(
        _ada_kernel,
        grid=(depth, n // tn),
        in_specs=[pl.BlockSpec((rows, d), lambda l, j: (0, 0)),
                  pl.BlockSpec((1, d, tn), lambda l, j: (l, 0, j)),
                  pl.BlockSpec((1, 1, tn), lambda l, j: (l, 0, j))],
        out_specs=pl.BlockSpec((1, rows, tn), lambda l, j: (l, 0, j)),
        out_shape=jax.ShapeDtypeStruct((depth, rows, n), F32),
        compiler_params=_cparams(("parallel", "parallel")),
        name="ada_mod",
    )(cvec, ada_w, ada_b.reshape(depth, 1, n))


def _ctx_mod_spec(layer, chunk, ctx_row):
    return pl.BlockSpec((1, 1, 1, D_MODEL), lambda b, i: (layer, ctx_row, 0, chunk))


def _lat_mod_spec(layer, chunk):
    return pl.BlockSpec((1, 1, 1, D_MODEL), lambda b, i: (layer, b, 0, chunk))


def _row_groups(n_top, tm, x_refs, ctx_refs, lat_refs):
    lat = [r[0, 0] for r in lat_refs]
    if n_top == 0:
        return [(slice(0, tm), x_refs[0][0], lat)]
    first = pl.program_id(1) == 0
    top = [jnp.where(first, c[0, 0], l) for c, l in zip(ctx_refs, lat)]
    if len(x_refs) == 1:
        x_ref = x_refs[0]
        return [(slice(0, n_top), x_ref[0, :n_top, :], top), (slice(n_top, tm), x_ref[0, n_top:, :], lat)]
    ctx_ref, subs = x_refs[0], x_refs[1:]
    groups = [(slice(0, n_top), jnp.where(first, ctx_ref[0], subs[0][0]), top)]
    return groups + [(slice(k * n_top, (k + 1) * n_top), subs[k][0], lat) for k in range(1, len(subs))]


def _x_operands(xs, tm, n_ctx):
    d = xs[0].shape[-1]
    if len(xs) == 1:
        return list(xs), [pl.BlockSpec((1, tm, d), lambda b, i: (b, i, 0))]
    ctx, lat = xs
    nsub = tm // n_ctx
    assert tm % n_ctx == 0 and lat.shape[1] % n_ctx == 0
    specs = [pl.BlockSpec((1, n_ctx, d), lambda b, i: (b, 0, 0))]
    for k in range(nsub):
        specs.append(pl.BlockSpec((1, n_ctx, d), lambda b, i, k=k: (b, jnp.maximum(i * nsub + k - 1, 0), 0)))
    return [ctx] + [lat] * nsub, specs


def _inproj_kernel(*refs, outs, n_top, n_chunk, n_x):
    x_refs, (nw_ref, shc_ref, scc_ref, shl_ref, scl_ref, w_ref) = refs[:n_x], refs[n_x:n_x + 6]
    out_refs, h_scr, g_scr = refs[n_x + 6:-2], refs[-2], refs[-1]
    tm = h_scr.shape[0]
    for rows, x, (sh, sc) in _row_groups(n_top, tm, x_refs, (shc_ref, scc_ref), (shl_ref, scl_ref)):
        h_scr[rows, :] = (_rms(x, nw_ref[...]) * (1.0 + sc) + sh).astype(BF16)
    for o_ref, (c0, c1, kind) in zip(out_refs, outs):
        if kind == "row" and (c0 % LANES or c1 % LANES):
            a0, a1 = c0 // LANES * LANES, min(-(-c1 // LANES) * LANES, w_ref.shape[1])
            val = jnp.dot(h_scr[...], w_ref[:, a0:a1], preferred_element_type=F32)
            o_ref[0] = val[:, c0 - a0:c1 - a0].astype(o_ref.dtype)
            continue
        for n0 in range(c0, c1, n_chunk):
            n1 = min(n0 + n_chunk, c1)
            val = jnp.dot(h_scr[...], w_ref[:, n0:n1], preferred_element_type=F32)
            if kind == "row":
                o_ref[0, :, n0 - c0:n1 - c0] = val.astype(o_ref.dtype)
            else:
                if val.shape[1] < LANES:
                    val = jnp.concatenate([val, jnp.zeros((tm, LANES - val.shape[1]), F32)], axis=1)
                g_scr[...] = val
                o_ref[0] = g_scr[...].T[:o_ref.shape[1], :].astype(o_ref.dtype)


def _inproj(xs, norm_w, mods, layer, w_bf16, outs, tm, n_ctx):
    bsz, d = xs[0].shape[0], xs[0].shape[-1]
    t = sum(a.shape[1] for a in xs)
    n = w_bf16.shape[1]
    x_args, x_specs = _x_operands(xs, tm, n_ctx)
    assert t % tm == 0 and (n_ctx == 0 or n_ctx <= tm)
    out_specs, out_shapes = [], []
    for c0, c1, kind, dtype, wd in outs:
        if kind == "row":
            out_specs.append(pl.BlockSpec((1, tm, wd), lambda b, i: (b, i, 0)))
            out_shapes.append(jax.ShapeDtypeStruct((bsz, t, wd), dtype))
        else:
            assert c1 - c0 <= LANES and c0 % LANES == 0 and tm % LANES == 0
            out_specs.append(pl.BlockSpec((1, wd, tm), lambda b, i: (b, 0, i)))
            out_shapes.append(jax.ShapeDtypeStruct((bsz, wd, t), dtype))
    ctx_spec = lambda c: _ctx_mod_spec(layer, c, bsz)
    lat_spec = lambda c: _lat_mod_spec(layer, c)
    const = lambda shape: pl.BlockSpec(shape, lambda b, i: (0, 0), pipeline_mode=pl.Buffered(1))
    return pl.pallas_call(
        functools.partial(_inproj_kernel, outs=tuple(o[:3] for o in outs), n_top=n_ctx, n_chunk=512,
                          n_x=len(x_args)),
        grid=(bsz, t // tm),
        in_specs=x_specs + [const((1, d)), ctx_spec(0), ctx_spec(1), lat_spec(0), lat_spec(1),
                            const((d, n))],
        out_specs=out_specs,
        out_shape=out_shapes,
        scratch_shapes=[pltpu.VMEM((tm, d), BF16), pltpu.VMEM((tm, LANES), F32)],
        compiler_params=_cparams(("parallel", "parallel")),
        name=f"inproj_l{layer}",
    )(*x_args, norm_w.reshape(1, d), mods, mods, mods, mods, w_bf16)


def _ffn_kernel(x_ref, nw_ref, shc_ref, scc_ref, gc_ref, shl_ref, scl_ref, gl_ref,
                w1_ref, w3_ref, w2_ref, fw_ref, o_ref, h_scr, *, final, n_top, th):
    tm, hid = h_scr.shape[0], w1_ref.shape[1]
    groups = _row_groups(n_top, tm, (x_ref,), (shc_ref, scc_ref, gc_ref), (shl_ref, scl_ref, gl_ref))
    for rows, x, (sh, sc, _) in groups:
        h_scr[rows, :] = (_rms(x, nw_ref[...]) * (1.0 + sc) + sh).astype(BF16)
    acc = None
    for c0 in range(0, hid, th):
        h = h_scr[...]
        a = jnp.dot(h, w1_ref[:, c0:c0 + th], preferred_element_type=F32)
        g = jnp.dot(h, w3_ref[:, c0:c0 + th], preferred_element_type=F32)
        t = ((a * jax.nn.sigmoid(a)) * g).astype(BF16)
        part = jnp.dot(t, w2_ref[c0:c0 + th, :], preferred_element_type=F32)
        acc = part if acc is None else acc + part
    for rows, x, (_, _, gate) in groups:
        y = x + gate * acc[rows, :]
        if final:
            y = _rms(y, fw_ref[...])
        o_ref[0, rows, :] = y


def _ffn(x, norm_w, mods, layer, w1, w3, w2, final_w, tm, th, n_ctx, final):
    bsz, tx, d = x.shape
    hid = w1.shape[2]
    assert n_ctx <= tm and tx % tm == 0 and hid % th == 0
    ctx_spec = lambda c: _ctx_mod_spec(layer, c, bsz)
    lat_spec = lambda c: _lat_mod_spec(layer, c)
    const = lambda shape: pl.BlockSpec(shape, lambda b, i: (0, 0), pipeline_mode=pl.Buffered(1))
    layer_w = lambda shape: pl.BlockSpec((None,) + shape, lambda b, i: (layer, 0, 0),
                                         pipeline_mode=pl.Buffered(1))
    return pl.pallas_call(
        functools.partial(_ffn_kernel, final=final, n_top=n_ctx, th=th),
        grid=(bsz, tx // tm),
        in_specs=[pl.BlockSpec((1, tm, d), lambda b, i: (b, i, 0)),
                  const((1, d)),
                  ctx_spec(3), ctx_spec(4), ctx_spec(5), lat_spec(3), lat_spec(4), lat_spec(5),
                  layer_w((d, hid)), layer_w((d, hid)), layer_w((hid, d)), const((1, d))],
        out_specs=pl.BlockSpec((1, tm, d), lambda b, i: (b, i, 0)),
        out_shape=jax.ShapeDtypeStruct((bsz, tx, d), F32),
        scratch_shapes=[pltpu.VMEM((tm, d), BF16)],
        compiler_params=_cparams(("parallel", "parallel")),
        name=f"ffn_l{layer}",
    )(x, norm_w.reshape(1, d), mods, mods, mods, mods, mods, mods, w1, w3, w2, final_w.reshape(1, d))


def _chunk_masks(size=CHUNK):
    row = lax.broadcasted_iota(jnp.int32, (size, size), 0)
    col = lax.broadcasted_iota(jnp.int32, (size, size), 1)
    return row, col


def _to_col(row_vec, eye):
    return jnp.sum(jnp.where(eye, jnp.broadcast_to(row_vec, eye.shape), 0.0), axis=1, keepdims=True)


def _to_row(col_vec, eye):
    return jnp.sum(jnp.where(eye, jnp.broadcast_to(col_vec, eye.shape), 0.0), axis=0, keepdims=True)


def _scan_chunk_index(i, rev, n_chunks, ctx_chunks):
    if not rev:
        return i
    return jnp.where(i < ctx_chunks, ctx_chunks - 1 - i, n_chunks + ctx_chunks - 1 - i)


def _mlstm_kernel(bias_ref, q_ref, k_ref, v_ref, g_ref, o_ref, num_scr, cl_scr, mi_scr, fc_scr, sc_scr,
                  *, chunk, n_chunks, ctx_chunks, group):
    hd = pl.program_id(1)
    row, col = _chunk_masks(chunk)
    eye = row == col
    kscale = 1.0 / math.sqrt(HEAD_DIM)
    ones_blk = jnp.ones((chunk, HEAD_DIM), BF16)

    def gate_terms(r, ig_raw, f_raw):
        incl = (col >= row) if r == 1 else (col <= row)
        ig_row = ig_raw + bias_ref[r * MLSTM_HEADS + hd]
        lf_row = jax.nn.log_sigmoid(f_raw + bias_ref[(2 + r) * MLSTM_HEADS + hd])
        f_col = jnp.sum(jnp.where(incl, jnp.broadcast_to(lf_row, incl.shape), 0.0),
                        axis=1, keepdims=True)
        f_row = _to_row(f_col, eye)
        f_last = jnp.sum(lf_row, axis=1, keepdims=True)
        dm = jnp.where(incl, f_col - f_row + ig_row, -jnp.inf)
        m_intra = jnp.max(dm, axis=1, keepdims=True)
        w_col = f_last - f_col + _to_col(ig_row, eye)
        m_loc = jnp.max(w_col, axis=0, keepdims=True)
        return f_col, f_last, jnp.exp(dm - m_intra), m_intra, jnp.exp(w_col - m_loc), m_loc

    def local_group(gi, _):
        chunks = [gi * group + g for g in range(group)]
        loaded = []
        for c in chunks:
            rows = pl.ds(pl.multiple_of(c * chunk, chunk), chunk)
            gates = [(g_ref[0, r * MLSTM_HEADS + hd, pl.ds(c, 1), :],
                      g_ref[0, (2 + r) * MLSTM_HEADS + hd, pl.ds(c, 1), :]) for r in (0, 1)]
            v1 = jnp.concatenate([v_ref[0, rows, :].astype(BF16), ones_blk], axis=-1)
            loaded.append((q_ref[0, rows, :], k_ref[0, rows, :].astype(F32) * kscale, v1, gates))
        chains = [(g, r) for g in range(group) for r in (0, 1)]
        terms = [gate_terms(r, *loaded[g][3][r]) for g, r in chains]
        qks = [_dot_nt(q, k) for q, k, _, _ in loaded]
        c_locs = [_dot_tn(tm[4] * loaded[g][1], loaded[g][2]) for (g, r), tm in zip(chains, terms)]
        nums = [_dot(tm[2] * qks[g], loaded[g][2]) for (g, r), tm in zip(chains, terms)]
        for (g, r), tm, c_loc, num in zip(chains, terms, c_locs, nums):
            f_col, f_last, _, m_intra, _, m_loc = tm
            c = chunks[g]
            num_scr[r, c] = num
            cl_scr[r, c] = c_loc
            mi_scr[r, c] = jnp.broadcast_to(m_intra, (chunk, HEAD_DIM))
            fc_scr[r, c] = jnp.broadcast_to(f_col, (chunk, HEAD_DIM))
            sc_scr[r, c, :8, :] = jnp.broadcast_to(f_last, (8, HEAD_DIM))
            sc_scr[r, c, 8:, :] = jnp.broadcast_to(m_loc, (8, HEAD_DIM))
        return 0

    lax.fori_loop(0, n_chunks // group, local_group, 0)
    o_ref[...] = jnp.zeros_like(o_ref)

    def step(i, carry):
        cs = [_scan_chunk_index(i, r == 1, n_chunks, ctx_chunks) for r in (0, 1)]
        rows = [pl.ds(pl.multiple_of(c * chunk, chunk), chunk) for c in cs]
        qcs = [_dot(q_ref[0, rw, :], st[0]) for rw, st in zip(rows, carry)]
        new = []
        for r in (0, 1):
            s_st, m_st = carry[r]
            c = cs[r]
            mi, na = mi_scr[r, c], num_scr[r, c]
            inter = fc_scr[r, c] + m_st[0:1, :]
            m_t = jnp.maximum(mi, inter)
            a_loc_t, a_inter = jnp.exp(mi - m_t), jnp.exp(inter - m_t)
            num = a_loc_t * na[:, :HEAD_DIM] + a_inter * qcs[r][:, :HEAD_DIM]
            den = a_loc_t * na[:, HEAD_DIM:] + a_inter * qcs[r][:, HEAD_DIM:]
            out = num / jnp.maximum(jnp.abs(den), jnp.exp(-m_t))
            o_ref[0, rows[r], :] = (o_ref[0, rows[r], :].astype(F32) + out).astype(o_ref.dtype)
            f_last, m_loc = sc_scr[r, c, :8, :], sc_scr[r, c, 8:, :]
            m_new = jnp.maximum(f_last + m_st, m_loc)
            a_prev = jnp.exp(f_last + m_st - m_new)[0:1, :]
            a_loc = jnp.exp(m_loc - m_new)[0:1, :]
            a_prev = jnp.concatenate([a_prev, a_prev], axis=1)
            a_loc = jnp.concatenate([a_loc, a_loc], axis=1)
            new.append((a_prev * s_st + a_loc * cl_scr[r, c], m_new))
        return tuple(new)

    init = (jnp.zeros((HEAD_DIM, 2 * HEAD_DIM), F32), jnp.zeros((8, HEAD_DIM), F32))
    lax.fori_loop(0, n_chunks, step, (init, init), unroll=2)


def _mlstm(qkv, gates, bias, n_ctx, chunk):
    bsz, t, _ = qkv.shape
    assert t % chunk == 0 and n_ctx % chunk == 0
    n_chunks = t // chunk
    gates_row = _chunked(gates, chunk)
    qkv_spec = lambda off: pl.BlockSpec((1, t, HEAD_DIM), lambda b, h: (b, 0, off + h))
    return pl.pallas_call(
        functools.partial(_mlstm_kernel, chunk=chunk, n_chunks=n_chunks, ctx_chunks=n_ctx // chunk,
                          group=math.gcd(9 if chunk <= CHUNK else 3, n_chunks)),
        grid=(bsz, MLSTM_HEADS),
        in_specs=[pl.BlockSpec(memory_space=pltpu.SMEM),
                  qkv_spec(0), qkv_spec(MLSTM_HEADS), qkv_spec(2 * MLSTM_HEADS),
                  pl.BlockSpec((1, 4 * MLSTM_HEADS, n_chunks, chunk), lambda b, h: (b, 0, 0, 0))],
        out_specs=pl.BlockSpec((1, t, HEAD_DIM), lambda b, h: (b, 0, h)),
        out_shape=jax.ShapeDtypeStruct((bsz, t, MLSTM_WIDTH), BF16),
        scratch_shapes=[pltpu.VMEM((2, n_chunks, chunk, 2 * HEAD_DIM), F32),
                        pltpu.VMEM((2, n_chunks, HEAD_DIM, 2 * HEAD_DIM), F32),
                        pltpu.VMEM((2, n_chunks, chunk, HEAD_DIM), F32),
                        pltpu.VMEM((2, n_chunks, chunk, HEAD_DIM), F32),
                        pltpu.VMEM((2, n_chunks, 16, HEAD_DIM), F32)],
        compiler_params=_cparams(("parallel", "parallel")),
        name="mlstm",
    )(bias, qkv, qkv, qkv, gates_row)


def _s5_prep_kernel(lr_ref, li_ref, ldt_ref, br_ref, bi_ref, cr_ref, ci_ref, bd_ref, cdt_ref, a_ref):
    lr, li = lr_ref[0, 0], li_ref[0, 0]
    dt = jnp.exp(ldt_ref[0, 0])
    mag, ang = jnp.exp(lr * dt), li * dt
    ab_re, ab_im = mag * jnp.cos(ang), mag * jnp.sin(ang)
    nr, ni = ab_re - 1.0, ab_im
    den = lr * lr + li * li
    co_re = (nr * lr + ni * li) / den
    co_im = (ni * lr - nr * li) / den
    b_re, b_im = br_ref[0, 0], bi_ref[0, 0]
    bb_re = co_re * b_re - co_im * b_im
    bb_im = co_re * b_im + co_im * b_re
    c_re, c_im = cr_ref[0, 0], ci_ref[0, 0]
    lane_group = lax.broadcasted_iota(jnp.int32, (S5_GROUP, S5_HALF_N), 1) // S5_STATE
    n = S5_HALF_N
    for g in range(S5_GROUPS // S5_HALVES):
        sel = lane_group == g
        rows = slice(g * S5_GROUP, (g + 1) * S5_GROUP)
        bd_ref[0, 0, rows, :n] = jnp.where(sel, bb_re, 0.0).astype(bd_ref.dtype)
        bd_ref[0, 0, rows, n:] = jnp.where(sel, bb_im, 0.0).astype(bd_ref.dtype)
        cdt_ref[0, 0, rows, :n] = jnp.where(sel, c_re, 0.0).astype(cdt_ref.dtype)
        cdt_ref[0, 0, rows, n:] = jnp.where(sel, -c_im, 0.0).astype(cdt_ref.dtype)
    a_ref[0, 0, :, :n] = jnp.broadcast_to(ab_re, (8, n))
    a_ref[0, 0, :, n:] = jnp.broadcast_to(ab_im, (8, n))


def _s5_prep(lam_re, lam_im, log_dt, b_re, b_im, c_re, c_im):
    gh = S5_GROUPS // S5_HALVES
    vec = lambda a: a.reshape(2, S5_HALVES, 1, S5_HALF_N)
    ldt = vec(jnp.broadcast_to(log_dt[:, :, None], (2, S5_GROUPS, S5_STATE)))
    bt = lambda a: a.reshape(2, S5_HALVES, gh, S5_STATE, S5_GROUP).transpose(0, 1, 4, 2, 3).reshape(
        2, S5_HALVES, S5_GROUP, S5_HALF_N)
    ct = lambda a: a.reshape(2, S5_HALVES, gh, S5_GROUP, S5_STATE).transpose(0, 1, 3, 2, 4).reshape(
        2, S5_HALVES, S5_GROUP, S5_HALF_N)
    vspec = pl.BlockSpec((1, 1, 1, S5_HALF_N), lambda r, h: (r, h, 0, 0))
    mspec = pl.BlockSpec((1, 1, S5_GROUP, S5_HALF_N), lambda r, h: (r, h, 0, 0))
    ospec = pl.BlockSpec((1, 1, S5_HALF_W, 2 * S5_HALF_N), lambda r, h: (r, h, 0, 0))
    return pl.pallas_call(
        _s5_prep_kernel,
        grid=(2, S5_HALVES),
        in_specs=[vspec, vspec, vspec, mspec, mspec, mspec, mspec],
        out_specs=[ospec, ospec, pl.BlockSpec((1, 1, 8, 2 * S5_HALF_N), lambda r, h: (r, h, 0, 0))],
        out_shape=[jax.ShapeDtypeStruct((2, S5_HALVES, S5_HALF_W, 2 * S5_HALF_N), BF16),
                   jax.ShapeDtypeStruct((2, S5_HALVES, S5_HALF_W, 2 * S5_HALF_N), BF16),
                   jax.ShapeDtypeStruct((2, S5_HALVES, 8, 2 * S5_HALF_N), F32)],
        compiler_params=_cparams(("parallel", "parallel")),
        name="s5_prep",
    )(vec(lam_re), vec(lam_im), ldt, bt(b_re), bt(b_im), ct(c_re), ct(c_im))


def _s5_kernel(uf0_ref, uf1_ref, ub0_ref, ub1_ref, bd_ref, cdt_ref, a_ref, yf_ref, yb_ref,
               lhs_scr, bu_scr, s_scr, st_scr, *, tb, bsz):
    n = S5_HALF_N
    rows = tb * bsz
    u_refs = ((uf0_ref, uf1_ref), (ub0_ref, ub1_ref))
    y_refs = (yf_ref, yb_ref)

    @pl.when(pl.program_id(1) == 0)
    def _():
        st_scr[...] = jnp.zeros_like(st_scr)

    for d in (0, 1):
        for j in (0, 1):
            lhs_scr[d, :, j * LANES:(j + 1) * LANES] = jnp.transpose(
                u_refs[d][j][...].astype(F32), (1, 0, 2)).reshape(rows, LANES)
    for d in (0, 1):
        bu_scr[d] = _dot(lhs_scr[d], bd_ref[d, 0])
    for d in (0, 1):
        a_re, a_im = a_ref[d, 0, :, :n], a_ref[d, 0, :, n:]
        s_re, s_im = st_scr[d, :, :n], st_scr[d, :, n:]
        for j in range(tb):
            t = j if d == 0 else tb - 1 - j
            sl = slice(t * bsz, (t + 1) * bsz)
            s_re, s_im = (a_re * s_re - a_im * s_im + bu_scr[d, sl, :n],
                          a_re * s_im + a_im * s_re + bu_scr[d, sl, n:])
            s_scr[d, sl, :n] = s_re
            s_scr[d, sl, n:] = s_im
        st_scr[d, :, :n] = s_re
        st_scr[d, :, n:] = s_im
        y = _dot_nt(s_scr[d], cdt_ref[d, 0])
        for j in (0, 1):
            y_refs[d][:, :, j * LANES:(j + 1) * LANES] = jnp.transpose(
                y[:, j * LANES:(j + 1) * LANES].reshape(tb, bsz, LANES), (1, 0, 2)).astype(y_refs[d].dtype)


def _s5(u, bd, cdt, avec, n_ctx, tb):
    bsz, t, _ = u.shape
    nb, ncb = t // tb, n_ctx // tb
    assert bsz == 8 and S5_HALF_W == 2 * LANES

    def rev(i):
        return jnp.where(i < ncb, ncb - 1 - i, nb + ncb - 1 - i)

    fwd = lambda i: i
    uspec = lambda order, j: pl.BlockSpec((bsz, tb, LANES), lambda h, i: (0, order(i), 2 * h + j))
    yspec = lambda order: pl.BlockSpec((bsz, tb, S5_HALF_W), lambda h, i: (0, order(i), h))
    wspec = pl.BlockSpec((2, 1, S5_HALF_W, 2 * S5_HALF_N), lambda h, i: (0, h, 0, 0))
    rows = tb * bsz
    return pl.pallas_call(
        functools.partial(_s5_kernel, tb=tb, bsz=bsz),
        grid=(S5_HALVES, nb),
        in_specs=[uspec(fwd, 0), uspec(fwd, 1), uspec(rev, 0), uspec(rev, 1), wspec, wspec,
                  pl.BlockSpec((2, 1, 8, 2 * S5_HALF_N), lambda h, i: (0, h, 0, 0))],
        out_specs=[yspec(fwd), yspec(rev)],
        out_shape=[jax.ShapeDtypeStruct((bsz, t, S5_WIDTH), BF16)] * 2,
        scratch_shapes=[pltpu.VMEM((2, rows, S5_HALF_W), F32),
                        pltpu.VMEM((2, rows, 2 * S5_HALF_N), F32),
                        pltpu.VMEM((2, rows, 2 * S5_HALF_N), F32),
                        pltpu.VMEM((2, bsz, 2 * S5_HALF_N), F32)],
        compiler_params=_cparams(("parallel", "arbitrary")),
        name="s5_scan",
    )(u, u, u, u, bd, cdt, avec)


def _even_finish_kernel(*refs, n_top, n_x):
    x_refs = refs[:n_x]
    ha_ref, op_ref, yf_ref, yb_ref, u_ref, mhw_ref, ds_ref, wg_ref, wo_ref, gc_ref, gl_ref, o_ref = refs[n_x:]
    ha = ha_ref[0].astype(F32)
    parts = []
    for h in range(MLSTM_HEADS):
        sl = slice(h * HEAD_DIM, (h + 1) * HEAD_DIM)
        parts.append(_rms(ha[:, sl], mhw_ref[:, sl]))
    a_out = jnp.concatenate(parts, axis=-1) * jax.nn.sigmoid(op_ref[0].astype(F32))
    yb = jax.nn.gelu(yf_ref[0].astype(F32) + yb_ref[0].astype(F32) + ds_ref[...] * u_ref[0].astype(F32))
    glu = _dot(yb, wg_ref[...])
    b_out = glu[:, :S5_WIDTH] * jax.nn.sigmoid(glu[:, S5_WIDTH:])
    y = _dot(a_out, wo_ref[:MLSTM_WIDTH, :]) + _dot(b_out, wo_ref[MLSTM_WIDTH:, :])
    for rows, x, (gate,) in _row_groups(n_top, y.shape[0], x_refs, (gc_ref,), (gl_ref,)):
        o_ref[0, rows, :] = x + gate * y[rows, :]


def _even_finish(xs, ha, o_pre, yf, yb, u, mods, layer, mh_w, d_skip, wg, wo, tm, n_ctx):
    bsz, t, _ = ha.shape
    d = xs[0].shape[-1]
    assert t % tm == 0 and n_ctx <= tm
    x_args, x_specs = _x_operands(xs, tm, n_ctx)
    full = lambda shape: pl.BlockSpec(shape, lambda b, i: tuple(0 for _ in shape),
                                      pipeline_mode=pl.Buffered(1))
    return pl.pallas_call(
        functools.partial(_even_finish_kernel, n_top=n_ctx, n_x=len(x_args)),
        grid=(bsz, t // tm),
        in_specs=x_specs + [
                  pl.BlockSpec((1, tm, MLSTM_WIDTH), lambda b, i: (b, i, 0)),
                  pl.BlockSpec((1, tm, MLSTM_WIDTH), lambda b, i: (b, i, 0)),
                  pl.BlockSpec((1, tm, S5_WIDTH), lambda b, i: (b, i, 0)),
                  pl.BlockSpec((1, tm, S5_WIDTH), lambda b, i: (b, i, 0)),
                  pl.BlockSpec((1, tm, S5_WIDTH), lambda b, i: (b, i, 0)),
                  full((1, MLSTM_WIDTH)), full((1, S5_WIDTH)),
                  full((S5_WIDTH, 2 * S5_WIDTH)), full((d, d)),
                  _ctx_mod_spec(layer, 2, bsz), _lat_mod_spec(layer, 2)],
        out_specs=pl.BlockSpec((1, tm, d), lambda b, i: (b, i, 0)),
        out_shape=jax.ShapeDtypeStruct((bsz, t, d), F32),
        compiler_params=_cparams(("parallel", "parallel")),
        name="even_finish",
    )(*x_args, ha, o_pre, yf, yb, u, mh_w.reshape(1, -1), d_skip.reshape(1, -1), wg, wo, mods, mods)


def _conv_chunk(x_ref, w, lanes, c, static, n_chunks, ctx_chunks):
    tpos = lax.broadcasted_iota(jnp.int32, (CHUNK, HEAD_DIM), 0)
    first, last = tpos == 0, tpos == CHUNK - 1
    wrow = lambda k: w[k:k + 1, :]

    def rows(ci, dtype=F32):
        start = ci * CHUNK if isinstance(ci, int) else pl.multiple_of(ci * CHUNK, CHUNK)
        return x_ref[0, pl.ds(start, CHUNK), lanes].astype(dtype)

    if static and c < ctx_chunks:
        cur = rows(c)
        zero = jnp.zeros((1, HEAD_DIM), F32)
        prev_tok = rows(c - 1)[CHUNK - 1:CHUNK, :] if c > 0 else zero
        next_tok = rows(c + 1)[0:1, :] if c < ctx_chunks - 1 else zero
        left = jnp.where(first, prev_tok, pltpu.roll(cur, 1, axis=0))
        right = jnp.where(last, next_tok, pltpu.roll(cur, CHUNK - 1, axis=0))
        return wrow(3) * left + wrow(4) * cur + wrow(5) * right

    if static:
        up_i, dn_i = max(c - 1, ctx_chunks), min(c + 1, n_chunks - 1)
        w_up = w[0:3, :] if c > ctx_chunks else jnp.zeros((3, HEAD_DIM), F32)
        w_dn = w[6:9, :] if c < n_chunks - 1 else jnp.zeros((3, HEAD_DIM), F32)
    else:
        up_i, dn_i = jnp.maximum(c - 1, ctx_chunks), jnp.minimum(c + 1, n_chunks - 1)
        w_up = jnp.where(c > ctx_chunks, w[0:3, :], 0.0)
        w_dn = jnp.where(c < n_chunks - 1, w[6:9, :], 0.0)
    up, cur, dn = rows(up_i, BF16), rows(c, BF16), rows(dn_i, BF16)
    w_up, w_mid, w_dn = w_up.astype(BF16), w[3:6, :].astype(BF16), w_dn.astype(BF16)
    r0, r1, r2 = ((w_up[dc:dc + 1, :] * up + w_mid[dc:dc + 1, :] * cur + w_dn[dc:dc + 1, :] * dn).astype(F32)
                  for dc in range(3))
    prev = jnp.where(first, 0.0, pltpu.roll(r0, 1, axis=0))
    nxt = jnp.where(last, 0.0, pltpu.roll(r2, CHUNK - 1, axis=0))
    return r1 + prev + nxt


def _unit_tri_inverses(mats, eye_f):
    ps = [eye_f - a for a in mats]
    pws = [_dot(a, a) for a in mats]
    k = 2
    while 2 * k < CHUNK:
        res = [_dot(jnp.concatenate([p, pw], axis=0), pw) for p, pw in zip(ps, pws)]
        ps = [p + r[:CHUNK] for p, r in zip(ps, res)]
        pws = [r[CHUNK:] for r in res]
        k *= 2
    return [p + _dot(p, pw) for p, pw in zip(ps, pws)]


def _gdn_kernel(par_ref, q_ref, k_ref, v_ref, wq_ref, wk_ref, wv_ref, g_ref, o_ref,
                qk_scr, ol_scr, nl_scr, gl_scr, *, n_chunks, ctx_chunks, group, heads):
    row, col = _chunk_masks()
    eye = row == col
    eye_f = eye.astype(F32)

    def gate_terms(hd, r, a_pre, b_pre):
        incl = (col >= row) if r == 1 else (col <= row)
        a_log = jnp.full((1, CHUNK), par_ref[r * GDN_HEADS + hd], F32)
        dt_b = par_ref[(2 + r) * GDN_HEADS + hd]
        g_row = -jnp.exp(a_log) * jax.nn.softplus(a_pre + dt_b)
        beta_col = _to_col(jax.nn.sigmoid(b_pre), eye)
        gc_col = jnp.sum(jnp.where(incl, jnp.broadcast_to(g_row, incl.shape), 0.0),
                         axis=1, keepdims=True)
        gc_row = _to_row(gc_col, eye)
        g_last = jnp.sum(g_row, axis=1, keepdims=True)
        gamma = jnp.exp(jnp.where(incl, gc_col - gc_row, -jnp.inf))
        return beta_col, gc_col, g_last, gamma

    for hh in range(heads):
        hd = pl.program_id(1) * heads + hh
        lanes = slice(hh * HEAD_DIM, (hh + 1) * HEAD_DIM)

        def load(c, static, hd=hd, lanes=lanes):
            gates = [(g_ref[0, r * GDN_HEADS + hd, pl.ds(c, 1), :],
                      g_ref[0, (2 + r) * GDN_HEADS + hd, pl.ds(c, 1), :]) for r in (0, 1)]
            q, k, v = (_conv_chunk(x_ref, w_ref[:, lanes], lanes, c, static, n_chunks, ctx_chunks)
                       for x_ref, w_ref in ((q_ref, wq_ref), (k_ref, wk_ref), (v_ref, wv_ref)))
            q, k, v = q * jax.nn.sigmoid(q), k * jax.nn.sigmoid(k), v * jax.nn.sigmoid(v)
            q = q * (lax.rsqrt(jnp.sum(q * q, axis=-1, keepdims=True) + EPS) * (1.0 / math.sqrt(HEAD_DIM)))
            k = k * lax.rsqrt(jnp.sum(k * k, axis=-1, keepdims=True) + EPS)
            return q, k, v, gates

        def local_group(gi, _, hh=hh, hd=hd, load=load):
            static = isinstance(gi, int)
            chunks = [gi * group + g for g in range(group)]
            loaded = [load(c, static) for c in chunks]
            kqs = [_dot_nt(jnp.concatenate([k, q], axis=0), k) for q, k, _, _ in loaded]
            chains = [(g, r) for g in range(group) for r in (0, 1)]
            terms = [gate_terms(hd, r, *loaded[g][3][r]) for g, r in chains]
            a_mats = []
            for (g, r), (beta_col, _, _, gamma) in zip(chains, terms):
                strict = (col > row) if r == 1 else (col < row)
                a_mats.append(jnp.where(strict, beta_col * kqs[g][:CHUNK] * gamma, 0.0))
            t_invs = _unit_tri_inverses(a_mats, eye_f)
            sols = []
            for (g, r), (beta_col, gc_col, _, _), t_inv in zip(chains, terms, t_invs):
                _, k, v, _ = loaded[g]
                rhs = jnp.concatenate([v * beta_col, k * (beta_col * jnp.exp(gc_col))], axis=-1)
                sols.append(_dot(t_inv, rhs))
            auws = [_dot(kqs[g][CHUNK:] * tm[3], sol) for (g, r), tm, sol in zip(chains, terms, sols)]
            kuws = [_dot_tn(loaded[g][1] * jnp.exp(tm[2] - tm[1]), sol)
                    for (g, r), tm, sol in zip(chains, terms, sols)]
            for (g, r), (_, gc_col, g_last, _), auw, kuw in zip(chains, terms, auws, kuws):
                c = chunks[g]
                q = loaded[g][0]
                qk_scr[hh, r, c, :CHUNK, :] = (q * jnp.exp(gc_col) - auw[:, HEAD_DIM:]).astype(qk_scr.dtype)
                qk_scr[hh, r, c, CHUNK:, :] = kuw[:, HEAD_DIM:].astype(qk_scr.dtype)
                ol_scr[hh, r, c] = auw[:, :HEAD_DIM].astype(ol_scr.dtype)
                nl_scr[hh, r, c] = kuw[:, :HEAD_DIM].astype(nl_scr.dtype)
                gl_scr[hh, r, c] = jnp.broadcast_to(jnp.exp(g_last), (8, HEAD_DIM))
            return 0

        n_static = -(-ctx_chunks // group)
        for gi in range(n_static):
            local_group(gi, 0)
        lax.fori_loop(n_static, n_chunks // group, local_group, 0)

    o_ref[...] = jnp.zeros_like(o_ref)
    streams = [(hh, r) for hh in range(heads) for r in (0, 1)]

    def step(i, carry):
        cs = [_scan_chunk_index(i, r == 1, n_chunks, ctx_chunks) for _, r in streams]
        res = [jnp.dot(qk_scr[hh, r, c], s.astype(BF16), preferred_element_type=F32)
               for (hh, r), c, s in zip(streams, cs, carry)]
        new = []
        for (hh, r), c, s, rs in zip(streams, cs, carry, res):
            rows = pl.ds(pl.multiple_of(c * CHUNK, CHUNK), CHUNK)
            lanes = slice(hh * HEAD_DIM, (hh + 1) * HEAD_DIM)
            o_ref[0, rows, lanes] = (o_ref[0, rows, lanes].astype(F32) + ol_scr[hh, r, c]
                                     + rs[:CHUNK]).astype(o_ref.dtype)
            new.append(gl_scr[hh, r, c][0:1, :] * s - rs[CHUNK:] + nl_scr[hh, r, c])
        return tuple(new)

    init = jnp.zeros((HEAD_DIM, HEAD_DIM), F32)
    lax.fori_loop(0, n_chunks, step, (init,) * len(streams), unroll=2)


def _gdn(qkv, conv_w, gates_row, par, n_ctx, heads=4):
    bsz, t, _ = qkv.shape
    assert GRID_W == CHUNK and n_ctx % CHUNK == 0
    n_chunks = t // CHUNK
    group = math.gcd(9, n_chunks)
    wd = heads * HEAD_DIM
    blocks = GDN_HEADS // heads
    spec = lambda off: pl.BlockSpec((1, t, wd), lambda b, h: (b, 0, off + h), pipeline_mode=pl.Buffered(1))
    wspec = lambda off: pl.BlockSpec((9, wd), lambda b, h: (0, off + h))
    conv_w9 = conv_w.reshape(9, GDN_QKV)
    return pl.pallas_call(
        functools.partial(_gdn_kernel, n_chunks=n_chunks, ctx_chunks=n_ctx // CHUNK, group=group,
                          heads=heads),
        grid=(bsz, blocks),
        in_specs=[pl.BlockSpec(memory_space=pltpu.SMEM),
                  spec(0), spec(blocks), spec(2 * blocks),
                  wspec(0), wspec(blocks), wspec(2 * blocks),
                  pl.BlockSpec((1, 4 * GDN_HEADS, n_chunks, CHUNK), lambda b, h: (b, 0, 0, 0))],
        out_specs=pl.BlockSpec((1, t, wd), lambda b, h: (b, 0, h)),
        out_shape=jax.ShapeDtypeStruct((bsz, t, D_MODEL), BF16),
        scratch_shapes=[pltpu.VMEM((heads, 2, n_chunks, CHUNK + HEAD_DIM, HEAD_DIM), BF16),
                        pltpu.VMEM((heads, 2, n_chunks, CHUNK, HEAD_DIM), BF16),
                        pltpu.VMEM((heads, 2, n_chunks, HEAD_DIM, HEAD_DIM), BF16),
                        pltpu.VMEM((heads, 2, n_chunks, 8, HEAD_DIM), F32)],
        compiler_params=_cparams(("parallel", "parallel")),
        name="gdn",
    )(par, qkv, qkv, qkv, conv_w9, conv_w9, conv_w9, gates_row)


def _odd_finish_kernel(x_ref, o_in_ref, z_ref, hw_ref, wo_ref, g_ref, o_ref):
    o = o_in_ref[0].astype(F32)
    parts = []
    for h in range(GDN_HEADS):
        sl = slice(h * HEAD_DIM, (h + 1) * HEAD_DIM)
        parts.append(_rms(o[:, sl], hw_ref[:, sl]))
    z = z_ref[0].astype(F32)
    y = jnp.concatenate(parts, axis=-1) * (z * jax.nn.sigmoid(z))
    o_ref[0] = x_ref[0] + g_ref[0, 0] * _dot(y, wo_ref[...])


def _odd_finish(xc, o, z, mods, layer, hw, wo, tm, n_ctx):
    bsz, t, d = xc.shape
    assert n_ctx % tm == 0 and (t - n_ctx) % tm == 0
    t0 = n_ctx // tm
    return pl.pallas_call(
        _odd_finish_kernel,
        grid=(bsz, (t - n_ctx) // tm),
        in_specs=[pl.BlockSpec((1, tm, d), lambda b, i: (b, i + t0, 0)),
                  pl.BlockSpec((1, tm, d), lambda b, i: (b, i + t0, 0)),
                  pl.BlockSpec((1, tm, d), lambda b, i: (b, i + t0, 0)),
                  pl.BlockSpec((1, d), lambda b, i: (0, 0)),
                  pl.BlockSpec((d, d), lambda b, i: (0, 0)),
                  _lat_mod_spec(layer, 2)],
        out_specs=pl.BlockSpec((1, tm, d), lambda b, i: (b, i, 0)),
        out_shape=jax.ShapeDtypeStruct((bsz, t - n_ctx, d), F32),
        compiler_params=_cparams(("parallel", "parallel")),
        name="odd_finish",
    )(xc, o, z, hw.reshape(1, d), wo, mods)


def _chunked(gates_t, chunk=CHUNK):
    bsz, g, t = gates_t.shape
    return gates_t.reshape(bsz, g, t // chunk, chunk)


def _pick_tile(total, target, at_least, multiple=8):
    for tile in range(min(target, total), multiple - 1, -1):
        if total % tile == 0 and tile % multiple == 0 and tile >= at_least:
            return tile
    raise ValueError(f"no row tile for {total} rows")


def kernel(x, c, ctx, c_ctx, ada_w, ada_b, norm1_w, norm2_w, ffn_w1, ffn_w3, ffn_w2, final_norm_w,
           ev_w_in, ev_i_bias, ev_f_bias, ev_head_norm_w, ev_lam_re, ev_lam_im, ev_log_dt,
           ev_b_re, ev_b_im, ev_c_re, ev_c_im, ev_d, ev_w_glu, ev_w_out,
           od_w_in, od_conv_w, od_a_log, od_dt_bias, od_head_norm_w, od_w_out):
    bsz, seq, d = x.shape
    n_ctx = ctx.shape[1]
    assert d == D_MODEL and seq % GRID_W == 0 and n_ctx % CHUNK == 0 and bsz % 8 == 0
    assert ada_w.shape[0] == 2
    t_all = n_ctx + seq
    tm = _pick_tile(t_all, 768, n_ctx, LANES)
    tm_lat = _pick_tile(math.gcd(seq, n_ctx), 512, 0)
    th = FFN_HIDDEN // 2
    tb = 2 * CHUNK if n_ctx % (2 * CHUNK) == 0 and seq % (2 * CHUNK) == 0 else CHUNK

    if tm % n_ctx == 0 and seq % n_ctx == 0:
        xs = (ctx, x)
    else:
        xs = (jnp.concatenate([ctx, x], axis=1),)
    mod_rows = 16
    cvec = jnp.zeros((mod_rows, d), F32).at[:bsz].set(c).at[bsz].set(c_ctx)
    mods = _ada(cvec, ada_w, ada_b).reshape(2, mod_rows, 1, 6 * d)
    w1, w3, w2 = ffn_w1.astype(BF16), ffn_w3.astype(BF16), ffn_w2.astype(BF16)

    qkv_w, n_g = 3 * MLSTM_WIDTH, 4 * MLSTM_HEADS
    w_in = ev_w_in[0]
    g0 = qkv_w + MLSTM_WIDTH
    n_even = w_in.shape[1]
    qkv, o_pre, u, gates = _inproj(
        xs, norm1_w[0], mods, 0, w_in.astype(BF16),
        [(0, qkv_w, "row", BF16, qkv_w), (qkv_w, g0, "row", BF16, g0 - qkv_w),
         (g0 + n_g, n_even, "row", BF16, S5_WIDTH), (g0, g0 + n_g, "gate", F32, n_g)], tm, n_ctx)
    bias = jnp.concatenate([ev_i_bias[0].reshape(-1), ev_f_bias[0].reshape(-1)]).astype(F32)
    mlstm_chunk = MLSTM_CHUNK if n_ctx % MLSTM_CHUNK == 0 and seq % MLSTM_CHUNK == 0 else CHUNK
    ha = _mlstm(qkv, gates, bias, n_ctx, mlstm_chunk)
    bd, cdt, avec = _s5_prep(ev_lam_re[0], ev_lam_im[0], ev_log_dt[0], ev_b_re[0], ev_b_im[0],
                             ev_c_re[0], ev_c_im[0])
    yf, yb = _s5(u, bd, cdt, avec, n_ctx, tb)
    xc = _even_finish(xs, ha, o_pre, yf, yb, u, mods, 0, ev_head_norm_w[0], ev_d[0],
                      ev_w_glu[0].astype(BF16), ev_w_out[0].astype(BF16), tm, n_ctx)
    xc = _ffn(xc, norm2_w[0], mods, 0, w1, w3, w2, final_norm_w, tm, th, n_ctx, False)

    n_g = 4 * GDN_HEADS
    qkv_pre, z, gates = _inproj(
        (xc,), norm1_w[1], mods, 1, od_w_in[0].astype(BF16),
        [(0, GDN_QKV, "row", BF16, GDN_QKV), (GDN_QKV, 4 * d, "row", BF16, d),
         (4 * d, 4 * d + n_g, "gate", F32, n_g)], tm, n_ctx)
    par = jnp.concatenate([od_a_log[0].reshape(-1), od_dt_bias[0].reshape(-1)]).astype(F32)
    o = _gdn(qkv_pre, od_conv_w[0], _chunked(gates), par, n_ctx)
    xl = _odd_finish(xc, o, z, mods, 1, od_head_norm_w[0], od_w_out[0].astype(BF16), tm_lat, n_ctx)
    return _ffn(xl, norm2_w[1], mods, 1, w1, w3, w2, final_norm_w, _pick_tile(seq, 512, 0),
                th, 0, True)
```

```python
import functools
import math

import jax
import jax.numpy as jnp
from jax import lax
from jax.experimental import pallas as pl
from jax.experimental.pallas import tpu as pltpu

F32 = jnp.float32
BF16 = jnp.bfloat16

D_MODEL = 1024
CHUNK = 64
MLSTM_CHUNK = 128
GRID_W = 64
EPS = 1e-6
LANES = 128
HEAD_DIM = 128
MLSTM_WIDTH = D_MODEL // 2
MLSTM_HEADS = MLSTM_WIDTH // HEAD_DIM
S5_WIDTH = D_MODEL - MLSTM_WIDTH
S5_GROUP = 16
S5_GROUPS = S5_WIDTH // S5_GROUP
S5_STATE = 64
S5_HALVES = 2
S5_HALF_W = S5_WIDTH // S5_HALVES
S5_HALF_N = (S5_GROUPS // S5_HALVES) * S5_STATE
GDN_HEADS = D_MODEL // HEAD_DIM
GDN_QKV = 3 * D_MODEL
FFN_HIDDEN = ((8 * D_MODEL + 3 * 256 - 1) // (3 * 256)) * 256
VMEM_LIMIT = 56 * 1024 * 1024

_NT = (((1,), (1,)), ((), ()))
_TN = (((0,), (0,)), ((), ()))


def _cparams(sem):
    return pltpu.CompilerParams(dimension_semantics=sem, vmem_limit_bytes=VMEM_LIMIT)


def _dot(a, b):
    return jnp.dot(a.astype(BF16), b.astype(BF16), preferred_element_type=F32)


def _dot_nt(a, b):
    return lax.dot_general(a.astype(BF16), b.astype(BF16), _NT, preferred_element_type=F32)


def _dot_tn(a, b):
    return lax.dot_general(a.astype(BF16), b.astype(BF16), _TN, preferred_element_type=F32)


def _rms(x, w):
    return x * lax.rsqrt(jnp.mean(x * x, axis=-1, keepdims=True) + EPS) * w


def _ada_kernel(s_ref, w_ref, b_ref, o_ref):
    s = s_ref[...]
    s = s * jax.nn.sigmoid(s)
    o_ref[0] = _dot(s, w_ref[0]) + b_ref[0]


def _ada(cvec, ada_w, ada_b):
    depth, d, n = ada_w.shape
    tn = 1536
    rows = cvec.shape[0]
    return pl.pallas_call(
        _ada_kernel,
        grid=(depth, n // tn),
        in_specs=[pl.BlockSpec((rows, d), lambda l, j: (0, 0)),
                  pl.BlockSpec((1, d, tn), lambda l, j: (l, 0, j)),
                  pl.BlockSpec((1, 1, tn), lambda l, j: (l, 0, j))],
        out_specs=pl.BlockSpec((1, rows, tn), lambda l, j: (l, 0, j)),
        out_shape=jax.ShapeDtypeStruct((depth, rows, n), F32),
        compiler_params=_cparams(("parallel", "parallel")),
        name="ada_mod",
    )(cvec, ada_w, ada_b.reshape(depth, 1, n))


def _ctx_mod_spec(layer, chunk, ctx_row):
    return pl.BlockSpec((1, 1, 1, D_MODEL), lambda b, i: (layer, ctx_row, 0, chunk))


def _lat_mod_spec(layer, chunk):
    return pl.BlockSpec((1, 1, 1, D_MODEL), lambda b, i: (layer, b, 0, chunk))


def _row_groups(n_top, tm, x_refs, ctx_refs, lat_refs):
    lat = [r[0, 0] for r in lat_refs]
    if n_top == 0:
        return [(slice(0, tm), x_refs[0][0], lat)]
    first = pl.program_id(1) == 0
    top = [jnp.where(first, c[0, 0], l) for c, l in zip(ctx_refs, lat)]
    if len(x_refs) == 1:
        x_ref = x_refs[0]
        return [(slice(0, n_top), x_ref[0, :n_top, :], top), (slice(n_top, tm), x_ref[0, n_top:, :], lat)]
    ctx_ref, subs = x_refs[0], x_refs[1:]
    groups = [(slice(0, n_top), jnp.where(first, ctx_ref[0], subs[0][0]), top)]
    return groups + [(slice(k * n_top, (k + 1) * n_top), subs[k][0], lat) for k in range(1, len(subs))]


def _x_operands(xs, tm, n_ctx):
    d = xs[0].shape[-1]
    if len(xs) == 1:
        return list(xs), [pl.BlockSpec((1, tm, d), lambda b, i: (b, i, 0))]
    ctx, lat = xs
    nsub = tm // n_ctx
    assert tm % n_ctx == 0 and lat.shape[1] % n_ctx == 0
    specs = [pl.BlockSpec((1, n_ctx, d), lambda b, i: (b, 0, 0))]
    for k in range(nsub):
        specs.append(pl.BlockSpec((1, n_ctx, d), lambda b, i, k=k: (b, jnp.maximum(i * nsub + k - 1, 0), 0)))
    return [ctx] + [lat] * nsub, specs


def _inproj_kernel(*refs, outs, n_top, n_chunk, n_x):
    x_refs, (nw_ref, shc_ref, scc_ref, shl_ref, scl_ref, w_ref) = refs[:n_x], refs[n_x:n_x + 6]
    out_refs, h_scr, g_scr = refs[n_x + 6:-2], refs[-2], refs[-1]
    tm = h_scr.shape[0]
    for rows, x, (sh, sc) in _row_groups(n_top, tm, x_refs, (shc_ref, scc_ref), (shl_ref, scl_ref)):
        h_scr[rows, :] = (_rms(x, nw_ref[...]) * (1.0 + sc) + sh).astype(BF16)
    for o_ref, (c0, c1, kind) in zip(out_refs, outs):
        if kind == "row" and (c0 % LANES or c1 % LANES):
            a0, a1 = c0 // LANES * LANES, min(-(-c1 // LANES) * LANES, w_ref.shape[1])
            val = jnp.dot(h_scr[...], w_ref[:, a0:a1], preferred_element_type=F32)
            o_ref[0] = val[:, c0 - a0:c1 - a0].astype(o_ref.dtype)
            continue
        for n0 in range(c0, c1, n_chunk):
            n1 = min(n0 + n_chunk, c1)
            val = jnp.dot(h_scr[...], w_ref[:, n0:n1], preferred_element_type=F32)
            if kind == "row":
                o_ref[0, :, n0 - c0:n1 - c0] = val.astype(o_ref.dtype)
            else:
                if val.shape[1] < LANES:
                    val = jnp.concatenate([val, jnp.zeros((tm, LANES - val.shape[1]), F32)], axis=1)
                g_scr[...] = val
                o_ref[0] = g_scr[...].T[:o_ref.shape[1], :].astype(o_ref.dtype)


def _inproj(xs, norm_w, mods, layer, w_bf16, outs, tm, n_ctx):
    bsz, d = xs[0].shape[0], xs[0].shape[-1]
    t = sum(a.shape[1] for a in xs)
    n = w_bf16.shape[1]
    x_args, x_specs = _x_operands(xs, tm, n_ctx)
    assert t % tm == 0 and (n_ctx == 0 or n_ctx <= tm)
    out_specs, out_shapes = [], []
    for c0, c1, kind, dtype, wd in outs:
        if kind == "row":
            out_specs.append(pl.BlockSpec((1, tm, wd), lambda b, i: (b, i, 0)))
            out_shapes.append(jax.ShapeDtypeStruct((bsz, t, wd), dtype))
        else:
            assert c1 - c0 <= LANES and c0 % LANES == 0 and tm % LANES == 0
            out_specs.append(pl.BlockSpec((1, wd, tm), lambda b, i: (b, 0, i)))
            out_shapes.append(jax.ShapeDtypeStruct((bsz, wd, t), dtype))
    ctx_spec = lambda c: _ctx_mod_spec(layer, c, bsz)
    lat_spec = lambda c: _lat_mod_spec(layer, c)
    const = lambda shape: pl.BlockSpec(shape, lambda b, i: (0, 0), pipeline_mode=pl.Buffered(1))
    return pl.pallas_call(
        functools.partial(_inproj_kernel, outs=tuple(o[:3] for o in outs), n_top=n_ctx, n_chunk=512,
                          n_x=len(x_args)),
        grid=(bsz, t // tm),
        in_specs=x_specs + [const((1, d)), ctx_spec(0), ctx_spec(1), lat_spec(0), lat_spec(1),
                            const((d, n))],
        out_specs=out_specs,
        out_shape=out_shapes,
        scratch_shapes=[pltpu.VMEM((tm, d), BF16), pltpu.VMEM((tm, LANES), F32)],
        compiler_params=_cparams(("parallel", "parallel")),
        name=f"inproj_l{layer}",
    )(*x_args, norm_w.reshape(1, d), mods, mods, mods, mods, w_bf16)


def _ffn_kernel(x_ref, nw_ref, shc_ref, scc_ref, gc_ref, shl_ref, scl_ref, gl_ref,
                w1_ref, w3_ref, w2_ref, fw_ref, o_ref, h_scr, *, final, n_top, th):
    tm, hid = h_scr.shape[0], w1_ref.shape[1]
    groups = _row_groups(n_top, tm, (x_ref,), (shc_ref, scc_ref, gc_ref), (shl_ref, scl_ref, gl_ref))
    for rows, x, (sh, sc, _) in groups:
        h_scr[rows, :] = (_rms(x, nw_ref[...]) * (1.0 + sc) + sh).astype(BF16)
    acc = None
    for c0 in range(0, hid, th):
        h = h_scr[...]
        a = jnp.dot(h, w1_ref[:, c0:c0 + th], preferred_element_type=F32)
        g = jnp.dot(h, w3_ref[:, c0:c0 + th], preferred_element_type=F32)
        t = ((a * jax.nn.sigmoid(a)) * g).astype(BF16)
        part = jnp.dot(t, w2_ref[c0:c0 + th, :], preferred_element_type=F32)
        acc = part if acc is None else acc + part
    for rows, x, (_, _, gate) in groups:
        y = x + gate * acc[rows, :]
        if final:
            y = _rms(y, fw_ref[...])
        o_ref[0, rows, :] = y


def _ffn(x, norm_w, mods, layer, w1, w3, w2, final_w, tm, th, n_ctx, final):
    bsz, tx, d = x.shape
    hid = w1.shape[2]
    assert n_ctx <= tm and tx % tm == 0 and hid % th == 0
    ctx_spec = lambda c: _ctx_mod_spec(layer, c, bsz)
    lat_spec = lambda c: _lat_mod_spec(layer, c)
    const = lambda shape: pl.BlockSpec(shape, lambda b, i: (0, 0), pipeline_mode=pl.Buffered(1))
    layer_w = lambda shape: pl.BlockSpec((None,) + shape, lambda b, i: (layer, 0, 0),
                                         pipeline_mode=pl.Buffered(1))
    return pl.pallas_call(
        functools.partial(_ffn_kernel, final=final, n_top=n_ctx, th=th),
        grid=(bsz, tx // tm),
        in_specs=[pl.BlockSpec((1, tm, d), lambda b, i: (b, i, 0)),
                  const((1, d)),
                  ctx_spec(3), ctx_spec(4), ctx_spec(5), lat_spec(3), lat_spec(4), lat_spec(5),
                  layer_w((d, hid)), layer_w((d, hid)), layer_w((hid, d)), const((1, d))],
        out_specs=pl.BlockSpec((1, tm, d), lambda b, i: (b, i, 0)),
        out_shape=jax.ShapeDtypeStruct((bsz, tx, d), F32),
        scratch_shapes=[pltpu.VMEM((tm, d), BF16)],
        compiler_params=_cparams(("parallel", "parallel")),
        name=f"ffn_l{layer}",
    )(x, norm_w.reshape(1, d), mods, mods, mods, mods, mods, mods, w1, w3, w2, final_w.reshape(1, d))


def _chunk_masks(size=CHUNK):
    row = lax.broadcasted_iota(jnp.int32, (size, size), 0)
    col = lax.broadcasted_iota(jnp.int32, (size, size), 1)
    return row, col


def _to_col(row_vec, eye):
    return jnp.sum(jnp.where(eye, jnp.broadcast_to(row_vec, eye.shape), 0.0), axis=1, keepdims=True)


def _to_row(col_vec, eye):
    return jnp.sum(jnp.where(eye, jnp.broadcast_to(col_vec, eye.shape), 0.0), axis=0, keepdims=True)


def _scan_chunk_index(i, rev, n_chunks, ctx_chunks):
    if not rev:
        return i
    return jnp.where(i < ctx_chunks, ctx_chunks - 1 - i, n_chunks + ctx_chunks - 1 - i)


def _mlstm_kernel(bias_ref, q_ref, k_ref, v_ref, g_ref, o_ref, num_scr, cl_scr, mi_scr, fc_scr, sc_scr,
                  *, chunk, n_chunks, ctx_chunks, group):
    hd = pl.program_id(1)
    row, col = _chunk_masks(chunk)
    eye = row == col
    kscale = 1.0 / math.sqrt(HEAD_DIM)
    ones_blk = jnp.ones((chunk, HEAD_DIM), BF16)

    def gate_terms(r, ig_raw, f_raw):
        incl = (col >= row) if r == 1 else (col <= row)
        ig_row = ig_raw + bias_ref[r * MLSTM_HEADS + hd]
        lf_row = jax.nn.log_sigmoid(f_raw + bias_ref[(2 + r) * MLSTM_HEADS + hd])
        f_col = jnp.sum(jnp.where(incl, jnp.broadcast_to(lf_row, incl.shape), 0.0),
                        axis=1, keepdims=True)
        f_row = _to_row(f_col, eye)
        f_last = jnp.sum(lf_row, axis=1, keepdims=True)
        dm = jnp.where(incl, f_col - f_row + ig_row, -jnp.inf)
        m_intra = jnp.max(dm, axis=1, keepdims=True)
        w_col = f_last - f_col + _to_col(ig_row, eye)
        m_loc = jnp.max(w_col, axis=0, keepdims=True)
        return f_col, f_last, jnp.exp(dm - m_intra), m_intra, jnp.exp(w_col - m_loc), m_loc

    def local_group(gi, _):
        chunks = [gi * group + g for g in range(group)]
        loaded = []
        for c in chunks:
            rows = pl.ds(pl.multiple_of(c * chunk, chunk), chunk)
            gates = [(g_ref[0, r * MLSTM_HEADS + hd, pl.ds(c, 1), :],
                      g_ref[0, (2 + r) * MLSTM_HEADS + hd, pl.ds(c, 1), :]) for r in (0, 1)]
            v1 = jnp.concatenate([v_ref[0, rows, :].astype(BF16), ones_blk], axis=-1)
            loaded.append((q_ref[0, rows, :], k_ref[0, rows, :].astype(F32) * kscale, v1, gates))
        chains = [(g, r) for g in range(group) for r in (0, 1)]
        terms = [gate_terms(r, *loaded[g][3][r]) for g, r in chains]
        qks = [_dot_nt(q, k) for q, k, _, _ in loaded]
        c_locs = [_dot_tn(tm[4] * loaded[g][1], loaded[g][2]) for (g, r), tm in zip(chains, terms)]
        nums = [_dot(tm[2] * qks[g], loaded[g][2]) for (g, r), tm in zip(chains, terms)]
        for (g, r), tm, c_loc, num in zip(chains, terms, c_locs, nums):
            f_col, f_last, _, m_intra, _, m_loc = tm
            c = chunks[g]
            num_scr[r, c] = num
            cl_scr[r, c] = c_loc
            mi_scr[r, c] = jnp.broadcast_to(m_intra, (chunk, HEAD_DIM))
            fc_scr[r, c] = jnp.broadcast_to(f_col, (chunk, HEAD_DIM))
            sc_scr[r, c, :8, :] = jnp.broadcast_to(f_last, (8, HEAD_DIM))
            sc_scr[r, c, 8:, :] = jnp.broadcast_to(m_loc, (8, HEAD_DIM))
        return 0

    lax.fori_loop(0, n_chunks // group, local_group, 0)
    o_ref[...] = jnp.zeros_like(o_ref)

    def step(i, carry):
        cs = [_scan_chunk_index(i, r == 1, n_chunks, ctx_chunks) for r in (0, 1)]
        rows = [pl.ds(pl.multiple_of(c * chunk, chunk), chunk) for c in cs]
        qcs = [_dot(q_ref[0, rw, :], st[0]) for rw, st in zip(rows, carry)]
        new = []
        for r in (0, 1):
            s_st, m_st = carry[r]
            c = cs[r]
            mi, na = mi_scr[r, c], num_scr[r, c]
            inter = fc_scr[r, c] + m_st[0:1, :]
            m_t = jnp.maximum(mi, inter)
            a_loc_t, a_inter = jnp.exp(mi - m_t), jnp.exp(inter - m_t)
            num = a_loc_t * na[:, :HEAD_DIM] + a_inter * qcs[r][:, :HEAD_DIM]
            den = a_loc_t * na[:, HEAD_DIM:] + a_inter * qcs[r][:, HEAD_DIM:]
            out = num / jnp.maximum(jnp.abs(den), jnp.exp(-m_t))
            o_ref[0, rows[r], :] = (o_ref[0, rows[r], :].astype(F32) + out).astype(o_ref.dtype)
            f_last, m_loc = sc_scr[r, c, :8, :], sc_scr[r, c, 8:, :]
            m_new = jnp.maximum(f_last + m_st, m_loc)
            a_prev = jnp.exp(f_last + m_st - m_new)[0:1, :]
            a_loc = jnp.exp(m_loc - m_new)[0:1, :]
            a_prev = jnp.concatenate([a_prev, a_prev], axis=1)
            a_loc = jnp.concatenate([a_loc, a_loc], axis=1)
            new.append((a_prev * s_st + a_loc * cl_scr[r, c], m_new))
        return tuple(new)

    init = (jnp.zeros((HEAD_DIM, 2 * HEAD_DIM), F32), jnp.zeros((8, HEAD_DIM), F32))
    lax.fori_loop(0, n_chunks, step, (init, init), unroll=math.gcd(6, n_chunks))


def _mlstm(qkv, gates, bias, n_ctx, chunk):
    bsz, t, _ = qkv.shape
    assert t % chunk == 0 and n_ctx % chunk == 0
    n_chunks = t // chunk
    gates_row = _chunked(gates, chunk)
    qkv_spec = lambda off: pl.BlockSpec((1, t, HEAD_DIM), lambda b, h: (b, 0, off + h))
    return pl.pallas_call(
        functools.partial(_mlstm_kernel, chunk=chunk, n_chunks=n_chunks, ctx_chunks=n_ctx // chunk,
                          group=math.gcd(9 if chunk <= CHUNK else 3, n_chunks)),
        grid=(bsz, MLSTM_HEADS),
        in_specs=[pl.BlockSpec(memory_space=pltpu.SMEM),
                  qkv_spec(0), qkv_spec(MLSTM_HEADS), qkv_spec(2 * MLSTM_HEADS),
                  pl.BlockSpec((1, 4 * MLSTM_HEADS, n_chunks, chunk), lambda b, h: (b, 0, 0, 0))],
        out_specs=pl.BlockSpec((1, t, HEAD_DIM), lambda b, h: (b, 0, h)),
        out_shape=jax.ShapeDtypeStruct((bsz, t, MLSTM_WIDTH), BF16),
        scratch_shapes=[pltpu.VMEM((2, n_chunks, chunk, 2 * HEAD_DIM), F32),
                        pltpu.VMEM((2, n_chunks, HEAD_DIM, 2 * HEAD_DIM), F32),
                        pltpu.VMEM((2, n_chunks, chunk, HEAD_DIM), F32),
                        pltpu.VMEM((2, n_chunks, chunk, HEAD_DIM), F32),
                        pltpu.VMEM((2, n_chunks, 16, HEAD_DIM), F32)],
        compiler_params=_cparams(("parallel", "parallel")),
        name="mlstm",
    )(bias, qkv, qkv, qkv, gates_row)


def _s5_prep_kernel(lr_ref, li_ref, ldt_ref, br_ref, bi_ref, cr_ref, ci_ref, bd_ref, cdt_ref, a_ref):
    lr, li = lr_ref[0, 0], li_ref[0, 0]
    dt = jnp.exp(ldt_ref[0, 0])
    mag, ang = jnp.exp(lr * dt), li * dt
    ab_re, ab_im = mag * jnp.cos(ang), mag * jnp.sin(ang)
    nr, ni = ab_re - 1.0, ab_im
    den = lr * lr + li * li
    co_re = (nr * lr + ni * li) / den
    co_im = (ni * lr - nr * li) / den
    b_re, b_im = br_ref[0, 0], bi_ref[0, 0]
    bb_re = co_re * b_re - co_im * b_im
    bb_im = co_re * b_im + co_im * b_re
    c_re, c_im = cr_ref[0, 0], ci_ref[0, 0]
    lane_group = lax.broadcasted_iota(jnp.int32, (S5_GROUP, S5_HALF_N), 1) // S5_STATE
    n = S5_HALF_N
    for g in range(S5_GROUPS // S5_HALVES):
        sel = lane_group == g
        rows = slice(g * S5_GROUP, (g + 1) * S5_GROUP)
        bd_ref[0, 0, rows, :n] = jnp.where(sel, bb_re, 0.0).astype(bd_ref.dtype)
        bd_ref[0, 0, rows, n:] = jnp.where(sel, bb_im, 0.0).astype(bd_ref.dtype)
        cdt_ref[0, 0, rows, :n] = jnp.where(sel, c_re, 0.0).astype(cdt_ref.dtype)
        cdt_ref[0, 0, rows, n:] = jnp.where(sel, -c_im, 0.0).astype(cdt_ref.dtype)
    a_ref[0, 0, :, :n] = jnp.broadcast_to(ab_re, (8, n))
    a_ref[0, 0, :, n:] = jnp.broadcast_to(ab_im, (8, n))


def _s5_prep(lam_re, lam_im, log_dt, b_re, b_im, c_re, c_im):
    gh = S5_GROUPS // S5_HALVES
    vec = lambda a: a.reshape(2, S5_HALVES, 1, S5_HALF_N)
    ldt = vec(jnp.broadcast_to(log_dt[:, :, None], (2, S5_GROUPS, S5_STATE)))
    bt = lambda a: a.reshape(2, S5_HALVES, gh, S5_STATE, S5_GROUP).transpose(0, 1, 4, 2, 3).reshape(
        2, S5_HALVES, S5_GROUP, S5_HALF_N)
    ct = lambda a: a.reshape(2, S5_HALVES, gh, S5_GROUP, S5_STATE).transpose(0, 1, 3, 2, 4).reshape(
        2, S5_HALVES, S5_GROUP, S5_HALF_N)
    vspec = pl.BlockSpec((1, 1, 1, S5_HALF_N), lambda r, h: (r, h, 0, 0))
    mspec = pl.BlockSpec((1, 1, S5_GROUP, S5_HALF_N), lambda r, h: (r, h, 0, 0))
    ospec = pl.BlockSpec((1, 1, S5_HALF_W, 2 * S5_HALF_N), lambda r, h: (r, h, 0, 0))
    return pl.pallas_call(
        _s5_prep_kernel,
        grid=(2, S5_HALVES),
        in_specs=[vspec, vspec, vspec, mspec, mspec, mspec, mspec],
        out_specs=[ospec, ospec, pl.BlockSpec((1, 1, 8, 2 * S5_HALF_N), lambda r, h: (r, h, 0, 0))],
        out_shape=[jax.ShapeDtypeStruct((2, S5_HALVES, S5_HALF_W, 2 * S5_HALF_N), BF16),
                   jax.ShapeDtypeStruct((2, S5_HALVES, S5_HALF_W, 2 * S5_HALF_N), BF16),
                   jax.ShapeDtypeStruct((2, S5_HALVES, 8, 2 * S5_HALF_N), F32)],
        compiler_params=_cparams(("parallel", "parallel")),
        name="s5_prep",
    )(vec(lam_re), vec(lam_im), ldt, bt(b_re), bt(b_im), ct(c_re), ct(c_im))


def _s5_kernel(uf0_ref, uf1_ref, ub0_ref, ub1_ref, bd_ref, cdt_ref, a_ref, yf_ref, yb_ref,
               lhs_scr, bu_scr, s_scr, st_scr, *, tb, bsz):
    n = S5_HALF_N
    rows = tb * bsz
    u_refs = ((uf0_ref, uf1_ref), (ub0_ref, ub1_ref))
    y_refs = (yf_ref, yb_ref)

    @pl.when(pl.program_id(1) == 0)
    def _():
        st_scr[...] = jnp.zeros_like(st_scr)

    for d in (0, 1):
        for j in (0, 1):
            lhs_scr[d, :, j * LANES:(j + 1) * LANES] = jnp.transpose(
                u_refs[d][j][...].astype(F32), (1, 0, 2)).reshape(rows, LANES)
    for d in (0, 1):
        bu_scr[d] = _dot(lhs_scr[d], bd_ref[d, 0])
    for d in (0, 1):
        a_re, a_im = a_ref[d, 0, :, :n], a_ref[d, 0, :, n:]
        s_re, s_im = st_scr[d, :, :n], st_scr[d, :, n:]
        for j in range(tb):
            t = j if d == 0 else tb - 1 - j
            sl = slice(t * bsz, (t + 1) * bsz)
            s_re, s_im = (a_re * s_re - a_im * s_im + bu_scr[d, sl, :n],
                          a_re * s_im + a_im * s_re + bu_scr[d, sl, n:])
            s_scr[d, sl, :n] = s_re
            s_scr[d, sl, n:] = s_im
        st_scr[d, :, :n] = s_re
        st_scr[d, :, n:] = s_im
        y = _dot_nt(s_scr[d], cdt_ref[d, 0])
        for j in (0, 1):
            y_refs[d][:, :, j * LANES:(j + 1) * LANES] = jnp.transpose(
                y[:, j * LANES:(j + 1) * LANES].reshape(tb, bsz, LANES), (1, 0, 2)).astype(y_refs[d].dtype)


def _s5(u, bd, cdt, avec, n_ctx, tb):
    bsz, t, _ = u.shape
    nb, ncb = t // tb, n_ctx // tb
    assert bsz == 8 and S5_HALF_W == 2 * LANES

    def rev(i):
        return jnp.where(i < ncb, ncb - 1 - i, nb + ncb - 1 - i)

    fwd = lambda i: i
    uspec = lambda order, j: pl.BlockSpec((bsz, tb, LANES), lambda h, i: (0, order(i), 2 * h + j))
    yspec = lambda order: pl.BlockSpec((bsz, tb, S5_HALF_W), lambda h, i: (0, order(i), h))
    wspec = pl.BlockSpec((2, 1, S5_HALF_W, 2 * S5_HALF_N), lambda h, i: (0, h, 0, 0))
    rows = tb * bsz
    return pl.pallas_call(
        functools.partial(_s5_kernel, tb=tb, bsz=bsz),
        grid=(S5_HALVES, nb),
        in_specs=[uspec(fwd, 0), uspec(fwd, 1), uspec(rev, 0), uspec(rev, 1), wspec, wspec,
                  pl.BlockSpec((2, 1, 8, 2 * S5_HALF_N), lambda h, i: (0, h, 0, 0))],
        out_specs=[yspec(fwd), yspec(rev)],
        out_shape=[jax.ShapeDtypeStruct((bsz, t, S5_WIDTH), BF16)] * 2,
        scratch_shapes=[pltpu.VMEM((2, rows, S5_HALF_W), F32),
                        pltpu.VMEM((2, rows, 2 * S5_HALF_N), F32),
                        pltpu.VMEM((2, rows, 2 * S5_HALF_N), F32),
                        pltpu.VMEM((2, bsz, 2 * S5_HALF_N), F32)],
        compiler_params=_cparams(("parallel", "arbitrary")),
        name="s5_scan",
    )(u, u, u, u, bd, cdt, avec)


def _even_finish_kernel(*refs, n_top, n_x):
    x_refs = refs[:n_x]
    ha_ref, op_ref, yf_ref, yb_ref, u_ref, mhw_ref, ds_ref, wg_ref, wo_ref, gc_ref, gl_ref, o_ref = refs[n_x:]
    ha = ha_ref[0].astype(F32)
    parts = []
    for h in range(MLSTM_HEADS):
        sl = slice(h * HEAD_DIM, (h + 1) * HEAD_DIM)
        parts.append(_rms(ha[:, sl], mhw_ref[:, sl]))
    a_out = jnp.concatenate(parts, axis=-1) * jax.nn.sigmoid(op_ref[0].astype(F32))
    yb = jax.nn.gelu(yf_ref[0].astype(F32) + yb_ref[0].astype(F32) + ds_ref[...] * u_ref[0].astype(F32))
    glu = _dot(yb, wg_ref[...])
    b_out = glu[:, :S5_WIDTH] * jax.nn.sigmoid(glu[:, S5_WIDTH:])
    y = _dot(a_out, wo_ref[:MLSTM_WIDTH, :]) + _dot(b_out, wo_ref[MLSTM_WIDTH:, :])
    for rows, x, (gate,) in _row_groups(n_top, y.shape[0], x_refs, (gc_ref,), (gl_ref,)):
        o_ref[0, rows, :] = x + gate * y[rows, :]


def _even_finish(xs, ha, o_pre, yf, yb, u, mods, layer, mh_w, d_skip, wg, wo, tm, n_ctx):
    bsz, t, _ = ha.shape
    d = xs[0].shape[-1]
    assert t % tm == 0 and n_ctx <= tm
    x_args, x_specs = _x_operands(xs, tm, n_ctx)
    full = lambda shape: pl.BlockSpec(shape, lambda b, i: tuple(0 for _ in shape),
                                      pipeline_mode=pl.Buffered(1))
    return pl.pallas_call(
        functools.partial(_even_finish_kernel, n_top=n_ctx, n_x=len(x_args)),
        grid=(bsz, t // tm),
        in_specs=x_specs + [
                  pl.BlockSpec((1, tm, MLSTM_WIDTH), lambda b, i: (b, i, 0)),
                  pl.BlockSpec((1, tm, MLSTM_WIDTH), lambda b, i: (b, i, 0)),
                  pl.BlockSpec((1, tm, S5_WIDTH), lambda b, i: (b, i, 0)),
                  pl.BlockSpec((1, tm, S5_WIDTH), lambda b, i: (b, i, 0)),
                  pl.BlockSpec((1, tm, S5_WIDTH), lambda b, i: (b, i, 0)),
                  full((1, MLSTM_WIDTH)), full((1, S5_WIDTH)),
                  full((S5_WIDTH, 2 * S5_WIDTH)), full((d, d)),
                  _ctx_mod_spec(layer, 2, bsz), _lat_mod_spec(layer, 2)],
        out_specs=pl.BlockSpec((1, tm, d), lambda b, i: (b, i, 0)),
        out_shape=jax.ShapeDtypeStruct((bsz, t, d), F32),
        compiler_params=_cparams(("parallel", "parallel")),
        name="even_finish",
    )(*x_args, ha, o_pre, yf, yb, u, mh_w.reshape(1, -1), d_skip.reshape(1, -1), wg, wo, mods, mods)


def _conv_chunk(x_ref, w, lanes, c, static, n_chunks, ctx_chunks):
    tpos = lax.broadcasted_iota(jnp.int32, (CHUNK, HEAD_DIM), 0)
    first, last = tpos == 0, tpos == CHUNK - 1
    wrow = lambda k: w[k:k + 1, :]

    def rows(ci, dtype=F32):
        start = ci * CHUNK if isinstance(ci, int) else pl.multiple_of(ci * CHUNK, CHUNK)
        return x_ref[0, pl.ds(start, CHUNK), lanes].astype(dtype)

    if static and c < ctx_chunks:
        cur = rows(c)
        zero = jnp.zeros((1, HEAD_DIM), F32)
        prev_tok = rows(c - 1)[CHUNK - 1:CHUNK, :] if c > 0 else zero
        next_tok = rows(c + 1)[0:1, :] if c < ctx_chunks - 1 else zero
        left = jnp.where(first, prev_tok, pltpu.roll(cur, 1, axis=0))
        right = jnp.where(last, next_tok, pltpu.roll(cur, CHUNK - 1, axis=0))
        return wrow(3) * left + wrow(4) * cur + wrow(5) * right

    if static:
        up_i, dn_i = max(c - 1, ctx_chunks), min(c + 1, n_chunks - 1)
        w_up = w[0:3, :] if c > ctx_chunks else jnp.zeros((3, HEAD_DIM), F32)
        w_dn = w[6:9, :] if c < n_chunks - 1 else jnp.zeros((3, HEAD_DIM), F32)
    else:
        up_i, dn_i = jnp.maximum(c - 1, ctx_chunks), jnp.minimum(c + 1, n_chunks - 1)
        w_up = jnp.where(c > ctx_chunks, w[0:3, :], 0.0)
        w_dn = jnp.where(c < n_chunks - 1, w[6:9, :], 0.0)
    up, cur, dn = rows(up_i, BF16), rows(c, BF16), rows(dn_i, BF16)
    w_up, w_mid, w_dn = w_up.astype(BF16), w[3:6, :].astype(BF16), w_dn.astype(BF16)
    r0, r1, r2 = ((w_up[dc:dc + 1, :] * up + w_mid[dc:dc + 1, :] * cur + w_dn[dc:dc + 1, :] * dn).astype(F32)
                  for dc in range(3))
    prev = jnp.where(first, 0.0, pltpu.roll(r0, 1, axis=0))
    nxt = jnp.where(last, 0.0, pltpu.roll(r2, CHUNK - 1, axis=0))
    return r1 + prev + nxt


def _unit_tri_inverses(mats, eye_f):
    ps = [eye_f - a for a in mats]
    pws = [_dot(a, a) for a in mats]
    k = 2
    while 2 * k < CHUNK:
        res = [_dot(jnp.concatenate([p, pw], axis=0), pw) for p, pw in zip(ps, pws)]
        ps = [p + r[:CHUNK] for p, r in zip(ps, res)]
        pws = [r[CHUNK:] for r in res]
        k *= 2
    return [p + _dot(p, pw) for p, pw in zip(ps, pws)]


def _gdn_kernel(par_ref, q_ref, k_ref, v_ref, wq_ref, wk_ref, wv_ref, g_ref, o_ref,
                qk_scr, ol_scr, nl_scr, gl_scr, *, n_chunks, ctx_chunks, group, heads):
    row, col = _chunk_masks()
    eye = row == col
    eye_f = eye.astype(F32)

    def gate_terms(hd, r, a_pre, b_pre):
        incl = (col >= row) if r == 1 else (col <= row)
        a_log = jnp.full((1, CHUNK), par_ref[r * GDN_HEADS + hd], F32)
        dt_b = par_ref[(2 + r) * GDN_HEADS + hd]
        g_row = -jnp.exp(a_log) * jax.nn.softplus(a_pre + dt_b)
        beta_col = _to_col(jax.nn.sigmoid(b_pre), eye)
        gc_col = jnp.sum(jnp.where(incl, jnp.broadcast_to(g_row, incl.shape), 0.0),
                         axis=1, keepdims=True)
        gc_row = _to_row(gc_col, eye)
        g_last = jnp.sum(g_row, axis=1, keepdims=True)
        gamma = jnp.exp(jnp.where(incl, gc_col - gc_row, -jnp.inf))
        return beta_col, gc_col, g_last, gamma

    for hh in range(heads):
        hd = pl.program_id(1) * heads + hh
        lanes = slice(hh * HEAD_DIM, (hh + 1) * HEAD_DIM)

        def load(c, static, hd=hd, lanes=lanes):
            gates = [(g_ref[0, r * GDN_HEADS + hd, pl.ds(c, 1), :],
                      g_ref[0, (2 + r) * GDN_HEADS + hd, pl.ds(c, 1), :]) for r in (0, 1)]
            q, k, v = (_conv_chunk(x_ref, w_ref[:, lanes], lanes, c, static, n_chunks, ctx_chunks)
                       for x_ref, w_ref in ((q_ref, wq_ref), (k_ref, wk_ref), (v_ref, wv_ref)))
            q, k, v = q * jax.nn.sigmoid(q), k * jax.nn.sigmoid(k), v * jax.nn.sigmoid(v)
            q = q * (lax.rsqrt(jnp.sum(q * q, axis=-1, keepdims=True) + EPS) * (1.0 / math.sqrt(HEAD_DIM)))
            k = k * lax.rsqrt(jnp.sum(k * k, axis=-1, keepdims=True) + EPS)
            return q, k, v, gates

        def local_group(gi, _, hh=hh, hd=hd, load=load):
            static = isinstance(gi, int)
            chunks = [gi * group + g for g in range(group)]
            loaded = [load(c, static) for c in chunks]
            kqs = [_dot_nt(jnp.concatenate([k, q], axis=0), k) for q, k, _, _ in loaded]
            chains = [(g, r) for g in range(group) for r in (0, 1)]
            terms = [gate_terms(hd, r, *loaded[g][3][r]) for g, r in chains]
            a_mats = []
            for (g, r), (beta_col, _, _, gamma) in zip(chains, terms):
                strict = (col > row) if r == 1 else (col < row)
                a_mats.append(jnp.where(strict, beta_col * kqs[g][:CHUNK] * gamma, 0.0))
            t_invs = _unit_tri_inverses(a_mats, eye_f)
            sols = []
            for (g, r), (beta_col, gc_col, _, _), t_inv in zip(chains, terms, t_invs):
                _, k, v, _ = loaded[g]
                rhs = jnp.concatenate([v * beta_col, k * (beta_col * jnp.exp(gc_col))], axis=-1)
                sols.append(_dot(t_inv, rhs))
            auws = [_dot(kqs[g][CHUNK:] * tm[3], sol) for (g, r), tm, sol in zip(chains, terms, sols)]
            kuws = [_dot_tn(loaded[g][1] * jnp.exp(tm[2] - tm[1]), sol)
                    for (g, r), tm, sol in zip(chains, terms, sols)]
            for (g, r), (_, gc_col, g_last, _), auw, kuw in zip(chains, terms, auws, kuws):
                c = chunks[g]
                q = loaded[g][0]
                qk_scr[hh, r, c, :CHUNK, :] = (q * jnp.exp(gc_col) - auw[:, HEAD_DIM:]).astype(qk_scr.dtype)
                qk_scr[hh, r, c, CHUNK:, :] = kuw[:, HEAD_DIM:].astype(qk_scr.dtype)
                ol_scr[hh, r, c] = auw[:, :HEAD_DIM].astype(ol_scr.dtype)
                nl_scr[hh, r, c] = kuw[:, :HEAD_DIM].astype(nl_scr.dtype)
                gl_scr[hh, r, c] = jnp.broadcast_to(jnp.exp(g_last), (8, HEAD_DIM))
            return 0

        n_static = -(-ctx_chunks // group)
        for gi in range(n_static):
            local_group(gi, 0)
        lax.fori_loop(n_static, n_chunks // group, local_group, 0)

    o_ref[...] = jnp.zeros_like(o_ref)
    streams = [(hh, r) for hh in range(heads) for r in (0, 1)]

    def step(i, carry):
        cs = [_scan_chunk_index(i, r == 1, n_chunks, ctx_chunks) for _, r in streams]
        res = [jnp.dot(qk_scr[hh, r, c], s.astype(BF16), preferred_element_type=F32)
               for (hh, r), c, s in zip(streams, cs, carry)]
        new = []
        for (hh, r), c, s, rs in zip(streams, cs, carry, res):
            rows = pl.ds(pl.multiple_of(c * CHUNK, CHUNK), CHUNK)
            lanes = slice(hh * HEAD_DIM, (hh + 1) * HEAD_DIM)
            o_ref[0, rows, lanes] = (o_ref[0, rows, lanes].astype(F32) + ol_scr[hh, r, c]
                                     + rs[:CHUNK]).astype(o_ref.dtype)
            new.append(gl_scr[hh, r, c][0:1, :] * s - rs[CHUNK:] + nl_scr[hh, r, c])
        return tuple(new)

    init = jnp.zeros((HEAD_DIM, HEAD_DIM), F32)
    lax.fori_loop(0, n_chunks, step, (init,) * len(streams), unroll=math.gcd(6, n_chunks))


def _gdn(qkv, conv_w, gates_row, par, n_ctx, heads=4):
    bsz, t, _ = qkv.shape
    assert GRID_W == CHUNK and n_ctx % CHUNK == 0
    n_chunks = t // CHUNK
    group = math.gcd(9, n_chunks)
    wd = heads * HEAD_DIM
    blocks = GDN_HEADS // heads
    spec = lambda off: pl.BlockSpec((1, t, wd), lambda b, h: (b, 0, off + h), pipeline_mode=pl.Buffered(1))
    wspec = lambda off: pl.BlockSpec((9, wd), lambda b, h: (0, off + h))
    conv_w9 = conv_w.reshape(9, GDN_QKV)
    return pl.pallas_call(
        functools.partial(_gdn_kernel, n_chunks=n_chunks, ctx_chunks=n_ctx // CHUNK, group=group,
                          heads=heads),
        grid=(bsz, blocks),
        in_specs=[pl.BlockSpec(memory_space=pltpu.SMEM),
                  spec(0), spec(blocks), spec(2 * blocks),
                  wspec(0), wspec(blocks), wspec(2 * blocks),
                  pl.BlockSpec((1, 4 * GDN_HEADS, n_chunks, CHUNK), lambda b, h: (b, 0, 0, 0))],
        out_specs=pl.BlockSpec((1, t, wd), lambda b, h: (b, 0, h)),
        out_shape=jax.ShapeDtypeStruct((bsz, t, D_MODEL), BF16),
        scratch_shapes=[pltpu.VMEM((heads, 2, n_chunks, CHUNK + HEAD_DIM, HEAD_DIM), BF16),
                        pltpu.VMEM((heads, 2, n_chunks, CHUNK, HEAD_DIM), BF16),
                        pltpu.VMEM((heads, 2, n_chunks, HEAD_DIM, HEAD_DIM), BF16),
                        pltpu.VMEM((heads, 2, n_chunks, 8, HEAD_DIM), F32)],
        compiler_params=_cparams(("parallel", "parallel")),
        name="gdn",
    )(par, qkv, qkv, qkv, conv_w9, conv_w9, conv_w9, gates_row)


def _odd_finish_kernel(x_ref, o_in_ref, z_ref, hw_ref, wo_ref, g_ref, o_ref):
    o = o_in_ref[0].astype(F32)
    parts = []
    for h in range(GDN_HEADS):
        sl = slice(h * HEAD_DIM, (h + 1) * HEAD_DIM)
        parts.append(_rms(o[:, sl], hw_ref[:, sl]))
    z = z_ref[0].astype(F32)
    y = jnp.concatenate(parts, axis=-1) * (z * jax.nn.sigmoid(z))
    o_ref[0] = x_ref[0] + g_ref[0, 0] * _dot(y, wo_ref[...])


def _odd_finish(xc, o, z, mods, layer, hw, wo, tm, n_ctx):
    bsz, t, d = xc.shape
    assert n_ctx % tm == 0 and (t - n_ctx) % tm == 0
    t0 = n_ctx // tm
    return pl.pallas_call(
        _odd_finish_kernel,
        grid=(bsz, (t - n_ctx) // tm),
        in_specs=[pl.BlockSpec((1, tm, d), lambda b, i: (b, i + t0, 0)),
                  pl.BlockSpec((1, tm, d), lambda b, i: (b, i + t0, 0)),
                  pl.BlockSpec((1, tm, d), lambda b, i: (b, i + t0, 0)),
                  pl.BlockSpec((1, d), lambda b, i: (0, 0)),
                  pl.BlockSpec((d, d), lambda b, i: (0, 0)),
                  _lat_mod_spec(layer, 2)],
        out_specs=pl.BlockSpec((1, tm, d), lambda b, i: (b, i, 0)),
        out_shape=jax.ShapeDtypeStruct((bsz, t - n_ctx, d), F32),
        compiler_params=_cparams(("parallel", "parallel")),
        name="odd_finish",
    )(xc, o, z, hw.reshape(1, d), wo, mods)


def _chunked(gates_t, chunk=CHUNK):
    bsz, g, t = gates_t.shape
    return gates_t.reshape(bsz, g, t // chunk, chunk)


def _pick_tile(total, target, at_least, multiple=8):
    for tile in range(min(target, total), multiple - 1, -1):
        if total % tile == 0 and tile % multiple == 0 and tile >= at_least:
            return tile
    raise ValueError(f"no row tile for {total} rows")


def kernel(x, c, ctx, c_ctx, ada_w, ada_b, norm1_w, norm2_w, ffn_w1, ffn_w3, ffn_w2, final_norm_w,
           ev_w_in, ev_i_bias, ev_f_bias, ev_head_norm_w, ev_lam_re, ev_lam_im, ev_log_dt,
           ev_b_re, ev_b_im, ev_c_re, ev_c_im, ev_d, ev_w_glu, ev_w_out,
           od_w_in, od_conv_w, od_a_log, od_dt_bias, od_head_norm_w, od_w_out):
    bsz, seq, d = x.shape
    n_ctx = ctx.shape[1]
    assert d == D_MODEL and seq % GRID_W == 0 and n_ctx % CHUNK == 0 and bsz % 8 == 0
    assert ada_w.shape[0] == 2
    t_all = n_ctx + seq
    tm = _pick_tile(t_all, 768, n_ctx, LANES)
    tm_lat = _pick_tile(math.gcd(seq, n_ctx), 512, 0)
    th = FFN_HIDDEN // 2
    tb = 2 * CHUNK if n_ctx % (2 * CHUNK) == 0 and seq % (2 * CHUNK) == 0 else CHUNK

    if tm % n_ctx == 0 and seq % n_ctx == 0:
        xs = (ctx, x)
    else:
        xs = (jnp.concatenate([ctx, x], axis=1),)
    mod_rows = 16
    cvec = jnp.zeros((mod_rows, d), F32).at[:bsz].set(c).at[bsz].set(c_ctx)
    mods = _ada(cvec, ada_w, ada_b).reshape(2, mod_rows, 1, 6 * d)
    w1, w3, w2 = ffn_w1.astype(BF16), ffn_w3.astype(BF16), ffn_w2.astype(BF16)

    qkv_w, n_g = 3 * MLSTM_WIDTH, 4 * MLSTM_HEADS
    w_in = ev_w_in[0]
    g0 = qkv_w + MLSTM_WIDTH
    n_even = w_in.shape[1]
    qkv, o_pre, u, gates = _inproj(
        xs, norm1_w[0], mods, 0, w_in.astype(BF16),
        [(0, qkv_w, "row", BF16, qkv_w), (qkv_w, g0, "row", BF16, g0 - qkv_w),
         (g0 + n_g, n_even, "row", BF16, S5_WIDTH), (g0, g0 + n_g, "gate", F32, n_g)], tm, n_ctx)
    bias = jnp.concatenate([ev_i_bias[0].reshape(-1), ev_f_bias[0].reshape(-1)]).astype(F32)
    mlstm_chunk = MLSTM_CHUNK if n_ctx % MLSTM_CHUNK == 0 and seq % MLSTM_CHUNK == 0 else CHUNK
    ha = _mlstm(qkv, gates, bias, n_ctx, mlstm_chunk)
    bd, cdt, avec = _s5_prep(ev_lam_re[0], ev_lam_im[0], ev_log_dt[0], ev_b_re[0], ev_b_im[0],
                             ev_c_re[0], ev_c_im[0])
    yf, yb = _s5(u, bd, cdt, avec, n_ctx, tb)
    xc = _even_finish(xs, ha, o_pre, yf, yb, u, mods, 0, ev_head_norm_w[0], ev_d[0],
                      ev_w_glu[0].astype(BF16), ev_w_out[0].astype(BF16), tm, n_ctx)
    xc = _ffn(xc, norm2_w[0], mods, 0, w1, w3, w2, final_norm_w, tm, th, n_ctx, False)

    n_g = 4 * GDN_HEADS
    qkv_pre, z, gates = _inproj(
        (xc,), norm1_w[1], mods, 1, od_w_in[0].astype(BF16),
        [(0, GDN_QKV, "row", BF16, GDN_QKV), (GDN_QKV, 4 * d, "row", BF16, d),
         (4 * d, 4 * d + n_g, "gate", F32, n_g)], tm, n_ctx)
    par = jnp.concatenate([od_a_log[0].reshape(-1), od_dt_bias[0].reshape(-1)]).astype(F32)
    o = _gdn(qkv_pre, od_conv_w[0], _chunked(gates), par, n_ctx)
    xl = _odd_finish(xc, o, z, mods, 1, od_head_norm_w[0], od_w_out[0].astype(BF16), tm_lat, n_ctx)
    return _ffn(xl, norm2_w[1], mods, 1, w1, w3, w2, final_norm_w, _pick_tile(seq, 512, 0),
                th, 0, True)
```

```python
import functools
import math

import jax
import jax.numpy as jnp
from jax import lax
from jax.experimental import pallas as pl
from jax.experimental.pallas import tpu as pltpu

F32 = jnp.float32
BF16 = jnp.bfloat16

D_MODEL = 1024
CHUNK = 64
MLSTM_CHUNK = 128
GRID_W = 64
EPS = 1e-6
LANES = 128
HEAD_DIM = 128
MLSTM_WIDTH = D_MODEL // 2
MLSTM_HEADS = MLSTM_WIDTH // HEAD_DIM
S5_WIDTH = D_MODEL - MLSTM_WIDTH
S5_GROUP = 16
S5_GROUPS = S5_WIDTH // S5_GROUP
S5_STATE = 64
S5_HALVES = 2
S5_HALF_W = S5_WIDTH // S5_HALVES
S5_HALF_N = (S5_GROUPS // S5_HALVES) * S5_STATE
GDN_HEADS = D_MODEL // HEAD_DIM
GDN_QKV = 3 * D_MODEL
FFN_HIDDEN = ((8 * D_MODEL + 3 * 256 - 1) // (3 * 256)) * 256
VMEM_LIMIT = 56 * 1024 * 1024

_NT = (((1,), (1,)), ((), ()))
_TN = (((0,), (0,)), ((), ()))


def _cparams(sem):
    return pltpu.CompilerParams(dimension_semantics=sem, vmem_limit_bytes=VMEM_LIMIT)


def _dot(a, b):
    return jnp.dot(a.astype(BF16), b.astype(BF16), preferred_element_type=F32)


def _dot_nt(a, b):
    return lax.dot_general(a.astype(BF16), b.astype(BF16), _NT, preferred_element_type=F32)


def _dot_tn(a, b):
    return lax.dot_general(a.astype(BF16), b.astype(BF16), _TN, preferred_element_type=F32)


def _rms(x, w):
    return x * lax.rsqrt(jnp.mean(x * x, axis=-1, keepdims=True) + EPS) * w


def _ada_kernel(s_ref, w_ref, b_ref, o_ref):
    s = s_ref[...]
    s = s * jax.nn.sigmoid(s)
    o_ref[0] = _dot(s, w_ref[0]) + b_ref[0]


def _ada(cvec, ada_w, ada_b):
    depth, d, n = ada_w.shape
    tn = 1536
    rows = cvec.shape[0]
    return pl.pallas_call(
        _ada_kernel,
        grid=(depth, n // tn),
        in_specs=[pl.BlockSpec((rows, d), lambda l, j: (0, 0)),
                  pl.BlockSpec((1, d, tn), lambda l, j: (l, 0, j)),
                  pl.BlockSpec((1, 1, tn), lambda l, j: (l, 0, j))],
        out_specs=pl.BlockSpec((1, rows, tn), lambda l, j: (l, 0, j)),
        out_shape=jax.ShapeDtypeStruct((depth, rows, n), F32),
        compiler_params=_cparams(("parallel", "parallel")),
        name="ada_mod",
    )(cvec, ada_w, ada_b.reshape(depth, 1, n))


def _ctx_mod_spec(layer, chunk, ctx_row):
    return pl.BlockSpec((1, 1, 1, D_MODEL), lambda b, i: (layer, ctx_row, 0, chunk))


def _lat_mod_spec(layer, chunk):
    return pl.BlockSpec((1, 1, 1, D_MODEL), lambda b, i: (layer, b, 0, chunk))


def _row_groups(n_top, tm, x_refs, ctx_refs, lat_refs):
    lat = [r[0, 0] for r in lat_refs]
    if n_top == 0:
        return [(slice(0, tm), x_refs[0][0], lat)]
    first = pl.program_id(1) == 0
    top = [jnp.where(first, c[0, 0], l) for c, l in zip(ctx_refs, lat)]
    if len(x_refs) == 1:
        x_ref = x_refs[0]
        return [(slice(0, n_top), x_ref[0, :n_top, :], top), (slice(n_top, tm), x_ref[0, n_top:, :], lat)]
    ctx_ref, subs = x_refs[0], x_refs[1:]
    groups = [(slice(0, n_top), jnp.where(first, ctx_ref[0], subs[0][0]), top)]
    return groups + [(slice(k * n_top, (k + 1) * n_top), subs[k][0], lat) for k in range(1, len(subs))]


def _x_operands(xs, tm, n_ctx):
    d = xs[0].shape[-1]
    if len(xs) == 1:
        return list(xs), [pl.BlockSpec((1, tm, d), lambda b, i: (b, i, 0))]
    ctx, lat = xs
    nsub = tm // n_ctx
    assert tm % n_ctx == 0 and lat.shape[1] % n_ctx == 0
    specs = [pl.BlockSpec((1, n_ctx, d), lambda b, i: (b, 0, 0))]
    for k in range(nsub):
        specs.append(pl.BlockSpec((1, n_ctx, d), lambda b, i, k=k: (b, jnp.maximum(i * nsub + k - 1, 0), 0)))
    return [ctx] + [lat] * nsub, specs


def _inproj_kernel(*refs, outs, n_top, n_chunk, n_x):
    x_refs, (nw_ref, shc_ref, scc_ref, shl_ref, scl_ref, w_ref) = refs[:n_x], refs[n_x:n_x + 6]
    out_refs, h_scr, g_scr = refs[n_x + 6:-2], refs[-2], refs[-1]
    tm = h_scr.shape[0]
    for rows, x, (sh, sc) in _row_groups(n_top, tm, x_refs, (shc_ref, scc_ref), (shl_ref, scl_ref)):
        h_scr[rows, :] = (_rms(x, nw_ref[...]) * (1.0 + sc) + sh).astype(BF16)
    for o_ref, (c0, c1, kind) in zip(out_refs, outs):
        if kind == "row" and (c0 % LANES or c1 % LANES):
            a0, a1 = c0 // LANES * LANES, min(-(-c1 // LANES) * LANES, w_ref.shape[1])
            val = jnp.dot(h_scr[...], w_ref[:, a0:a1], preferred_element_type=F32)
            o_ref[0] = val[:, c0 - a0:c1 - a0].astype(o_ref.dtype)
            continue
        for n0 in range(c0, c1, n_chunk):
            n1 = min(n0 + n_chunk, c1)
            val = jnp.dot(h_scr[...], w_ref[:, n0:n1], preferred_element_type=F32)
            if kind == "row":
                o_ref[0, :, n0 - c0:n1 - c0] = val.astype(o_ref.dtype)
            else:
                if val.shape[1] < LANES:
                    val = jnp.concatenate([val, jnp.zeros((tm, LANES - val.shape[1]), F32)], axis=1)
                g_scr[...] = val
                o_ref[0] = g_scr[...].T[:o_ref.shape[1], :].astype(o_ref.dtype)


def _inproj(xs, norm_w, mods, layer, w_bf16, outs, tm, n_ctx):
    bsz, d = xs[0].shape[0], xs[0].shape[-1]
    t = sum(a.shape[1] for a in xs)
    n = w_bf16.shape[1]
    x_args, x_specs = _x_operands(xs, tm, n_ctx)
    assert t % tm == 0 and (n_ctx == 0 or n_ctx <= tm)
    out_specs, out_shapes = [], []
    for c0, c1, kind, dtype, wd in outs:
        if kind == "row":
            out_specs.append(pl.BlockSpec((1, tm, wd), lambda b, i: (b, i, 0)))
            out_shapes.append(jax.ShapeDtypeStruct((bsz, t, wd), dtype))
        else:
            assert c1 - c0 <= LANES and c0 % LANES == 0 and tm % LANES == 0
            out_specs.append(pl.BlockSpec((1, wd, tm), lambda b, i: (b, 0, i)))
            out_shapes.append(jax.ShapeDtypeStruct((bsz, wd, t), dtype))
    ctx_spec = lambda c: _ctx_mod_spec(layer, c, bsz)
    lat_spec = lambda c: _lat_mod_spec(layer, c)
    const = lambda shape: pl.BlockSpec(shape, lambda b, i: (0, 0), pipeline_mode=pl.Buffered(1))
    return pl.pallas_call(
        functools.partial(_inproj_kernel, outs=tuple(o[:3] for o in outs), n_top=n_ctx, n_chunk=512,
                          n_x=len(x_args)),
        grid=(bsz, t // tm),
        in_specs=x_specs + [const((1, d)), ctx_spec(0), ctx_spec(1), lat_spec(0), lat_spec(1),
                            const((d, n))],
        out_specs=out_specs,
        out_shape=out_shapes,
        scratch_shapes=[pltpu.VMEM((tm, d), BF16), pltpu.VMEM((tm, LANES), F32)],
        compiler_params=_cparams(("parallel", "parallel")),
        name=f"inproj_l{layer}",
    )(*x_args, norm_w.reshape(1, d), mods, mods, mods, mods, w_bf16)


def _ffn_kernel(x_ref, nw_ref, shc_ref, scc_ref, gc_ref, shl_ref, scl_ref, gl_ref,
                w1_ref, w3_ref, w2_ref, fw_ref, o_ref, h_scr, *, final, n_top, th):
    tm, hid = h_scr.shape[0], w1_ref.shape[1]
    groups = _row_groups(n_top, tm, (x_ref,), (shc_ref, scc_ref, gc_ref), (shl_ref, scl_ref, gl_ref))
    for rows, x, (sh, sc, _) in groups:
        h_scr[rows, :] = (_rms(x, nw_ref[...]) * (1.0 + sc) + sh).astype(BF16)
    acc = None
    for c0 in range(0, hid, th):
        h = h_scr[...]
        a = jnp.dot(h, w1_ref[:, c0:c0 + th], preferred_element_type=F32)
        g = jnp.dot(h, w3_ref[:, c0:c0 + th], preferred_element_type=F32)
        t = ((a * jax.nn.sigmoid(a)) * g).astype(BF16)
        part = jnp.dot(t, w2_ref[c0:c0 + th, :], preferred_element_type=F32)
        acc = part if acc is None else acc + part
    for rows, x, (_, _, gate) in groups:
        y = x + gate * acc[rows, :]
        if final:
            y = _rms(y, fw_ref[...])
        o_ref[0, rows, :] = y


def _ffn(x, norm_w, mods, layer, w1, w3, w2, final_w, tm, th, n_ctx, final):
    bsz, tx, d = x.shape
    hid = w1.shape[2]
    assert n_ctx <= tm and tx % tm == 0 and hid % th == 0
    ctx_spec = lambda c: _ctx_mod_spec(layer, c, bsz)
    lat_spec = lambda c: _lat_mod_spec(layer, c)
    const = lambda shape: pl.BlockSpec(shape, lambda b, i: (0, 0), pipeline_mode=pl.Buffered(1))
    layer_w = lambda shape: pl.BlockSpec((None,) + shape, lambda b, i: (layer, 0, 0),
                                         pipeline_mode=pl.Buffered(1))
    return pl.pallas_call(
        functools.partial(_ffn_kernel, final=final, n_top=n_ctx, th=th),
        grid=(bsz, tx // tm),
        in_specs=[pl.BlockSpec((1, tm, d), lambda b, i: (b, i, 0)),
                  const((1, d)),
                  ctx_spec(3), ctx_spec(4), ctx_spec(5), lat_spec(3), lat_spec(4), lat_spec(5),
                  layer_w((d, hid)), layer_w((d, hid)), layer_w((hid, d)), const((1, d))],
        out_specs=pl.BlockSpec((1, tm, d), lambda b, i: (b, i, 0)),
        out_shape=jax.ShapeDtypeStruct((bsz, tx, d), F32),
        scratch_shapes=[pltpu.VMEM((tm, d), BF16)],
        compiler_params=_cparams(("parallel", "parallel")),
        name=f"ffn_l{layer}",
    )(x, norm_w.reshape(1, d), mods, mods, mods, mods, mods, mods, w1, w3, w2, final_w.reshape(1, d))


def _chunk_masks(size=CHUNK):
    row = lax.broadcasted_iota(jnp.int32, (size, size), 0)
    col = lax.broadcasted_iota(jnp.int32, (size, size), 1)
    return row, col


def _to_col(row_vec, eye):
    return jnp.sum(jnp.where(eye, jnp.broadcast_to(row_vec, eye.shape), 0.0), axis=1, keepdims=True)


def _to_row(col_vec, eye):
    return jnp.sum(jnp.where(eye, jnp.broadcast_to(col_vec, eye.shape), 0.0), axis=0, keepdims=True)


def _scan_chunk_index(i, rev, n_chunks, ctx_chunks):
    if not rev:
        return i
    return jnp.where(i < ctx_chunks, ctx_chunks - 1 - i, n_chunks + ctx_chunks - 1 - i)


def _mlstm_kernel(bias_ref, q_ref, k_ref, v_ref, g_ref, o_ref, num_scr, cl_scr, mi_scr, fc_scr, sc_scr,
                  *, chunk, n_chunks, ctx_chunks, group):
    hd = pl.program_id(1)
    row, col = _chunk_masks(chunk)
    eye = row == col
    kscale = 1.0 / math.sqrt(HEAD_DIM)
    ones_blk = jnp.ones((chunk, HEAD_DIM), BF16)

    def gate_terms(r, ig_raw, f_raw):
        incl = (col >= row) if r == 1 else (col <= row)
        ig_row = ig_raw + bias_ref[r * MLSTM_HEADS + hd]
        lf_row = jax.nn.log_sigmoid(f_raw + bias_ref[(2 + r) * MLSTM_HEADS + hd])
        f_col = jnp.sum(jnp.where(incl, jnp.broadcast_to(lf_row, incl.shape), 0.0),
                        axis=1, keepdims=True)
        f_row = _to_row(f_col, eye)
        f_last = jnp.sum(lf_row, axis=1, keepdims=True)
        dm = jnp.where(incl, f_col - f_row + ig_row, -jnp.inf)
        m_intra = jnp.max(dm, axis=1, keepdims=True)
        w_col = f_last - f_col + _to_col(ig_row, eye)
        m_loc = jnp.max(w_col, axis=0, keepdims=True)
        return f_col, f_last, jnp.exp(dm - m_intra), m_intra, jnp.exp(w_col - m_loc), m_loc

    def local_group(gi, _):
        chunks = [gi * group + g for g in range(group)]
        loaded = []
        for c in chunks:
            rows = pl.ds(pl.multiple_of(c * chunk, chunk), chunk)
            gates = [(g_ref[0, r * MLSTM_HEADS + hd, pl.ds(c, 1), :],
                      g_ref[0, (2 + r) * MLSTM_HEADS + hd, pl.ds(c, 1), :]) for r in (0, 1)]
            v1 = jnp.concatenate([v_ref[0, rows, :].astype(BF16), ones_blk], axis=-1)
            loaded.append((q_ref[0, rows, :], k_ref[0, rows, :].astype(F32) * kscale, v1, gates))
        chains = [(g, r) for g in range(group) for r in (0, 1)]
        terms = [gate_terms(r, *loaded[g][3][r]) for g, r in chains]
        qks = [_dot_nt(q, k) for q, k, _, _ in loaded]
        c_locs = [_dot_tn(tm[4] * loaded[g][1], loaded[g][2]) for (g, r), tm in zip(chains, terms)]
        nums = [_dot(tm[2] * qks[g], loaded[g][2]) for (g, r), tm in zip(chains, terms)]
        for (g, r), tm, c_loc, num in zip(chains, terms, c_locs, nums):
            f_col, f_last, _, m_intra, _, m_loc = tm
            c = chunks[g]
            num_scr[r, c] = num
            cl_scr[r, c] = c_loc
            mi_scr[r, c] = jnp.broadcast_to(m_intra, (chunk, HEAD_DIM))
            fc_scr[r, c] = jnp.broadcast_to(f_col, (chunk, HEAD_DIM))
            sc_scr[r, c, :8, :] = jnp.broadcast_to(f_last, (8, HEAD_DIM))
            sc_scr[r, c, 8:, :] = jnp.broadcast_to(m_loc, (8, HEAD_DIM))
        return 0

    lax.fori_loop(0, n_chunks // group, local_group, 0)
    o_ref[...] = jnp.zeros_like(o_ref)

    def step(i, carry):
        cs = [_scan_chunk_index(i, r == 1, n_chunks, ctx_chunks) for r in (0, 1)]
        rows = [pl.ds(pl.multiple_of(c * chunk, chunk), chunk) for c in cs]
        qcs = [_dot(q_ref[0, rw, :], st[0]) for rw, st in zip(rows, carry)]
        new = []
        for r in (0, 1):
            s_st, m_st = carry[r]
            c = cs[r]
            mi, na = mi_scr[r, c], num_scr[r, c]
            inter = fc_scr[r, c] + m_st[0:1, :]
            m_t = jnp.maximum(mi, inter)
            a_loc_t, a_inter = jnp.exp(mi - m_t), jnp.exp(inter - m_t)
            num = a_loc_t * na[:, :HEAD_DIM] + a_inter * qcs[r][:, :HEAD_DIM]
            den = a_loc_t * na[:, HEAD_DIM:] + a_inter * qcs[r][:, HEAD_DIM:]
            out = num / jnp.maximum(jnp.abs(den), jnp.exp(-m_t))
            o_ref[0, rows[r], :] = (o_ref[0, rows[r], :].astype(F32) + out).astype(o_ref.dtype)
            f_last, m_loc = sc_scr[r, c, :8, :], sc_scr[r, c, 8:, :]
            m_new = jnp.maximum(f_last + m_st, m_loc)
            a_prev = jnp.exp(f_last + m_st - m_new)[0:1, :]
            a_loc = jnp.exp(m_loc - m_new)[0:1, :]
            a_prev = jnp.concatenate([a_prev, a_prev], axis=1)
            a_loc = jnp.concatenate([a_loc, a_loc], axis=1)
            new.append((a_prev * s_st + a_loc * cl_scr[r, c], m_new))
        return tuple(new)

    init = (jnp.zeros((HEAD_DIM, 2 * HEAD_DIM), F32), jnp.zeros((8, HEAD_DIM), F32))
    lax.fori_loop(0, n_chunks, step, (init, init), unroll=math.gcd(6, n_chunks))


def _mlstm(qkv, gates, bias, n_ctx, chunk):
    bsz, t, _ = qkv.shape
    assert t % chunk == 0 and n_ctx % chunk == 0
    n_chunks = t // chunk
    gates_row = _chunked(gates, chunk)
    qkv_spec = lambda off: pl.BlockSpec((1, t, HEAD_DIM), lambda b, h: (b, 0, off + h))
    return pl.pallas_call(
        functools.partial(_mlstm_kernel, chunk=chunk, n_chunks=n_chunks, ctx_chunks=n_ctx // chunk,
                          group=math.gcd(9 if chunk <= CHUNK else 3, n_chunks)),
        grid=(bsz, MLSTM_HEADS),
        in_specs=[pl.BlockSpec(memory_space=pltpu.SMEM),
                  qkv_spec(0), qkv_spec(MLSTM_HEADS), qkv_spec(2 * MLSTM_HEADS),
                  pl.BlockSpec((1, 4 * MLSTM_HEADS, n_chunks, chunk), lambda b, h: (b, 0, 0, 0))],
        out_specs=pl.BlockSpec((1, t, HEAD_DIM), lambda b, h: (b, 0, h)),
        out_shape=jax.ShapeDtypeStruct((bsz, t, MLSTM_WIDTH), BF16),
        scratch_shapes=[pltpu.VMEM((2, n_chunks, chunk, 2 * HEAD_DIM), F32),
                        pltpu.VMEM((2, n_chunks, HEAD_DIM, 2 * HEAD_DIM), F32),
                        pltpu.VMEM((2, n_chunks, chunk, HEAD_DIM), F32),
                        pltpu.VMEM((2, n_chunks, chunk, HEAD_DIM), F32),
                        pltpu.VMEM((2, n_chunks, 16, HEAD_DIM), F32)],
        compiler_params=_cparams(("parallel", "parallel")),
        name="mlstm",
    )(bias, qkv, qkv, qkv, gates_row)


def _s5_prep_kernel(lr_ref, li_ref, ldt_ref, br_ref, bi_ref, cr_ref, ci_ref, bd_ref, cdt_ref, a_ref):
    lr, li = lr_ref[0, 0], li_ref[0, 0]
    dt = jnp.exp(ldt_ref[0, 0])
    mag, ang = jnp.exp(lr * dt), li * dt
    ab_re, ab_im = mag * jnp.cos(ang), mag * jnp.sin(ang)
    nr, ni = ab_re - 1.0, ab_im
    den = lr * lr + li * li
    co_re = (nr * lr + ni * li) / den
    co_im = (ni * lr - nr * li) / den
    b_re, b_im = br_ref[0, 0], bi_ref[0, 0]
    bb_re = co_re * b_re - co_im * b_im
    bb_im = co_re * b_im + co_im * b_re
    c_re, c_im = cr_ref[0, 0], ci_ref[0, 0]
    lane_group = lax.broadcasted_iota(jnp.int32, (S5_GROUP, S5_HALF_N), 1) // S5_STATE
    n = S5_HALF_N
    for g in range(S5_GROUPS // S5_HALVES):
        sel = lane_group == g
        rows = slice(g * S5_GROUP, (g + 1) * S5_GROUP)
        bd_ref[0, 0, rows, :n] = jnp.where(sel, bb_re, 0.0).astype(bd_ref.dtype)
        bd_ref[0, 0, rows, n:] = jnp.where(sel, bb_im, 0.0).astype(bd_ref.dtype)
        cdt_ref[0, 0, rows, :n] = jnp.where(sel, c_re, 0.0).astype(cdt_ref.dtype)
        cdt_ref[0, 0, rows, n:] = jnp.where(sel, -c_im, 0.0).astype(cdt_ref.dtype)
    a_ref[0, 0, :, :n] = jnp.broadcast_to(ab_re, (8, n))
    a_ref[0, 0, :, n:] = jnp.broadcast_to(ab_im, (8, n))


def _s5_prep(lam_re, lam_im, log_dt, b_re, b_im, c_re, c_im):
    gh = S5_GROUPS // S5_HALVES
    vec = lambda a: a.reshape(2, S5_HALVES, 1, S5_HALF_N)
    ldt = vec(jnp.broadcast_to(log_dt[:, :, None], (2, S5_GROUPS, S5_STATE)))
    bt = lambda a: a.reshape(2, S5_HALVES, gh, S5_STATE, S5_GROUP).transpose(0, 1, 4, 2, 3).reshape(
        2, S5_HALVES, S5_GROUP, S5_HALF_N)
    ct = lambda a: a.reshape(2, S5_HALVES, gh, S5_GROUP, S5_STATE).transpose(0, 1, 3, 2, 4).reshape(
        2, S5_HALVES, S5_GROUP, S5_HALF_N)
    vspec = pl.BlockSpec((1, 1, 1, S5_HALF_N), lambda r, h: (r, h, 0, 0))
    mspec = pl.BlockSpec((1, 1, S5_GROUP, S5_HALF_N), lambda r, h: (r, h, 0, 0))
    ospec = pl.BlockSpec((1, 1, S5_HALF_W, 2 * S5_HALF_N), lambda r, h: (r, h, 0, 0))
    return pl.pallas_call(
        _s5_prep_kernel,
        grid=(2, S5_HALVES),
        in_specs=[vspec, vspec, vspec, mspec, mspec, mspec, mspec],
        out_specs=[ospec, ospec, pl.BlockSpec((1, 1, 8, 2 * S5_HALF_N), lambda r, h: (r, h, 0, 0))],
        out_shape=[jax.ShapeDtypeStruct((2, S5_HALVES, S5_HALF_W, 2 * S5_HALF_N), BF16),
                   jax.ShapeDtypeStruct((2, S5_HALVES, S5_HALF_W, 2 * S5_HALF_N), BF16),
                   jax.ShapeDtypeStruct((2, S5_HALVES, 8, 2 * S5_HALF_N), F32)],
        compiler_params=_cparams(("parallel", "parallel")),
        name="s5_prep",
    )(vec(lam_re), vec(lam_im), ldt, bt(b_re), bt(b_im), ct(c_re), ct(c_im))


def _s5_kernel(uf0_ref, uf1_ref, ub0_ref, ub1_ref, bd_ref, cdt_ref, a_ref, yf_ref, yb_ref,
               lhs_scr, bu_scr, s_scr, st_scr, *, tb, bsz):
    n = S5_HALF_N
    rows = tb * bsz
    u_refs = ((uf0_ref, uf1_ref), (ub0_ref, ub1_ref))
    y_refs = (yf_ref, yb_ref)

    @pl.when(pl.program_id(1) == 0)
    def _():
        st_scr[...] = jnp.zeros_like(st_scr)

    for d in (0, 1):
        for j in (0, 1):
            lhs_scr[d, :, j * LANES:(j + 1) * LANES] = jnp.transpose(
                u_refs[d][j][...].astype(F32), (1, 0, 2)).reshape(rows, LANES)
    for d in (0, 1):
        bu_scr[d] = _dot(lhs_scr[d], bd_ref[d, 0])
    for d in (0, 1):
        a_re, a_im = a_ref[d, 0, :, :n], a_ref[d, 0, :, n:]
        s_re, s_im = st_scr[d, :, :n], st_scr[d, :, n:]
        for j in range(tb):
            t = j if d == 0 else tb - 1 - j
            sl = slice(t * bsz, (t + 1) * bsz)
            s_re, s_im = (a_re * s_re - a_im * s_im + bu_scr[d, sl, :n],
                          a_re * s_im + a_im * s_re + bu_scr[d, sl, n:])
            s_scr[d, sl, :n] = s_re
            s_scr[d, sl, n:] = s_im
        st_scr[d, :, :n] = s_re
        st_scr[d, :, n:] = s_im
        y = _dot_nt(s_scr[d], cdt_ref[d, 0])
        for j in (0, 1):
            y_refs[d][:, :, j * LANES:(j + 1) * LANES] = jnp.transpose(
                y[:, j * LANES:(j + 1) * LANES].reshape(tb, bsz, LANES), (1, 0, 2)).astype(y_refs[d].dtype)


def _s5(u, bd, cdt, avec, n_ctx, tb):
    bsz, t, _ = u.shape
    nb, ncb = t // tb, n_ctx // tb
    assert bsz == 8 and S5_HALF_W == 2 * LANES

    def rev(i):
        return jnp.where(i < ncb, ncb - 1 - i, nb + ncb - 1 - i)

    fwd = lambda i: i
    uspec = lambda order, j: pl.BlockSpec((bsz, tb, LANES), lambda h, i: (0, order(i), 2 * h + j))
    yspec = lambda order: pl.BlockSpec((bsz, tb, S5_HALF_W), lambda h, i: (0, order(i), h))
    wspec = pl.BlockSpec((2, 1, S5_HALF_W, 2 * S5_HALF_N), lambda h, i: (0, h, 0, 0))
    rows = tb * bsz
    return pl.pallas_call(
        functools.partial(_s5_kernel, tb=tb, bsz=bsz),
        grid=(S5_HALVES, nb),
        in_specs=[uspec(fwd, 0), uspec(fwd, 1), uspec(rev, 0), uspec(rev, 1), wspec, wspec,
                  pl.BlockSpec((2, 1, 8, 2 * S5_HALF_N), lambda h, i: (0, h, 0, 0))],
        out_specs=[yspec(fwd), yspec(rev)],
        out_shape=[jax.ShapeDtypeStruct((bsz, t, S5_WIDTH), BF16)] * 2,
        scratch_shapes=[pltpu.VMEM((2, rows, S5_HALF_W), F32),
                        pltpu.VMEM((2, rows, 2 * S5_HALF_N), F32),
                        pltpu.VMEM((2, rows, 2 * S5_HALF_N), F32),
                        pltpu.VMEM((2, bsz, 2 * S5_HALF_N), F32)],
        compiler_params=_cparams(("parallel", "arbitrary")),
        name="s5_scan",
    )(u, u, u, u, bd, cdt, avec)


def _even_finish_kernel(*refs, n_top, n_x):
    x_refs = refs[:n_x]
    ha_ref, op_ref, yf_ref, yb_ref, u_ref, mhw_ref, ds_ref, wg_ref, wo_ref, gc_ref, gl_ref, o_ref = refs[n_x:]
    ha = ha_ref[0].astype(F32)
    parts = []
    for h in range(MLSTM_HEADS):
        sl = slice(h * HEAD_DIM, (h + 1) * HEAD_DIM)
        parts.append(_rms(ha[:, sl], mhw_ref[:, sl]))
    a_out = jnp.concatenate(parts, axis=-1) * jax.nn.sigmoid(op_ref[0].astype(F32))
    yb = jax.nn.gelu(yf_ref[0].astype(F32) + yb_ref[0].astype(F32) + ds_ref[...] * u_ref[0].astype(F32))
    glu = _dot(yb, wg_ref[...])
    b_out = glu[:, :S5_WIDTH] * jax.nn.sigmoid(glu[:, S5_WIDTH:])
    y = _dot(a_out, wo_ref[:MLSTM_WIDTH, :]) + _dot(b_out, wo_ref[MLSTM_WIDTH:, :])
    for rows, x, (gate,) in _row_groups(n_top, y.shape[0], x_refs, (gc_ref,), (gl_ref,)):
        o_ref[0, rows, :] = x + gate * y[rows, :]


def _even_finish(xs, ha, o_pre, yf, yb, u, mods, layer, mh_w, d_skip, wg, wo, tm, n_ctx):
    bsz, t, _ = ha.shape
    d = xs[0].shape[-1]
    assert t % tm == 0 and n_ctx <= tm
    x_args, x_specs = _x_operands(xs, tm, n_ctx)
    full = lambda shape: pl.BlockSpec(shape, lambda b, i: tuple(0 for _ in shape),
                                      pipeline_mode=pl.Buffered(1))
    return pl.pallas_call(
        functools.partial(_even_finish_kernel, n_top=n_ctx, n_x=len(x_args)),
        grid=(bsz, t // tm),
        in_specs=x_specs + [
                  pl.BlockSpec((1, tm, MLSTM_WIDTH), lambda b, i: (b, i, 0)),
                  pl.BlockSpec((1, tm, MLSTM_WIDTH), lambda b, i: (b, i, 0)),
                  pl.BlockSpec((1, tm, S5_WIDTH), lambda b, i: (b, i, 0)),
                  pl.BlockSpec((1, tm, S5_WIDTH), lambda b, i: (b, i, 0)),
                  pl.BlockSpec((1, tm, S5_WIDTH), lambda b, i: (b, i, 0)),
                  full((1, MLSTM_WIDTH)), full((1, S5_WIDTH)),
                  full((S5_WIDTH, 2 * S5_WIDTH)), full((d, d)),
                  _ctx_mod_spec(layer, 2, bsz), _lat_mod_spec(layer, 2)],
        out_specs=pl.BlockSpec((1, tm, d), lambda b, i: (b, i, 0)),
        out_shape=jax.ShapeDtypeStruct((bsz, t, d), F32),
        compiler_params=_cparams(("parallel", "parallel")),
        name="even_finish",
    )(*x_args, ha, o_pre, yf, yb, u, mh_w.reshape(1, -1), d_skip.reshape(1, -1), wg, wo, mods, mods)


def _conv_chunk(x_ref, w, lanes, c, static, n_chunks, ctx_chunks):
    tpos = lax.broadcasted_iota(jnp.int32, (CHUNK, HEAD_DIM), 0)
    first, last = tpos == 0, tpos == CHUNK - 1
    wrow = lambda k: w[k:k + 1, :]

    def rows(ci, dtype=F32):
        start = ci * CHUNK if isinstance(ci, int) else pl.multiple_of(ci * CHUNK, CHUNK)
        return x_ref[0, pl.ds(start, CHUNK), lanes].astype(dtype)

    if static and c < ctx_chunks:
        cur = rows(c)
        zero = jnp.zeros((1, HEAD_DIM), F32)
        prev_tok = rows(c - 1)[CHUNK - 1:CHUNK, :] if c > 0 else zero
        next_tok = rows(c + 1)[0:1, :] if c < ctx_chunks - 1 else zero
        left = jnp.where(first, prev_tok, pltpu.roll(cur, 1, axis=0))
        right = jnp.where(last, next_tok, pltpu.roll(cur, CHUNK - 1, axis=0))
        return wrow(3) * left + wrow(4) * cur + wrow(5) * right

    if static:
        up_i, dn_i = max(c - 1, ctx_chunks), min(c + 1, n_chunks - 1)
        w_up = w[0:3, :] if c > ctx_chunks else jnp.zeros((3, HEAD_DIM), F32)
        w_dn = w[6:9, :] if c < n_chunks - 1 else jnp.zeros((3, HEAD_DIM), F32)
    else:
        up_i, dn_i = jnp.maximum(c - 1, ctx_chunks), jnp.minimum(c + 1, n_chunks - 1)
        w_up = jnp.where(c > ctx_chunks, w[0:3, :], 0.0)
        w_dn = jnp.where(c < n_chunks - 1, w[6:9, :], 0.0)
    up, cur, dn = rows(up_i, BF16), rows(c, BF16), rows(dn_i, BF16)
    w_up, w_mid, w_dn = w_up.astype(BF16), w[3:6, :].astype(BF16), w_dn.astype(BF16)
    r0, r1, r2 = ((w_up[dc:dc + 1, :] * up + w_mid[dc:dc + 1, :] * cur + w_dn[dc:dc + 1, :] * dn).astype(F32)
                  for dc in range(3))
    prev = jnp.where(first, 0.0, pltpu.roll(r0, 1, axis=0))
    nxt = jnp.where(last, 0.0, pltpu.roll(r2, CHUNK - 1, axis=0))
    return r1 + prev + nxt


def _unit_tri_inverses(mats, eye_f):
    ps = [eye_f - a for a in mats]
    pws = [_dot(a, a) for a in mats]
    k = 2
    while 2 * k < CHUNK:
        res = [_dot(jnp.concatenate([p, pw], axis=0), pw) for p, pw in zip(ps, pws)]
        ps = [p + r[:CHUNK] for p, r in zip(ps, res)]
        pws = [r[CHUNK:] for r in res]
        k *= 2
    return [p + _dot(p, pw) for p, pw in zip(ps, pws)]


def _gdn_kernel(par_ref, q_ref, k_ref, v_ref, wq_ref, wk_ref, wv_ref, g_ref, o_ref,
                qk_scr, ol_scr, nl_scr, gl_scr, *, n_chunks, ctx_chunks, group, heads):
    row, col = _chunk_masks()
    eye = row == col
    eye_f = eye.astype(F32)

    def gate_terms(hd, r, a_pre, b_pre):
        incl = (col >= row) if r == 1 else (col <= row)
        a_log = jnp.full((1, CHUNK), par_ref[r * GDN_HEADS + hd], F32)
        dt_b = par_ref[(2 + r) * GDN_HEADS + hd]
        g_row = -jnp.exp(a_log) * jax.nn.softplus(a_pre + dt_b)
        beta_col = _to_col(jax.nn.sigmoid(b_pre), eye)
        gc_col = jnp.sum(jnp.where(incl, jnp.broadcast_to(g_row, incl.shape), 0.0),
                         axis=1, keepdims=True)
        gc_row = _to_row(gc_col, eye)
        g_last = jnp.sum(g_row, axis=1, keepdims=True)
        gamma = jnp.exp(jnp.where(incl, gc_col - gc_row, -jnp.inf))
        return beta_col, gc_col, g_last, gamma

    for hh in range(heads):
        hd = pl.program_id(1) * heads + hh
        lanes = slice(hh * HEAD_DIM, (hh + 1) * HEAD_DIM)

        def load(c, static, hd=hd, lanes=lanes):
            gates = [(g_ref[0, r * GDN_HEADS + hd, pl.ds(c, 1), :],
                      g_ref[0, (2 + r) * GDN_HEADS + hd, pl.ds(c, 1), :]) for r in (0, 1)]
            q, k, v = (_conv_chunk(x_ref, w_ref[:, lanes], lanes, c, static, n_chunks, ctx_chunks)
                       for x_ref, w_ref in ((q_ref, wq_ref), (k_ref, wk_ref), (v_ref, wv_ref)))
            q, k, v = q * jax.nn.sigmoid(q), k * jax.nn.sigmoid(k), v * jax.nn.sigmoid(v)
            q = q * (lax.rsqrt(jnp.sum(q * q, axis=-1, keepdims=True) + EPS) * (1.0 / math.sqrt(HEAD_DIM)))
            k = k * lax.rsqrt(jnp.sum(k * k, axis=-1, keepdims=True) + EPS)
            return q, k, v, gates

        def local_group(gi, _, hh=hh, hd=hd, load=load):
            static = isinstance(gi, int)
            chunks = [gi * group + g for g in range(group)]
            loaded = [load(c, static) for c in chunks]
            kqs = [_dot_nt(jnp.concatenate([k, q], axis=0), k) for q, k, _, _ in loaded]
            chains = [(g, r) for g in range(group) for r in (0, 1)]
            terms = [gate_terms(hd, r, *loaded[g][3][r]) for g, r in chains]
            a_mats = []
            for (g, r), (beta_col, _, _, gamma) in zip(chains, terms):
                strict = (col > row) if r == 1 else (col < row)
                a_mats.append(jnp.where(strict, beta_col * kqs[g][:CHUNK] * gamma, 0.0))
            t_invs = _unit_tri_inverses(a_mats, eye_f)
            sols = []
            for (g, r), (beta_col, gc_col, _, _), t_inv in zip(chains, terms, t_invs):
                _, k, v, _ = loaded[g]
                rhs = jnp.concatenate([v * beta_col, k * (beta_col * jnp.exp(gc_col))], axis=-1)
                sols.append(_dot(t_inv, rhs))
            auws = [_dot(kqs[g][CHUNK:] * tm[3], sol) for (g, r), tm, sol in zip(chains, terms, sols)]
            kuws = [_dot_tn(loaded[g][1] * jnp.exp(tm[2] - tm[1]), sol)
                    for (g, r), tm, sol in zip(chains, terms, sols)]
            for (g, r), (_, gc_col, g_last, _), auw, kuw in zip(chains, terms, auws, kuws):
                c = chunks[g]
                q = loaded[g][0]
                qk_scr[hh, r, c, :CHUNK, :] = (q * jnp.exp(gc_col) - auw[:, HEAD_DIM:]).astype(qk_scr.dtype)
                qk_scr[hh, r, c, CHUNK:, :] = kuw[:, HEAD_DIM:].astype(qk_scr.dtype)
                ol_scr[hh, r, c] = auw[:, :HEAD_DIM].astype(ol_scr.dtype)
                nl_scr[hh, r, c] = kuw[:, :HEAD_DIM].astype(nl_scr.dtype)
                gl_scr[hh, r, c] = jnp.broadcast_to(jnp.exp(g_last), (8, HEAD_DIM))
            return 0

        n_static = -(-ctx_chunks // group)
        for gi in range(n_static):
            local_group(gi, 0)
        lax.fori_loop(n_static, n_chunks // group, local_group, 0)

    o_ref[...] = jnp.zeros_like(o_ref)
    streams = [(hh, r) for hh in range(heads) for r in (0, 1)]

    def step(i, carry):
        cs = [_scan_chunk_index(i, r == 1, n_chunks, ctx_chunks) for _, r in streams]
        res = [jnp.dot(qk_scr[hh, r, c], s.astype(BF16), preferred_element_type=F32)
               for (hh, r), c, s in zip(streams, cs, carry)]
        new = []
        for (hh, r), c, s, rs in zip(streams, cs, carry, res):
            rows = pl.ds(pl.multiple_of(c * CHUNK, CHUNK), CHUNK)
            lanes = slice(hh * HEAD_DIM, (hh + 1) * HEAD_DIM)
            o_ref[0, rows, lanes] = (o_ref[0, rows, lanes].astype(F32) + ol_scr[hh, r, c]
                                     + rs[:CHUNK]).astype(o_ref.dtype)
            new.append(gl_scr[hh, r, c][0:1, :] * s - rs[CHUNK:] + nl_scr[hh, r, c])
        return tuple(new)

    init = jnp.zeros((HEAD_DIM, HEAD_DIM), F32)
    lax.fori_loop(0, n_chunks, step, (init,) * len(streams), unroll=math.gcd(6, n_chunks))


def _gdn(qkv, conv_w, gates_row, par, n_ctx, heads=4):
    bsz, t, _ = qkv.shape
    assert GRID_W == CHUNK and n_ctx % CHUNK == 0
    n_chunks = t // CHUNK
    group = math.gcd(9, n_chunks)
    wd = heads * HEAD_DIM
    blocks = GDN_HEADS // heads
    spec = lambda off: pl.BlockSpec((1, t, wd), lambda b, h: (b, 0, off + h), pipeline_mode=pl.Buffered(1))
    wspec = lambda off: pl.BlockSpec((9, wd), lambda b, h: (0, off + h))
    conv_w9 = conv_w.reshape(9, GDN_QKV)
    return pl.pallas_call(
        functools.partial(_gdn_kernel, n_chunks=n_chunks, ctx_chunks=n_ctx // CHUNK, group=group,
                          heads=heads),
        grid=(bsz, blocks),
        in_specs=[pl.BlockSpec(memory_space=pltpu.SMEM),
                  spec(0), spec(blocks), spec(2 * blocks),
                  wspec(0), wspec(blocks), wspec(2 * blocks),
                  pl.BlockSpec((1, 4 * GDN_HEADS, n_chunks, CHUNK), lambda b, h: (b, 0, 0, 0))],
        out_specs=pl.BlockSpec((1, t, wd), lambda b, h: (b, 0, h)),
        out_shape=jax.ShapeDtypeStruct((bsz, t, D_MODEL), BF16),
        scratch_shapes=[pltpu.VMEM((heads, 2, n_chunks, CHUNK + HEAD_DIM, HEAD_DIM), BF16),
                        pltpu.VMEM((heads, 2, n_chunks, CHUNK, HEAD_DIM), BF16),
                        pltpu.VMEM((heads, 2, n_chunks, HEAD_DIM, HEAD_DIM), BF16),
                        pltpu.VMEM((heads, 2, n_chunks, 8, HEAD_DIM), F32)],
        compiler_params=_cparams(("parallel", "parallel")),
        name="gdn",
    )(par, qkv, qkv, qkv, conv_w9, conv_w9, conv_w9, gates_row)


def _odd_finish_kernel(x_ref, o_in_ref, z_ref, hw_ref, wo_ref, g_ref, o_ref):
    o = o_in_ref[0].astype(F32)
    parts = []
    for h in range(GDN_HEADS):
        sl = slice(h * HEAD_DIM, (h + 1) * HEAD_DIM)
        parts.append(_rms(o[:, sl], hw_ref[:, sl]))
    z = z_ref[0].astype(F32)
    y = jnp.concatenate(parts, axis=-1) * (z * jax.nn.sigmoid(z))
    o_ref[0] = x_ref[0] + g_ref[0, 0] * _dot(y, wo_ref[...])


def _odd_finish(xc, o, z, mods, layer, hw, wo, tm, n_ctx):
    bsz, t, d = xc.shape
    assert n_ctx % tm == 0 and (t - n_ctx) % tm == 0
    t0 = n_ctx // tm
    return pl.pallas_call(
        _odd_finish_kernel,
        grid=(bsz, (t - n_ctx) // tm),
        in_specs=[pl.BlockSpec((1, tm, d), lambda b, i: (b, i + t0, 0)),
                  pl.BlockSpec((1, tm, d), lambda b, i: (b, i + t0, 0)),
                  pl.BlockSpec((1, tm, d), lambda b, i: (b, i + t0, 0)),
                  pl.BlockSpec((1, d), lambda b, i: (0, 0)),
                  pl.BlockSpec((d, d), lambda b, i: (0, 0)),
                  _lat_mod_spec(layer, 2)],
        out_specs=pl.BlockSpec((1, tm, d), lambda b, i: (b, i, 0)),
        out_shape=jax.ShapeDtypeStruct((bsz, t - n_ctx, d), F32),
        compiler_params=_cparams(("parallel", "parallel")),
        name="odd_finish",
    )(xc, o, z, hw.reshape(1, d), wo, mods)


def _chunked(gates_t, chunk=CHUNK):
    bsz, g, t = gates_t.shape
    return gates_t.reshape(bsz, g, t // chunk, chunk)


def _pick_tile(total, target, at_least, multiple=8):
    for tile in range(min(target, total), multiple - 1, -1):
        if total % tile == 0 and tile % multiple == 0 and tile >= at_least:
            return tile
    raise ValueError(f"no row tile for {total} rows")


def kernel(x, c, ctx, c_ctx, ada_w, ada_b, norm1_w, norm2_w, ffn_w1, ffn_w3, ffn_w2, final_norm_w,
           ev_w_in, ev_i_bias, ev_f_bias, ev_head_norm_w, ev_lam_re, ev_lam_im, ev_log_dt,
           ev_b_re, ev_b_im, ev_c_re, ev_c_im, ev_d, ev_w_glu, ev_w_out,
           od_w_in, od_conv_w, od_a_log, od_dt_bias, od_head_norm_w, od_w_out):
    bsz, seq, d = x.shape
    n_ctx = ctx.shape[1]
    assert d == D_MODEL and seq % GRID_W == 0 and n_ctx % CHUNK == 0 and bsz % 8 == 0
    assert ada_w.shape[0] == 2
    t_all = n_ctx + seq
    tm = _pick_tile(t_all, 768, n_ctx, LANES)
    tm_lat = _pick_tile(math.gcd(seq, n_ctx), 512, 0)
    th = 2 * LANES
    tb = 2 * CHUNK if n_ctx % (2 * CHUNK) == 0 and seq % (2 * CHUNK) == 0 else CHUNK

    if tm % n_ctx == 0 and seq % n_ctx == 0:
        xs = (ctx, x)
    else:
        xs = (jnp.concatenate([ctx, x], axis=1),)
    mod_rows = 16
    cvec = jnp.zeros((mod_rows, d), F32).at[:bsz].set(c).at[bsz].set(c_ctx)
    mods = _ada(cvec, ada_w, ada_b).reshape(2, mod_rows, 1, 6 * d)
    w1, w3, w2 = ffn_w1.astype(BF16), ffn_w3.astype(BF16), ffn_w2.astype(BF16)

    qkv_w, n_g = 3 * MLSTM_WIDTH, 4 * MLSTM_HEADS
    w_in = ev_w_in[0]
    g0 = qkv_w + MLSTM_WIDTH
    n_even = w_in.shape[1]
    qkv, o_pre, u, gates = _inproj(
        xs, norm1_w[0], mods, 0, w_in.astype(BF16),
        [(0, qkv_w, "row", BF16, qkv_w), (qkv_w, g0, "row", BF16, g0 - qkv_w),
         (g0 + n_g, n_even, "row", BF16, S5_WIDTH), (g0, g0 + n_g, "gate", F32, n_g)], tm, n_ctx)
    bias = jnp.concatenate([ev_i_bias[0].reshape(-1), ev_f_bias[0].reshape(-1)]).astype(F32)
    mlstm_chunk = MLSTM_CHUNK if n_ctx % MLSTM_CHUNK == 0 and seq % MLSTM_CHUNK == 0 else CHUNK
    ha = _mlstm(qkv, gates, bias, n_ctx, mlstm_chunk)
    bd, cdt, avec = _s5_prep(ev_lam_re[0], ev_lam_im[0], ev_log_dt[0], ev_b_re[0], ev_b_im[0],
                             ev_c_re[0], ev_c_im[0])
    yf, yb = _s5(u, bd, cdt, avec, n_ctx, tb)
    xc = _even_finish(xs, ha, o_pre, yf, yb, u, mods, 0, ev_head_norm_w[0], ev_d[0],
                      ev_w_glu[0].astype(BF16), ev_w_out[0].astype(BF16), tm, n_ctx)
    xc = _ffn(xc, norm2_w[0], mods, 0, w1, w3, w2, final_norm_w, tm, th, n_ctx, False)

    n_g = 4 * GDN_HEADS
    qkv_pre, z, gates = _inproj(
        (xc,), norm1_w[1], mods, 1, od_w_in[0].astype(BF16),
        [(0, GDN_QKV, "row", BF16, GDN_QKV), (GDN_QKV, 4 * d, "row", BF16, d),
         (4 * d, 4 * d + n_g, "gate", F32, n_g)], tm, n_ctx)
    par = jnp.concatenate([od_a_log[0].reshape(-1), od_dt_bias[0].reshape(-1)]).astype(F32)
    o = _gdn(qkv_pre, od_conv_w[0], _chunked(gates), par, n_ctx)
    xl = _odd_finish(xc, o, z, mods, 1, od_head_norm_w[0], od_w_out[0].astype(BF16), tm_lat, n_ctx)
    return _ffn(xl, norm2_w[1], mods, 1, w1, w3, w2, final_norm_w, _pick_tile(seq, 512, 0),
                th, 0, True)
```
